```python
import math
import jax, jax.numpy as jnp
from jax import lax
import numpy as np

D_MODEL = 1024
BATCH = 8
SEQ = 2048
DEPTH = 1
DEC_BATCH = 128
DEC_SEQ = 4
PAST_LEN = 16384
PAGE_SIZE = 128

GLA_HEADS = 4
GLA_DK = D_MODEL // 8
GLA_DV = D_MODEL // 4
GLA_QK = GLA_HEADS * GLA_DK
GLA_VW = GLA_HEADS * GLA_DV
GLA_GATE_RANK = 16
GLA_TAU = 16.0
GLA_CHUNK = 64
GDN_HEADS = 8
GDN_DK = D_MODEL // 8
GDN_DV = D_MODEL // 8
GDN_QK = GDN_HEADS * GDN_DK
GDN_VW = GDN_HEADS * GDN_DV
GDN_CONV = 4
GDN_CONV_DIM = 2 * GDN_QK + GDN_VW
GDN_CHUNK = 64
IN_SPLITS = (GLA_QK, GLA_QK, GLA_VW, GLA_VW, GLA_GATE_RANK,
             GDN_CONV_DIM, GDN_VW, GDN_HEADS, GDN_HEADS, D_MODEL, D_MODEL)
D_IN = 2 * GLA_QK + 2 * GLA_VW + GLA_GATE_RANK + GDN_CONV_DIM + GDN_VW + 2 * GDN_HEADS + 2 * D_MODEL
N_GROUPS = 4
EXPERTS_PER_GROUP = 4
D_EXPERT = D_MODEL // 2
TOP_K = 2
N_MOD = 6
NORM_EPS = 1e-6
L2_EPS = 1e-6

kernel_name = 'hybrid_gla_gdn_hmoe_adaln_step'


def _rms_norm(x, w):
    xf = x.astype(jnp.float32)
    y = xf * lax.rsqrt(jnp.mean(xf * xf, axis=-1, keepdims=True) + NORM_EPS)
    return (y * w.astype(jnp.float32)).astype(x.dtype)


def _l2norm(x):
    return x * lax.rsqrt(jnp.sum(x * x, axis=-1, keepdims=True) + L2_EPS)


def _split(x, sizes):
    out, off = [], 0
    for s in sizes:
        out.append(x[..., off:off + s])
        off += s
    return out


def _to_chunks(x, C):
    B, T, H = x.shape[:3]
    x = x.reshape((B, T // C, C, H) + x.shape[3:])
    return x.transpose((1, 0, 3, 2) + tuple(range(4, x.ndim)))


def _from_chunks(o):
    N, B, H, C, d = o.shape
    return o.transpose(1, 0, 3, 2, 4).reshape(B, N * C, H, d)


def _gla_chunked(q, k, v, log_a, S0):
    C = math.gcd(q.shape[1], GLA_CHUNK)
    causal = jnp.tril(jnp.ones((C, C), dtype=bool))

    def step(S, inp):
        qc, kc, vc, gc = inp
        b = jnp.cumsum(gc, axis=2)
        diff = b[:, :, :, None, :] - b[:, :, None, :, :]
        dec = jnp.exp(jnp.where(causal[:, :, None], diff, -jnp.inf))
        A = jnp.einsum('bhtd,bhsd,bhtsd->bhts', qc, kc, dec)
        o = (jnp.einsum('bhtd,bhdv->bhtv', qc * jnp.exp(b), S)
             + jnp.einsum('bhts,bhsv->bhtv', A, vc))
        b_last = b[:, :, -1:, :]
        S = (jnp.exp(b_last[:, :, 0, :, None]) * S
             + jnp.einsum('bhsd,bhsv->bhdv', kc * jnp.exp(b_last - b), vc))
        return S, o

    S, o = lax.scan(step, S0, (_to_chunks(q, C), _to_chunks(k, C), _to_chunks(v, C), _to_chunks(log_a, C)))
    return _from_chunks(o), S


def _gdn_chunked(q, k, v, log_g, beta, S0):
    C = math.gcd(q.shape[1], GDN_CHUNK)
    dv = v.shape[-1]
    incl = jnp.tril(jnp.ones((C, C), dtype=bool))
    strict = jnp.tril(jnp.ones((C, C), dtype=bool), k=-1)

    def step(S, inp):
        qc, kc, vc, gc, bc = inp
        b = jnp.cumsum(gc, axis=-1)
        diff = b[..., :, None] - b[..., None, :]
        dec = jnp.exp(jnp.where(incl, diff, -jnp.inf))
        kb = kc * bc[..., None]
        L = jnp.where(strict, jnp.einsum('bhtd,bhsd->bhts', kb, kc) * dec, 0.0)
        rhs = jnp.concatenate([vc * bc[..., None], kb * jnp.exp(b)[..., None]], axis=-1)
        X = lax.linalg.triangular_solve(L, rhs, left_side=True, lower=True, unit_diagonal=True)
        u, w = X[..., :dv], X[..., dv:]
        v_new = u - jnp.einsum('bhtd,bhdv->bhtv', w, S)
        attn = jnp.einsum('bhtd,bhsd->bhts', qc, kc) * dec
        o = (jnp.einsum('bhtd,bhdv->bhtv', qc * jnp.exp(b)[..., None], S)
             + jnp.einsum('bhts,bhsv->bhtv', attn, v_new))
        b_last = b[..., -1:]
        S = (jnp.exp(b_last)[..., None] * S
             + jnp.einsum('bhsd,bhsv->bhdv', kc * jnp.exp(b_last - b)[..., None], v_new))
        return S, o

    S, o = lax.scan(step, S0, (_to_chunks(q, C), _to_chunks(k, C), _to_chunks(v, C),
                               _to_chunks(log_g, C), _to_chunks(beta, C)))
    return _from_chunks(o), S


def _causal_conv(x, buf, w):
    T = x.shape[1]
    xp = jnp.concatenate([buf.astype(x.dtype), x], axis=1)
    y = xp[:, 0:T] * w[0]
    for j in range(1, GDN_CONV):
        y = y + xp[:, j:j + T] * w[j]
    return jax.nn.silu(y), xp[:, -(GDN_CONV - 1):]


def _mixer(h, S_gla, S_gdn, conv_buf, w_in, w_gla_a2, b_gla_a, gla_norm_w, w_conv,
           gdn_A_log, gdn_dt_bias, gdn_norm_w, w_out):
    f32 = jnp.float32
    B, T, _ = h.shape
    proj = h @ w_in
    qa, ka, va, ga, ra, qkv_b, zb, beta_raw, a_raw, gate_a, gate_b = _split(proj, IN_SPLITS)
    q = qa.reshape(B, T, GLA_HEADS, GLA_DK).astype(f32) * GLA_DK ** -0.5
    k = ka.reshape(B, T, GLA_HEADS, GLA_DK).astype(f32)
    v = va.reshape(B, T, GLA_HEADS, GLA_DV).astype(f32)
    log_a = jax.nn.log_sigmoid((ra @ w_gla_a2 + b_gla_a).astype(f32)) / GLA_TAU
    log_a = log_a.reshape(B, T, GLA_HEADS, GLA_DK)
    o, S_gla = _gla_chunked(q, k, v, log_a, S_gla.astype(f32))
    o_a = _rms_norm(o, gla_norm_w).reshape(B, T, GLA_VW) * jax.nn.silu(ga.astype(f32))
    qkv, conv_buf = _causal_conv(qkv_b, conv_buf, w_conv)
    qb, kb, vb = _split(qkv, (GDN_QK, GDN_QK, GDN_VW))
    q = _l2norm(qb.reshape(B, T, GDN_HEADS, GDN_DK).astype(f32)) * GDN_DK ** -0.5
    k = _l2norm(kb.reshape(B, T, GDN_HEADS, GDN_DK).astype(f32))
    v = vb.reshape(B, T, GDN_HEADS, GDN_DV).astype(f32)
    beta = jax.nn.sigmoid(beta_raw.astype(f32))
    log_g = -jnp.exp(gdn_A_log.astype(f32)) * jax.nn.softplus(a_raw.astype(f32) + gdn_dt_bias.astype(f32))
    o, S_gdn = _gdn_chunked(q, k, v, log_g, beta, S_gdn.astype(f32))
    o_b = _rms_norm(o, gdn_norm_w).reshape(B, T, GDN_VW) * jax.nn.silu(zb.astype(f32))
    merged = jax.nn.sigmoid(gate_a.astype(f32)) * o_a + jax.nn.sigmoid(gate_b.astype(f32)) * o_b
    return merged.astype(h.dtype) @ w_out, S_gla, S_gdn, conv_buf


def _moe(h, w_gr, b_gr, w_er, b_er, w_g, w_u, w_d):
    f32 = jnp.float32
    B, T, D = h.shape
    M = B * T
    hf = h.reshape(M, D)
    gp = jax.nn.softmax((hf @ w_gr).astype(f32) + b_gr.astype(f32), axis=-1)
    g_w, g_idx = lax.top_k(gp, 1)
    el = ((hf @ w_er).astype(f32) + b_er.astype(f32)).reshape(M, N_GROUPS, EXPERTS_PER_GROUP)
    el = jnp.take_along_axis(el, g_idx[:, :, None], axis=1)[:, 0]
    e_w, e_idx = lax.top_k(jax.nn.softmax(el, axis=-1), TOP_K)
    e_w = e_w / jnp.sum(e_w, axis=-1, keepdims=True)
    within = jnp.einsum('mk,mke->me', e_w, jax.nn.one_hot(e_idx, EXPERTS_PER_GROUP, dtype=f32))
    comb = ((jax.nn.one_hot(g_idx[:, 0], N_GROUPS, dtype=f32) * g_w)[:, :, None]
            * within[:, None, :]).astype(h.dtype)
    out = jnp.zeros_like(hf)
    for g in range(N_GROUPS):
        a = jnp.einsum('md,edf->mef', hf, w_g[g])
        u = jnp.einsum('md,edf->mef', hf, w_u[g])
        hid = jax.nn.silu(a) * u * comb[:, g, :, None]
        out = out + jnp.einsum('mef,efd->md', hid, w_d[g])
    return out.reshape(B, T, D)


def _trunk(x, c, st_gla, st_gdn, st_conv, params):
    (w_ada, b_ada, norm1_w, w_in, w_gla_a2, b_gla_a, gla_norm_w, w_conv, gdn_A_log, gdn_dt_bias,
     gdn_norm_w, w_out, norm2_w, w_group_router, b_group_router, w_expert_router, b_expert_router,
     w_exp_gate, w_exp_up, w_exp_down, final_norm_w) = params
    new_gla, new_gdn, new_conv = [], [], []
    cs = jax.nn.silu(c)
    for l in range(DEPTH):
        mod = (cs @ w_ada[l] + b_ada[l]).reshape(c.shape[0], N_MOD, D_MODEL)[:, :, None, :]
        shift1, scale1, gate1, shift2, scale2, gate2 = [mod[:, i] for i in range(N_MOD)]
        h = _rms_norm(x, norm1_w[l]) * (1 + scale1) + shift1
        y, sg, sd, cb = _mixer(h, st_gla[l], st_gdn[l], st_conv[l], w_in[l], w_gla_a2[l], b_gla_a[l],
                               gla_norm_w[l], w_conv[l], gdn_A_log[l], gdn_dt_bias[l], gdn_norm_w[l], w_out[l])
        x = x + gate1 * y
        h = _rms_norm(x, norm2_w[l]) * (1 + scale2) + shift2
        x = x + gate2 * _moe(h, w_group_router[l], b_group_router[l], w_expert_router[l],
                             b_expert_router[l], w_exp_gate[l], w_exp_up[l], w_exp_down[l])
        new_gla.append(sg)
        new_gdn.append(sd)
        new_conv.append(cb)
    return _rms_norm(x, final_norm_w), jnp.stack(new_gla), jnp.stack(new_gdn), jnp.stack(new_conv)


def setup_inputs(seed: int = 0) -> dict:
    key = jax.random.key(seed)
    ks = jax.random.split(key, 32)
    f32 = jnp.float32

    def nrm(k, shape, scale):
        return jax.random.normal(k, shape, f32) * scale

    L, D, G, E, F = DEPTH, D_MODEL, N_GROUPS, EXPERTS_PER_GROUP, D_EXPERT
    dt = jnp.exp(jax.random.uniform(ks[10], (L, GDN_HEADS), f32) * (math.log(0.1) - math.log(0.001)) + math.log(0.001))
    return {
        'x_prompt': nrm(ks[0], (BATCH, SEQ, D), 1.0),
        'x_sample': nrm(ks[1], (DEC_BATCH, DEC_SEQ, D), 1.0),
        'c_prompt': nrm(ks[2], (BATCH, D), 1.0),
        'c_sample': nrm(ks[3], (DEC_BATCH, D), 1.0),
        'state_gla': nrm(ks[4], (L, DEC_BATCH, GLA_HEADS, GLA_DK, GLA_DV), 0.5),
        'state_gdn': nrm(ks[5], (L, DEC_BATCH, GDN_HEADS, GDN_DK, GDN_DV), 0.1),
        'state_conv': nrm(ks[6], (L, DEC_BATCH, GDN_CONV - 1, GDN_CONV_DIM), 1.0),
        'w_ada': nrm(ks[7], (L, D, N_MOD * D), 0.5 * D ** -0.5),
        'b_ada': nrm(ks[8], (L, N_MOD * D), 0.02),
        'norm1_w': 1.0 + nrm(ks[9], (L, D), 0.02),
        'w_in': nrm(ks[11], (L, D, D_IN), D ** -0.5),
        'w_gla_a2': nrm(ks[12], (L, GLA_GATE_RANK, GLA_QK), GLA_GATE_RANK ** -0.5),
        'b_gla_a': nrm(ks[13], (L, GLA_QK), 0.1),
        'gla_norm_w': 1.0 + nrm(ks[14], (L, GLA_DV), 0.02),
        'w_conv': nrm(ks[15], (L, GDN_CONV, GDN_CONV_DIM), GDN_CONV ** -0.5),
        'gdn_A_log': jnp.log(jax.random.uniform(ks[16], (L, GDN_HEADS), f32, 1.0, 16.0)),
        'gdn_dt_bias': dt + jnp.log(-jnp.expm1(-dt)),
        'gdn_norm_w': 1.0 + nrm(ks[17], (L, GDN_DV), 0.02),
        'w_out': nrm(ks[18], (L, D, D), D ** -0.5),
        'norm2_w': 1.0 + nrm(ks[19], (L, D), 0.02),
        'w_group_router': nrm(ks[20], (L, D, G), D ** -0.5),
        'b_group_router': nrm(ks[21], (L, G), 0.01),
        'w_expert_router': nrm(ks[22], (L, D, G * E), D ** -0.5),
        'b_expert_router': nrm(ks[23], (L, G * E), 0.01),
        'w_exp_gate': nrm(ks[24], (L, G, E, D, F), D ** -0.5),
        'w_exp_up': nrm(ks[25], (L, G, E, D, F), D ** -0.5),
        'w_exp_down': nrm(ks[26], (L, G, E, F, D), F ** -0.5),
        'final_norm_w': 1.0 + nrm(ks[27], (D,), 0.02),
    }


def reference(x_prompt, x_sample, c_prompt, c_sample, state_gla, state_gdn, state_conv,
              w_ada, b_ada, norm1_w, w_in, w_gla_a2, b_gla_a, gla_norm_w, w_conv, gdn_A_log,
              gdn_dt_bias, gdn_norm_w, w_out, norm2_w, w_group_router, b_group_router,
              w_expert_router, b_expert_router, w_exp_gate, w_exp_up, w_exp_down, final_norm_w):
    params = (w_ada, b_ada, norm1_w, w_in, w_gla_a2, b_gla_a, gla_norm_w, w_conv, gdn_A_log,
              gdn_dt_bias, gdn_norm_w, w_out, norm2_w, w_group_router, b_group_router,
              w_expert_router, b_expert_router, w_exp_gate, w_exp_up, w_exp_down, final_norm_w)
    Bp = x_prompt.shape[0]
    zero_gla = jnp.zeros((DEPTH, Bp, GLA_HEADS, GLA_DK, GLA_DV), jnp.float32)
    zero_gdn = jnp.zeros((DEPTH, Bp, GDN_HEADS, GDN_DK, GDN_DV), jnp.float32)
    zero_conv = jnp.zeros((DEPTH, Bp, GDN_CONV - 1, GDN_CONV_DIM), x_prompt.dtype)
    y_prompt, gla_p, gdn_p, conv_p = _trunk(x_prompt, c_prompt, zero_gla, zero_gdn, zero_conv, params)
    y_sample, gla_s, gdn_s, conv_s = _trunk(x_sample, c_sample, state_gla, state_gdn, state_conv, params)
    return (y_prompt, y_sample,
            gla_p.astype(state_gla.dtype), gdn_p.astype(state_gdn.dtype), conv_p.astype(state_conv.dtype),
            gla_s.astype(state_gla.dtype), gdn_s.astype(state_gdn.dtype), conv_s.astype(state_conv.dtype))
```

```python
import functools
import math

import jax
import jax.numpy as jnp
from jax import lax
from jax.experimental import pallas as pl
from jax.experimental.pallas import tpu as pltpu

F32 = jnp.float32
BF16 = jnp.bfloat16
HIGHEST = lax.Precision.HIGHEST

D_MODEL = 1024
GLA_HEADS = 4
GLA_DK = 128
GLA_DV = 256
GLA_QK = GLA_HEADS * GLA_DK
GLA_VW = GLA_HEADS * GLA_DV
GLA_GATE_RANK = 16
GLA_TAU = 16.0
GLA_SUBCHUNK = 16
GDN_HEADS = 8
GDN_DK = 128
GDN_DV = 128
GDN_QK = GDN_HEADS * GDN_DK
GDN_VW = GDN_HEADS * GDN_DV
GDN_CONV = 4
GDN_CONV_DIM = 2 * GDN_QK + GDN_VW
N_GROUPS = 4
EXPERTS_PER_GROUP = 4
N_EXPERTS = N_GROUPS * EXPERTS_PER_GROUP
D_EXPERT = D_MODEL // 2
N_MOD = 6
NORM_EPS = 1e-6
L2_EPS = 1e-6

LANES = 128
SUBLANES = 8
VMEM_LIMIT = 48 * 1024 * 1024

PROJ_MAIN = 2 * GLA_QK + 2 * GLA_VW + GDN_CONV_DIM + GDN_VW + 2 * D_MODEL
SMALL_RA, SMALL_BETA, SMALL_A = 0, GLA_GATE_RANK, GLA_GATE_RANK + GDN_HEADS
SMALL_ROWS = 32
ROUTER_GROUP_STRIDE = 8
ROUTER_ROWS = ROUTER_GROUP_STRIDE * (1 + N_GROUPS)

NT_DIMS = (((1,), (1,)), ((), ()))
TN_DIMS = (((0,), (0,)), ((), ()))


def _sigmoid(x):
    return 1.0 / (1.0 + jnp.exp(-x))


def _silu(x):
    return x * _sigmoid(x)


def _softplus(x):
    return jnp.maximum(x, 0.0) + jnp.log1p(jnp.exp(-jnp.abs(x)))


def _rms(x, w):
    return x * lax.rsqrt(jnp.mean(x * x, axis=-1, keepdims=True) + NORM_EPS) * w


def _params(*sem):
    return pltpu.CompilerParams(dimension_semantics=sem, vmem_limit_bytes=VMEM_LIMIT)


def _ada_kernel(c_ref, w_ref, b_ref, o_ref):
    cs = _silu(c_ref[...])
    o_ref[...] = jnp.dot(cs.astype(BF16), w_ref[...].astype(BF16), preferred_element_type=F32) + b_ref[...]


def _ada_mod(c_all, w_ada, b_ada):
    rows, d = c_all.shape
    n = w_ada.shape[1]
    tn = 1024
    return pl.pallas_call(
        _ada_kernel,
        out_shape=jax.ShapeDtypeStruct((rows, n), F32),
        grid=(n // tn,),
        in_specs=[pl.BlockSpec((rows, d), lambda j: (0, 0)),
                  pl.BlockSpec((d, tn), lambda j: (0, j)),
                  pl.BlockSpec((1, tn), lambda j: (0, j))],
        out_specs=pl.BlockSpec((rows, tn), lambda j: (0, j)),
        compiler_params=_params("arbitrary"),
        name="ada_mod",
    )(c_all, w_ada, b_ada.reshape(1, n))


def _inproj_kernel(x_ref, shift_ref, scale_ref, nw_ref, w_ref, ws_ref, wst_ref,
                   o_ref, os_ref, ost_ref, h_scr):
    @pl.when(pl.program_id(1) == 0)
    def _():
        h = _rms(x_ref[...], nw_ref[...]) * (1.0 + scale_ref[...]) + shift_ref[...]
        hb = h.astype(BF16)
        h_scr[...] = hb
        os_ref[...] = jnp.dot(hb, ws_ref[...], preferred_element_type=F32)
        ost_ref[...] = lax.dot_general(wst_ref[...], hb, NT_DIMS, preferred_element_type=F32)

    o_ref[...] = jnp.dot(h_scr[...], w_ref[...], preferred_element_type=F32).astype(o_ref.dtype)


def _inproj(x2d, shift, scale, mod_specs, nw, w_main, w_small, w_small_t, tm, out_dtype):
    m, d = x2d.shape
    n = w_main.shape[1]
    tn = 1024
    return pl.pallas_call(
        _inproj_kernel,
        out_shape=(jax.ShapeDtypeStruct((m, n), out_dtype),
                   jax.ShapeDtypeStruct((m, LANES), F32),
                   jax.ShapeDtypeStruct((SMALL_ROWS, m), F32)),
        grid=(m // tm, n // tn),
        in_specs=[pl.BlockSpec((tm, d), lambda i, j: (i, 0)),
                  mod_specs[0], mod_specs[1],
                  pl.BlockSpec((1, d), lambda i, j: (0, 0)),
                  pl.BlockSpec((d, tn), lambda i, j: (0, j)),
                  pl.BlockSpec((d, LANES), lambda i, j: (0, 0)),
                  pl.BlockSpec((SMALL_ROWS, d), lambda i, j: (0, 0))],
        out_specs=(pl.BlockSpec((tm, tn), lambda i, j: (i, j)),
                   pl.BlockSpec((tm, LANES), lambda i, j: (i, 0)),
                   pl.BlockSpec((SMALL_ROWS, tm), lambda i, j: (0, i))),
        scratch_shapes=[pltpu.VMEM((tm, d), BF16)],
        compiler_params=_params("parallel", "arbitrary"),
        name="inproj",
    )(x2d, shift, scale, nw, w_main, w_small, w_small_t)


def _gla_kernel(*refs, chunk, sub, nchunks, has_state, t_valid):
    if has_state:
        (qk_ref, v_ref, ga_ref, gt_ref, sm_ref, wa2_ref, ba_ref, nw_ref, s0_ref,
         o_ref, sout_ref, st_scr) = refs
    else:
        (qk_ref, v_ref, ga_ref, gt_ref, sm_ref, wa2_ref, ba_ref, nw_ref,
         o_ref, sout_ref, st_scr) = refs
    n = pl.program_id(1)
    masked = t_valid < chunk * nchunks

    @pl.when(n == 0)
    def _():
        for h in range(GLA_HEADS):
            if has_state:
                st_scr[h] = s0_ref[0, h].T
            else:
                st_scr[h] = jnp.zeros((GLA_DV, GLA_DK), F32)

    row = lax.broadcasted_iota(jnp.int32, (chunk, chunk), 0)
    col = lax.broadcasted_iota(jnp.int32, (chunk, chunk), 1)
    ra = sm_ref[:, SMALL_RA:SMALL_RA + GLA_GATE_RANK]
    x = jnp.dot(ra.astype(BF16), wa2_ref[...].astype(BF16), preferred_element_type=F32) + ba_ref[...]
    g = (jnp.minimum(x, 0.0) - jnp.log1p(jnp.exp(-jnp.abs(x)))) * (1.0 / GLA_TAU)
    if masked:
        valid = (lax.broadcasted_iota(jnp.int32, (chunk, 1), 0) + n * chunk) < t_valid
        g = jnp.where(valid, g, 0.0)
    tri = (row >= col).astype(F32)
    b_all = jnp.dot(tri, g, precision=HIGHEST, preferred_element_type=F32)

    for h in range(GLA_HEADS):
        q = qk_ref[:, h * GLA_DK:(h + 1) * GLA_DK].astype(F32) * (GLA_DK ** -0.5)
        k = qk_ref[:, GLA_QK + h * GLA_DK:GLA_QK + (h + 1) * GLA_DK].astype(F32)
        if masked:
            k = jnp.where(valid, k, 0.0)
        v = v_ref[:, h * GLA_DV:(h + 1) * GLA_DV].astype(BF16)
        b = b_all[:, h * GLA_DK:(h + 1) * GLA_DK]
        st = st_scr[h]
        o = lax.dot_general((q * jnp.exp(b)).astype(BF16), st.astype(BF16), NT_DIMS,
                            preferred_element_type=F32)
        intra = []
        for i in range(chunk // sub):
            r0, r1 = i * sub, (i + 1) * sub
            bref = b[r0 - 1:r0] if i > 0 else jnp.zeros((1, GLA_DK), F32)
            qt = (q[r0:r1] * jnp.exp(b[r0:r1] - bref)).astype(BF16)
            kt = (k[0:r1] * jnp.exp(bref - b[0:r1])).astype(BF16)
            a = lax.dot_general(qt, kt, NT_DIMS, preferred_element_type=F32)
            causal = (lax.broadcasted_iota(jnp.int32, (sub, r1), 1)
                      <= lax.broadcasted_iota(jnp.int32, (sub, r1), 0) + r0)
            a = jnp.where(causal, a, 0.0)
            intra.append(jnp.dot(a.astype(BF16), v[0:r1], preferred_element_type=F32))
        o = o + (intra[0] if len(intra) == 1 else jnp.concatenate(intra, axis=0))
        bl = b[chunk - 1:chunk]
        kt = (k * jnp.exp(bl - b)).astype(BF16)
        st_scr[h] = st * jnp.exp(bl) + lax.dot_general(v, kt, TN_DIMS, preferred_element_type=F32)

        sl = slice(h * GLA_DV, (h + 1) * GLA_DV)
        oa = _rms(o, nw_ref[...]) * _silu(ga_ref[:, sl].astype(F32)) * _sigmoid(gt_ref[:, sl].astype(F32))
        o_ref[:, sl] = oa.astype(o_ref.dtype)

    @pl.when(n == nchunks - 1)
    def _():
        for h in range(GLA_HEADS):
            sout_ref[0, h] = st_scr[h].T


def _gla(proj, small, wa2, ba, nw, s0, *, nb, t_pad, chunk, sub, t_valid, out_dtype):
    nchunks = t_pad // chunk
    has_state = s0 is not None
    blk = lambda c: pl.BlockSpec((chunk, 1024), lambda b, n, c=c: (b * nchunks + n, c))
    in_specs = [blk(0), blk(1), blk(2), blk(7),
                pl.BlockSpec((chunk, LANES), lambda b, n: (b * nchunks + n, 0)),
                pl.BlockSpec(wa2.shape, lambda b, n: (0, 0)),
                pl.BlockSpec(ba.shape, lambda b, n: (0, 0)),
                pl.BlockSpec(nw.shape, lambda b, n: (0, 0))]
    args = [proj, proj, proj, proj, small, wa2, ba, nw]
    state_spec = pl.BlockSpec((1, GLA_HEADS, GLA_DK, GLA_DV), lambda b, n: (b, 0, 0, 0))
    if has_state:
        in_specs.append(state_spec)
        args.append(s0)
    return pl.pallas_call(
        functools.partial(_gla_kernel, chunk=chunk, sub=sub, nchunks=nchunks,
                          has_state=has_state, t_valid=t_valid),
        out_shape=(jax.ShapeDtypeStruct((nb * t_pad, GLA_VW), out_dtype),
                   jax.ShapeDtypeStruct((nb, GLA_HEADS, GLA_DK, GLA_DV), F32)),
        grid=(nb, nchunks),
        in_specs=in_specs,
        out_specs=(pl.BlockSpec((chunk, GLA_VW), lambda b, n: (b * nchunks + n, 0)), state_spec),
        scratch_shapes=[pltpu.VMEM((GLA_HEADS, GLA_DV, GLA_DK), F32)],
        compiler_params=_params("parallel", "arbitrary"),
        name="gla",
    )(*args)


def _gdn_kernel(*refs, chunk, nchunks, has_state, t_valid):
    if has_state:
        (qkv_ref, zb_ref, gt_ref, sm_ref, smt_ref, wc_ref, alog_ref, dtb_ref, alogc_ref, dtbc_ref, nw_ref,
         s0_ref, c0_ref, o_ref, sout_ref, cout_ref, s_scr, xp_scr) = refs
    else:
        (qkv_ref, zb_ref, gt_ref, sm_ref, smt_ref, wc_ref, alog_ref, dtb_ref, alogc_ref, dtbc_ref, nw_ref,
         o_ref, sout_ref, cout_ref, s_scr, xp_scr) = refs
    n = pl.program_id(1)
    masked = t_valid < chunk * nchunks
    pad = SUBLANES

    @pl.when(n == 0)
    def _():
        if has_state:
            for h in range(GDN_HEADS):
                s_scr[h] = s0_ref[0, h]
            xp_scr[0:pad, :] = jnp.zeros((pad, GDN_CONV_DIM), F32)
            xp_scr[pad - (GDN_CONV - 1):pad, :] = c0_ref[0]
        else:
            s_scr[...] = jnp.zeros_like(s_scr)
            xp_scr[0:pad, :] = jnp.zeros((pad, GDN_CONV_DIM), F32)

    xp_scr[pad:pad + chunk, :] = qkv_ref[...].astype(F32)
    y = xp_scr[pad:pad + chunk, :] * wc_ref[GDN_CONV - 1:GDN_CONV, :]
    for j in range(1, GDN_CONV):
        y = y + xp_scr[pad - j:pad - j + chunk, :] * wc_ref[GDN_CONV - 1 - j:GDN_CONV - j, :]
    qkv = _silu(y)

    @pl.when(n == nchunks - 1)
    def _():
        last = t_valid - (nchunks - 1) * chunk
        cout_ref[0] = xp_scr[pad + last - (GDN_CONV - 1):pad + last, :]

    xp_scr[0:pad, :] = xp_scr[chunk:chunk + pad, :]

    row = lax.broadcasted_iota(jnp.int32, (chunk, chunk), 0)
    col = lax.broadcasted_iota(jnp.int32, (chunk, chunk), 1)
    eye = (row == col).astype(F32)
    a_col = sm_ref[:, SMALL_A:SMALL_A + GDN_HEADS]
    g_col = -jnp.exp(alog_ref[...]) * _softplus(a_col + dtb_ref[...])
    beta_col = _sigmoid(sm_ref[:, SMALL_BETA:SMALL_BETA + GDN_HEADS])
    a_row = smt_ref[SMALL_A:SMALL_A + GDN_HEADS, :]
    g_row = -jnp.exp(alogc_ref[...]) * _softplus(a_row + dtbc_ref[...])
    if masked:
        valid_c = (lax.broadcasted_iota(jnp.int32, (chunk, 1), 0) + n * chunk) < t_valid
        valid_r = (lax.broadcasted_iota(jnp.int32, (1, chunk), 1) + n * chunk) < t_valid
        g_col = jnp.where(valid_c, g_col, 0.0)
        beta_col = jnp.where(valid_c, beta_col, 0.0)
        g_row = jnp.where(valid_r, g_row, 0.0)
    b_col = jnp.dot((row >= col).astype(F32), g_col, precision=HIGHEST, preferred_element_type=F32)
    b_row = jnp.dot(g_row, (row <= col).astype(F32), precision=HIGHEST, preferred_element_type=F32)

    for h in range(GDN_HEADS):
        qh = qkv[:, h * GDN_DK:(h + 1) * GDN_DK]
        kh = qkv[:, GDN_QK + h * GDN_DK:GDN_QK + (h + 1) * GDN_DK]
        v = qkv[:, 2 * GDN_QK + h * GDN_DV:2 * GDN_QK + (h + 1) * GDN_DV]
        q = qh * lax.rsqrt(jnp.sum(qh * qh, axis=-1, keepdims=True) + L2_EPS) * (GDN_DK ** -0.5)
        k = kh * lax.rsqrt(jnp.sum(kh * kh, axis=-1, keepdims=True) + L2_EPS)
        bc = b_col[:, h:h + 1]
        br = b_row[h:h + 1, :]
        beta = beta_col[:, h:h + 1]
        dec = jnp.where(row >= col, jnp.exp(jnp.minimum(bc - br, 0.0)), 0.0)
        kb = k * beta
        kbf = k.astype(BF16)
        kk = lax.dot_general(kb.astype(BF16), kbf, NT_DIMS, preferred_element_type=F32)
        qk = lax.dot_general(q.astype(BF16), kbf, NT_DIMS, preferred_element_type=F32)
        p = jnp.where(row > col, -(kk * dec), 0.0)
        tinv = eye + p
        for _ in range(int(math.log2(chunk)) - 1):
            p = jnp.dot(p, p, precision=HIGHEST, preferred_element_type=F32)
            tinv = tinv + jnp.dot(tinv, p, precision=HIGHEST, preferred_element_type=F32)
        rhs = jnp.concatenate([v * beta, kb * jnp.exp(bc)], axis=-1).astype(BF16)
        uw = jnp.dot(tinv.astype(BF16), rhs, preferred_element_type=F32)
        s = s_scr[h]
        sb = s.astype(BF16)
        v_new = uw[:, :GDN_DV] - jnp.dot(uw[:, GDN_DV:].astype(BF16), sb, preferred_element_type=F32)
        vnb = v_new.astype(BF16)
        o = (jnp.dot((q * jnp.exp(bc)).astype(BF16), sb, preferred_element_type=F32)
             + jnp.dot((qk * dec).astype(BF16), vnb, preferred_element_type=F32))
        bl = bc[chunk - 1:chunk, :]
        s_scr[h] = jnp.exp(bl) * s + lax.dot_general((k * jnp.exp(bl - bc)).astype(BF16), vnb, TN_DIMS,
                                                     preferred_element_type=F32)
        sl = slice(h * GDN_DV, (h + 1) * GDN_DV)
        ob = _rms(o, nw_ref[...]) * _silu(zb_ref[:, sl].astype(F32)) * _sigmoid(gt_ref[:, sl].astype(F32))
        o_ref[:, sl] = ob.astype(o_ref.dtype)

    @pl.when(n == nchunks - 1)
    def _():
        for h in range(GDN_HEADS):
            sout_ref[0, h] = s_scr[h]


def _gdn(proj, small, small_t3, wc, alog, dtb, nw, s0, c0, *, nb, t_pad, chunk, t_valid, out_dtype):
    nchunks = t_pad // chunk
    has_state = s0 is not None
    full = lambda a: pl.BlockSpec(a.shape, lambda b, n: (0,) * a.ndim)
    alog_r, dtb_r = alog.reshape(1, GDN_HEADS), dtb.reshape(1, GDN_HEADS)
    alog_c, dtb_c = alog.reshape(GDN_HEADS, 1), dtb.reshape(GDN_HEADS, 1)
    in_specs = [pl.BlockSpec((chunk, GDN_CONV_DIM), lambda b, n: (b * nchunks + n, 1)),
                pl.BlockSpec((chunk, 1024), lambda b, n: (b * nchunks + n, 6)),
                pl.BlockSpec((chunk, 1024), lambda b, n: (b * nchunks + n, 8)),
                pl.BlockSpec((chunk, LANES), lambda b, n: (b * nchunks + n, 0)),
                pl.BlockSpec((None, SMALL_ROWS, chunk), lambda b, n: (b * nchunks + n, 0, 0)),
                full(wc), full(alog_r), full(dtb_r), full(alog_c), full(dtb_c), full(nw)]
    args = [proj, proj, proj, small, small_t3, wc, alog_r, dtb_r, alog_c, dtb_c, nw]
    state_spec = pl.BlockSpec((1, GDN_HEADS, GDN_DK, GDN_DV), lambda b, n: (b, 0, 0, 0))
    conv_spec = pl.BlockSpec((1, GDN_CONV - 1, GDN_CONV_DIM), lambda b, n: (b, 0, 0))
    if has_state:
        in_specs += [state_spec, conv_spec]
        args += [s0, c0]
    return pl.pallas_call(
        functools.partial(_gdn_kernel, chunk=chunk, nchunks=nchunks, has_state=has_state, t_valid=t_valid),
        out_shape=(jax.ShapeDtypeStruct((nb * t_pad, GDN_VW), out_dtype),
                   jax.ShapeDtypeStruct((nb, GDN_HEADS, GDN_DK, GDN_DV), F32),
                   jax.ShapeDtypeStruct((nb, GDN_CONV - 1, GDN_CONV_DIM), F32)),
        grid=(nb, nchunks),
        in_specs=in_specs,
        out_specs=(pl.BlockSpec((chunk, GDN_VW), lambda b, n: (b * nchunks + n, 0)), state_spec, conv_spec),
        scratch_shapes=[pltpu.VMEM((GDN_HEADS, GDN_DK, GDN_DV), F32),
                        pltpu.VMEM((chunk + 2 * SUBLANES, GDN_CONV_DIM), F32)],
        compiler_params=_params("parallel", "arbitrary"),
        name="gdn",
    )(*args)


def _post_kernel(ma_ref, mb_ref, x_ref, gate_ref, shift_ref, scale_ref, wo_ref, nw_ref, wrt_ref, brt_ref,
                 x1_ref, h2_ref, comb_ref):
    merged = (ma_ref[...].astype(F32) + mb_ref[...].astype(F32)).astype(BF16)
    y = jnp.dot(merged, wo_ref[...], preferred_element_type=F32)
    x1 = x_ref[...] + gate_ref[...] * y
    x1_ref[...] = x1
    h2 = (_rms(x1, nw_ref[...]) * (1.0 + scale_ref[...]) + shift_ref[...]).astype(BF16)
    h2_ref[...] = h2
    lt = lax.dot_general(wrt_ref[...], h2, NT_DIMS, preferred_element_type=F32) + brt_ref[...]
    tm = lt.shape[1]
    gl = lt[0:N_GROUPS]
    gidx = lax.broadcasted_iota(jnp.int32, (N_GROUPS, tm), 0)
    gmax = jnp.max(gl, axis=0, keepdims=True)
    g_w = 1.0 / jnp.sum(jnp.exp(gl - gmax), axis=0, keepdims=True)
    g_sel = jnp.min(jnp.where(gl == gmax, gidx, N_GROUPS), axis=0, keepdims=True)
    el = jnp.zeros((EXPERTS_PER_GROUP, tm), F32)
    for g in range(N_GROUPS):
        r = ROUTER_GROUP_STRIDE * (1 + g)
        el = el + jnp.where(g_sel == g, lt[r:r + EXPERTS_PER_GROUP], 0.0)
    eidx = lax.broadcasted_iota(jnp.int32, (EXPERTS_PER_GROUP, tm), 0)
    m1 = jnp.max(el, axis=0, keepdims=True)
    i1 = jnp.min(jnp.where(el == m1, eidx, EXPERTS_PER_GROUP), axis=0, keepdims=True)
    el2 = jnp.where(eidx == i1, -jnp.inf, el)
    m2 = jnp.max(el2, axis=0, keepdims=True)
    i2 = jnp.min(jnp.where(el2 == m2, eidx, EXPERTS_PER_GROUP), axis=0, keepdims=True)
    r21 = jnp.exp(m2 - m1)
    w1 = 1.0 / (1.0 + r21)
    w2 = r21 / (1.0 + r21)
    xrow = lax.broadcasted_iota(jnp.int32, (N_EXPERTS, tm), 0)
    xg, xe = xrow // EXPERTS_PER_GROUP, xrow % EXPERTS_PER_GROUP
    comb_t = (jnp.where(xg == g_sel, g_w, 0.0)
              * (jnp.where(xe == i1, w1, 0.0) + jnp.where(xe == i2, w2, 0.0)))
    ident = (lax.broadcasted_iota(jnp.int32, (N_EXPERTS, LANES), 0)
             == lax.broadcasted_iota(jnp.int32, (N_EXPERTS, LANES), 1)).astype(F32)
    comb_ref[...] = lax.dot_general(comb_t, ident, TN_DIMS, precision=HIGHEST, preferred_element_type=F32)


def _post(ma, mb, x2d, gate, shift, scale, mod_specs, wo, nw, wrt, brt, tm):
    m, d = x2d.shape
    row = lambda: pl.BlockSpec((tm, d), lambda i: (i, 0))
    full = lambda a: pl.BlockSpec(a.shape, lambda i: (0,) * a.ndim)
    return pl.pallas_call(
        _post_kernel,
        out_shape=(jax.ShapeDtypeStruct((m, d), F32),
                   jax.ShapeDtypeStruct((m, d), BF16),
                   jax.ShapeDtypeStruct((m, LANES), F32)),
        grid=(m // tm,),
        in_specs=[row(), row(), row(), mod_specs[0], mod_specs[1], mod_specs[2],
                  full(wo), full(nw), full(wrt), full(brt)],
        out_specs=(row(), row(), pl.BlockSpec((tm, LANES), lambda i: (i, 0))),
        compiler_params=_params("parallel"),
        name="post_mixer",
    )(ma, mb, x2d, gate, shift, scale, wo, nw, wrt, brt)


def _moe_kernel(h_ref, comb_ref, x1_ref, gate_ref, wg_ref, wu_ref, wd_ref, nw_ref, o_ref, acc_scr):
    e = pl.program_id(1)

    @pl.when(e == 0)
    def _():
        acc_scr[...] = jnp.zeros_like(acc_scr)

    h = h_ref[...]
    a = jnp.dot(h, wg_ref[0], preferred_element_type=F32)
    u = jnp.dot(h, wu_ref[0], preferred_element_type=F32)
    lane = lax.broadcasted_iota(jnp.int32, comb_ref.shape, 1)
    cw = jnp.sum(jnp.where(lane == e, comb_ref[...], 0.0), axis=-1, keepdims=True)
    hid = _silu(a) * u * cw
    acc_scr[...] += jnp.dot(hid.astype(BF16), wd_ref[0], preferred_element_type=F32)

    @pl.when(e == N_EXPERTS - 1)
    def _():
        x2 = x1_ref[...] + gate_ref[...] * acc_scr[...]
        o_ref[...] = _rms(x2, nw_ref[...])


def _moe(h2, comb, x1, gate, gate_spec, wg, wu, wd, nw, tm):
    m, d = x1.shape
    f = wg.shape[-1]
    return pl.pallas_call(
        _moe_kernel,
        out_shape=jax.ShapeDtypeStruct((m, d), F32),
        grid=(m // tm, N_EXPERTS),
        in_specs=[pl.BlockSpec((tm, d), lambda i, e: (i, 0)),
                  pl.BlockSpec((tm, LANES), lambda i, e: (i, 0)),
                  pl.BlockSpec((tm, d), lambda i, e: (i, 0)),
                  gate_spec,
                  pl.BlockSpec((1, d, f), lambda i, e: (e, 0, 0)),
                  pl.BlockSpec((1, d, f), lambda i, e: (e, 0, 0)),
                  pl.BlockSpec((1, f, d), lambda i, e: (e, 0, 0)),
                  pl.BlockSpec((1, d), lambda i, e: (0, 0))],
        out_specs=pl.BlockSpec((tm, d), lambda i, e: (i, 0)),
        scratch_shapes=[pltpu.VMEM((tm, d), F32)],
        compiler_params=_params("parallel", "arbitrary"),
        name="moe",
    )(h2, comb, x1, gate, wg, wu, wd, nw)


def _mod_spec_batch(idx, tm, t, ngrid):
    per = t // tm
    if ngrid == 2:
        return pl.BlockSpec((None, None, 1, D_MODEL), lambda i, j: (i // per, idx, 0, 0))
    return pl.BlockSpec((None, None, 1, D_MODEL), lambda i: (i // per, idx, 0, 0))


def _mod_spec_rows(tm, ngrid):
    if ngrid == 2:
        return pl.BlockSpec((tm, D_MODEL), lambda i, j: (i, 0))
    return pl.BlockSpec((tm, D_MODEL), lambda i: (i, 0))


def _trunk(x, mod, s_gla, s_gdn, s_conv, w, *, chunk, sub, t_valid, tm, tm_moe, act_dtype):
    nb, t_pad, d = x.shape
    m = nb * t_pad
    x2d = x.reshape(m, d)
    if t_pad % tm == 0:
        mod4 = mod.reshape(nb, N_MOD, 1, d)
        mods = [mod4] * N_MOD
        spec = lambda idx, ngrid: _mod_spec_batch(idx, tm, t_pad, ngrid)
    else:
        mods = [jnp.repeat(mod[:, i], t_pad, axis=0) for i in range(N_MOD)]
        spec = lambda idx, ngrid: _mod_spec_rows(tm, ngrid)

    proj, small, small_t = _inproj(x2d, mods[0], mods[1], (spec(0, 2), spec(1, 2)), w["norm1"],
                                   w["w_main"], w["w_small"], w["w_small_t"], tm, act_dtype)
    small_t3 = small_t.reshape(SMALL_ROWS, m // chunk, chunk).transpose(1, 0, 2)
    ma, new_gla = _gla(proj, small, w["wa2"], w["ba"], w["gla_norm"], s_gla,
                       nb=nb, t_pad=t_pad, chunk=chunk, sub=sub, t_valid=t_valid, out_dtype=act_dtype)
    mb, new_gdn, new_conv = _gdn(proj, small, small_t3, w["w_conv"], w["a_log"], w["dt_bias"], w["gdn_norm"],
                                 s_gdn, s_conv, nb=nb, t_pad=t_pad, chunk=chunk, t_valid=t_valid,
                                 out_dtype=act_dtype)
    x1, h2, comb = _post(ma, mb, x2d, mods[2], mods[3], mods[4], (spec(2, 1), spec(3, 1), spec(4, 1)),
                         w["w_out"], w["norm2"], w["w_router_t"], w["b_router_t"], tm)
    gate2_spec = _mod_spec_batch(5, tm_moe, t_pad, 2) if t_pad % tm == 0 else _mod_spec_rows(tm_moe, 2)
    y = _moe(h2, comb, x1, mods[5], gate2_spec, w["w_gate"], w["w_up"], w["w_down"], w["final_norm"], tm_moe)
    return y.reshape(nb, t_pad, d), new_gla, new_gdn, new_conv


def _prep_weights(w_in, w_gla_a2, b_gla_a, gla_norm_w, w_conv, gdn_A_log, gdn_dt_bias, gdn_norm_w, w_out,
                  norm1_w, norm2_w, w_group_router, b_group_router, w_expert_router, b_expert_router,
                  w_exp_gate, w_exp_up, w_exp_down, final_norm_w):
    d = D_MODEL
    o = 0
    cols = {}
    for name, width in (("gla", 2 * GLA_QK + 2 * GLA_VW), ("ra", GLA_GATE_RANK), ("qkv", GDN_CONV_DIM),
                        ("zb", GDN_VW), ("beta", GDN_HEADS), ("a", GDN_HEADS), ("gates", 2 * D_MODEL)):
        cols[name] = w_in[:, o:o + width]
        o += width
    w_main = jnp.concatenate([cols["gla"], cols["qkv"], cols["zb"], cols["gates"]], axis=1).astype(BF16)
    small = jnp.concatenate([cols["ra"], cols["beta"], cols["a"]], axis=1)
    w_small = jnp.pad(small, ((0, 0), (0, LANES - small.shape[1]))).astype(BF16)
    w_small_t = small.T.astype(BF16)
    wr_t = jnp.zeros((ROUTER_ROWS, d), F32)
    wr_t = wr_t.at[0:N_GROUPS].set(w_group_router.T)
    br_t = jnp.zeros((ROUTER_ROWS, 1), F32)
    br_t = br_t.at[0:N_GROUPS, 0].set(b_group_router)
    for g in range(N_GROUPS):
        r = ROUTER_GROUP_STRIDE * (1 + g)
        es = slice(g * EXPERTS_PER_GROUP, (g + 1) * EXPERTS_PER_GROUP)
        wr_t = wr_t.at[r:r + EXPERTS_PER_GROUP].set(w_expert_router[:, es].T)
        br_t = br_t.at[r:r + EXPERTS_PER_GROUP, 0].set(b_expert_router[es])
    return dict(
        w_main=w_main, w_small=w_small, w_small_t=w_small_t,
        norm1=norm1_w.reshape(1, d), norm2=norm2_w.reshape(1, d), final_norm=final_norm_w.reshape(1, d),
        wa2=w_gla_a2, ba=b_gla_a.reshape(1, GLA_QK), gla_norm=gla_norm_w.reshape(1, GLA_DV),
        w_conv=w_conv, a_log=gdn_A_log, dt_bias=gdn_dt_bias, gdn_norm=gdn_norm_w.reshape(1, GDN_DV),
        w_out=w_out.astype(BF16), w_router_t=wr_t.astype(BF16), b_router_t=br_t,
        w_gate=w_exp_gate.reshape(N_EXPERTS, d, D_EXPERT).astype(BF16),
        w_up=w_exp_up.reshape(N_EXPERTS, d, D_EXPERT).astype(BF16),
        w_down=w_exp_down.reshape(N_EXPERTS, D_EXPERT, d).astype(BF16),
    )


def kernel(x_prompt, x_sample, c_prompt, c_sample, state_gla, state_gdn, state_conv, w_ada, b_ada, norm1_w, w_in, w_gla_a2, b_gla_a, gla_norm_w, w_conv, gdn_A_log, gdn_dt_bias, gdn_norm_w, w_out, norm2_w, w_group_router, b_group_router, w_expert_router, b_expert_router, w_exp_gate, w_exp_up, w_exp_down, final_norm_w):
    assert w_ada.shape[0] == 1, "single layer"
    bp, tp, d = x_prompt.shape
    bs, ts, _ = x_sample.shape
    w = _prep_weights(w_in[0], w_gla_a2[0], b_gla_a[0], gla_norm_w[0], w_conv[0], gdn_A_log[0], gdn_dt_bias[0],
                      gdn_norm_w[0], w_out[0], norm1_w[0], norm2_w[0], w_group_router[0], b_group_router[0],
                      w_expert_router[0], b_expert_router[0], w_exp_gate[0], w_exp_up[0], w_exp_down[0],
                      final_norm_w)
    mod = _ada_mod(jnp.concatenate([c_prompt, c_sample], axis=0), w_ada[0], b_ada[0]).reshape(bp + bs, N_MOD, d)

    y_p, gla_p, gdn_p, conv_p = _trunk(x_prompt, mod[:bp], None, None, None, w,
                                       chunk=64, sub=GLA_SUBCHUNK, t_valid=tp, tm=min(1024, tp),
                                       tm_moe=min(512, tp), act_dtype=BF16)
    ts_pad = SUBLANES
    xs = jnp.pad(x_sample, ((0, 0), (0, ts_pad - ts), (0, 0)))
    y_s, gla_s, gdn_s, conv_s = _trunk(xs, mod[bp:], state_gla[0], state_gdn[0], state_conv[0], w,
                                       chunk=ts_pad, sub=ts_pad, t_valid=ts, tm=min(512, bs * ts_pad),
                                       tm_moe=min(512, bs * ts_pad), act_dtype=F32)
    return (y_p, y_s[:, :ts], gla_p[None], gdn_p[None], conv_p[None], gla_s[None], gdn_s[None], conv_s[None])
```

```python
import functools
import math

import jax
import jax.numpy as jnp
from jax import lax
from jax.experimental import pallas as pl
from jax.experimental.pallas import tpu as pltpu

F32 = jnp.float32
BF16 = jnp.bfloat16
HIGHEST = lax.Precision.HIGHEST

D_MODEL = 1024
GLA_HEADS = 4
GLA_DK = 128
GLA_DV = 256
GLA_QK = GLA_HEADS * GLA_DK
GLA_VW = GLA_HEADS * GLA_DV
GLA_GATE_RANK = 16
GLA_TAU = 16.0
GLA_SUBCHUNK = 16
GDN_HEADS = 8
GDN_DK = 128
GDN_DV = 128
GDN_QK = GDN_HEADS * GDN_DK
GDN_VW = GDN_HEADS * GDN_DV
GDN_CONV = 4
GDN_CONV_DIM = 2 * GDN_QK + GDN_VW
GDN_INV_BLOCK = 16
N_GROUPS = 4
EXPERTS_PER_GROUP = 4
N_EXPERTS = N_GROUPS * EXPERTS_PER_GROUP
D_EXPERT = D_MODEL // 2
N_MOD = 6
NORM_EPS = 1e-6
L2_EPS = 1e-6

LANES = 128
SUBLANES = 8
VMEM_LIMIT = 48 * 1024 * 1024

PROJ_MAIN = 2 * GLA_QK + 2 * GLA_VW + GDN_CONV_DIM + GDN_VW + 2 * D_MODEL
SMALL_RA, SMALL_BETA, SMALL_A = 0, GLA_GATE_RANK, GLA_GATE_RANK + GDN_HEADS
SMALL_ROWS = 32
ROUTER_GROUP_STRIDE = 8
ROUTER_ROWS = ROUTER_GROUP_STRIDE * (1 + N_GROUPS)

NT_DIMS = (((1,), (1,)), ((), ()))
TN_DIMS = (((0,), (0,)), ((), ()))


def _sigmoid(x):
    return 1.0 / (1.0 + jnp.exp(-x))


def _silu(x):
    return x * _sigmoid(x)


def _softplus(x):
    return jnp.maximum(x, 0.0) + jnp.log1p(jnp.exp(-jnp.abs(x)))


def _rms(x, w):
    return x * lax.rsqrt(jnp.mean(x * x, axis=-1, keepdims=True) + NORM_EPS) * w


def _params(*sem):
    return pltpu.CompilerParams(dimension_semantics=sem, vmem_limit_bytes=VMEM_LIMIT)


def _ada_kernel(c_ref, w_ref, b_ref, o_ref):
    cs = _silu(c_ref[...])
    o_ref[...] = jnp.dot(cs.astype(BF16), w_ref[...].astype(BF16), preferred_element_type=F32) + b_ref[...]


def _ada_mod(c_all, w_ada, b_ada):
    rows, d = c_all.shape
    n = w_ada.shape[1]
    tn = 1024
    return pl.pallas_call(
        _ada_kernel,
        out_shape=jax.ShapeDtypeStruct((rows, n), F32),
        grid=(n // tn,),
        in_specs=[pl.BlockSpec((rows, d), lambda j: (0, 0)),
                  pl.BlockSpec((d, tn), lambda j: (0, j)),
                  pl.BlockSpec((1, tn), lambda j: (0, j))],
        out_specs=pl.BlockSpec((rows, tn), lambda j: (0, j)),
        compiler_params=_params("arbitrary"),
        name="ada_mod",
    )(c_all, w_ada, b_ada.reshape(1, n))


def _inproj_kernel(x_ref, shift_ref, scale_ref, nw_ref, w_ref, ws_ref, wst_ref,
                   o_ref, os_ref, ost_ref, h_scr):
    @pl.when(pl.program_id(1) == 0)
    def _():
        h = _rms(x_ref[...], nw_ref[...]) * (1.0 + scale_ref[...]) + shift_ref[...]
        hb = h.astype(BF16)
        h_scr[...] = hb
        os_ref[...] = jnp.dot(hb, ws_ref[...], preferred_element_type=F32)
        ost_ref[...] = lax.dot_general(wst_ref[...], hb, NT_DIMS, preferred_element_type=F32)

    o_ref[...] = jnp.dot(h_scr[...], w_ref[...], preferred_element_type=F32).astype(o_ref.dtype)


def _inproj(x2d, shift, scale, mod_specs, nw, w_main, w_small, w_small_t, tm, out_dtype):
    m, d = x2d.shape
    n = w_main.shape[1]
    tn = 1024
    return pl.pallas_call(
        _inproj_kernel,
        out_shape=(jax.ShapeDtypeStruct((m, n), out_dtype),
                   jax.ShapeDtypeStruct((m, LANES), F32),
                   jax.ShapeDtypeStruct((SMALL_ROWS, m), F32)),
        grid=(m // tm, n // tn),
        in_specs=[pl.BlockSpec((tm, d), lambda i, j: (i, 0)),
                  mod_specs[0], mod_specs[1],
                  pl.BlockSpec((1, d), lambda i, j: (0, 0)),
                  pl.BlockSpec((d, tn), lambda i, j: (0, j)),
                  pl.BlockSpec((d, LANES), lambda i, j: (0, 0)),
                  pl.BlockSpec((SMALL_ROWS, d), lambda i, j: (0, 0))],
        out_specs=(pl.BlockSpec((tm, tn), lambda i, j: (i, j)),
                   pl.BlockSpec((tm, LANES), lambda i, j: (i, 0)),
                   pl.BlockSpec((SMALL_ROWS, tm), lambda i, j: (0, i))),
        scratch_shapes=[pltpu.VMEM((tm, d), BF16)],
        compiler_params=_params("parallel", "arbitrary"),
        name="inproj",
    )(x2d, shift, scale, nw, w_main, w_small, w_small_t)


def _gla_kernel(*refs, chunk, sub, nchunks, has_state, t_valid):
    if has_state:
        (qk_ref, v_ref, ga_ref, gt_ref, sm_ref, wa2_ref, ba_ref, nw_ref, s0_ref,
         o_ref, sout_ref, st_scr) = refs
    else:
        (qk_ref, v_ref, ga_ref, gt_ref, sm_ref, wa2_ref, ba_ref, nw_ref,
         o_ref, sout_ref, st_scr) = refs
    n = pl.program_id(1)
    masked = t_valid < chunk * nchunks

    @pl.when(n == 0)
    def _():
        for h in range(GLA_HEADS):
            if has_state:
                st_scr[h] = s0_ref[0, h].T
            else:
                st_scr[h] = jnp.zeros((GLA_DV, GLA_DK), F32)

    row = lax.broadcasted_iota(jnp.int32, (chunk, chunk), 0)
    col = lax.broadcasted_iota(jnp.int32, (chunk, chunk), 1)
    ra = sm_ref[:, SMALL_RA:SMALL_RA + GLA_GATE_RANK]
    x = jnp.dot(ra.astype(BF16), wa2_ref[...].astype(BF16), preferred_element_type=F32) + ba_ref[...]
    g = (jnp.minimum(x, 0.0) - jnp.log1p(jnp.exp(-jnp.abs(x)))) * (1.0 / GLA_TAU)
    if masked:
        valid = (lax.broadcasted_iota(jnp.int32, (chunk, 1), 0) + n * chunk) < t_valid
        g = jnp.where(valid, g, 0.0)
    tri = (row >= col).astype(F32)
    b_all = jnp.dot(tri, g, precision=HIGHEST, preferred_element_type=F32)

    heads = range(GLA_HEADS)
    rowi = lax.broadcasted_iota(jnp.int32, (chunk, 1), 0)
    q, k, v, b = [], [], [], []
    for h in heads:
        q.append(qk_ref[:, h * GLA_DK:(h + 1) * GLA_DK].astype(F32) * (GLA_DK ** -0.5))
        kh = qk_ref[:, GLA_QK + h * GLA_DK:GLA_QK + (h + 1) * GLA_DK].astype(F32)
        k.append(jnp.where(valid, kh, 0.0) if masked else kh)
        v.append(v_ref[:, h * GLA_DV:(h + 1) * GLA_DV].astype(BF16))
        b.append(b_all[:, h * GLA_DK:(h + 1) * GLA_DK])
    st = [st_scr[h] for h in heads]
    o = [lax.dot_general((q[h] * jnp.exp(b[h])).astype(BF16), st[h].astype(BF16), NT_DIMS,
                         preferred_element_type=F32) for h in heads]
    blocks = [[] for _ in heads]
    for i in range(chunk // sub):
        r0, r1 = i * sub, (i + 1) * sub
        for h in heads:
            bref = b[h][r0 - 1:r0] if i > 0 else jnp.zeros((1, GLA_DK), F32)
            qt = (q[h][r0:r1] * jnp.exp(b[h][r0:r1] - bref)).astype(BF16)
            expo = bref - b[h]
            if r1 < chunk:
                expo = jnp.where(rowi < r1, expo, 0.0)
            kt = (k[h] * jnp.exp(expo)).astype(BF16)
            blocks[h].append(lax.dot_general(qt, kt, NT_DIMS, preferred_element_type=F32))
    for h in heads:
        a = blocks[h][0] if len(blocks[h]) == 1 else jnp.concatenate(blocks[h], axis=0)
        a = jnp.where(col <= row, a, 0.0)
        o[h] = o[h] + jnp.dot(a.astype(BF16), v[h], preferred_element_type=F32)
    for h in heads:
        bl = b[h][chunk - 1:chunk]
        kt = (k[h] * jnp.exp(bl - b[h])).astype(BF16)
        st_scr[h] = st[h] * jnp.exp(bl) + lax.dot_general(v[h], kt, TN_DIMS, preferred_element_type=F32)
    for h in heads:
        sl = slice(h * GLA_DV, (h + 1) * GLA_DV)
        oa = _rms(o[h], nw_ref[...]) * _silu(ga_ref[:, sl].astype(F32)) * _sigmoid(gt_ref[:, sl].astype(F32))
        o_ref[:, sl] = oa.astype(o_ref.dtype)

    @pl.when(n == nchunks - 1)
    def _():
        for h in range(GLA_HEADS):
            sout_ref[0, h] = st_scr[h].T


def _gla(proj, small, wa2, ba, nw, s0, *, nb, t_pad, chunk, sub, t_valid, out_dtype):
    nchunks = t_pad // chunk
    has_state = s0 is not None
    blk = lambda c: pl.BlockSpec((chunk, 1024), lambda b, n, c=c: (b * nchunks + n, c))
    in_specs = [blk(0), blk(1), blk(2), blk(7),
                pl.BlockSpec((chunk, LANES), lambda b, n: (b * nchunks + n, 0)),
                pl.BlockSpec(wa2.shape, lambda b, n: (0, 0)),
                pl.BlockSpec(ba.shape, lambda b, n: (0, 0)),
                pl.BlockSpec(nw.shape, lambda b, n: (0, 0))]
    args = [proj, proj, proj, proj, small, wa2, ba, nw]
    state_spec = pl.BlockSpec((1, GLA_HEADS, GLA_DK, GLA_DV), lambda b, n: (b, 0, 0, 0))
    if has_state:
        in_specs.append(state_spec)
        args.append(s0)
    return pl.pallas_call(
        functools.partial(_gla_kernel, chunk=chunk, sub=sub, nchunks=nchunks,
                          has_state=has_state, t_valid=t_valid),
        out_shape=(jax.ShapeDtypeStruct((nb * t_pad, GLA_VW), out_dtype),
                   jax.ShapeDtypeStruct((nb, GLA_HEADS, GLA_DK, GLA_DV), F32)),
        grid=(nb, nchunks),
        in_specs=in_specs,
        out_specs=(pl.BlockSpec((chunk, GLA_VW), lambda b, n: (b * nchunks + n, 0)), state_spec),
        scratch_shapes=[pltpu.VMEM((GLA_HEADS, GLA_DV, GLA_DK), F32)],
        compiler_params=_params("parallel", "arbitrary"),
        name="gla",
    )(*args)


def _gdn_kernel(*refs, chunk, nchunks, has_state, t_valid):
    if has_state:
        (qkv_ref, zb_ref, gt_ref, sm_ref, smt_ref, wc_ref, alog_ref, dtb_ref, alogc_ref, dtbc_ref, nw_ref,
         s0_ref, c0_ref, o_ref, sout_ref, cout_ref, s_scr, xp_scr) = refs
    else:
        (qkv_ref, zb_ref, gt_ref, sm_ref, smt_ref, wc_ref, alog_ref, dtb_ref, alogc_ref, dtbc_ref, nw_ref,
         o_ref, sout_ref, cout_ref, s_scr, xp_scr) = refs
    n = pl.program_id(1)
    masked = t_valid < chunk * nchunks
    pad = SUBLANES

    @pl.when(n == 0)
    def _():
        if has_state:
            for h in range(GDN_HEADS):
                s_scr[h] = s0_ref[0, h]
            xp_scr[0:pad, :] = jnp.zeros((pad, GDN_CONV_DIM), F32)
            xp_scr[pad - (GDN_CONV - 1):pad, :] = c0_ref[0]
        else:
            s_scr[...] = jnp.zeros_like(s_scr)
            xp_scr[0:pad, :] = jnp.zeros((pad, GDN_CONV_DIM), F32)

    xp_scr[pad:pad + chunk, :] = qkv_ref[...].astype(F32)
    y = xp_scr[pad:pad + chunk, :] * wc_ref[GDN_CONV - 1:GDN_CONV, :]
    for j in range(1, GDN_CONV):
        y = y + xp_scr[pad - j:pad - j + chunk, :] * wc_ref[GDN_CONV - 1 - j:GDN_CONV - j, :]
    qkv = _silu(y)

    @pl.when(n == nchunks - 1)
    def _():
        last = t_valid - (nchunks - 1) * chunk
        cout_ref[0] = xp_scr[pad + last - (GDN_CONV - 1):pad + last, :]

    xp_scr[0:pad, :] = xp_scr[chunk:chunk + pad, :]

    row = lax.broadcasted_iota(jnp.int32, (chunk, chunk), 0)
    col = lax.broadcasted_iota(jnp.int32, (chunk, chunk), 1)
    a_col = sm_ref[:, SMALL_A:SMALL_A + GDN_HEADS]
    g_col = -jnp.exp(alog_ref[...]) * _softplus(a_col + dtb_ref[...])
    beta_col = _sigmoid(sm_ref[:, SMALL_BETA:SMALL_BETA + GDN_HEADS])
    a_row = smt_ref[SMALL_A:SMALL_A + GDN_HEADS, :]
    g_row = -jnp.exp(alogc_ref[...]) * _softplus(a_row + dtbc_ref[...])
    if masked:
        valid_c = (lax.broadcasted_iota(jnp.int32, (chunk, 1), 0) + n * chunk) < t_valid
        valid_r = (lax.broadcasted_iota(jnp.int32, (1, chunk), 1) + n * chunk) < t_valid
        g_col = jnp.where(valid_c, g_col, 0.0)
        beta_col = jnp.where(valid_c, beta_col, 0.0)
        g_row = jnp.where(valid_r, g_row, 0.0)
    b_col = jnp.dot((row >= col).astype(F32), g_col, precision=HIGHEST, preferred_element_type=F32)
    b_row = jnp.dot(g_row, (row <= col).astype(F32), precision=HIGHEST, preferred_element_type=F32)

    heads = range(GDN_HEADS)
    bdot = lambda a, b: jnp.dot(a.astype(BF16), b.astype(BF16), preferred_element_type=F32)
    q, k, v, kb, dec, bc, beta = [], [], [], [], [], [], []
    for h in heads:
        qh = qkv[:, h * GDN_DK:(h + 1) * GDN_DK]
        kh = qkv[:, GDN_QK + h * GDN_DK:GDN_QK + (h + 1) * GDN_DK]
        v.append(qkv[:, 2 * GDN_QK + h * GDN_DV:2 * GDN_QK + (h + 1) * GDN_DV])
        q.append(qh * lax.rsqrt(jnp.sum(qh * qh, axis=-1, keepdims=True) + L2_EPS) * (GDN_DK ** -0.5))
        k.append(kh * lax.rsqrt(jnp.sum(kh * kh, axis=-1, keepdims=True) + L2_EPS))
        bc.append(b_col[:, h:h + 1])
        beta.append(beta_col[:, h:h + 1])
        dec.append(jnp.where(row >= col, jnp.exp(jnp.minimum(bc[h] - b_row[h:h + 1, :], 0.0)), 0.0))
        kb.append(k[h] * beta[h])
    kq = [lax.dot_general(jnp.concatenate([kb[h], q[h]], axis=0).astype(BF16), k[h].astype(BF16), NT_DIMS,
                          preferred_element_type=F32) for h in heads]
    blk = min(GDN_INV_BLOCK, chunk)
    nblk = chunk // blk
    same_blk = (row // blk) == (col // blk)
    lmat = [jnp.where(row > col, kq[h][:chunk] * dec[h], 0.0) for h in heads]
    p = [jnp.where(same_blk, -lmat[h], 0.0) for h in heads]
    r = p
    for _ in range(int(math.log2(blk)) - 1):
        p = [bdot(p[h], p[h]) for h in heads]
        r = [r[h] + p[h] + bdot(r[h], p[h]) for h in heads]
    if nblk > 1:
        lo = [jnp.where(same_blk, 0.0, lmat[h]) for h in heads]
        p = [-(lo[h] + bdot(r[h], lo[h])) for h in heads]
        qm = p
        for _ in range(int(math.log2(nblk)) - 1):
            p = [bdot(p[h], p[h]) for h in heads]
            qm = [qm[h] + p[h] + bdot(qm[h], p[h]) for h in heads]
        r = [r[h] + qm[h] + bdot(qm[h], r[h]) for h in heads]
    rhs = [jnp.concatenate([v[h] * beta[h], kb[h] * jnp.exp(bc[h])], axis=-1) for h in heads]
    uw = [rhs[h] + bdot(r[h], rhs[h]) for h in heads]
    s = [s_scr[h] for h in heads]
    ws = [bdot(jnp.concatenate([uw[h][:, GDN_DV:], q[h] * jnp.exp(bc[h])], axis=0), s[h]) for h in heads]
    v_new = [uw[h][:, :GDN_DV] - ws[h][:chunk] for h in heads]
    o = [ws[h][chunk:] + bdot(kq[h][chunk:] * dec[h], v_new[h]) for h in heads]
    for h in heads:
        bl = bc[h][chunk - 1:chunk, :]
        s_scr[h] = jnp.exp(bl) * s[h] + lax.dot_general((k[h] * jnp.exp(bl - bc[h])).astype(BF16),
                                                        v_new[h].astype(BF16), TN_DIMS, preferred_element_type=F32)
    for h in heads:
        sl = slice(h * GDN_DV, (h + 1) * GDN_DV)
        ob = _rms(o[h], nw_ref[...]) * _silu(zb_ref[:, sl].astype(F32)) * _sigmoid(gt_ref[:, sl].astype(F32))
        o_ref[:, sl] = ob.astype(o_ref.dtype)

    @pl.when(n == nchunks - 1)
    def _():
        for h in range(GDN_HEADS):
            sout_ref[0, h] = s_scr[h]


def _gdn(proj, small, small_t3, wc, alog, dtb, nw, s0, c0, *, nb, t_pad, chunk, t_valid, out_dtype):
    nchunks = t_pad // chunk
    has_state = s0 is not None
    full = lambda a: pl.BlockSpec(a.shape, lambda b, n: (0,) * a.ndim)
    alog_r, dtb_r = alog.reshape(1, GDN_HEADS), dtb.reshape(1, GDN_HEADS)
    alog_c, dtb_c = alog.reshape(GDN_HEADS, 1), dtb.reshape(GDN_HEADS, 1)
    in_specs = [pl.BlockSpec((chunk, GDN_CONV_DIM), lambda b, n: (b * nchunks + n, 1)),
                pl.BlockSpec((chunk, 1024), lambda b, n: (b * nchunks + n, 6)),
                pl.BlockSpec((chunk, 1024), lambda b, n: (b * nchunks + n, 8)),
                pl.BlockSpec((chunk, LANES), lambda b, n: (b * nchunks + n, 0)),
                pl.BlockSpec((None, SMALL_ROWS, chunk), lambda b, n: (b * nchunks + n, 0, 0)),
                full(wc), full(alog_r), full(dtb_r), full(alog_c), full(dtb_c), full(nw)]
    args = [proj, proj, proj, small, small_t3, wc, alog_r, dtb_r, alog_c, dtb_c, nw]
    state_spec = pl.BlockSpec((1, GDN_HEADS, GDN_DK, GDN_DV), lambda b, n: (b, 0, 0, 0))
    conv_spec = pl.BlockSpec((1, GDN_CONV - 1, GDN_CONV_DIM), lambda b, n: (b, 0, 0))
    if has_state:
        in_specs += [state_spec, conv_spec]
        args += [s0, c0]
    return pl.pallas_call(
        functools.partial(_gdn_kernel, chunk=chunk, nchunks=nchunks, has_state=has_state, t_valid=t_valid),
        out_shape=(jax.ShapeDtypeStruct((nb * t_pad, GDN_VW), out_dtype),
                   jax.ShapeDtypeStruct((nb, GDN_HEADS, GDN_DK, GDN_DV), F32),
                   jax.ShapeDtypeStruct((nb, GDN_CONV - 1, GDN_CONV_DIM), F32)),
        grid=(nb, nchunks),
        in_specs=in_specs,
        out_specs=(pl.BlockSpec((chunk, GDN_VW), lambda b, n: (b * nchunks + n, 0)), state_spec, conv_spec),
        scratch_shapes=[pltpu.VMEM((GDN_HEADS, GDN_DK, GDN_DV), F32),
                        pltpu.VMEM((chunk + 2 * SUBLANES, GDN_CONV_DIM), F32)],
        compiler_params=_params("parallel", "arbitrary"),
        name="gdn",
    )(*args)


def _post_kernel(ma_ref, mb_ref, x_ref, gate_ref, shift_ref, scale_ref, wo_ref, nw_ref, wrt_ref, brt_ref,
                 x1_ref, h2_ref, comb_ref):
    merged = (ma_ref[...].astype(F32) + mb_ref[...].astype(F32)).astype(BF16)
    y = jnp.dot(merged, wo_ref[...], preferred_element_type=F32)
    x1 = x_ref[...] + gate_ref[...] * y
    x1_ref[...] = x1
    h2 = (_rms(x1, nw_ref[...]) * (1.0 + scale_ref[...]) + shift_ref[...]).astype(BF16)
    h2_ref[...] = h2
    lt = lax.dot_general(wrt_ref[...], h2, NT_DIMS, preferred_element_type=F32) + brt_ref[...]
    tm = lt.shape[1]
    gl = lt[0:N_GROUPS]
    gidx = lax.broadcasted_iota(jnp.int32, (N_GROUPS, tm), 0)
    gmax = jnp.max(gl, axis=0, keepdims=True)
    g_w = 1.0 / jnp.sum(jnp.exp(gl - gmax), axis=0, keepdims=True)
    g_sel = jnp.min(jnp.where(gl == gmax, gidx, N_GROUPS), axis=0, keepdims=True)
    el = jnp.zeros((EXPERTS_PER_GROUP, tm), F32)
    for g in range(N_GROUPS):
        r = ROUTER_GROUP_STRIDE * (1 + g)
        el = el + jnp.where(g_sel == g, lt[r:r + EXPERTS_PER_GROUP], 0.0)
    eidx = lax.broadcasted_iota(jnp.int32, (EXPERTS_PER_GROUP, tm), 0)
    m1 = jnp.max(el, axis=0, keepdims=True)
    i1 = jnp.min(jnp.where(el == m1, eidx, EXPERTS_PER_GROUP), axis=0, keepdims=True)
    el2 = jnp.where(eidx == i1, -jnp.inf, el)
    m2 = jnp.max(el2, axis=0, keepdims=True)
    i2 = jnp.min(jnp.where(el2 == m2, eidx, EXPERTS_PER_GROUP), axis=0, keepdims=True)
    r21 = jnp.exp(m2 - m1)
    w1 = 1.0 / (1.0 + r21)
    w2 = r21 / (1.0 + r21)
    xrow = lax.broadcasted_iota(jnp.int32, (N_EXPERTS, tm), 0)
    xg, xe = xrow // EXPERTS_PER_GROUP, xrow % EXPERTS_PER_GROUP
    comb_t = (jnp.where(xg == g_sel, g_w, 0.0)
              * (jnp.where(xe == i1, w1, 0.0) + jnp.where(xe == i2, w2, 0.0)))
    ident = (lax.broadcasted_iota(jnp.int32, (N_EXPERTS, LANES), 0)
             == lax.broadcasted_iota(jnp.int32, (N_EXPERTS, LANES), 1)).astype(F32)
    comb_ref[...] = lax.dot_general(comb_t, ident, TN_DIMS, precision=HIGHEST, preferred_element_type=F32)


def _post(ma, mb, x2d, gate, shift, scale, mod_specs, wo, nw, wrt, brt, tm):
    m, d = x2d.shape
    row = lambda: pl.BlockSpec((tm, d), lambda i: (i, 0))
    full = lambda a: pl.BlockSpec(a.shape, lambda i: (0,) * a.ndim)
    return pl.pallas_call(
        _post_kernel,
        out_shape=(jax.ShapeDtypeStruct((m, d), F32),
                   jax.ShapeDtypeStruct((m, d), BF16),
                   jax.ShapeDtypeStruct((m, LANES), F32)),
        grid=(m // tm,),
        in_specs=[row(), row(), row(), mod_specs[0], mod_specs[1], mod_specs[2],
                  full(wo), full(nw), full(wrt), full(brt)],
        out_specs=(row(), row(), pl.BlockSpec((tm, LANES), lambda i: (i, 0))),
        compiler_params=_params("parallel"),
        name="post_mixer",
    )(ma, mb, x2d, gate, shift, scale, wo, nw, wrt, brt)


def _moe_kernel(h_ref, comb_ref, x1_ref, gate_ref, wg_ref, wu_ref, wd_ref, nw_ref, o_ref, acc_scr):
    e = pl.program_id(1)

    @pl.when(e == 0)
    def _():
        acc_scr[...] = jnp.zeros_like(acc_scr)

    h = h_ref[...]
    a = jnp.dot(h, wg_ref[0], preferred_element_type=F32)
    u = jnp.dot(h, wu_ref[0], preferred_element_type=F32)
    lane = lax.broadcasted_iota(jnp.int32, comb_ref.shape, 1)
    cw = jnp.sum(jnp.where(lane == e, comb_ref[...], 0.0), axis=-1, keepdims=True)
    hid = _silu(a) * u * cw
    acc_scr[...] += jnp.dot(hid.astype(BF16), wd_ref[0], preferred_element_type=F32)

    @pl.when(e == N_EXPERTS - 1)
    def _():
        x2 = x1_ref[...] + gate_ref[...] * acc_scr[...]
        o_ref[...] = _rms(x2, nw_ref[...])


def _moe(h2, comb, x1, gate, gate_spec, wg, wu, wd, nw, tm):
    m, d = x1.shape
    f = wg.shape[-1]
    return pl.pallas_call(
        _moe_kernel,
        out_shape=jax.ShapeDtypeStruct((m, d), F32),
        grid=(m // tm, N_EXPERTS),
        in_specs=[pl.BlockSpec((tm, d), lambda i, e: (i, 0)),
                  pl.BlockSpec((tm, LANES), lambda i, e: (i, 0)),
                  pl.BlockSpec((tm, d), lambda i, e: (i, 0)),
                  gate_spec,
                  pl.BlockSpec((1, d, f), lambda i, e: (e, 0, 0)),
                  pl.BlockSpec((1, d, f), lambda i, e: (e, 0, 0)),
                  pl.BlockSpec((1, f, d), lambda i, e: (e, 0, 0)),
                  pl.BlockSpec((1, d), lambda i, e: (0, 0))],
        out_specs=pl.BlockSpec((tm, d), lambda i, e: (i, 0)),
        scratch_shapes=[pltpu.VMEM((tm, d), F32)],
        compiler_params=_params("parallel", "arbitrary"),
        name="moe",
    )(h2, comb, x1, gate, wg, wu, wd, nw)


def _mod_spec_batch(idx, tm, t, ngrid):
    per = t // tm
    if ngrid == 2:
        return pl.BlockSpec((None, None, 1, D_MODEL), lambda i, j: (i // per, idx, 0, 0))
    return pl.BlockSpec((None, None, 1, D_MODEL), lambda i: (i // per, idx, 0, 0))


def _mod_spec_rows(tm, ngrid):
    if ngrid == 2:
        return pl.BlockSpec((tm, D_MODEL), lambda i, j: (i, 0))
    return pl.BlockSpec((tm, D_MODEL), lambda i: (i, 0))


def _trunk(x, mod, s_gla, s_gdn, s_conv, w, *, chunk, sub, t_valid, tm, tm_moe, act_dtype):
    nb, t_pad, d = x.shape
    m = nb * t_pad
    x2d = x.reshape(m, d)
    if t_pad % tm == 0:
        mod4 = mod.reshape(nb, N_MOD, 1, d)
        mods = [mod4] * N_MOD
        spec = lambda idx, ngrid: _mod_spec_batch(idx, tm, t_pad, ngrid)
    else:
        mods = [jnp.repeat(mod[:, i], t_pad, axis=0) for i in range(N_MOD)]
        spec = lambda idx, ngrid: _mod_spec_rows(tm, ngrid)

    proj, small, small_t = _inproj(x2d, mods[0], mods[1], (spec(0, 2), spec(1, 2)), w["norm1"],
                                   w["w_main"], w["w_small"], w["w_small_t"], tm, act_dtype)
    small_t3 = small_t.reshape(SMALL_ROWS, m // chunk, chunk).transpose(1, 0, 2)
    ma, new_gla = _gla(proj, small, w["wa2"], w["ba"], w["gla_norm"], s_gla,
                       nb=nb, t_pad=t_pad, chunk=chunk, sub=sub, t_valid=t_valid, out_dtype=act_dtype)
    mb, new_gdn, new_conv = _gdn(proj, small, small_t3, w["w_conv"], w["a_log"], w["dt_bias"], w["gdn_norm"],
                                 s_gdn, s_conv, nb=nb, t_pad=t_pad, chunk=chunk, t_valid=t_valid,
                                 out_dtype=act_dtype)
    x1, h2, comb = _post(ma, mb, x2d, mods[2], mods[3], mods[4], (spec(2, 1), spec(3, 1), spec(4, 1)),
                         w["w_out"], w["norm2"], w["w_router_t"], w["b_router_t"], tm)
    gate2_spec = _mod_spec_batch(5, tm_moe, t_pad, 2) if t_pad % tm == 0 else _mod_spec_rows(tm_moe, 2)
    y = _moe(h2, comb, x1, mods[5], gate2_spec, w["w_gate"], w["w_up"], w["w_down"], w["final_norm"], tm_moe)
    return y.reshape(nb, t_pad, d), new_gla, new_gdn, new_conv


def _prep_weights(w_in, w_gla_a2, b_gla_a, gla_norm_w, w_conv, gdn_A_log, gdn_dt_bias, gdn_norm_w, w_out,
                  norm1_w, norm2_w, w_group_router, b_group_router, w_expert_router, b_expert_router,
                  w_exp_gate, w_exp_up, w_exp_down, final_norm_w):
    d = D_MODEL
    o = 0
    cols = {}
    for name, width in (("gla", 2 * GLA_QK + 2 * GLA_VW), ("ra", GLA_GATE_RANK), ("qkv", GDN_CONV_DIM),
                        ("zb", GDN_VW), ("beta", GDN_HEADS), ("a", GDN_HEADS), ("gates", 2 * D_MODEL)):
        cols[name] = w_in[:, o:o + width]
        o += width
    w_main = jnp.concatenate([cols["gla"], cols["qkv"], cols["zb"], cols["gates"]], axis=1).astype(BF16)
    small = jnp.concatenate([cols["ra"], cols["beta"], cols["a"]], axis=1)
    w_small = jnp.pad(small, ((0, 0), (0, LANES - small.shape[1]))).astype(BF16)
    w_small_t = small.T.astype(BF16)
    wr_t = jnp.zeros((ROUTER_ROWS, d), F32)
    wr_t = wr_t.at[0:N_GROUPS].set(w_group_router.T)
    br_t = jnp.zeros((ROUTER_ROWS, 1), F32)
    br_t = br_t.at[0:N_GROUPS, 0].set(b_group_router)
    for g in range(N_GROUPS):
        r = ROUTER_GROUP_STRIDE * (1 + g)
        es = slice(g * EXPERTS_PER_GROUP, (g + 1) * EXPERTS_PER_GROUP)
        wr_t = wr_t.at[r:r + EXPERTS_PER_GROUP].set(w_expert_router[:, es].T)
        br_t = br_t.at[r:r + EXPERTS_PER_GROUP, 0].set(b_expert_router[es])
    return dict(
        w_main=w_main, w_small=w_small, w_small_t=w_small_t,
        norm1=norm1_w.reshape(1, d), norm2=norm2_w.reshape(1, d), final_norm=final_norm_w.reshape(1, d),
        wa2=w_gla_a2, ba=b_gla_a.reshape(1, GLA_QK), gla_norm=gla_norm_w.reshape(1, GLA_DV),
        w_conv=w_conv, a_log=gdn_A_log, dt_bias=gdn_dt_bias, gdn_norm=gdn_norm_w.reshape(1, GDN_DV),
        w_out=w_out.astype(BF16), w_router_t=wr_t.astype(BF16), b_router_t=br_t,
        w_gate=w_exp_gate.reshape(N_EXPERTS, d, D_EXPERT).astype(BF16),
        w_up=w_exp_up.reshape(N_EXPERTS, d, D_EXPERT).astype(BF16),
        w_down=w_exp_down.reshape(N_EXPERTS, D_EXPERT, d).astype(BF16),
    )


def kernel(x_prompt, x_sample, c_prompt, c_sample, state_gla, state_gdn, state_conv, w_ada, b_ada, norm1_w, w_in, w_gla_a2, b_gla_a, gla_norm_w, w_conv, gdn_A_log, gdn_dt_bias, gdn_norm_w, w_out, norm2_w, w_group_router, b_group_router, w_expert_router, b_expert_router, w_exp_gate, w_exp_up, w_exp_down, final_norm_w):
    assert w_ada.shape[0] == 1, "single layer"
    bp, tp, d = x_prompt.shape
    bs, ts, _ = x_sample.shape
    w = _prep_weights(w_in[0], w_gla_a2[0], b_gla_a[0], gla_norm_w[0], w_conv[0], gdn_A_log[0], gdn_dt_bias[0],
                      gdn_norm_w[0], w_out[0], norm1_w[0], norm2_w[0], w_group_router[0], b_group_router[0],
                      w_expert_router[0], b_expert_router[0], w_exp_gate[0], w_exp_up[0], w_exp_down[0],
                      final_norm_w)
    mod = _ada_mod(jnp.concatenate([c_prompt, c_sample], axis=0), w_ada[0], b_ada[0]).reshape(bp + bs, N_MOD, d)

    y_p, gla_p, gdn_p, conv_p = _trunk(x_prompt, mod[:bp], None, None, None, w,
                                       chunk=64, sub=GLA_SUBCHUNK, t_valid=tp, tm=min(1024, tp),
                                       tm_moe=min(512, tp), act_dtype=BF16)
    ts_pad = SUBLANES
    xs = jnp.pad(x_sample, ((0, 0), (0, ts_pad - ts), (0, 0)))
    y_s, gla_s, gdn_s, conv_s = _trunk(xs, mod[bp:], state_gla[0], state_gdn[0], state_conv[0], w,
                                       chunk=ts_pad, sub=ts_pad, t_valid=ts, tm=min(512, bs * ts_pad),
                                       tm_moe=min(512, bs * ts_pad), act_dtype=F32)
    return (y_p, y_s[:, :ts], gla_p[None], gdn_p[None], conv_p[None], gla_s[None], gdn_s[None], conv_s[None])
```

```python
import functools
import math

import jax
import jax.numpy as jnp
from jax import lax
from jax.experimental import pallas as pl
from jax.experimental.pallas import tpu as pltpu

F32 = jnp.float32
BF16 = jnp.bfloat16
HIGHEST = lax.Precision.HIGHEST

D_MODEL = 1024
GLA_HEADS = 4
GLA_DK = 128
GLA_DV = 256
GLA_QK = GLA_HEADS * GLA_DK
GLA_VW = GLA_HEADS * GLA_DV
GLA_GATE_RANK = 16
GLA_TAU = 16.0
GLA_SUBCHUNK = 16
GDN_HEADS = 8
GDN_DK = 128
GDN_DV = 128
GDN_QK = GDN_HEADS * GDN_DK
GDN_VW = GDN_HEADS * GDN_DV
GDN_CONV = 4
GDN_CONV_DIM = 2 * GDN_QK + GDN_VW
GDN_INV_BLOCK = 16
N_GROUPS = 4
EXPERTS_PER_GROUP = 4
N_EXPERTS = N_GROUPS * EXPERTS_PER_GROUP
D_EXPERT = D_MODEL // 2
N_MOD = 6
NORM_EPS = 1e-6
L2_EPS = 1e-6

LANES = 128
SUBLANES = 8
VMEM_LIMIT = 48 * 1024 * 1024

PROJ_MAIN = 2 * GLA_QK + 2 * GLA_VW + GDN_CONV_DIM + GDN_VW + 2 * D_MODEL
SMALL_RA, SMALL_BETA, SMALL_A = 0, GLA_GATE_RANK, GLA_GATE_RANK + GDN_HEADS
SMALL_ROWS = 32
ROUTER_GROUP_STRIDE = 8
ROUTER_ROWS = ROUTER_GROUP_STRIDE * (1 + N_GROUPS)

HX_WIDTH = D_MODEL + LANES
FINAL_TILE = 256

NT_DIMS = (((1,), (1,)), ((), ()))
TN_DIMS = (((0,), (0,)), ((), ()))


def _sigmoid(x):
    return 1.0 / (1.0 + jnp.exp(-x))


def _silu(x):
    return x * _sigmoid(x)


def _softplus(x):
    return jnp.maximum(x, 0.0) + jnp.log1p(jnp.exp(-jnp.abs(x)))


def _rms(x, w):
    return x * lax.rsqrt(jnp.mean(x * x, axis=-1, keepdims=True) + NORM_EPS) * w


def _params(*sem):
    return pltpu.CompilerParams(dimension_semantics=sem, vmem_limit_bytes=VMEM_LIMIT)


def _ada_kernel(c_ref, w_ref, b_ref, o_ref):
    cs = _silu(c_ref[...])
    o_ref[...] = jnp.dot(cs.astype(BF16), w_ref[...].astype(BF16), preferred_element_type=F32) + b_ref[...]


def _ada_mod(c_all, w_ada, b_ada):
    rows, d = c_all.shape
    n = w_ada.shape[1]
    tn = 1024
    return pl.pallas_call(
        _ada_kernel,
        out_shape=jax.ShapeDtypeStruct((rows, n), F32),
        grid=(n // tn,),
        in_specs=[pl.BlockSpec((rows, d), lambda j: (0, 0)),
                  pl.BlockSpec((d, tn), lambda j: (0, j)),
                  pl.BlockSpec((1, tn), lambda j: (0, j))],
        out_specs=pl.BlockSpec((rows, tn), lambda j: (0, j)),
        compiler_params=_params("arbitrary"),
        name="ada_mod",
    )(c_all, w_ada, b_ada.reshape(1, n))


def _inproj_kernel(x_ref, shift_ref, scale_ref, nw_ref, w_ref, ws_ref, wst_ref,
                   o_ref, os_ref, ost_ref, h_scr):
    @pl.when(pl.program_id(1) == 0)
    def _():
        h = _rms(x_ref[...], nw_ref[...]) * (1.0 + scale_ref[...]) + shift_ref[...]
        hb = h.astype(BF16)
        h_scr[...] = hb
        os_ref[...] = jnp.dot(hb, ws_ref[...], preferred_element_type=F32)
        ost_ref[...] = lax.dot_general(wst_ref[...], hb, NT_DIMS, preferred_element_type=F32)

    o_ref[...] = jnp.dot(h_scr[...], w_ref[...], preferred_element_type=F32).astype(o_ref.dtype)


def _inproj(x2d, shift, scale, mod_specs, nw, w_main, w_small, w_small_t, tm, out_dtype):
    m, d = x2d.shape
    n = w_main.shape[1]
    tn = 1024
    return pl.pallas_call(
        _inproj_kernel,
        out_shape=(jax.ShapeDtypeStruct((m, n), out_dtype),
                   jax.ShapeDtypeStruct((m, LANES), F32),
                   jax.ShapeDtypeStruct((SMALL_ROWS, m), F32)),
        grid=(m // tm, n // tn),
        in_specs=[pl.BlockSpec((tm, d), lambda i, j: (i, 0)),
                  mod_specs[0], mod_specs[1],
                  pl.BlockSpec((1, d), lambda i, j: (0, 0)),
                  pl.BlockSpec((d, tn), lambda i, j: (0, j)),
                  pl.BlockSpec((d, LANES), lambda i, j: (0, 0)),
                  pl.BlockSpec((SMALL_ROWS, d), lambda i, j: (0, 0))],
        out_specs=(pl.BlockSpec((tm, tn), lambda i, j: (i, j)),
                   pl.BlockSpec((tm, LANES), lambda i, j: (i, 0)),
                   pl.BlockSpec((SMALL_ROWS, tm), lambda i, j: (0, i))),
        scratch_shapes=[pltpu.VMEM((tm, d), BF16)],
        compiler_params=_params("parallel", "arbitrary"),
        name="inproj",
    )(x2d, shift, scale, nw, w_main, w_small, w_small_t)


def _gla_kernel(*refs, chunk, sub, nchunks, has_state, t_valid):
    if has_state:
        (qk_ref, v_ref, ga_ref, gt_ref, sm_ref, wa2_ref, ba_ref, nw_ref, s0_ref,
         o_ref, sout_ref, st_scr) = refs
    else:
        (qk_ref, v_ref, ga_ref, gt_ref, sm_ref, wa2_ref, ba_ref, nw_ref,
         o_ref, sout_ref, st_scr) = refs
    n = pl.program_id(1)
    masked = t_valid < chunk * nchunks

    @pl.when(n == 0)
    def _():
        for h in range(GLA_HEADS):
            if has_state:
                st_scr[h] = s0_ref[0, h].T
            else:
                st_scr[h] = jnp.zeros((GLA_DV, GLA_DK), F32)

    row = lax.broadcasted_iota(jnp.int32, (chunk, chunk), 0)
    col = lax.broadcasted_iota(jnp.int32, (chunk, chunk), 1)
    ra = sm_ref[:, SMALL_RA:SMALL_RA + GLA_GATE_RANK]
    x = jnp.dot(ra.astype(BF16), wa2_ref[...].astype(BF16), preferred_element_type=F32) + ba_ref[...]
    g = (jnp.minimum(x, 0.0) - jnp.log1p(jnp.exp(-jnp.abs(x)))) * (1.0 / GLA_TAU)
    if masked:
        valid = (lax.broadcasted_iota(jnp.int32, (chunk, 1), 0) + n * chunk) < t_valid
        g = jnp.where(valid, g, 0.0)
    tri = (row >= col).astype(F32)
    b_all = jnp.dot(tri, g, precision=HIGHEST, preferred_element_type=F32)

    heads = range(GLA_HEADS)
    rowi = lax.broadcasted_iota(jnp.int32, (chunk, 1), 0)
    q, k, v, b = [], [], [], []
    for h in heads:
        q.append(qk_ref[:, h * GLA_DK:(h + 1) * GLA_DK].astype(F32) * (GLA_DK ** -0.5))
        kh = qk_ref[:, GLA_QK + h * GLA_DK:GLA_QK + (h + 1) * GLA_DK].astype(F32)
        k.append(jnp.where(valid, kh, 0.0) if masked else kh)
        v.append(v_ref[:, h * GLA_DV:(h + 1) * GLA_DV].astype(BF16))
        b.append(b_all[:, h * GLA_DK:(h + 1) * GLA_DK])
    st = [st_scr[h] for h in heads]
    o = [lax.dot_general((q[h] * jnp.exp(b[h])).astype(BF16), st[h].astype(BF16), NT_DIMS,
                         preferred_element_type=F32) for h in heads]
    blocks = [[] for _ in heads]
    for i in range(chunk // sub):
        r0, r1 = i * sub, (i + 1) * sub
        for h in heads:
            bref = b[h][r0 - 1:r0] if i > 0 else jnp.zeros((1, GLA_DK), F32)
            qt = (q[h][r0:r1] * jnp.exp(b[h][r0:r1] - bref)).astype(BF16)
            expo = bref - b[h]
            if r1 < chunk:
                expo = jnp.where(rowi < r1, expo, 0.0)
            kt = (k[h] * jnp.exp(expo)).astype(BF16)
            blocks[h].append(lax.dot_general(qt, kt, NT_DIMS, preferred_element_type=F32))
    for h in heads:
        a = blocks[h][0] if len(blocks[h]) == 1 else jnp.concatenate(blocks[h], axis=0)
        a = jnp.where(col <= row, a, 0.0)
        o[h] = o[h] + jnp.dot(a.astype(BF16), v[h], preferred_element_type=F32)
    for h in heads:
        bl = b[h][chunk - 1:chunk]
        kt = (k[h] * jnp.exp(bl - b[h])).astype(BF16)
        st_scr[h] = st[h] * jnp.exp(bl) + lax.dot_general(v[h], kt, TN_DIMS, preferred_element_type=F32)
    for h in heads:
        sl = slice(h * GLA_DV, (h + 1) * GLA_DV)
        oa = _rms(o[h], nw_ref[...]) * _silu(ga_ref[:, sl].astype(F32)) * _sigmoid(gt_ref[:, sl].astype(F32))
        o_ref[:, sl] = oa.astype(o_ref.dtype)

    @pl.when(n == nchunks - 1)
    def _():
        for h in range(GLA_HEADS):
            sout_ref[0, h] = st_scr[h].T


def _gla(proj, small, wa2, ba, nw, s0, *, nb, t_pad, chunk, sub, t_valid, out_dtype):
    nchunks = t_pad // chunk
    has_state = s0 is not None
    blk = lambda c: pl.BlockSpec((chunk, 1024), lambda b, n, c=c: (b * nchunks + n, c))
    in_specs = [blk(0), blk(1), blk(2), blk(7),
                pl.BlockSpec((chunk, LANES), lambda b, n: (b * nchunks + n, 0)),
                pl.BlockSpec(wa2.shape, lambda b, n: (0, 0)),
                pl.BlockSpec(ba.shape, lambda b, n: (0, 0)),
                pl.BlockSpec(nw.shape, lambda b, n: (0, 0))]
    args = [proj, proj, proj, proj, small, wa2, ba, nw]
    state_spec = pl.BlockSpec((1, GLA_HEADS, GLA_DK, GLA_DV), lambda b, n: (b, 0, 0, 0))
    if has_state:
        in_specs.append(state_spec)
        args.append(s0)
    return pl.pallas_call(
        functools.partial(_gla_kernel, chunk=chunk, sub=sub, nchunks=nchunks,
                          has_state=has_state, t_valid=t_valid),
        out_shape=(jax.ShapeDtypeStruct((nb * t_pad, GLA_VW), out_dtype),
                   jax.ShapeDtypeStruct((nb, GLA_HEADS, GLA_DK, GLA_DV), F32)),
        grid=(nb, nchunks),
        in_specs=in_specs,
        out_specs=(pl.BlockSpec((chunk, GLA_VW), lambda b, n: (b * nchunks + n, 0)), state_spec),
        scratch_shapes=[pltpu.VMEM((GLA_HEADS, GLA_DV, GLA_DK), F32)],
        compiler_params=_params("parallel", "arbitrary"),
        name="gla",
    )(*args)


def _gdn_kernel(*refs, chunk, nchunks, has_state, t_valid):
    if has_state:
        (qkv_ref, zb_ref, gt_ref, sm_ref, smt_ref, wc_ref, alog_ref, dtb_ref, alogc_ref, dtbc_ref, nw_ref,
         s0_ref, c0_ref, o_ref, sout_ref, cout_ref, s_scr, xp_scr) = refs
    else:
        (qkv_ref, zb_ref, gt_ref, sm_ref, smt_ref, wc_ref, alog_ref, dtb_ref, alogc_ref, dtbc_ref, nw_ref,
         o_ref, sout_ref, cout_ref, s_scr, xp_scr) = refs
    n = pl.program_id(1)
    masked = t_valid < chunk * nchunks
    pad = SUBLANES

    @pl.when(n == 0)
    def _():
        if has_state:
            for h in range(GDN_HEADS):
                s_scr[h] = s0_ref[0, h]
            xp_scr[0:pad, :] = jnp.zeros((pad, GDN_CONV_DIM), F32)
            xp_scr[pad - (GDN_CONV - 1):pad, :] = c0_ref[0]
        else:
            s_scr[...] = jnp.zeros_like(s_scr)
            xp_scr[0:pad, :] = jnp.zeros((pad, GDN_CONV_DIM), F32)

    xp_scr[pad:pad + chunk, :] = qkv_ref[...].astype(F32)
    y = xp_scr[pad:pad + chunk, :] * wc_ref[GDN_CONV - 1:GDN_CONV, :]
    for j in range(1, GDN_CONV):
        y = y + xp_scr[pad - j:pad - j + chunk, :] * wc_ref[GDN_CONV - 1 - j:GDN_CONV - j, :]
    qkv = _silu(y)

    @pl.when(n == nchunks - 1)
    def _():
        last = t_valid - (nchunks - 1) * chunk
        cout_ref[0] = xp_scr[pad + last - (GDN_CONV - 1):pad + last, :]

    xp_scr[0:pad, :] = xp_scr[chunk:chunk + pad, :]

    row = lax.broadcasted_iota(jnp.int32, (chunk, chunk), 0)
    col = lax.broadcasted_iota(jnp.int32, (chunk, chunk), 1)
    a_col = sm_ref[:, SMALL_A:SMALL_A + GDN_HEADS]
    g_col = -jnp.exp(alog_ref[...]) * _softplus(a_col + dtb_ref[...])
    beta_col = _sigmoid(sm_ref[:, SMALL_BETA:SMALL_BETA + GDN_HEADS])
    a_row = smt_ref[SMALL_A:SMALL_A + GDN_HEADS, :]
    g_row = -jnp.exp(alogc_ref[...]) * _softplus(a_row + dtbc_ref[...])
    if masked:
        valid_c = (lax.broadcasted_iota(jnp.int32, (chunk, 1), 0) + n * chunk) < t_valid
        valid_r = (lax.broadcasted_iota(jnp.int32, (1, chunk), 1) + n * chunk) < t_valid
        g_col = jnp.where(valid_c, g_col, 0.0)
        beta_col = jnp.where(valid_c, beta_col, 0.0)
        g_row = jnp.where(valid_r, g_row, 0.0)
    b_col = jnp.dot((row >= col).astype(F32), g_col, precision=HIGHEST, preferred_element_type=F32)
    b_row = jnp.dot(g_row, (row <= col).astype(F32), precision=HIGHEST, preferred_element_type=F32)

    heads = range(GDN_HEADS)
    bdot = lambda a, b: jnp.dot(a.astype(BF16), b.astype(BF16), preferred_element_type=F32)
    q, k, v, kb, dec, bc, beta = [], [], [], [], [], [], []
    for h in heads:
        qh = qkv[:, h * GDN_DK:(h + 1) * GDN_DK]
        kh = qkv[:, GDN_QK + h * GDN_DK:GDN_QK + (h + 1) * GDN_DK]
        v.append(qkv[:, 2 * GDN_QK + h * GDN_DV:2 * GDN_QK + (h + 1) * GDN_DV])
        q.append(qh * lax.rsqrt(jnp.sum(qh * qh, axis=-1, keepdims=True) + L2_EPS) * (GDN_DK ** -0.5))
        k.append(kh * lax.rsqrt(jnp.sum(kh * kh, axis=-1, keepdims=True) + L2_EPS))
        bc.append(b_col[:, h:h + 1])
        beta.append(beta_col[:, h:h + 1])
        dec.append(jnp.where(row >= col, jnp.exp(jnp.minimum(bc[h] - b_row[h:h + 1, :], 0.0)), 0.0))
        kb.append(k[h] * beta[h])
    kq = [lax.dot_general(jnp.concatenate([kb[h], q[h]], axis=0).astype(BF16), k[h].astype(BF16), NT_DIMS,
                          preferred_element_type=F32) for h in heads]
    blk = min(GDN_INV_BLOCK, chunk)
    nblk = chunk // blk
    same_blk = (row // blk) == (col // blk)
    lmat = [jnp.where(row > col, kq[h][:chunk] * dec[h], 0.0) for h in heads]
    p = [jnp.where(same_blk, -lmat[h], 0.0) for h in heads]
    r = p
    for _ in range(int(math.log2(blk)) - 1):
        p = [bdot(p[h], p[h]) for h in heads]
        r = [r[h] + p[h] + bdot(r[h], p[h]) for h in heads]
    if nblk > 1:
        lo = [jnp.where(same_blk, 0.0, lmat[h]) for h in heads]
        p = [-(lo[h] + bdot(r[h], lo[h])) for h in heads]
        qm = p
        for _ in range(int(math.log2(nblk)) - 1):
            p = [bdot(p[h], p[h]) for h in heads]
            qm = [qm[h] + p[h] + bdot(qm[h], p[h]) for h in heads]
        r = [r[h] + qm[h] + bdot(qm[h], r[h]) for h in heads]
    rhs = [jnp.concatenate([v[h] * beta[h], kb[h] * jnp.exp(bc[h])], axis=-1) for h in heads]
    uw = [rhs[h] + bdot(r[h], rhs[h]) for h in heads]
    s = [s_scr[h] for h in heads]
    ws = [bdot(jnp.concatenate([uw[h][:, GDN_DV:], q[h] * jnp.exp(bc[h])], axis=0), s[h]) for h in heads]
    v_new = [uw[h][:, :GDN_DV] - ws[h][:chunk] for h in heads]
    o = [ws[h][chunk:] + bdot(kq[h][chunk:] * dec[h], v_new[h]) for h in heads]
    for h in heads:
        bl = bc[h][chunk - 1:chunk, :]
        s_scr[h] = jnp.exp(bl) * s[h] + lax.dot_general((k[h] * jnp.exp(bl - bc[h])).astype(BF16),
                                                        v_new[h].astype(BF16), TN_DIMS, preferred_element_type=F32)
    for h in heads:
        sl = slice(h * GDN_DV, (h + 1) * GDN_DV)
        ob = _rms(o[h], nw_ref[...]) * _silu(zb_ref[:, sl].astype(F32)) * _sigmoid(gt_ref[:, sl].astype(F32))
        o_ref[:, sl] = ob.astype(o_ref.dtype)

    @pl.when(n == nchunks - 1)
    def _():
        for h in range(GDN_HEADS):
            sout_ref[0, h] = s_scr[h]


def _gdn(proj, small, small_t3, wc, alog, dtb, nw, s0, c0, *, nb, t_pad, chunk, t_valid, out_dtype):
    nchunks = t_pad // chunk
    has_state = s0 is not None
    full = lambda a: pl.BlockSpec(a.shape, lambda b, n: (0,) * a.ndim)
    alog_r, dtb_r = alog.reshape(1, GDN_HEADS), dtb.reshape(1, GDN_HEADS)
    alog_c, dtb_c = alog.reshape(GDN_HEADS, 1), dtb.reshape(GDN_HEADS, 1)
    in_specs = [pl.BlockSpec((chunk, GDN_CONV_DIM), lambda b, n: (b * nchunks + n, 1)),
                pl.BlockSpec((chunk, 1024), lambda b, n: (b * nchunks + n, 6)),
                pl.BlockSpec((chunk, 1024), lambda b, n: (b * nchunks + n, 8)),
                pl.BlockSpec((chunk, LANES), lambda b, n: (b * nchunks + n, 0)),
                pl.BlockSpec((None, SMALL_ROWS, chunk), lambda b, n: (b * nchunks + n, 0, 0)),
                full(wc), full(alog_r), full(dtb_r), full(alog_c), full(dtb_c), full(nw)]
    args = [proj, proj, proj, small, small_t3, wc, alog_r, dtb_r, alog_c, dtb_c, nw]
    state_spec = pl.BlockSpec((1, GDN_HEADS, GDN_DK, GDN_DV), lambda b, n: (b, 0, 0, 0))
    conv_spec = pl.BlockSpec((1, GDN_CONV - 1, GDN_CONV_DIM), lambda b, n: (b, 0, 0))
    if has_state:
        in_specs += [state_spec, conv_spec]
        args += [s0, c0]
    return pl.pallas_call(
        functools.partial(_gdn_kernel, chunk=chunk, nchunks=nchunks, has_state=has_state, t_valid=t_valid),
        out_shape=(jax.ShapeDtypeStruct((nb * t_pad, GDN_VW), out_dtype),
                   jax.ShapeDtypeStruct((nb, GDN_HEADS, GDN_DK, GDN_DV), F32),
                   jax.ShapeDtypeStruct((nb, GDN_CONV - 1, GDN_CONV_DIM), F32)),
        grid=(nb, nchunks),
        in_specs=in_specs,
        out_specs=(pl.BlockSpec((chunk, GDN_VW), lambda b, n: (b * nchunks + n, 0)), state_spec, conv_spec),
        scratch_shapes=[pltpu.VMEM((GDN_HEADS, GDN_DK, GDN_DV), F32),
                        pltpu.VMEM((chunk + 2 * SUBLANES, GDN_CONV_DIM), F32)],
        compiler_params=_params("parallel", "arbitrary"),
        name="gdn",
    )(*args)


def _post_kernel(ma_ref, mb_ref, x_ref, gate_ref, shift_ref, scale_ref, wo_ref, nw_ref, wrt_ref, brt_ref,
                 x1_ref, hx_ref, gid_ref):
    merged = (ma_ref[...].astype(F32) + mb_ref[...].astype(F32)).astype(BF16)
    y = jnp.dot(merged, wo_ref[...], preferred_element_type=F32)
    x1 = x_ref[...] + gate_ref[...] * y
    x1_ref[...] = x1
    h2 = _rms(x1, nw_ref[...]) * (1.0 + scale_ref[...]) + shift_ref[...]
    hx_ref[:, 0:D_MODEL] = h2
    lt = lax.dot_general(wrt_ref[...], h2.astype(BF16), NT_DIMS, preferred_element_type=F32) + brt_ref[...]
    tm = lt.shape[1]
    gl = lt[0:N_GROUPS]
    gidx = lax.broadcasted_iota(jnp.int32, (N_GROUPS, tm), 0)
    gmax = jnp.max(gl, axis=0, keepdims=True)
    g_w = 1.0 / jnp.sum(jnp.exp(gl - gmax), axis=0, keepdims=True)
    g_sel = jnp.min(jnp.where(gl == gmax, gidx, N_GROUPS), axis=0, keepdims=True)
    el = jnp.zeros((EXPERTS_PER_GROUP, tm), F32)
    for g in range(N_GROUPS):
        r = ROUTER_GROUP_STRIDE * (1 + g)
        el = el + jnp.where(g_sel == g, lt[r:r + EXPERTS_PER_GROUP], 0.0)
    eidx = lax.broadcasted_iota(jnp.int32, (EXPERTS_PER_GROUP, tm), 0)
    m1 = jnp.max(el, axis=0, keepdims=True)
    i1 = jnp.min(jnp.where(el == m1, eidx, EXPERTS_PER_GROUP), axis=0, keepdims=True)
    el2 = jnp.where(eidx == i1, -jnp.inf, el)
    m2 = jnp.max(el2, axis=0, keepdims=True)
    i2 = jnp.min(jnp.where(el2 == m2, eidx, EXPERTS_PER_GROUP), axis=0, keepdims=True)
    r21 = jnp.exp(m2 - m1)
    w1 = 1.0 / (1.0 + r21)
    w2 = r21 / (1.0 + r21)
    comb_t = g_w * (jnp.where(eidx == i1, w1, 0.0) + jnp.where(eidx == i2, w2, 0.0))
    ident = (lax.broadcasted_iota(jnp.int32, (EXPERTS_PER_GROUP, LANES), 0)
             == lax.broadcasted_iota(jnp.int32, (EXPERTS_PER_GROUP, LANES), 1)).astype(F32)
    hx_ref[:, D_MODEL:] = lax.dot_general(comb_t, ident, TN_DIMS, precision=HIGHEST, preferred_element_type=F32)
    gid_ref[...] = jnp.broadcast_to(g_sel, gid_ref.shape)


def _post(ma, mb, x2d, gate, shift, scale, mod_specs, wo, nw, wrt, brt, tm):
    m, d = x2d.shape
    row = lambda: pl.BlockSpec((tm, d), lambda i: (i, 0))
    full = lambda a: pl.BlockSpec(a.shape, lambda i: (0,) * a.ndim)
    return pl.pallas_call(
        _post_kernel,
        out_shape=(jax.ShapeDtypeStruct((m, d), F32),
                   jax.ShapeDtypeStruct((m, HX_WIDTH), F32),
                   jax.ShapeDtypeStruct((SUBLANES, m), jnp.int32)),
        grid=(m // tm,),
        in_specs=[row(), row(), row(), mod_specs[0], mod_specs[1], mod_specs[2],
                  full(wo), full(nw), full(wrt), full(brt)],
        out_specs=(row(), pl.BlockSpec((tm, HX_WIDTH), lambda i: (i, 0)),
                   pl.BlockSpec((SUBLANES, tm), lambda i: (0, i))),
        compiler_params=_params("parallel"),
        name="post_mixer",
    )(ma, mb, x2d, gate, shift, scale, wo, nw, wrt, brt)


def _route(gid, tm):
    m = gid.shape[0]
    ntiles_max = m // tm + N_GROUPS - 1
    onehot = (gid[:, None] == jnp.arange(N_GROUPS, dtype=jnp.int32)[None, :]).astype(jnp.int32)
    incl = jnp.cumsum(onehot, axis=0)
    tiles = (incl[-1] + tm - 1) // tm
    tile_end = jnp.cumsum(tiles)
    pos = jnp.sum(onehot * ((tile_end - tiles) * tm + incl - onehot), axis=1)
    src = jnp.zeros((ntiles_max * tm,), jnp.int32).at[pos].set(jnp.arange(m, dtype=jnp.int32))
    t = jnp.arange(ntiles_max, dtype=jnp.int32)
    tile_group = jnp.minimum(jnp.sum((t[:, None] >= tile_end[None, :]).astype(jnp.int32), axis=1), N_GROUPS - 1)
    return pos, src, tile_group, tile_end[-1:]


def _row_gather(idx_ref, src_hbm, buf, sem, slot, rows):
    for r in range(rows):
        pltpu.make_async_copy(src_hbm.at[pl.ds(idx_ref[0, r], 1), :], buf.at[slot, pl.ds(r, 1), :],
                              sem.at[slot]).start()


def _row_gather_wait(src_hbm, buf, sem, slot, rows):
    pltpu.make_async_copy(src_hbm.at[pl.ds(0, rows), :], buf.at[slot], sem.at[slot]).wait()


def _experts_kernel(tg_ref, nt_ref, src_ref, srcn_ref, hx_hbm, wg_ref, wu_ref, wd_ref, y_ref, xbuf, sem, *, tm):
    t = pl.program_id(0)
    nt = nt_ref[0]
    slot = t % 2

    @pl.when(t == 0)
    def _():
        _row_gather(src_ref, hx_hbm, xbuf, sem, 0, tm)

    @pl.when(t < nt)
    def _():
        _row_gather(srcn_ref, hx_hbm, xbuf, sem, 1 - slot, tm)
        _row_gather_wait(hx_hbm, xbuf, sem, slot, tm)
        x = xbuf[slot, :, 0:D_MODEL].astype(BF16)
        acc = jnp.zeros((tm, D_MODEL), F32)
        for e in range(EXPERTS_PER_GROUP):
            a = jnp.dot(x, wg_ref[0, e], preferred_element_type=F32)
            u = jnp.dot(x, wu_ref[0, e], preferred_element_type=F32)
            cw = xbuf[slot, :, D_MODEL + e:D_MODEL + e + 1]
            acc = acc + jnp.dot((_silu(a) * u * cw).astype(BF16), wd_ref[0, e], preferred_element_type=F32)
        y_ref[...] = acc

        @pl.when(t == nt - 1)
        def _():
            _row_gather_wait(hx_hbm, xbuf, sem, 1 - slot, tm)

    @pl.when(t >= nt)
    def _():
        y_ref[...] = jnp.zeros_like(y_ref)


def _experts(hx, src, tile_group, ntiles, wg, wu, wd, tm):
    ntiles_max = tile_group.shape[0]
    d, f = D_MODEL, D_EXPERT
    src3 = src.reshape(ntiles_max, 1, tm)
    wspec = lambda shape: pl.BlockSpec((1,) + shape, lambda t, tg, nt: (tg[t], 0, 0, 0))
    return pl.pallas_call(
        functools.partial(_experts_kernel, tm=tm),
        out_shape=jax.ShapeDtypeStruct((ntiles_max * tm, d), F32),
        grid_spec=pltpu.PrefetchScalarGridSpec(
            num_scalar_prefetch=2,
            grid=(ntiles_max,),
            in_specs=[pl.BlockSpec((None, 1, tm), lambda t, tg, nt: (t, 0, 0), memory_space=pltpu.SMEM),
                      pl.BlockSpec((None, 1, tm), lambda t, tg, nt: (jnp.maximum(jnp.minimum(t + 1, nt[0] - 1), 0), 0, 0),
                                   memory_space=pltpu.SMEM),
                      pl.BlockSpec(memory_space=pl.ANY),
                      wspec((EXPERTS_PER_GROUP, d, f)), wspec((EXPERTS_PER_GROUP, d, f)),
                      wspec((EXPERTS_PER_GROUP, f, d))],
            out_specs=pl.BlockSpec((tm, d), lambda t, tg, nt: (t, 0)),
            scratch_shapes=[pltpu.VMEM((2, tm, HX_WIDTH), F32), pltpu.SemaphoreType.DMA((2,))]),
        compiler_params=_params("arbitrary"),
        name="experts",
    )(tile_group, ntiles, src3, src3, hx, wg, wu, wd)


def _final_kernel(pos_ref, posn_ref, ys_hbm, x1_ref, gate_ref, nw_ref, o_ref, ybuf, sem, *, tm, nsteps):
    i = pl.program_id(0)
    slot = i % 2

    @pl.when(i == 0)
    def _():
        _row_gather(pos_ref, ys_hbm, ybuf, sem, 0, tm)

    _row_gather(posn_ref, ys_hbm, ybuf, sem, 1 - slot, tm)
    _row_gather_wait(ys_hbm, ybuf, sem, slot, tm)
    o_ref[...] = _rms(x1_ref[...] + gate_ref[...] * ybuf[slot], nw_ref[...])

    @pl.when(i == nsteps - 1)
    def _():
        _row_gather_wait(ys_hbm, ybuf, sem, 1 - slot, tm)


def _final(ys, pos, x1, gate, gate_spec, nw, tm):
    m, d = x1.shape
    nsteps = m // tm
    pos3 = pos.reshape(nsteps, 1, tm)
    return pl.pallas_call(
        functools.partial(_final_kernel, tm=tm, nsteps=nsteps),
        out_shape=jax.ShapeDtypeStruct((m, d), F32),
        grid=(nsteps,),
        in_specs=[pl.BlockSpec((None, 1, tm), lambda i: (i, 0, 0), memory_space=pltpu.SMEM),
                  pl.BlockSpec((None, 1, tm), lambda i: (jnp.minimum(i + 1, nsteps - 1), 0, 0),
                               memory_space=pltpu.SMEM),
                  pl.BlockSpec(memory_space=pl.ANY),
                  pl.BlockSpec((tm, d), lambda i: (i, 0)),
                  gate_spec,
                  pl.BlockSpec((1, d), lambda i: (0, 0))],
        out_specs=pl.BlockSpec((tm, d), lambda i: (i, 0)),
        scratch_shapes=[pltpu.VMEM((2, tm, d), F32), pltpu.SemaphoreType.DMA((2,))],
        compiler_params=_params("arbitrary"),
        name="final",
    )(pos3, pos3, ys, x1, gate, nw)


def _mod_spec_batch(idx, tm, t, ngrid):
    per = t // tm
    if ngrid == 2:
        return pl.BlockSpec((None, None, 1, D_MODEL), lambda i, j: (i // per, idx, 0, 0))
    return pl.BlockSpec((None, None, 1, D_MODEL), lambda i: (i // per, idx, 0, 0))


def _mod_spec_rows(tm, ngrid):
    if ngrid == 2:
        return pl.BlockSpec((tm, D_MODEL), lambda i, j: (i, 0))
    return pl.BlockSpec((tm, D_MODEL), lambda i: (i, 0))


def _trunk(x, mod, s_gla, s_gdn, s_conv, w, *, chunk, sub, t_valid, tm, tm_moe, act_dtype):
    nb, t_pad, d = x.shape
    m = nb * t_pad
    x2d = x.reshape(m, d)
    if t_pad % tm == 0:
        mod4 = mod.reshape(nb, N_MOD, 1, d)
        mods = [mod4] * N_MOD
        spec = lambda idx, ngrid: _mod_spec_batch(idx, tm, t_pad, ngrid)
    else:
        mods = [jnp.repeat(mod[:, i], t_pad, axis=0) for i in range(N_MOD)]
        spec = lambda idx, ngrid: _mod_spec_rows(tm, ngrid)

    proj, small, small_t = _inproj(x2d, mods[0], mods[1], (spec(0, 2), spec(1, 2)), w["norm1"],
                                   w["w_main"], w["w_small"], w["w_small_t"], tm, act_dtype)
    small_t3 = small_t.reshape(SMALL_ROWS, m // chunk, chunk).transpose(1, 0, 2)
    ma, new_gla = _gla(proj, small, w["wa2"], w["ba"], w["gla_norm"], s_gla,
                       nb=nb, t_pad=t_pad, chunk=chunk, sub=sub, t_valid=t_valid, out_dtype=act_dtype)
    mb, new_gdn, new_conv = _gdn(proj, small, small_t3, w["w_conv"], w["a_log"], w["dt_bias"], w["gdn_norm"],
                                 s_gdn, s_conv, nb=nb, t_pad=t_pad, chunk=chunk, t_valid=t_valid,
                                 out_dtype=act_dtype)
    x1, hx, gid = _post(ma, mb, x2d, mods[2], mods[3], mods[4], (spec(2, 1), spec(3, 1), spec(4, 1)),
                        w["w_out"], w["norm2"], w["w_router_t"], w["b_router_t"], tm)
    pos, src, tile_group, ntiles = _route(gid[0], tm_moe)
    ys = _experts(hx, src, tile_group, ntiles, w["w_gate"], w["w_up"], w["w_down"], tm_moe)
    tm_fin = min(tm, FINAL_TILE)
    gate2_spec = _mod_spec_batch(5, tm_fin, t_pad, 1) if t_pad % tm == 0 else _mod_spec_rows(tm_fin, 1)
    y = _final(ys, pos, x1, mods[5], gate2_spec, w["final_norm"], tm_fin)
    return y.reshape(nb, t_pad, d), new_gla, new_gdn, new_conv


def _prep_weights(w_in, w_gla_a2, b_gla_a, gla_norm_w, w_conv, gdn_A_log, gdn_dt_bias, gdn_norm_w, w_out,
                  norm1_w, norm2_w, w_group_router, b_group_router, w_expert_router, b_expert_router,
                  w_exp_gate, w_exp_up, w_exp_down, final_norm_w):
    d = D_MODEL
    o = 0
    cols = {}
    for name, width in (("gla", 2 * GLA_QK + 2 * GLA_VW), ("ra", GLA_GATE_RANK), ("qkv", GDN_CONV_DIM),
                        ("zb", GDN_VW), ("beta", GDN_HEADS), ("a", GDN_HEADS), ("gates", 2 * D_MODEL)):
        cols[name] = w_in[:, o:o + width]
        o += width
    w_main = jnp.concatenate([cols["gla"], cols["qkv"], cols["zb"], cols["gates"]], axis=1).astype(BF16)
    small = jnp.concatenate([cols["ra"], cols["beta"], cols["a"]], axis=1)
    w_small = jnp.pad(small, ((0, 0), (0, LANES - small.shape[1]))).astype(BF16)
    w_small_t = small.T.astype(BF16)
    wr_t = jnp.zeros((ROUTER_ROWS, d), F32)
    wr_t = wr_t.at[0:N_GROUPS].set(w_group_router.T)
    br_t = jnp.zeros((ROUTER_ROWS, 1), F32)
    br_t = br_t.at[0:N_GROUPS, 0].set(b_group_router)
    for g in range(N_GROUPS):
        r = ROUTER_GROUP_STRIDE * (1 + g)
        es = slice(g * EXPERTS_PER_GROUP, (g + 1) * EXPERTS_PER_GROUP)
        wr_t = wr_t.at[r:r + EXPERTS_PER_GROUP].set(w_expert_router[:, es].T)
        br_t = br_t.at[r:r + EXPERTS_PER_GROUP, 0].set(b_expert_router[es])
    return dict(
        w_main=w_main, w_small=w_small, w_small_t=w_small_t,
        norm1=norm1_w.reshape(1, d), norm2=norm2_w.reshape(1, d), final_norm=final_norm_w.reshape(1, d),
        wa2=w_gla_a2, ba=b_gla_a.reshape(1, GLA_QK), gla_norm=gla_norm_w.reshape(1, GLA_DV),
        w_conv=w_conv, a_log=gdn_A_log, dt_bias=gdn_dt_bias, gdn_norm=gdn_norm_w.reshape(1, GDN_DV),
        w_out=w_out.astype(BF16), w_router_t=wr_t.astype(BF16), b_router_t=br_t,
        w_gate=w_exp_gate.astype(BF16), w_up=w_exp_up.astype(BF16), w_down=w_exp_down.astype(BF16),
    )


def kernel(x_prompt, x_sample, c_prompt, c_sample, state_gla, state_gdn, state_conv, w_ada, b_ada, norm1_w, w_in, w_gla_a2, b_gla_a, gla_norm_w, w_conv, gdn_A_log, gdn_dt_bias, gdn_norm_w, w_out, norm2_w, w_group_router, b_group_router, w_expert_router, b_expert_router, w_exp_gate, w_exp_up, w_exp_down, final_norm_w):
    assert w_ada.shape[0] == 1, "single layer"
    bp, tp, d = x_prompt.shape
    bs, ts, _ = x_sample.shape
    w = _prep_weights(w_in[0], w_gla_a2[0], b_gla_a[0], gla_norm_w[0], w_conv[0], gdn_A_log[0], gdn_dt_bias[0],
                      gdn_norm_w[0], w_out[0], norm1_w[0], norm2_w[0], w_group_router[0], b_group_router[0],
                      w_expert_router[0], b_expert_router[0], w_exp_gate[0], w_exp_up[0], w_exp_down[0],
                      final_norm_w)
    mod = _ada_mod(jnp.concatenate([c_prompt, c_sample], axis=0), w_ada[0], b_ada[0]).reshape(bp + bs, N_MOD, d)

    y_p, gla_p, gdn_p, conv_p = _trunk(x_prompt, mod[:bp], None, None, None, w,
                                       chunk=64, sub=GLA_SUBCHUNK, t_valid=tp, tm=min(1024, tp),
                                       tm_moe=min(512, tp), act_dtype=BF16)
    ts_pad = SUBLANES
    xs = jnp.pad(x_sample, ((0, 0), (0, ts_pad - ts), (0, 0)))
    y_s, gla_s, gdn_s, conv_s = _trunk(xs, mod[bp:], state_gla[0], state_gdn[0], state_conv[0], w,
                                       chunk=ts_pad, sub=ts_pad, t_valid=ts, tm=min(512, bs * ts_pad),
                                       tm_moe=min(256, bs * ts_pad), act_dtype=F32)
    return (y_p, y_s[:, :ts], gla_p[None], gdn_p[None], conv_p[None], gla_s[None], gdn_s[None], conv_s[None])
```

```python
import functools
import math

import jax
import jax.numpy as jnp
from jax import lax
from jax.experimental import pallas as pl
from jax.experimental.pallas import tpu as pltpu

F32 = jnp.float32
BF16 = jnp.bfloat16
HIGHEST = lax.Precision.HIGHEST

D_MODEL = 1024
GLA_HEADS = 4
GLA_DK = 128
GLA_DV = 256
GLA_QK = GLA_HEADS * GLA_DK
GLA_VW = GLA_HEADS * GLA_DV
GLA_GATE_RANK = 16
GLA_TAU = 16.0
GLA_SUBCHUNK = 16
GDN_HEADS = 8
GDN_DK = 128
GDN_DV = 128
GDN_QK = GDN_HEADS * GDN_DK
GDN_VW = GDN_HEADS * GDN_DV
GDN_CONV = 4
GDN_CONV_DIM = 2 * GDN_QK + GDN_VW
GDN_INV_BLOCK = 16
N_GROUPS = 4
EXPERTS_PER_GROUP = 4
N_EXPERTS = N_GROUPS * EXPERTS_PER_GROUP
D_EXPERT = D_MODEL // 2
N_MOD = 6
NORM_EPS = 1e-6
L2_EPS = 1e-6

LANES = 128
SUBLANES = 8
VMEM_LIMIT = 48 * 1024 * 1024

PROJ_MAIN = 2 * GLA_QK + 2 * GLA_VW + GDN_CONV_DIM + GDN_VW + 2 * D_MODEL
INPROJ_TN = PROJ_MAIN // 4
SMALL_RA, SMALL_BETA, SMALL_A = 0, GLA_GATE_RANK, GLA_GATE_RANK + GDN_HEADS
SMALL_ROWS = 32
ROUTER_GROUP_STRIDE = 8
ROUTER_ROWS = ROUTER_GROUP_STRIDE * (1 + N_GROUPS)

HX_WIDTH = D_MODEL + LANES
FINAL_TILE = 256

NT_DIMS = (((1,), (1,)), ((), ()))
TN_DIMS = (((0,), (0,)), ((), ()))


def _sigmoid(x):
    return 1.0 / (1.0 + jnp.exp(-x))


def _silu(x):
    return x * _sigmoid(x)


def _softplus(x):
    return jnp.maximum(x, 0.0) + jnp.log1p(jnp.exp(-jnp.abs(x)))


def _rms(x, w):
    return x * lax.rsqrt(jnp.mean(x * x, axis=-1, keepdims=True) + NORM_EPS) * w


def _params(*sem):
    return pltpu.CompilerParams(dimension_semantics=sem, vmem_limit_bytes=VMEM_LIMIT)


def _ada_kernel(c_ref, w_ref, b_ref, o_ref):
    cs = _silu(c_ref[...])
    o_ref[...] = jnp.dot(cs.astype(BF16), w_ref[...].astype(BF16), preferred_element_type=F32) + b_ref[...]


def _ada_mod(c_all, w_ada, b_ada):
    rows, d = c_all.shape
    n = w_ada.shape[1]
    tn = 1024
    return pl.pallas_call(
        _ada_kernel,
        out_shape=jax.ShapeDtypeStruct((rows, n), F32),
        grid=(n // tn,),
        in_specs=[pl.BlockSpec((rows, d), lambda j: (0, 0)),
                  pl.BlockSpec((d, tn), lambda j: (0, j)),
                  pl.BlockSpec((1, tn), lambda j: (0, j))],
        out_specs=pl.BlockSpec((rows, tn), lambda j: (0, j)),
        compiler_params=_params("arbitrary"),
        name="ada_mod",
    )(c_all, w_ada, b_ada.reshape(1, n))


def _inproj_kernel(x_ref, shift_ref, scale_ref, nw_ref, w_ref, ws_ref, wst_ref,
                   o_ref, os_ref, ost_ref, h_scr):
    @pl.when(pl.program_id(1) == 0)
    def _():
        h = _rms(x_ref[...], nw_ref[...]) * (1.0 + scale_ref[...]) + shift_ref[...]
        hb = h.astype(BF16)
        h_scr[...] = hb
        os_ref[...] = jnp.dot(hb, ws_ref[...], preferred_element_type=F32)
        ost_ref[...] = lax.dot_general(wst_ref[...], hb, NT_DIMS, preferred_element_type=F32)

    o_ref[...] = jnp.dot(h_scr[...], w_ref[...], preferred_element_type=F32).astype(o_ref.dtype)


def _inproj(x2d, shift, scale, mod_specs, nw, w_main, w_small, w_small_t, tm, out_dtype):
    m, d = x2d.shape
    n = w_main.shape[1]
    tn = INPROJ_TN
    return pl.pallas_call(
        _inproj_kernel,
        out_shape=(jax.ShapeDtypeStruct((m, n), out_dtype),
                   jax.ShapeDtypeStruct((m, LANES), F32),
                   jax.ShapeDtypeStruct((SMALL_ROWS, m), F32)),
        grid=(m // tm, n // tn),
        in_specs=[pl.BlockSpec((tm, d), lambda i, j: (i, 0)),
                  mod_specs[0], mod_specs[1],
                  pl.BlockSpec((1, d), lambda i, j: (0, 0)),
                  pl.BlockSpec((d, tn), lambda i, j: (0, j)),
                  pl.BlockSpec((d, LANES), lambda i, j: (0, 0)),
                  pl.BlockSpec((SMALL_ROWS, d), lambda i, j: (0, 0))],
        out_specs=(pl.BlockSpec((tm, tn), lambda i, j: (i, j)),
                   pl.BlockSpec((tm, LANES), lambda i, j: (i, 0)),
                   pl.BlockSpec((SMALL_ROWS, tm), lambda i, j: (0, i))),
        scratch_shapes=[pltpu.VMEM((tm, d), BF16)],
        compiler_params=_params("parallel", "arbitrary"),
        name="inproj",
    )(x2d, shift, scale, nw, w_main, w_small, w_small_t)


def _gla_kernel(*refs, chunk, sub, nchunks, nseq, has_state, t_valid):
    if has_state:
        (qk_ref, v_ref, ga_ref, gt_ref, sm_ref, wa2_ref, ba_ref, nw_ref, s0_ref,
         o_ref, sout_ref, st_scr) = refs
    else:
        (qk_ref, v_ref, ga_ref, gt_ref, sm_ref, wa2_ref, ba_ref, nw_ref,
         o_ref, sout_ref, st_scr) = refs
    n = pl.program_id(1)
    masked = t_valid < chunk * nchunks
    units = [(s, h) for s in range(nseq) for h in range(GLA_HEADS)]

    @pl.when(n == 0)
    def _():
        for u, (s, h) in enumerate(units):
            st_scr[u] = s0_ref[s, h].T if has_state else jnp.zeros((GLA_DV, GLA_DK), F32)

    row = lax.broadcasted_iota(jnp.int32, (chunk, chunk), 0)
    col = lax.broadcasted_iota(jnp.int32, (chunk, chunk), 1)
    rowi = lax.broadcasted_iota(jnp.int32, (chunk, 1), 0)
    valid = (rowi + n * chunk) < t_valid
    tri = (row >= col).astype(F32)
    wa2 = wa2_ref[...].astype(BF16)
    b_all = []
    for s in range(nseq):
        ra = sm_ref[s, :, SMALL_RA:SMALL_RA + GLA_GATE_RANK]
        x = jnp.dot(ra.astype(BF16), wa2, preferred_element_type=F32) + ba_ref[...]
        g = (jnp.minimum(x, 0.0) - jnp.log1p(jnp.exp(-jnp.abs(x)))) * (1.0 / GLA_TAU)
        if masked:
            g = jnp.where(valid, g, 0.0)
        b_all.append(jnp.dot(tri, g, precision=HIGHEST, preferred_element_type=F32))

    q, k, v, b = [], [], [], []
    for s, h in units:
        q.append(qk_ref[s, :, h * GLA_DK:(h + 1) * GLA_DK].astype(F32) * (GLA_DK ** -0.5))
        kh = qk_ref[s, :, GLA_QK + h * GLA_DK:GLA_QK + (h + 1) * GLA_DK].astype(F32)
        k.append(jnp.where(valid, kh, 0.0) if masked else kh)
        v.append(v_ref[s, :, h * GLA_DV:(h + 1) * GLA_DV].astype(BF16))
        b.append(b_all[s][:, h * GLA_DK:(h + 1) * GLA_DK])
    nu = range(len(units))
    st = [st_scr[u] for u in nu]
    o = [lax.dot_general((q[u] * jnp.exp(b[u])).astype(BF16), st[u].astype(BF16), NT_DIMS,
                         preferred_element_type=F32) for u in nu]
    blocks = [[] for _ in nu]
    for i in range(chunk // sub):
        r0, r1 = i * sub, (i + 1) * sub
        for u in nu:
            bref = b[u][r0 - 1:r0] if i > 0 else jnp.zeros((1, GLA_DK), F32)
            qt = (q[u][r0:r1] * jnp.exp(b[u][r0:r1] - bref)).astype(BF16)
            expo = bref - b[u]
            if r1 < chunk:
                expo = jnp.where(rowi < r1, expo, 0.0)
            kt = (k[u] * jnp.exp(expo)).astype(BF16)
            blocks[u].append(lax.dot_general(qt, kt, NT_DIMS, preferred_element_type=F32))
    for u in nu:
        a = blocks[u][0] if len(blocks[u]) == 1 else jnp.concatenate(blocks[u], axis=0)
        a = jnp.where(col <= row, a, 0.0)
        o[u] = o[u] + jnp.dot(a.astype(BF16), v[u], preferred_element_type=F32)
    for u in nu:
        bl = b[u][chunk - 1:chunk]
        kt = (k[u] * jnp.exp(bl - b[u])).astype(BF16)
        st_scr[u] = st[u] * jnp.exp(bl) + lax.dot_general(v[u], kt, TN_DIMS, preferred_element_type=F32)
    for u, (s, h) in enumerate(units):
        sl = slice(h * GLA_DV, (h + 1) * GLA_DV)
        oa = (_rms(o[u], nw_ref[...]) * _silu(ga_ref[s, :, sl].astype(F32))
              * _sigmoid(gt_ref[s, :, sl].astype(F32)))
        o_ref[s, :, sl] = oa.astype(o_ref.dtype)

    @pl.when(n == nchunks - 1)
    def _():
        for u, (s, h) in enumerate(units):
            sout_ref[s, h] = st_scr[u].T


def _gla(proj3, small3, wa2, ba, nw, s0, *, chunk, sub, nseq, t_valid, out_dtype):
    nb, t_pad, _ = proj3.shape
    nchunks = t_pad // chunk
    has_state = s0 is not None
    blk = lambda c: pl.BlockSpec((nseq, chunk, 1024), lambda b, n, c=c: (b, n, c))
    full = lambda a: pl.BlockSpec(a.shape, lambda b, n: (0,) * a.ndim)
    in_specs = [blk(0), blk(1), blk(2), blk(7),
                pl.BlockSpec((nseq, chunk, LANES), lambda b, n: (b, n, 0)),
                full(wa2), full(ba), full(nw)]
    args = [proj3, proj3, proj3, proj3, small3, wa2, ba, nw]
    state_spec = pl.BlockSpec((nseq, GLA_HEADS, GLA_DK, GLA_DV), lambda b, n: (b, 0, 0, 0))
    if has_state:
        in_specs.append(state_spec)
        args.append(s0)
    return pl.pallas_call(
        functools.partial(_gla_kernel, chunk=chunk, sub=sub, nchunks=nchunks, nseq=nseq,
                          has_state=has_state, t_valid=t_valid),
        out_shape=(jax.ShapeDtypeStruct((nb, t_pad, GLA_VW), out_dtype),
                   jax.ShapeDtypeStruct((nb, GLA_HEADS, GLA_DK, GLA_DV), F32)),
        grid=(nb // nseq, nchunks),
        in_specs=in_specs,
        out_specs=(pl.BlockSpec((nseq, chunk, GLA_VW), lambda b, n: (b, n, 0)), state_spec),
        scratch_shapes=[pltpu.VMEM((nseq * GLA_HEADS, GLA_DV, GLA_DK), F32)],
        compiler_params=_params("parallel", "arbitrary"),
        name="gla",
    )(*args)


def _gdn_kernel(*refs, chunk, nchunks, nseq, has_state, t_valid):
    if has_state:
        (qkv_ref, zb_ref, gt_ref, sm_ref, smt_ref, wc_ref, alog_ref, dtb_ref, alogc_ref, dtbc_ref, nw_ref,
         s0_ref, c0_ref, o_ref, sout_ref, cout_ref, s_scr, xp_scr) = refs
    else:
        (qkv_ref, zb_ref, gt_ref, sm_ref, smt_ref, wc_ref, alog_ref, dtb_ref, alogc_ref, dtbc_ref, nw_ref,
         o_ref, sout_ref, cout_ref, s_scr, xp_scr) = refs
    n = pl.program_id(1)
    masked = t_valid < chunk * nchunks
    pad = SUBLANES
    units = [(s, h) for s in range(nseq) for h in range(GDN_HEADS)]
    nu = range(len(units))

    @pl.when(n == 0)
    def _():
        for u, (s, h) in enumerate(units):
            s_scr[u] = s0_ref[s, h] if has_state else jnp.zeros((GDN_DK, GDN_DV), F32)
        for s in range(nseq):
            xp_scr[s, 0:pad, :] = jnp.zeros((pad, GDN_CONV_DIM), F32)
            if has_state:
                xp_scr[s, pad - (GDN_CONV - 1):pad, :] = c0_ref[s]

    row = lax.broadcasted_iota(jnp.int32, (chunk, chunk), 0)
    col = lax.broadcasted_iota(jnp.int32, (chunk, chunk), 1)
    valid_c = (lax.broadcasted_iota(jnp.int32, (chunk, 1), 0) + n * chunk) < t_valid
    valid_r = (lax.broadcasted_iota(jnp.int32, (1, chunk), 1) + n * chunk) < t_valid
    tri_c = (row >= col).astype(F32)
    tri_r = (row <= col).astype(F32)
    b_col, b_row, beta_col = [], [], []
    for s in range(nseq):
        xp_scr[s, pad:pad + chunk, :] = qkv_ref[s].astype(F32)
        g_col = -jnp.exp(alog_ref[...]) * _softplus(sm_ref[s, :, SMALL_A:SMALL_A + GDN_HEADS] + dtb_ref[...])
        bt = _sigmoid(sm_ref[s, :, SMALL_BETA:SMALL_BETA + GDN_HEADS])
        g_row = -jnp.exp(alogc_ref[...]) * _softplus(smt_ref[s, SMALL_A:SMALL_A + GDN_HEADS, :] + dtbc_ref[...])
        if masked:
            g_col = jnp.where(valid_c, g_col, 0.0)
            bt = jnp.where(valid_c, bt, 0.0)
            g_row = jnp.where(valid_r, g_row, 0.0)
        beta_col.append(bt)
        b_col.append(jnp.dot(tri_c, g_col, precision=HIGHEST, preferred_element_type=F32))
        b_row.append(jnp.dot(g_row, tri_r, precision=HIGHEST, preferred_element_type=F32))

    def conv_silu(s, c0):
        cs = slice(c0, c0 + LANES)
        y = xp_scr[s, pad:pad + chunk, cs] * wc_ref[GDN_CONV - 1:GDN_CONV, cs]
        for j in range(1, GDN_CONV):
            y = y + xp_scr[s, pad - j:pad - j + chunk, cs] * wc_ref[GDN_CONV - 1 - j:GDN_CONV - j, cs]
        return _silu(y)

    bdot = lambda a, b: jnp.dot(a.astype(BF16), b.astype(BF16), preferred_element_type=F32)
    q, k, v, kb, dec, bc, beta = [], [], [], [], [], [], []
    for u, (s, h) in enumerate(units):
        qh = conv_silu(s, h * GDN_DK)
        kh = conv_silu(s, GDN_QK + h * GDN_DK)
        v.append(conv_silu(s, 2 * GDN_QK + h * GDN_DV))
        q.append(qh * lax.rsqrt(jnp.sum(qh * qh, axis=-1, keepdims=True) + L2_EPS) * (GDN_DK ** -0.5))
        k.append(kh * lax.rsqrt(jnp.sum(kh * kh, axis=-1, keepdims=True) + L2_EPS))
        bc.append(b_col[s][:, h:h + 1])
        beta.append(beta_col[s][:, h:h + 1])
        dec.append(jnp.where(row >= col, jnp.exp(jnp.minimum(bc[u] - b_row[s][h:h + 1, :], 0.0)), 0.0))
        kb.append(k[u] * beta[u])
    kq = [lax.dot_general(jnp.concatenate([kb[u], q[u]], axis=0).astype(BF16), k[u].astype(BF16), NT_DIMS,
                          preferred_element_type=F32) for u in nu]
    blk = min(GDN_INV_BLOCK, chunk)
    nblk = chunk // blk
    same_blk = (row // blk) == (col // blk)
    lmat = [jnp.where(row > col, kq[u][:chunk] * dec[u], 0.0) for u in nu]
    p = [jnp.where(same_blk, -lmat[u], 0.0) for u in nu]
    r = p
    for _ in range(int(math.log2(blk)) - 1):
        p = [bdot(p[u], p[u]) for u in nu]
        r = [r[u] + p[u] + bdot(r[u], p[u]) for u in nu]
    if nblk > 1:
        lo = [jnp.where(same_blk, 0.0, lmat[u]) for u in nu]
        p = [-(lo[u] + bdot(r[u], lo[u])) for u in nu]
        qm = p
        for _ in range(int(math.log2(nblk)) - 1):
            p = [bdot(p[u], p[u]) for u in nu]
            qm = [qm[u] + p[u] + bdot(qm[u], p[u]) for u in nu]
        r = [r[u] + qm[u] + bdot(qm[u], r[u]) for u in nu]
    rhs = [jnp.concatenate([v[u] * beta[u], kb[u] * jnp.exp(bc[u])], axis=-1) for u in nu]
    uw = [rhs[u] + bdot(r[u], rhs[u]) for u in nu]
    st = [s_scr[u] for u in nu]
    ws = [bdot(jnp.concatenate([uw[u][:, GDN_DV:], q[u] * jnp.exp(bc[u])], axis=0), st[u]) for u in nu]
    v_new = [uw[u][:, :GDN_DV] - ws[u][:chunk] for u in nu]
    o = [ws[u][chunk:] + bdot(kq[u][chunk:] * dec[u], v_new[u]) for u in nu]
    for u in nu:
        bl = bc[u][chunk - 1:chunk, :]
        s_scr[u] = jnp.exp(bl) * st[u] + lax.dot_general((k[u] * jnp.exp(bl - bc[u])).astype(BF16),
                                                         v_new[u].astype(BF16), TN_DIMS, preferred_element_type=F32)
    for u, (s, h) in enumerate(units):
        sl = slice(h * GDN_DV, (h + 1) * GDN_DV)
        ob = (_rms(o[u], nw_ref[...]) * _silu(zb_ref[s, :, sl].astype(F32))
              * _sigmoid(gt_ref[s, :, sl].astype(F32)))
        o_ref[s, :, sl] = ob.astype(o_ref.dtype)

    @pl.when(n == nchunks - 1)
    def _():
        last = t_valid - (nchunks - 1) * chunk
        for s in range(nseq):
            cout_ref[s] = xp_scr[s, pad + last - (GDN_CONV - 1):pad + last, :]
        for u, (s, h) in enumerate(units):
            sout_ref[s, h] = s_scr[u]

    for s in range(nseq):
        xp_scr[s, 0:pad, :] = xp_scr[s, chunk:chunk + pad, :]


def _gdn(proj3, small3, small_t4, wc, alog, dtb, nw, s0, c0, *, chunk, nseq, t_valid, out_dtype):
    nb, t_pad, _ = proj3.shape
    nchunks = t_pad // chunk
    has_state = s0 is not None
    full = lambda a: pl.BlockSpec(a.shape, lambda b, n: (0,) * a.ndim)
    alog_r, dtb_r = alog.reshape(1, GDN_HEADS), dtb.reshape(1, GDN_HEADS)
    alog_c, dtb_c = alog.reshape(GDN_HEADS, 1), dtb.reshape(GDN_HEADS, 1)
    in_specs = [pl.BlockSpec((nseq, chunk, GDN_CONV_DIM), lambda b, n: (b, n, 1)),
                pl.BlockSpec((nseq, chunk, 1024), lambda b, n: (b, n, 6)),
                pl.BlockSpec((nseq, chunk, 1024), lambda b, n: (b, n, 8)),
                pl.BlockSpec((nseq, chunk, LANES), lambda b, n: (b, n, 0)),
                pl.BlockSpec((nseq, None, SMALL_ROWS, chunk), lambda b, n: (b, n, 0, 0)),
                full(wc), full(alog_r), full(dtb_r), full(alog_c), full(dtb_c), full(nw)]
    args = [proj3, proj3, proj3, small3, small_t4, wc, alog_r, dtb_r, alog_c, dtb_c, nw]
    state_spec = pl.BlockSpec((nseq, GDN_HEADS, GDN_DK, GDN_DV), lambda b, n: (b, 0, 0, 0))
    conv_spec = pl.BlockSpec((nseq, GDN_CONV - 1, GDN_CONV_DIM), lambda b, n: (b, 0, 0))
    if has_state:
        in_specs += [state_spec, conv_spec]
        args += [s0, c0]
    return pl.pallas_call(
        functools.partial(_gdn_kernel, chunk=chunk, nchunks=nchunks, nseq=nseq, has_state=has_state,
                          t_valid=t_valid),
        out_shape=(jax.ShapeDtypeStruct((nb, t_pad, GDN_VW), out_dtype),
                   jax.ShapeDtypeStruct((nb, GDN_HEADS, GDN_DK, GDN_DV), F32),
                   jax.ShapeDtypeStruct((nb, GDN_CONV - 1, GDN_CONV_DIM), F32)),
        grid=(nb // nseq, nchunks),
        in_specs=in_specs,
        out_specs=(pl.BlockSpec((nseq, chunk, GDN_VW), lambda b, n: (b, n, 0)), state_spec, conv_spec),
        scratch_shapes=[pltpu.VMEM((nseq * GDN_HEADS, GDN_DK, GDN_DV), F32),
                        pltpu.VMEM((nseq, chunk + 2 * SUBLANES, GDN_CONV_DIM), F32)],
        compiler_params=_params("parallel", "arbitrary"),
        name="gdn",
    )(*args)


def _post_kernel(ma_ref, mb_ref, x_ref, gate_ref, shift_ref, scale_ref, wo_ref, nw_ref, wrt_ref, brt_ref,
                 x1_ref, hx_ref, gid_ref):
    merged = (ma_ref[...].astype(F32) + mb_ref[...].astype(F32)).astype(BF16)
    y = jnp.dot(merged, wo_ref[...], preferred_element_type=F32)
    x1 = x_ref[...] + gate_ref[...] * y
    x1_ref[...] = x1
    h2 = _rms(x1, nw_ref[...]) * (1.0 + scale_ref[...]) + shift_ref[...]
    hx_ref[:, 0:D_MODEL] = h2
    lt = lax.dot_general(wrt_ref[...], h2.astype(BF16), NT_DIMS, preferred_element_type=F32) + brt_ref[...]
    tm = lt.shape[1]
    gl = lt[0:N_GROUPS]
    gidx = lax.broadcasted_iota(jnp.int32, (N_GROUPS, tm), 0)
    gmax = jnp.max(gl, axis=0, keepdims=True)
    g_w = 1.0 / jnp.sum(jnp.exp(gl - gmax), axis=0, keepdims=True)
    g_sel = jnp.min(jnp.where(gl == gmax, gidx, N_GROUPS), axis=0, keepdims=True)
    el = jnp.zeros((EXPERTS_PER_GROUP, tm), F32)
    for g in range(N_GROUPS):
        r = ROUTER_GROUP_STRIDE * (1 + g)
        el = el + jnp.where(g_sel == g, lt[r:r + EXPERTS_PER_GROUP], 0.0)
    eidx = lax.broadcasted_iota(jnp.int32, (EXPERTS_PER_GROUP, tm), 0)
    m1 = jnp.max(el, axis=0, keepdims=True)
    i1 = jnp.min(jnp.where(el == m1, eidx, EXPERTS_PER_GROUP), axis=0, keepdims=True)
    el2 = jnp.where(eidx == i1, -jnp.inf, el)
    m2 = jnp.max(el2, axis=0, keepdims=True)
    i2 = jnp.min(jnp.where(el2 == m2, eidx, EXPERTS_PER_GROUP), axis=0, keepdims=True)
    r21 = jnp.exp(m2 - m1)
    w1 = 1.0 / (1.0 + r21)
    w2 = r21 / (1.0 + r21)
    comb_t = g_w * (jnp.where(eidx == i1, w1, 0.0) + jnp.where(eidx == i2, w2, 0.0))
    ident = (lax.broadcasted_iota(jnp.int32, (EXPERTS_PER_GROUP, LANES), 0)
             == lax.broadcasted_iota(jnp.int32, (EXPERTS_PER_GROUP, LANES), 1)).astype(F32)
    hx_ref[:, D_MODEL:] = lax.dot_general(comb_t, ident, TN_DIMS, precision=HIGHEST, preferred_element_type=F32)
    gid_ref[...] = jnp.broadcast_to(g_sel, gid_ref.shape)


def _post(ma, mb, x2d, gate, shift, scale, mod_specs, wo, nw, wrt, brt, tm):
    m, d = x2d.shape
    row = lambda: pl.BlockSpec((tm, d), lambda i: (i, 0))
    full = lambda a: pl.BlockSpec(a.shape, lambda i: (0,) * a.ndim)
    return pl.pallas_call(
        _post_kernel,
        out_shape=(jax.ShapeDtypeStruct((m, d), F32),
                   jax.ShapeDtypeStruct((m, HX_WIDTH), F32),
                   jax.ShapeDtypeStruct((SUBLANES, m), jnp.int32)),
        grid=(m // tm,),
        in_specs=[row(), row(), row(), mod_specs[0], mod_specs[1], mod_specs[2],
                  full(wo), full(nw), full(wrt), full(brt)],
        out_specs=(row(), pl.BlockSpec((tm, HX_WIDTH), lambda i: (i, 0)),
                   pl.BlockSpec((SUBLANES, tm), lambda i: (0, i))),
        compiler_params=_params("parallel"),
        name="post_mixer",
    )(ma, mb, x2d, gate, shift, scale, wo, nw, wrt, brt)


def _route(gid, tm):
    m = gid.shape[0]
    ntiles_max = m // tm + N_GROUPS - 1
    onehot = (gid[:, None] == jnp.arange(N_GROUPS, dtype=jnp.int32)[None, :]).astype(jnp.int32)
    incl = jnp.cumsum(onehot, axis=0)
    tiles = (incl[-1] + tm - 1) // tm
    tile_end = jnp.cumsum(tiles)
    pos = jnp.sum(onehot * ((tile_end - tiles) * tm + incl - onehot), axis=1)
    src = jnp.zeros((ntiles_max * tm,), jnp.int32).at[pos].set(jnp.arange(m, dtype=jnp.int32))
    t = jnp.arange(ntiles_max, dtype=jnp.int32)
    tile_group = jnp.minimum(jnp.sum((t[:, None] >= tile_end[None, :]).astype(jnp.int32), axis=1), N_GROUPS - 1)
    return pos, src, tile_group, tile_end[-1:]


def _row_gather(idx_ref, src_hbm, buf, sem, slot, rows):
    for r in range(rows):
        pltpu.make_async_copy(src_hbm.at[pl.ds(idx_ref[0, r], 1), :], buf.at[slot, pl.ds(r, 1), :],
                              sem.at[slot]).start()


def _row_gather_wait(src_hbm, buf, sem, slot, rows):
    pltpu.make_async_copy(src_hbm.at[pl.ds(0, rows), :], buf.at[slot], sem.at[slot]).wait()


def _experts_kernel(tg_ref, nt_ref, src_ref, srcn_ref, hx_hbm, wg_ref, wu_ref, wd_ref, y_ref, xbuf, sem, *, tm):
    t = pl.program_id(0)
    nt = nt_ref[0]
    slot = t % 2

    @pl.when(t == 0)
    def _():
        _row_gather(src_ref, hx_hbm, xbuf, sem, 0, tm)

    @pl.when(t < nt)
    def _():
        _row_gather(srcn_ref, hx_hbm, xbuf, sem, 1 - slot, tm)
        _row_gather_wait(hx_hbm, xbuf, sem, slot, tm)
        x = xbuf[slot, :, 0:D_MODEL].astype(BF16)
        acc = jnp.zeros((tm, D_MODEL), F32)
        for e in range(EXPERTS_PER_GROUP):
            a = jnp.dot(x, wg_ref[0, e], preferred_element_type=F32)
            u = jnp.dot(x, wu_ref[0, e], preferred_element_type=F32)
            cw = xbuf[slot, :, D_MODEL + e:D_MODEL + e + 1]
            acc = acc + jnp.dot((_silu(a) * u * cw).astype(BF16), wd_ref[0, e], preferred_element_type=F32)
        y_ref[...] = acc

        @pl.when(t == nt - 1)
        def _():
            _row_gather_wait(hx_hbm, xbuf, sem, 1 - slot, tm)

    @pl.when(t >= nt)
    def _():
        y_ref[...] = jnp.zeros_like(y_ref)


def _experts(hx, src, tile_group, ntiles, wg, wu, wd, tm):
    ntiles_max = tile_group.shape[0]
    d, f = D_MODEL, D_EXPERT
    src3 = src.reshape(ntiles_max, 1, tm)
    wspec = lambda shape: pl.BlockSpec((1,) + shape, lambda t, tg, nt: (tg[t], 0, 0, 0))
    return pl.pallas_call(
        functools.partial(_experts_kernel, tm=tm),
        out_shape=jax.ShapeDtypeStruct((ntiles_max * tm, d), F32),
        grid_spec=pltpu.PrefetchScalarGridSpec(
            num_scalar_prefetch=2,
            grid=(ntiles_max,),
            in_specs=[pl.BlockSpec((None, 1, tm), lambda t, tg, nt: (t, 0, 0), memory_space=pltpu.SMEM),
                      pl.BlockSpec((None, 1, tm), lambda t, tg, nt: (jnp.maximum(jnp.minimum(t + 1, nt[0] - 1), 0), 0, 0),
                                   memory_space=pltpu.SMEM),
                      pl.BlockSpec(memory_space=pl.ANY),
                      wspec((EXPERTS_PER_GROUP, d, f)), wspec((EXPERTS_PER_GROUP, d, f)),
                      wspec((EXPERTS_PER_GROUP, f, d))],
            out_specs=pl.BlockSpec((tm, d), lambda t, tg, nt: (t, 0)),
            scratch_shapes=[pltpu.VMEM((2, tm, HX_WIDTH), F32), pltpu.SemaphoreType.DMA((2,))]),
        compiler_params=_params("arbitrary"),
        name="experts",
    )(tile_group, ntiles, src3, src3, hx, wg, wu, wd)


def _final_kernel(pos_ref, posn_ref, ys_hbm, x1_ref, gate_ref, nw_ref, o_ref, ybuf, sem, *, tm, nsteps):
    i = pl.program_id(0)
    slot = i % 2

    @pl.when(i == 0)
    def _():
        _row_gather(pos_ref, ys_hbm, ybuf, sem, 0, tm)

    _row_gather(posn_ref, ys_hbm, ybuf, sem, 1 - slot, tm)
    _row_gather_wait(ys_hbm, ybuf, sem, slot, tm)
    o_ref[...] = _rms(x1_ref[...] + gate_ref[...] * ybuf[slot], nw_ref[...])

    @pl.when(i == nsteps - 1)
    def _():
        _row_gather_wait(ys_hbm, ybuf, sem, 1 - slot, tm)


def _final(ys, pos, x1, gate, gate_spec, nw, tm):
    m, d = x1.shape
    nsteps = m // tm
    pos3 = pos.reshape(nsteps, 1, tm)
    return pl.pallas_call(
        functools.partial(_final_kernel, tm=tm, nsteps=nsteps),
        out_shape=jax.ShapeDtypeStruct((m, d), F32),
        grid=(nsteps,),
        in_specs=[pl.BlockSpec((None, 1, tm), lambda i: (i, 0, 0), memory_space=pltpu.SMEM),
                  pl.BlockSpec((None, 1, tm), lambda i: (jnp.minimum(i + 1, nsteps - 1), 0, 0),
                               memory_space=pltpu.SMEM),
                  pl.BlockSpec(memory_space=pl.ANY),
                  pl.BlockSpec((tm, d), lambda i: (i, 0)),
                  gate_spec,
                  pl.BlockSpec((1, d), lambda i: (0, 0))],
        out_specs=pl.BlockSpec((tm, d), lambda i: (i, 0)),
        scratch_shapes=[pltpu.VMEM((2, tm, d), F32), pltpu.SemaphoreType.DMA((2,))],
        compiler_params=_params("arbitrary"),
        name="final",
    )(pos3, pos3, ys, x1, gate, nw)


def _mod_spec_batch(idx, tm, t, ngrid):
    per = t // tm
    if ngrid == 2:
        return pl.BlockSpec((None, None, 1, D_MODEL), lambda i, j: (i // per, idx, 0, 0))
    return pl.BlockSpec((None, None, 1, D_MODEL), lambda i: (i // per, idx, 0, 0))


def _mod_spec_rows(tm, ngrid):
    if ngrid == 2:
        return pl.BlockSpec((tm, D_MODEL), lambda i, j: (i, 0))
    return pl.BlockSpec((tm, D_MODEL), lambda i: (i, 0))


def _trunk(x, mod, s_gla, s_gdn, s_conv, w, *, chunk, sub, nseq, t_valid, tm, tm_moe, act_dtype):
    nb, t_pad, d = x.shape
    m = nb * t_pad
    x2d = x.reshape(m, d)
    if t_pad % tm == 0:
        mod4 = mod.reshape(nb, N_MOD, 1, d)
        mods = [mod4] * N_MOD
        spec = lambda idx, ngrid: _mod_spec_batch(idx, tm, t_pad, ngrid)
    else:
        mods = [jnp.repeat(mod[:, i], t_pad, axis=0) for i in range(N_MOD)]
        spec = lambda idx, ngrid: _mod_spec_rows(tm, ngrid)

    proj, small, small_t = _inproj(x2d, mods[0], mods[1], (spec(0, 2), spec(1, 2)), w["norm1"],
                                   w["w_main"], w["w_small"], w["w_small_t"], tm, act_dtype)
    proj3 = proj.reshape(nb, t_pad, PROJ_MAIN)
    small3 = small.reshape(nb, t_pad, LANES)
    small_t4 = small_t.reshape(SMALL_ROWS, nb, t_pad // chunk, chunk).transpose(1, 2, 0, 3)
    ma, new_gla = _gla(proj3, small3, w["wa2"], w["ba"], w["gla_norm"], s_gla,
                       chunk=chunk, sub=sub, nseq=nseq, t_valid=t_valid, out_dtype=act_dtype)
    mb, new_gdn, new_conv = _gdn(proj3, small3, small_t4, w["w_conv"], w["a_log"], w["dt_bias"], w["gdn_norm"],
                                 s_gdn, s_conv, chunk=chunk, nseq=nseq, t_valid=t_valid, out_dtype=act_dtype)
    x1, hx, gid = _post(ma.reshape(m, d), mb.reshape(m, d), x2d, mods[2], mods[3], mods[4],
                        (spec(2, 1), spec(3, 1), spec(4, 1)),
                        w["w_out"], w["norm2"], w["w_router_t"], w["b_router_t"], tm)
    pos, src, tile_group, ntiles = _route(gid[0], tm_moe)
    ys = _experts(hx, src, tile_group, ntiles, w["w_gate"], w["w_up"], w["w_down"], tm_moe)
    tm_fin = min(tm, FINAL_TILE)
    gate2_spec = _mod_spec_batch(5, tm_fin, t_pad, 1) if t_pad % tm == 0 else _mod_spec_rows(tm_fin, 1)
    y = _final(ys, pos, x1, mods[5], gate2_spec, w["final_norm"], tm_fin)
    return y.reshape(nb, t_pad, d), new_gla, new_gdn, new_conv


def _prep_weights(w_in, w_gla_a2, b_gla_a, gla_norm_w, w_conv, gdn_A_log, gdn_dt_bias, gdn_norm_w, w_out,
                  norm1_w, norm2_w, w_group_router, b_group_router, w_expert_router, b_expert_router,
                  w_exp_gate, w_exp_up, w_exp_down, final_norm_w):
    d = D_MODEL
    o = 0
    cols = {}
    for name, width in (("gla", 2 * GLA_QK + 2 * GLA_VW), ("ra", GLA_GATE_RANK), ("qkv", GDN_CONV_DIM),
                        ("zb", GDN_VW), ("beta", GDN_HEADS), ("a", GDN_HEADS), ("gates", 2 * D_MODEL)):
        cols[name] = w_in[:, o:o + width]
        o += width
    w_main = jnp.concatenate([cols[c].astype(BF16) for c in ("gla", "qkv", "zb", "gates")], axis=1)
    small = jnp.concatenate([cols["ra"], cols["beta"], cols["a"]], axis=1)
    w_small = jnp.pad(small, ((0, 0), (0, LANES - small.shape[1]))).astype(BF16)
    w_small_t = small.T.astype(BF16)
    wr_t = jnp.zeros((ROUTER_ROWS, d), F32)
    wr_t = wr_t.at[0:N_GROUPS].set(w_group_router.T)
    br_t = jnp.zeros((ROUTER_ROWS, 1), F32)
    br_t = br_t.at[0:N_GROUPS, 0].set(b_group_router)
    for g in range(N_GROUPS):
        r = ROUTER_GROUP_STRIDE * (1 + g)
        es = slice(g * EXPERTS_PER_GROUP, (g + 1) * EXPERTS_PER_GROUP)
        wr_t = wr_t.at[r:r + EXPERTS_PER_GROUP].set(w_expert_router[:, es].T)
        br_t = br_t.at[r:r + EXPERTS_PER_GROUP, 0].set(b_expert_router[es])
    return dict(
        w_main=w_main, w_small=w_small, w_small_t=w_small_t,
        norm1=norm1_w.reshape(1, d), norm2=norm2_w.reshape(1, d), final_norm=final_norm_w.reshape(1, d),
        wa2=w_gla_a2, ba=b_gla_a.reshape(1, GLA_QK), gla_norm=gla_norm_w.reshape(1, GLA_DV),
        w_conv=w_conv, a_log=gdn_A_log, dt_bias=gdn_dt_bias, gdn_norm=gdn_norm_w.reshape(1, GDN_DV),
        w_out=w_out.astype(BF16), w_router_t=wr_t.astype(BF16), b_router_t=br_t,
        w_gate=w_exp_gate.astype(BF16), w_up=w_exp_up.astype(BF16), w_down=w_exp_down.astype(BF16),
    )


def kernel(x_prompt, x_sample, c_prompt, c_sample, state_gla, state_gdn, state_conv, w_ada, b_ada, norm1_w, w_in, w_gla_a2, b_gla_a, gla_norm_w, w_conv, gdn_A_log, gdn_dt_bias, gdn_norm_w, w_out, norm2_w, w_group_router, b_group_router, w_expert_router, b_expert_router, w_exp_gate, w_exp_up, w_exp_down, final_norm_w):
    assert w_ada.shape[0] == 1, "single layer"
    bp, tp, d = x_prompt.shape
    bs, ts, _ = x_sample.shape
    w = _prep_weights(w_in[0], w_gla_a2[0], b_gla_a[0], gla_norm_w[0], w_conv[0], gdn_A_log[0], gdn_dt_bias[0],
                      gdn_norm_w[0], w_out[0], norm1_w[0], norm2_w[0], w_group_router[0], b_group_router[0],
                      w_expert_router[0], b_expert_router[0], w_exp_gate[0], w_exp_up[0], w_exp_down[0],
                      final_norm_w)
    mod = _ada_mod(jnp.concatenate([c_prompt, c_sample], axis=0), w_ada[0], b_ada[0]).reshape(bp + bs, N_MOD, d)

    y_p, gla_p, gdn_p, conv_p = _trunk(x_prompt, mod[:bp], None, None, None, w,
                                       chunk=64, sub=GLA_SUBCHUNK, nseq=2, t_valid=tp, tm=min(1024, tp),
                                       tm_moe=min(512, tp), act_dtype=BF16)
    ts_pad = SUBLANES
    xs = jnp.pad(x_sample, ((0, 0), (0, ts_pad - ts), (0, 0)))
    y_s, gla_s, gdn_s, conv_s = _trunk(xs, mod[bp:], state_gla[0], state_gdn[0], state_conv[0], w,
                                       chunk=ts_pad, sub=ts_pad, nseq=4, t_valid=ts, tm=min(512, bs * ts_pad),
                                       tm_moe=min(256, bs * ts_pad), act_dtype=F32)
    return (y_p, y_s[:, :ts], gla_p[None], gdn_p[None], conv_p[None], gla_s[None], gdn_s[None], conv_s[None])
```

```python
import functools
import itertools
import math

import jax
import numpy as np
import jax.numpy as jnp
from jax import lax
from jax.experimental import pallas as pl
from jax.experimental.pallas import tpu as pltpu

F32 = jnp.float32
BF16 = jnp.bfloat16
HIGHEST = lax.Precision.HIGHEST

D_MODEL = 1024
GLA_HEADS = 4
GLA_DK = 128
GLA_DV = 256
GLA_QK = GLA_HEADS * GLA_DK
GLA_VW = GLA_HEADS * GLA_DV
GLA_GATE_RANK = 16
GLA_TAU = 16.0
GLA_SUBCHUNK = 16
GDN_HEADS = 8
GDN_DK = 128
GDN_DV = 128
GDN_QK = GDN_HEADS * GDN_DK
GDN_VW = GDN_HEADS * GDN_DV
GDN_CONV = 4
GDN_CONV_DIM = 2 * GDN_QK + GDN_VW
GDN_INV_BLOCK = 16
N_GROUPS = 4
EXPERTS_PER_GROUP = 4
N_EXPERTS = N_GROUPS * EXPERTS_PER_GROUP
D_EXPERT = D_MODEL // 2
N_MOD = 6
NORM_EPS = 1e-6
L2_EPS = 1e-6

LANES = 128
SUBLANES = 8
VMEM_LIMIT = 48 * 1024 * 1024

PROJ_MAIN = 2 * GLA_QK + 2 * GLA_VW + GDN_CONV_DIM + GDN_VW + 2 * D_MODEL
INPROJ_TN = PROJ_MAIN // 4
SMALL_RA, SMALL_BETA, SMALL_A = 0, GLA_GATE_RANK, GLA_GATE_RANK + GDN_HEADS
SMALL_ROWS = 32
ROUTER_GROUP_STRIDE = 8
ROUTER_ROWS = ROUTER_GROUP_STRIDE * (1 + N_GROUPS)

EXPERT_PAIRS = tuple(itertools.combinations(range(EXPERTS_PER_GROUP), 2))
GROUP_CLASSES = np.array([[g] + list(range(EXPERTS_PER_GROUP)) for g in range(N_GROUPS)], np.int32)
PAIR_CLASSES = np.array([[g, a, b] for g in range(N_GROUPS) for a, b in EXPERT_PAIRS], np.int32)

HX_WIDTH = D_MODEL + LANES
FINAL_TILE = 256

NT_DIMS = (((1,), (1,)), ((), ()))
TN_DIMS = (((0,), (0,)), ((), ()))


def _sigmoid(x):
    return 1.0 / (1.0 + jnp.exp(-x))


def _silu(x):
    return x * _sigmoid(x)


def _softplus(x):
    return jnp.maximum(x, 0.0) + jnp.log1p(jnp.exp(-jnp.abs(x)))


def _rms(x, w):
    return x * lax.rsqrt(jnp.mean(x * x, axis=-1, keepdims=True) + NORM_EPS) * w


def _params(*sem):
    return pltpu.CompilerParams(dimension_semantics=sem, vmem_limit_bytes=VMEM_LIMIT)


def _ada_kernel(c_ref, w_ref, b_ref, o_ref):
    cs = _silu(c_ref[...])
    o_ref[...] = jnp.dot(cs.astype(BF16), w_ref[...].astype(BF16), preferred_element_type=F32) + b_ref[...]


def _ada_mod(c_all, w_ada, b_ada):
    rows, d = c_all.shape
    n = w_ada.shape[1]
    tn = 1024
    return pl.pallas_call(
        _ada_kernel,
        out_shape=jax.ShapeDtypeStruct((rows, n), F32),
        grid=(n // tn,),
        in_specs=[pl.BlockSpec((rows, d), lambda j: (0, 0)),
                  pl.BlockSpec((d, tn), lambda j: (0, j)),
                  pl.BlockSpec((1, tn), lambda j: (0, j))],
        out_specs=pl.BlockSpec((rows, tn), lambda j: (0, j)),
        compiler_params=_params("arbitrary"),
        name="ada_mod",
    )(c_all, w_ada, b_ada.reshape(1, n))


def _inproj_kernel(x_ref, shift_ref, scale_ref, nw_ref, w_ref, ws_ref, wst_ref,
                   o_ref, os_ref, ost_ref, h_scr):
    @pl.when(pl.program_id(1) == 0)
    def _():
        h = _rms(x_ref[...], nw_ref[...]) * (1.0 + scale_ref[...]) + shift_ref[...]
        hb = h.astype(BF16)
        h_scr[...] = hb
        os_ref[...] = jnp.dot(hb, ws_ref[...], preferred_element_type=F32)
        ost_ref[...] = lax.dot_general(wst_ref[...], hb, NT_DIMS, preferred_element_type=F32)

    o_ref[...] = jnp.dot(h_scr[...], w_ref[...], preferred_element_type=F32).astype(o_ref.dtype)


def _inproj(x2d, shift, scale, mod_specs, nw, w_main, w_small, w_small_t, tm, out_dtype):
    m, d = x2d.shape
    n = w_main.shape[1]
    tn = INPROJ_TN
    return pl.pallas_call(
        _inproj_kernel,
        out_shape=(jax.ShapeDtypeStruct((m, n), out_dtype),
                   jax.ShapeDtypeStruct((m, LANES), F32),
                   jax.ShapeDtypeStruct((SMALL_ROWS, m), F32)),
        grid=(m // tm, n // tn),
        in_specs=[pl.BlockSpec((tm, d), lambda i, j: (i, 0)),
                  mod_specs[0], mod_specs[1],
                  pl.BlockSpec((1, d), lambda i, j: (0, 0)),
                  pl.BlockSpec((d, tn), lambda i, j: (0, j)),
                  pl.BlockSpec((d, LANES), lambda i, j: (0, 0)),
                  pl.BlockSpec((SMALL_ROWS, d), lambda i, j: (0, 0))],
        out_specs=(pl.BlockSpec((tm, tn), lambda i, j: (i, j)),
                   pl.BlockSpec((tm, LANES), lambda i, j: (i, 0)),
                   pl.BlockSpec((SMALL_ROWS, tm), lambda i, j: (0, i))),
        scratch_shapes=[pltpu.VMEM((tm, d), BF16)],
        compiler_params=_params("parallel", "arbitrary"),
        name="inproj",
    )(x2d, shift, scale, nw, w_main, w_small, w_small_t)


def _gla_kernel(*refs, chunk, sub, nchunks, nseq, has_state, t_valid):
    if has_state:
        (qk_ref, v_ref, ga_ref, gt_ref, sm_ref, wa2_ref, ba_ref, nw_ref, s0_ref,
         o_ref, sout_ref, st_scr) = refs
    else:
        (qk_ref, v_ref, ga_ref, gt_ref, sm_ref, wa2_ref, ba_ref, nw_ref,
         o_ref, sout_ref, st_scr) = refs
    n = pl.program_id(1)
    masked = t_valid < chunk * nchunks
    units = [(s, h) for s in range(nseq) for h in range(GLA_HEADS)]

    @pl.when(n == 0)
    def _():
        for u, (s, h) in enumerate(units):
            st_scr[u] = s0_ref[s, h].T if has_state else jnp.zeros((GLA_DV, GLA_DK), F32)

    row = lax.broadcasted_iota(jnp.int32, (chunk, chunk), 0)
    col = lax.broadcasted_iota(jnp.int32, (chunk, chunk), 1)
    rowi = lax.broadcasted_iota(jnp.int32, (chunk, 1), 0)
    valid = (rowi + n * chunk) < t_valid
    tri = (row >= col).astype(F32)
    wa2 = wa2_ref[...].astype(BF16)
    b_all = []
    for s in range(nseq):
        ra = sm_ref[s, :, SMALL_RA:SMALL_RA + GLA_GATE_RANK]
        x = jnp.dot(ra.astype(BF16), wa2, preferred_element_type=F32) + ba_ref[...]
        g = (jnp.minimum(x, 0.0) - jnp.log1p(jnp.exp(-jnp.abs(x)))) * (1.0 / GLA_TAU)
        if masked:
            g = jnp.where(valid, g, 0.0)
        b_all.append(jnp.dot(tri, g, precision=HIGHEST, preferred_element_type=F32))

    q, k, v, b = [], [], [], []
    for s, h in units:
        q.append(qk_ref[s, :, h * GLA_DK:(h + 1) * GLA_DK].astype(F32) * (GLA_DK ** -0.5))
        kh = qk_ref[s, :, GLA_QK + h * GLA_DK:GLA_QK + (h + 1) * GLA_DK].astype(F32)
        k.append(jnp.where(valid, kh, 0.0) if masked else kh)
        v.append(v_ref[s, :, h * GLA_DV:(h + 1) * GLA_DV].astype(BF16))
        b.append(b_all[s][:, h * GLA_DK:(h + 1) * GLA_DK])
    nu = range(len(units))
    st = [st_scr[u] for u in nu]
    o = [lax.dot_general((q[u] * jnp.exp(b[u])).astype(BF16), st[u].astype(BF16), NT_DIMS,
                         preferred_element_type=F32) for u in nu]
    blocks = [[] for _ in nu]
    for i in range(chunk // sub):
        r0, r1 = i * sub, (i + 1) * sub
        for u in nu:
            bref = b[u][r0 - 1:r0] if i > 0 else jnp.zeros((1, GLA_DK), F32)
            qt = (q[u][r0:r1] * jnp.exp(b[u][r0:r1] - bref)).astype(BF16)
            expo = bref - b[u]
            if r1 < chunk:
                expo = jnp.where(rowi < r1, expo, 0.0)
            kt = (k[u] * jnp.exp(expo)).astype(BF16)
            blocks[u].append(lax.dot_general(qt, kt, NT_DIMS, preferred_element_type=F32))
    for u in nu:
        a = blocks[u][0] if len(blocks[u]) == 1 else jnp.concatenate(blocks[u], axis=0)
        a = jnp.where(col <= row, a, 0.0)
        o[u] = o[u] + jnp.dot(a.astype(BF16), v[u], preferred_element_type=F32)
    for u in nu:
        bl = b[u][chunk - 1:chunk]
        kt = (k[u] * jnp.exp(bl - b[u])).astype(BF16)
        st_scr[u] = st[u] * jnp.exp(bl) + lax.dot_general(v[u], kt, TN_DIMS, preferred_element_type=F32)
    for u, (s, h) in enumerate(units):
        sl = slice(h * GLA_DV, (h + 1) * GLA_DV)
        oa = (_rms(o[u], nw_ref[...]) * _silu(ga_ref[s, :, sl].astype(F32))
              * _sigmoid(gt_ref[s, :, sl].astype(F32)))
        o_ref[s, :, sl] = oa.astype(o_ref.dtype)

    @pl.when(n == nchunks - 1)
    def _():
        for u, (s, h) in enumerate(units):
            sout_ref[s, h] = st_scr[u].T


def _gla(proj3, small3, wa2, ba, nw, s0, *, chunk, sub, nseq, t_valid, out_dtype):
    nb, t_pad, _ = proj3.shape
    nchunks = t_pad // chunk
    has_state = s0 is not None
    blk = lambda c: pl.BlockSpec((nseq, chunk, 1024), lambda b, n, c=c: (b, n, c))
    full = lambda a: pl.BlockSpec(a.shape, lambda b, n: (0,) * a.ndim)
    in_specs = [blk(0), blk(1), blk(2), blk(7),
                pl.BlockSpec((nseq, chunk, LANES), lambda b, n: (b, n, 0)),
                full(wa2), full(ba), full(nw)]
    args = [proj3, proj3, proj3, proj3, small3, wa2, ba, nw]
    state_spec = pl.BlockSpec((nseq, GLA_HEADS, GLA_DK, GLA_DV), lambda b, n: (b, 0, 0, 0))
    if has_state:
        in_specs.append(state_spec)
        args.append(s0)
    return pl.pallas_call(
        functools.partial(_gla_kernel, chunk=chunk, sub=sub, nchunks=nchunks, nseq=nseq,
                          has_state=has_state, t_valid=t_valid),
        out_shape=(jax.ShapeDtypeStruct((nb, t_pad, GLA_VW), out_dtype),
                   jax.ShapeDtypeStruct((nb, GLA_HEADS, GLA_DK, GLA_DV), F32)),
        grid=(nb // nseq, nchunks),
        in_specs=in_specs,
        out_specs=(pl.BlockSpec((nseq, chunk, GLA_VW), lambda b, n: (b, n, 0)), state_spec),
        scratch_shapes=[pltpu.VMEM((nseq * GLA_HEADS, GLA_DV, GLA_DK), F32)],
        compiler_params=_params("parallel", "arbitrary"),
        name="gla",
    )(*args)


def _gdn_kernel(*refs, chunk, nchunks, nseq, has_state, t_valid):
    if has_state:
        (qkv_ref, zb_ref, gt_ref, sm_ref, smt_ref, wc_ref, alog_ref, dtb_ref, alogc_ref, dtbc_ref, nw_ref,
         s0_ref, c0_ref, o_ref, sout_ref, cout_ref, s_scr, xp_scr) = refs
    else:
        (qkv_ref, zb_ref, gt_ref, sm_ref, smt_ref, wc_ref, alog_ref, dtb_ref, alogc_ref, dtbc_ref, nw_ref,
         o_ref, sout_ref, cout_ref, s_scr, xp_scr) = refs
    n = pl.program_id(1)
    masked = t_valid < chunk * nchunks
    pad = SUBLANES
    units = [(s, h) for s in range(nseq) for h in range(GDN_HEADS)]
    nu = range(len(units))

    @pl.when(n == 0)
    def _():
        for u, (s, h) in enumerate(units):
            s_scr[u] = s0_ref[s, h] if has_state else jnp.zeros((GDN_DK, GDN_DV), F32)
        for s in range(nseq):
            xp_scr[s, 0:pad, :] = jnp.zeros((pad, GDN_CONV_DIM), F32)
            if has_state:
                xp_scr[s, pad - (GDN_CONV - 1):pad, :] = c0_ref[s]

    row = lax.broadcasted_iota(jnp.int32, (chunk, chunk), 0)
    col = lax.broadcasted_iota(jnp.int32, (chunk, chunk), 1)
    valid_c = (lax.broadcasted_iota(jnp.int32, (chunk, 1), 0) + n * chunk) < t_valid
    valid_r = (lax.broadcasted_iota(jnp.int32, (1, chunk), 1) + n * chunk) < t_valid
    tri_c = (row >= col).astype(F32)
    tri_r = (row <= col).astype(F32)
    b_col, b_row, beta_col = [], [], []
    for s in range(nseq):
        xp_scr[s, pad:pad + chunk, :] = qkv_ref[s].astype(F32)
        g_col = -jnp.exp(alog_ref[...]) * _softplus(sm_ref[s, :, SMALL_A:SMALL_A + GDN_HEADS] + dtb_ref[...])
        bt = _sigmoid(sm_ref[s, :, SMALL_BETA:SMALL_BETA + GDN_HEADS])
        g_row = -jnp.exp(alogc_ref[...]) * _softplus(smt_ref[s, SMALL_A:SMALL_A + GDN_HEADS, :] + dtbc_ref[...])
        if masked:
            g_col = jnp.where(valid_c, g_col, 0.0)
            bt = jnp.where(valid_c, bt, 0.0)
            g_row = jnp.where(valid_r, g_row, 0.0)
        beta_col.append(bt)
        b_col.append(jnp.dot(tri_c, g_col, precision=HIGHEST, preferred_element_type=F32))
        b_row.append(jnp.dot(g_row, tri_r, precision=HIGHEST, preferred_element_type=F32))

    def conv_silu(s, c0):
        cs = slice(c0, c0 + LANES)
        y = xp_scr[s, pad:pad + chunk, cs] * wc_ref[GDN_CONV - 1:GDN_CONV, cs]
        for j in range(1, GDN_CONV):
            y = y + xp_scr[s, pad - j:pad - j + chunk, cs] * wc_ref[GDN_CONV - 1 - j:GDN_CONV - j, cs]
        return _silu(y)

    bdot = lambda a, b: jnp.dot(a.astype(BF16), b.astype(BF16), preferred_element_type=F32)
    q, k, v, kb, dec, bc, beta = [], [], [], [], [], [], []
    for u, (s, h) in enumerate(units):
        qh = conv_silu(s, h * GDN_DK)
        kh = conv_silu(s, GDN_QK + h * GDN_DK)
        v.append(conv_silu(s, 2 * GDN_QK + h * GDN_DV))
        q.append(qh * lax.rsqrt(jnp.sum(qh * qh, axis=-1, keepdims=True) + L2_EPS) * (GDN_DK ** -0.5))
        k.append(kh * lax.rsqrt(jnp.sum(kh * kh, axis=-1, keepdims=True) + L2_EPS))
        bc.append(b_col[s][:, h:h + 1])
        beta.append(beta_col[s][:, h:h + 1])
        dec.append(jnp.where(row >= col, jnp.exp(jnp.minimum(bc[u] - b_row[s][h:h + 1, :], 0.0)), 0.0))
        kb.append(k[u] * beta[u])
    kq = [lax.dot_general(jnp.concatenate([kb[u], q[u]], axis=0).astype(BF16), k[u].astype(BF16), NT_DIMS,
                          preferred_element_type=F32) for u in nu]
    blk = min(GDN_INV_BLOCK, chunk)
    nblk = chunk // blk
    same_blk = (row // blk) == (col // blk)
    lmat = [jnp.where(row > col, kq[u][:chunk] * dec[u], 0.0) for u in nu]
    p = [jnp.where(same_blk, -lmat[u], 0.0) for u in nu]
    r = p
    for _ in range(int(math.log2(blk)) - 1):
        p = [bdot(p[u], p[u]) for u in nu]
        r = [r[u] + p[u] + bdot(r[u], p[u]) for u in nu]
    if nblk > 1:
        lo = [jnp.where(same_blk, 0.0, lmat[u]) for u in nu]
        p = [-(lo[u] + bdot(r[u], lo[u])) for u in nu]
        qm = p
        for _ in range(int(math.log2(nblk)) - 1):
            p = [bdot(p[u], p[u]) for u in nu]
            qm = [qm[u] + p[u] + bdot(qm[u], p[u]) for u in nu]
        r = [r[u] + qm[u] + bdot(qm[u], r[u]) for u in nu]
    rhs = [jnp.concatenate([v[u] * beta[u], kb[u] * jnp.exp(bc[u])], axis=-1) for u in nu]
    uw = [rhs[u] + bdot(r[u], rhs[u]) for u in nu]
    st = [s_scr[u] for u in nu]
    ws = [bdot(jnp.concatenate([uw[u][:, GDN_DV:], q[u] * jnp.exp(bc[u])], axis=0), st[u]) for u in nu]
    v_new = [uw[u][:, :GDN_DV] - ws[u][:chunk] for u in nu]
    o = [ws[u][chunk:] + bdot(kq[u][chunk:] * dec[u], v_new[u]) for u in nu]
    for u in nu:
        bl = bc[u][chunk - 1:chunk, :]
        s_scr[u] = jnp.exp(bl) * st[u] + lax.dot_general((k[u] * jnp.exp(bl - bc[u])).astype(BF16),
                                                         v_new[u].astype(BF16), TN_DIMS, preferred_element_type=F32)
    for u, (s, h) in enumerate(units):
        sl = slice(h * GDN_DV, (h + 1) * GDN_DV)
        ob = (_rms(o[u], nw_ref[...]) * _silu(zb_ref[s, :, sl].astype(F32))
              * _sigmoid(gt_ref[s, :, sl].astype(F32)))
        o_ref[s, :, sl] = ob.astype(o_ref.dtype)

    @pl.when(n == nchunks - 1)
    def _():
        last = t_valid - (nchunks - 1) * chunk
        for s in range(nseq):
            cout_ref[s] = xp_scr[s, pad + last - (GDN_CONV - 1):pad + last, :]
        for u, (s, h) in enumerate(units):
            sout_ref[s, h] = s_scr[u]

    for s in range(nseq):
        xp_scr[s, 0:pad, :] = xp_scr[s, chunk:chunk + pad, :]


def _gdn(proj3, small3, small_t4, wc, alog, dtb, nw, s0, c0, *, chunk, nseq, t_valid, out_dtype):
    nb, t_pad, _ = proj3.shape
    nchunks = t_pad // chunk
    has_state = s0 is not None
    full = lambda a: pl.BlockSpec(a.shape, lambda b, n: (0,) * a.ndim)
    alog_r, dtb_r = alog.reshape(1, GDN_HEADS), dtb.reshape(1, GDN_HEADS)
    alog_c, dtb_c = alog.reshape(GDN_HEADS, 1), dtb.reshape(GDN_HEADS, 1)
    in_specs = [pl.BlockSpec((nseq, chunk, GDN_CONV_DIM), lambda b, n: (b, n, 1)),
                pl.BlockSpec((nseq, chunk, 1024), lambda b, n: (b, n, 6)),
                pl.BlockSpec((nseq, chunk, 1024), lambda b, n: (b, n, 8)),
                pl.BlockSpec((nseq, chunk, LANES), lambda b, n: (b, n, 0)),
                pl.BlockSpec((nseq, None, SMALL_ROWS, chunk), lambda b, n: (b, n, 0, 0)),
                full(wc), full(alog_r), full(dtb_r), full(alog_c), full(dtb_c), full(nw)]
    args = [proj3, proj3, proj3, small3, small_t4, wc, alog_r, dtb_r, alog_c, dtb_c, nw]
    state_spec = pl.BlockSpec((nseq, GDN_HEADS, GDN_DK, GDN_DV), lambda b, n: (b, 0, 0, 0))
    conv_spec = pl.BlockSpec((nseq, GDN_CONV - 1, GDN_CONV_DIM), lambda b, n: (b, 0, 0))
    if has_state:
        in_specs += [state_spec, conv_spec]
        args += [s0, c0]
    return pl.pallas_call(
        functools.partial(_gdn_kernel, chunk=chunk, nchunks=nchunks, nseq=nseq, has_state=has_state,
                          t_valid=t_valid),
        out_shape=(jax.ShapeDtypeStruct((nb, t_pad, GDN_VW), out_dtype),
                   jax.ShapeDtypeStruct((nb, GDN_HEADS, GDN_DK, GDN_DV), F32),
                   jax.ShapeDtypeStruct((nb, GDN_CONV - 1, GDN_CONV_DIM), F32)),
        grid=(nb // nseq, nchunks),
        in_specs=in_specs,
        out_specs=(pl.BlockSpec((nseq, chunk, GDN_VW), lambda b, n: (b, n, 0)), state_spec, conv_spec),
        scratch_shapes=[pltpu.VMEM((nseq * GDN_HEADS, GDN_DK, GDN_DV), F32),
                        pltpu.VMEM((nseq, chunk + 2 * SUBLANES, GDN_CONV_DIM), F32)],
        compiler_params=_params("parallel", "arbitrary"),
        name="gdn",
    )(*args)


def _post_kernel(ma_ref, mb_ref, x_ref, gate_ref, shift_ref, scale_ref, wo_ref, nw_ref, wrt_ref, brt_ref,
                 x1_ref, hx_ref, gid_ref):
    merged = (ma_ref[...].astype(F32) + mb_ref[...].astype(F32)).astype(BF16)
    y = jnp.dot(merged, wo_ref[...], preferred_element_type=F32)
    x1 = x_ref[...] + gate_ref[...] * y
    x1_ref[...] = x1
    h2 = _rms(x1, nw_ref[...]) * (1.0 + scale_ref[...]) + shift_ref[...]
    hx_ref[:, 0:D_MODEL] = h2
    lt = lax.dot_general(wrt_ref[...], h2.astype(BF16), NT_DIMS, preferred_element_type=F32) + brt_ref[...]
    tm = lt.shape[1]
    gl = lt[0:N_GROUPS]
    gidx = lax.broadcasted_iota(jnp.int32, (N_GROUPS, tm), 0)
    gmax = jnp.max(gl, axis=0, keepdims=True)
    g_w = 1.0 / jnp.sum(jnp.exp(gl - gmax), axis=0, keepdims=True)
    g_sel = jnp.min(jnp.where(gl == gmax, gidx, N_GROUPS), axis=0, keepdims=True)
    el = jnp.zeros((EXPERTS_PER_GROUP, tm), F32)
    for g in range(N_GROUPS):
        r = ROUTER_GROUP_STRIDE * (1 + g)
        el = el + jnp.where(g_sel == g, lt[r:r + EXPERTS_PER_GROUP], 0.0)
    eidx = lax.broadcasted_iota(jnp.int32, (EXPERTS_PER_GROUP, tm), 0)
    m1 = jnp.max(el, axis=0, keepdims=True)
    i1 = jnp.min(jnp.where(el == m1, eidx, EXPERTS_PER_GROUP), axis=0, keepdims=True)
    el2 = jnp.where(eidx == i1, -jnp.inf, el)
    m2 = jnp.max(el2, axis=0, keepdims=True)
    i2 = jnp.min(jnp.where(el2 == m2, eidx, EXPERTS_PER_GROUP), axis=0, keepdims=True)
    r21 = jnp.exp(m2 - m1)
    w1 = 1.0 / (1.0 + r21)
    w2 = r21 / (1.0 + r21)
    comb_t = g_w * (jnp.where(eidx == i1, w1, 0.0) + jnp.where(eidx == i2, w2, 0.0))
    ident = (lax.broadcasted_iota(jnp.int32, (EXPERTS_PER_GROUP, LANES), 0)
             == lax.broadcasted_iota(jnp.int32, (EXPERTS_PER_GROUP, LANES), 1)).astype(F32)
    hx_ref[:, D_MODEL:] = lax.dot_general(comb_t, ident, TN_DIMS, precision=HIGHEST, preferred_element_type=F32)
    lo, hi = jnp.minimum(i1, i2), jnp.maximum(i1, i2)
    pair = jnp.right_shift(lo * (2 * EXPERTS_PER_GROUP - 1 - lo), 1) + hi - lo - 1
    rid = lax.broadcasted_iota(jnp.int32, gid_ref.shape, 0)
    gid_ref[...] = jnp.where(rid == 0, g_sel, g_sel * len(EXPERT_PAIRS) + pair)


def _post(ma, mb, x2d, gate, shift, scale, mod_specs, wo, nw, wrt, brt, tm):
    m, d = x2d.shape
    row = lambda: pl.BlockSpec((tm, d), lambda i: (i, 0))
    full = lambda a: pl.BlockSpec(a.shape, lambda i: (0,) * a.ndim)
    return pl.pallas_call(
        _post_kernel,
        out_shape=(jax.ShapeDtypeStruct((m, d), F32),
                   jax.ShapeDtypeStruct((m, HX_WIDTH), F32),
                   jax.ShapeDtypeStruct((SUBLANES, m), jnp.int32)),
        grid=(m // tm,),
        in_specs=[row(), row(), row(), mod_specs[0], mod_specs[1], mod_specs[2],
                  full(wo), full(nw), full(wrt), full(brt)],
        out_specs=(row(), pl.BlockSpec((tm, HX_WIDTH), lambda i: (i, 0)),
                   pl.BlockSpec((SUBLANES, tm), lambda i: (0, i))),
        compiler_params=_params("parallel"),
        name="post_mixer",
    )(ma, mb, x2d, gate, shift, scale, wo, nw, wrt, brt)


def _route(cls, ncls, tm):
    m = cls.shape[0]
    ntiles_max = m // tm + ncls - 1
    onehot = (cls[:, None] == jnp.arange(ncls, dtype=jnp.int32)[None, :]).astype(jnp.int32)
    incl = jnp.cumsum(onehot, axis=0)
    tiles = (incl[-1] + tm - 1) // tm
    tile_end = jnp.cumsum(tiles)
    pos = jnp.sum(onehot * ((tile_end - tiles) * tm + incl - onehot), axis=1)
    src = jnp.zeros((ntiles_max * tm,), jnp.int32).at[pos].set(jnp.arange(m, dtype=jnp.int32))
    t = jnp.arange(ntiles_max, dtype=jnp.int32)
    tile_class = jnp.minimum(jnp.sum((t[:, None] >= tile_end[None, :]).astype(jnp.int32), axis=1), ncls - 1)
    return pos, src, tile_class, tile_end[-1:]


def _row_gather(idx_ref, src_hbm, buf, sem, slot, rows):
    for r in range(rows):
        pltpu.make_async_copy(src_hbm.at[pl.ds(idx_ref[0, r], 1), :], buf.at[slot, pl.ds(r, 1), :],
                              sem.at[slot]).start()


def _row_gather_wait(src_hbm, buf, sem, slot, rows):
    pltpu.make_async_copy(src_hbm.at[pl.ds(0, rows), :], buf.at[slot], sem.at[slot]).wait()


def _experts_kernel(meta_ref, nt_ref, src_ref, srcn_ref, hx_hbm, *rest, tm, nslot):
    w_refs, (y_ref, xbuf, sem) = rest[:3 * nslot], rest[3 * nslot:]
    t = pl.program_id(0)
    nt = nt_ref[0]
    slot = t % 2

    @pl.when(t == 0)
    def _():
        _row_gather(src_ref, hx_hbm, xbuf, sem, 0, tm)

    @pl.when(t < nt)
    def _():
        _row_gather_wait(hx_hbm, xbuf, sem, slot, tm)
        _row_gather(srcn_ref, hx_hbm, xbuf, sem, 1 - slot, tm)
        x = xbuf[slot, :, 0:D_MODEL].astype(BF16)
        acc = jnp.zeros((tm, D_MODEL), F32)
        for j in range(nslot):
            wg_ref, wu_ref, wd_ref = w_refs[3 * j:3 * j + 3]
            e = meta_ref[t * (1 + nslot) + 1 + j]
            a = jnp.dot(x, wg_ref[0, 0], preferred_element_type=F32)
            u = jnp.dot(x, wu_ref[0, 0], preferred_element_type=F32)
            cw = jnp.zeros((tm, 1), F32)
            for c in range(EXPERTS_PER_GROUP):
                cw = cw + jnp.where(e == c, xbuf[slot, :, D_MODEL + c:D_MODEL + c + 1], 0.0)
            acc = acc + jnp.dot((_silu(a) * u * cw).astype(BF16), wd_ref[0, 0], preferred_element_type=F32)
        y_ref[...] = acc

        @pl.when(t == nt - 1)
        def _():
            _row_gather_wait(hx_hbm, xbuf, sem, 1 - slot, tm)

    @pl.when(t >= nt)
    def _():
        y_ref[...] = jnp.zeros_like(y_ref)


def _experts(hx, src, tile_class, ntiles, class_table, wg, wu, wd, tm):
    ntiles_max = tile_class.shape[0]
    nslot = class_table.shape[1] - 1
    d, f = D_MODEL, D_EXPERT
    src3 = src.reshape(ntiles_max, 1, tm)
    meta = jnp.asarray(class_table, jnp.int32)[tile_class].reshape(-1)
    stride = 1 + nslot
    wspec = lambda j, shape: pl.BlockSpec(
        (1, 1) + shape, lambda t, meta, nt, j=j: (meta[t * stride], meta[t * stride + 1 + j], 0, 0))
    w_specs, w_args = [], []
    for j in range(nslot):
        w_specs += [wspec(j, (d, f)), wspec(j, (d, f)), wspec(j, (f, d))]
        w_args += [wg, wu, wd]
    return pl.pallas_call(
        functools.partial(_experts_kernel, tm=tm, nslot=nslot),
        out_shape=jax.ShapeDtypeStruct((ntiles_max * tm, d), F32),
        grid_spec=pltpu.PrefetchScalarGridSpec(
            num_scalar_prefetch=2,
            grid=(ntiles_max,),
            in_specs=[pl.BlockSpec((None, 1, tm), lambda t, meta, nt: (t, 0, 0), memory_space=pltpu.SMEM),
                      pl.BlockSpec((None, 1, tm),
                                   lambda t, meta, nt: (jnp.maximum(jnp.minimum(t + 1, nt[0] - 1), 0), 0, 0),
                                   memory_space=pltpu.SMEM),
                      pl.BlockSpec(memory_space=pl.ANY)] + w_specs,
            out_specs=pl.BlockSpec((tm, d), lambda t, meta, nt: (t, 0)),
            scratch_shapes=[pltpu.VMEM((2, tm, HX_WIDTH), F32), pltpu.SemaphoreType.DMA((2,))]),
        compiler_params=_params("arbitrary"),
        name="experts",
    )(meta, ntiles, src3, src3, hx, *w_args)


def _final_kernel(pos_ref, posn_ref, ys_hbm, x1_ref, gate_ref, nw_ref, o_ref, ybuf, sem, *, tm, nsteps):
    i = pl.program_id(0)
    slot = i % 2

    @pl.when(i == 0)
    def _():
        _row_gather(pos_ref, ys_hbm, ybuf, sem, 0, tm)

    _row_gather(posn_ref, ys_hbm, ybuf, sem, 1 - slot, tm)
    _row_gather_wait(ys_hbm, ybuf, sem, slot, tm)
    o_ref[...] = _rms(x1_ref[...] + gate_ref[...] * ybuf[slot], nw_ref[...])

    @pl.when(i == nsteps - 1)
    def _():
        _row_gather_wait(ys_hbm, ybuf, sem, 1 - slot, tm)


def _final(ys, pos, x1, gate, gate_spec, nw, tm):
    m, d = x1.shape
    nsteps = m // tm
    pos3 = pos.reshape(nsteps, 1, tm)
    return pl.pallas_call(
        functools.partial(_final_kernel, tm=tm, nsteps=nsteps),
        out_shape=jax.ShapeDtypeStruct((m, d), F32),
        grid=(nsteps,),
        in_specs=[pl.BlockSpec((None, 1, tm), lambda i: (i, 0, 0), memory_space=pltpu.SMEM),
                  pl.BlockSpec((None, 1, tm), lambda i: (jnp.minimum(i + 1, nsteps - 1), 0, 0),
                               memory_space=pltpu.SMEM),
                  pl.BlockSpec(memory_space=pl.ANY),
                  pl.BlockSpec((tm, d), lambda i: (i, 0)),
                  gate_spec,
                  pl.BlockSpec((1, d), lambda i: (0, 0))],
        out_specs=pl.BlockSpec((tm, d), lambda i: (i, 0)),
        scratch_shapes=[pltpu.VMEM((2, tm, d), F32), pltpu.SemaphoreType.DMA((2,))],
        compiler_params=_params("arbitrary"),
        name="final",
    )(pos3, pos3, ys, x1, gate, nw)


def _mod_spec_batch(idx, tm, t, ngrid):
    per = t // tm
    if ngrid == 2:
        return pl.BlockSpec((None, None, 1, D_MODEL), lambda i, j: (i // per, idx, 0, 0))
    return pl.BlockSpec((None, None, 1, D_MODEL), lambda i: (i // per, idx, 0, 0))


def _mod_spec_rows(tm, ngrid):
    if ngrid == 2:
        return pl.BlockSpec((tm, D_MODEL), lambda i, j: (i, 0))
    return pl.BlockSpec((tm, D_MODEL), lambda i: (i, 0))


def _trunk(x, mod, s_gla, s_gdn, s_conv, w, *, chunk, sub, nseq, t_valid, tm, tm_moe, pair_classes, act_dtype):
    nb, t_pad, d = x.shape
    m = nb * t_pad
    x2d = x.reshape(m, d)
    if t_pad % tm == 0:
        mod4 = mod.reshape(nb, N_MOD, 1, d)
        mods = [mod4] * N_MOD
        spec = lambda idx, ngrid: _mod_spec_batch(idx, tm, t_pad, ngrid)
    else:
        mods = [jnp.repeat(mod[:, i], t_pad, axis=0) for i in range(N_MOD)]
        spec = lambda idx, ngrid: _mod_spec_rows(tm, ngrid)

    proj, small, small_t = _inproj(x2d, mods[0], mods[1], (spec(0, 2), spec(1, 2)), w["norm1"],
                                   w["w_main"], w["w_small"], w["w_small_t"], tm, act_dtype)
    proj3 = proj.reshape(nb, t_pad, PROJ_MAIN)
    small3 = small.reshape(nb, t_pad, LANES)
    small_t4 = small_t.reshape(SMALL_ROWS, nb, t_pad // chunk, chunk).transpose(1, 2, 0, 3)
    ma, new_gla = _gla(proj3, small3, w["wa2"], w["ba"], w["gla_norm"], s_gla,
                       chunk=chunk, sub=sub, nseq=nseq, t_valid=t_valid, out_dtype=act_dtype)
    mb, new_gdn, new_conv = _gdn(proj3, small3, small_t4, w["w_conv"], w["a_log"], w["dt_bias"], w["gdn_norm"],
                                 s_gdn, s_conv, chunk=chunk, nseq=nseq, t_valid=t_valid, out_dtype=act_dtype)
    x1, hx, gid = _post(ma.reshape(m, d), mb.reshape(m, d), x2d, mods[2], mods[3], mods[4],
                        (spec(2, 1), spec(3, 1), spec(4, 1)),
                        w["w_out"], w["norm2"], w["w_router_t"], w["b_router_t"], tm)
    class_table = PAIR_CLASSES if pair_classes else GROUP_CLASSES
    pos, src, tile_class, ntiles = _route(gid[1 if pair_classes else 0], class_table.shape[0], tm_moe)
    ys = _experts(hx, src, tile_class, ntiles, class_table, w["w_gate"], w["w_up"], w["w_down"], tm_moe)
    tm_fin = min(tm, FINAL_TILE)
    gate2_spec = _mod_spec_batch(5, tm_fin, t_pad, 1) if t_pad % tm == 0 else _mod_spec_rows(tm_fin, 1)
    y = _final(ys, pos, x1, mods[5], gate2_spec, w["final_norm"], tm_fin)
    return y.reshape(nb, t_pad, d), new_gla, new_gdn, new_conv


def _prep_weights(w_in, w_gla_a2, b_gla_a, gla_norm_w, w_conv, gdn_A_log, gdn_dt_bias, gdn_norm_w, w_out,
                  norm1_w, norm2_w, w_group_router, b_group_router, w_expert_router, b_expert_router,
                  w_exp_gate, w_exp_up, w_exp_down, final_norm_w):
    d = D_MODEL
    o = 0
    cols = {}
    for name, width in (("gla", 2 * GLA_QK + 2 * GLA_VW), ("ra", GLA_GATE_RANK), ("qkv", GDN_CONV_DIM),
                        ("zb", GDN_VW), ("beta", GDN_HEADS), ("a", GDN_HEADS), ("gates", 2 * D_MODEL)):
        cols[name] = w_in[:, o:o + width]
        o += width
    w_main = jnp.concatenate([cols[c].astype(BF16) for c in ("gla", "qkv", "zb", "gates")], axis=1)
    small = jnp.concatenate([cols["ra"], cols["beta"], cols["a"]], axis=1)
    w_small = jnp.pad(small, ((0, 0), (0, LANES - small.shape[1]))).astype(BF16)
    w_small_t = small.T.astype(BF16)
    wr_t = jnp.zeros((ROUTER_ROWS, d), F32)
    wr_t = wr_t.at[0:N_GROUPS].set(w_group_router.T)
    br_t = jnp.zeros((ROUTER_ROWS, 1), F32)
    br_t = br_t.at[0:N_GROUPS, 0].set(b_group_router)
    for g in range(N_GROUPS):
        r = ROUTER_GROUP_STRIDE * (1 + g)
        es = slice(g * EXPERTS_PER_GROUP, (g + 1) * EXPERTS_PER_GROUP)
        wr_t = wr_t.at[r:r + EXPERTS_PER_GROUP].set(w_expert_router[:, es].T)
        br_t = br_t.at[r:r + EXPERTS_PER_GROUP, 0].set(b_expert_router[es])
    return dict(
        w_main=w_main, w_small=w_small, w_small_t=w_small_t,
        norm1=norm1_w.reshape(1, d), norm2=norm2_w.reshape(1, d), final_norm=final_norm_w.reshape(1, d),
        wa2=w_gla_a2, ba=b_gla_a.reshape(1, GLA_QK), gla_norm=gla_norm_w.reshape(1, GLA_DV),
        w_conv=w_conv, a_log=gdn_A_log, dt_bias=gdn_dt_bias, gdn_norm=gdn_norm_w.reshape(1, GDN_DV),
        w_out=w_out.astype(BF16), w_router_t=wr_t.astype(BF16), b_router_t=br_t,
        w_gate=w_exp_gate.astype(BF16), w_up=w_exp_up.astype(BF16), w_down=w_exp_down.astype(BF16),
    )


def kernel(x_prompt, x_sample, c_prompt, c_sample, state_gla, state_gdn, state_conv, w_ada, b_ada, norm1_w, w_in, w_gla_a2, b_gla_a, gla_norm_w, w_conv, gdn_A_log, gdn_dt_bias, gdn_norm_w, w_out, norm2_w, w_group_router, b_group_router, w_expert_router, b_expert_router, w_exp_gate, w_exp_up, w_exp_down, final_norm_w):
    assert w_ada.shape[0] == 1, "single layer"
    bp, tp, d = x_prompt.shape
    bs, ts, _ = x_sample.shape
    w = _prep_weights(w_in[0], w_gla_a2[0], b_gla_a[0], gla_norm_w[0], w_conv[0], gdn_A_log[0], gdn_dt_bias[0],
                      gdn_norm_w[0], w_out[0], norm1_w[0], norm2_w[0], w_group_router[0], b_group_router[0],
                      w_expert_router[0], b_expert_router[0], w_exp_gate[0], w_exp_up[0], w_exp_down[0],
                      final_norm_w)
    mod = _ada_mod(jnp.concatenate([c_prompt, c_sample], axis=0), w_ada[0], b_ada[0]).reshape(bp + bs, N_MOD, d)

    y_p, gla_p, gdn_p, conv_p = _trunk(x_prompt, mod[:bp], None, None, None, w,
                                       chunk=64, sub=GLA_SUBCHUNK, nseq=2, t_valid=tp, tm=min(1024, tp),
                                       tm_moe=min(256, tp), pair_classes=True, act_dtype=BF16)
    ts_pad = SUBLANES
    xs = jnp.pad(x_sample, ((0, 0), (0, ts_pad - ts), (0, 0)))
    y_s, gla_s, gdn_s, conv_s = _trunk(xs, mod[bp:], state_gla[0], state_gdn[0], state_conv[0], w,
                                       chunk=ts_pad, sub=ts_pad, nseq=8, t_valid=ts, tm=min(512, bs * ts_pad),
                                       tm_moe=min(256, bs * ts_pad), pair_classes=False, act_dtype=F32)
    return (y_p, y_s[:, :ts], gla_p[None], gdn_p[None], conv_p[None], gla_s[None], gdn_s[None], conv_s[None])
```

```python
import functools
import itertools
import math

import jax
import numpy as np
import jax.numpy as jnp
from jax import lax
from jax.experimental import pallas as pl
from jax.experimental.pallas import tpu as pltpu

F32 = jnp.float32
BF16 = jnp.bfloat16
HIGHEST = lax.Precision.HIGHEST

D_MODEL = 1024
GLA_HEADS = 4
GLA_DK = 128
GLA_DV = 256
GLA_QK = GLA_HEADS * GLA_DK
GLA_VW = GLA_HEADS * GLA_DV
GLA_GATE_RANK = 16
GLA_TAU = 16.0
GLA_SUBCHUNK = 16
GDN_HEADS = 8
GDN_DK = 128
GDN_DV = 128
GDN_QK = GDN_HEADS * GDN_DK
GDN_VW = GDN_HEADS * GDN_DV
GDN_CONV = 4
GDN_CONV_DIM = 2 * GDN_QK + GDN_VW
GDN_INV_BLOCK = 16
N_GROUPS = 4
EXPERTS_PER_GROUP = 4
N_EXPERTS = N_GROUPS * EXPERTS_PER_GROUP
D_EXPERT = D_MODEL // 2
N_MOD = 6
NORM_EPS = 1e-6
L2_EPS = 1e-6

LANES = 128
SUBLANES = 8
VMEM_LIMIT = 48 * 1024 * 1024

PROJ_MAIN = 2 * GLA_QK + 2 * GLA_VW + GDN_CONV_DIM + GDN_VW + 2 * D_MODEL
INPROJ_TN = PROJ_MAIN // 4
SMALL_RA, SMALL_BETA, SMALL_A = 0, GLA_GATE_RANK, GLA_GATE_RANK + GDN_HEADS
SMALL_ROWS = 32
ROUTER_GROUP_STRIDE = 8
ROUTER_ROWS = ROUTER_GROUP_STRIDE * (1 + N_GROUPS)

EXPERT_PAIRS = tuple(itertools.combinations(range(EXPERTS_PER_GROUP), 2))
GROUP_CLASSES = np.array([[g] + list(range(EXPERTS_PER_GROUP)) for g in range(N_GROUPS)], np.int32)
PAIR_CLASSES = np.array([[g, a, b] for g in range(N_GROUPS) for a, b in EXPERT_PAIRS], np.int32)

HX_WIDTH = D_MODEL + LANES
FINAL_TILE = 256

NT_DIMS = (((1,), (1,)), ((), ()))
TN_DIMS = (((0,), (0,)), ((), ()))


def _sigmoid(x):
    return 1.0 / (1.0 + jnp.exp(-x))


def _silu(x):
    return x * _sigmoid(x)


def _softplus(x):
    return jnp.maximum(x, 0.0) + jnp.log1p(jnp.exp(-jnp.abs(x)))


def _rms(x, w):
    return x * lax.rsqrt(jnp.mean(x * x, axis=-1, keepdims=True) + NORM_EPS) * w


def _params(*sem):
    return pltpu.CompilerParams(dimension_semantics=sem, vmem_limit_bytes=VMEM_LIMIT)


def _ada_kernel(c_ref, w_ref, b_ref, o_ref):
    cs = _silu(c_ref[...])
    o_ref[...] = jnp.dot(cs.astype(BF16), w_ref[...].astype(BF16), preferred_element_type=F32) + b_ref[...]


def _ada_mod(c_all, w_ada, b_ada):
    rows, d = c_all.shape
    n = w_ada.shape[1]
    tn = 1024
    return pl.pallas_call(
        _ada_kernel,
        out_shape=jax.ShapeDtypeStruct((rows, n), F32),
        grid=(n // tn,),
        in_specs=[pl.BlockSpec((rows, d), lambda j: (0, 0)),
                  pl.BlockSpec((d, tn), lambda j: (0, j)),
                  pl.BlockSpec((1, tn), lambda j: (0, j))],
        out_specs=pl.BlockSpec((rows, tn), lambda j: (0, j)),
        compiler_params=_params("arbitrary"),
        name="ada_mod",
    )(c_all, w_ada, b_ada.reshape(1, n))


def _inproj_kernel(x_ref, shift_ref, scale_ref, nw_ref, w_ref, ws_ref, wst_ref,
                   o_ref, os_ref, ost_ref, h_scr):
    @pl.when(pl.program_id(1) == 0)
    def _():
        h = _rms(x_ref[...], nw_ref[...]) * (1.0 + scale_ref[...]) + shift_ref[...]
        hb = h.astype(BF16)
        h_scr[...] = hb
        os_ref[...] = jnp.dot(hb, ws_ref[...], preferred_element_type=F32)
        ost_ref[...] = lax.dot_general(wst_ref[...], hb, NT_DIMS, preferred_element_type=F32)

    o_ref[...] = jnp.dot(h_scr[...], w_ref[...], preferred_element_type=F32).astype(o_ref.dtype)


def _inproj(x2d, shift, scale, mod_specs, nw, w_main, w_small, w_small_t, tm, out_dtype):
    m, d = x2d.shape
    n = w_main.shape[1]
    tn = INPROJ_TN
    return pl.pallas_call(
        _inproj_kernel,
        out_shape=(jax.ShapeDtypeStruct((m, n), out_dtype),
                   jax.ShapeDtypeStruct((m, LANES), F32),
                   jax.ShapeDtypeStruct((SMALL_ROWS, m), F32)),
        grid=(m // tm, n // tn),
        in_specs=[pl.BlockSpec((tm, d), lambda i, j: (i, 0)),
                  mod_specs[0], mod_specs[1],
                  pl.BlockSpec((1, d), lambda i, j: (0, 0)),
                  pl.BlockSpec((d, tn), lambda i, j: (0, j)),
                  pl.BlockSpec((d, LANES), lambda i, j: (0, 0)),
                  pl.BlockSpec((SMALL_ROWS, d), lambda i, j: (0, 0))],
        out_specs=(pl.BlockSpec((tm, tn), lambda i, j: (i, j)),
                   pl.BlockSpec((tm, LANES), lambda i, j: (i, 0)),
                   pl.BlockSpec((SMALL_ROWS, tm), lambda i, j: (0, i))),
        scratch_shapes=[pltpu.VMEM((tm, d), BF16)],
        compiler_params=_params("parallel", "arbitrary"),
        name="inproj",
    )(x2d, shift, scale, nw, w_main, w_small, w_small_t)


def _gla_stages(ins, s0_ref, sout_ref, st_scr, out, *, chunk, sub, nchunks, nseq, t_valid):
    qk_ref, v_ref, ga_ref, gt_ref, sm_ref, wa2_ref, ba_ref, nw_ref = ins
    has_state = s0_ref is not None
    n = pl.program_id(1)
    masked = t_valid < chunk * nchunks
    units = [(s, h) for s in range(nseq) for h in range(GLA_HEADS)]

    @pl.when(n == 0)
    def _():
        for u, (s, h) in enumerate(units):
            st_scr[u] = s0_ref[s, h].T if has_state else jnp.zeros((GLA_DV, GLA_DK), F32)

    row = lax.broadcasted_iota(jnp.int32, (chunk, chunk), 0)
    col = lax.broadcasted_iota(jnp.int32, (chunk, chunk), 1)
    rowi = lax.broadcasted_iota(jnp.int32, (chunk, 1), 0)
    valid = (rowi + n * chunk) < t_valid
    tri = (row >= col).astype(F32)
    wa2 = wa2_ref[...].astype(BF16)
    b_all = []
    for s in range(nseq):
        ra = sm_ref[s, :, SMALL_RA:SMALL_RA + GLA_GATE_RANK]
        x = jnp.dot(ra.astype(BF16), wa2, preferred_element_type=F32) + ba_ref[...]
        g = (jnp.minimum(x, 0.0) - jnp.log1p(jnp.exp(-jnp.abs(x)))) * (1.0 / GLA_TAU)
        if masked:
            g = jnp.where(valid, g, 0.0)
        b_all.append(jnp.dot(tri, g, precision=HIGHEST, preferred_element_type=F32))
    yield

    q, k, v, b = [], [], [], []
    for s, h in units:
        q.append(qk_ref[s, :, h * GLA_DK:(h + 1) * GLA_DK].astype(F32) * (GLA_DK ** -0.5))
        kh = qk_ref[s, :, GLA_QK + h * GLA_DK:GLA_QK + (h + 1) * GLA_DK].astype(F32)
        k.append(jnp.where(valid, kh, 0.0) if masked else kh)
        v.append(v_ref[s, :, h * GLA_DV:(h + 1) * GLA_DV].astype(BF16))
        b.append(b_all[s][:, h * GLA_DK:(h + 1) * GLA_DK])
    nu = range(len(units))
    st = [st_scr[u] for u in nu]
    o = [lax.dot_general((q[u] * jnp.exp(b[u])).astype(BF16), st[u].astype(BF16), NT_DIMS,
                         preferred_element_type=F32) for u in nu]
    yield
    blocks = [[] for _ in nu]
    for i in range(chunk // sub):
        r0, r1 = i * sub, (i + 1) * sub
        for u in nu:
            bref = b[u][r0 - 1:r0] if i > 0 else jnp.zeros((1, GLA_DK), F32)
            qt = (q[u][r0:r1] * jnp.exp(b[u][r0:r1] - bref)).astype(BF16)
            expo = bref - b[u]
            if r1 < chunk:
                expo = jnp.where(rowi < r1, expo, 0.0)
            kt = (k[u] * jnp.exp(expo)).astype(BF16)
            blocks[u].append(lax.dot_general(qt, kt, NT_DIMS, preferred_element_type=F32))
        yield
    for u in nu:
        a = blocks[u][0] if len(blocks[u]) == 1 else jnp.concatenate(blocks[u], axis=0)
        a = jnp.where(col <= row, a, 0.0)
        o[u] = o[u] + jnp.dot(a.astype(BF16), v[u], preferred_element_type=F32)
    yield
    for u in nu:
        bl = b[u][chunk - 1:chunk]
        kt = (k[u] * jnp.exp(bl - b[u])).astype(BF16)
        st_scr[u] = st[u] * jnp.exp(bl) + lax.dot_general(v[u], kt, TN_DIMS, preferred_element_type=F32)
    yield
    for u, (s, h) in enumerate(units):
        sl = slice(h * GLA_DV, (h + 1) * GLA_DV)
        out[s, h] = (_rms(o[u], nw_ref[...]) * _silu(ga_ref[s, :, sl].astype(F32))
                     * _sigmoid(gt_ref[s, :, sl].astype(F32)))

    @pl.when(n == nchunks - 1)
    def _():
        for u, (s, h) in enumerate(units):
            sout_ref[s, h] = st_scr[u].T


def _gdn_stages(ins, s0_ref, c0_ref, sout_ref, cout_ref, s_scr, xp_scr, out, *, chunk, nchunks, nseq, t_valid):
    qkv_ref, zb_ref, gt_ref, sm_ref, smt_ref, wc_ref, alog_ref, dtb_ref, alogc_ref, dtbc_ref, nw_ref = ins
    has_state = s0_ref is not None
    n = pl.program_id(1)
    masked = t_valid < chunk * nchunks
    pad = SUBLANES
    units = [(s, h) for s in range(nseq) for h in range(GDN_HEADS)]
    nu = range(len(units))

    @pl.when(n == 0)
    def _():
        for u, (s, h) in enumerate(units):
            s_scr[u] = s0_ref[s, h] if has_state else jnp.zeros((GDN_DK, GDN_DV), F32)
        for s in range(nseq):
            xp_scr[s, 0:pad, :] = jnp.zeros((pad, GDN_CONV_DIM), F32)
            if has_state:
                xp_scr[s, pad - (GDN_CONV - 1):pad, :] = c0_ref[s]

    row = lax.broadcasted_iota(jnp.int32, (chunk, chunk), 0)
    col = lax.broadcasted_iota(jnp.int32, (chunk, chunk), 1)
    valid_c = (lax.broadcasted_iota(jnp.int32, (chunk, 1), 0) + n * chunk) < t_valid
    valid_r = (lax.broadcasted_iota(jnp.int32, (1, chunk), 1) + n * chunk) < t_valid
    tri_c = (row >= col).astype(F32)
    tri_r = (row <= col).astype(F32)
    b_col, b_row, beta_col = [], [], []
    for s in range(nseq):
        xp_scr[s, pad:pad + chunk, :] = qkv_ref[s].astype(F32)
        g_col = -jnp.exp(alog_ref[...]) * _softplus(sm_ref[s, :, SMALL_A:SMALL_A + GDN_HEADS] + dtb_ref[...])
        bt = _sigmoid(sm_ref[s, :, SMALL_BETA:SMALL_BETA + GDN_HEADS])
        g_row = -jnp.exp(alogc_ref[...]) * _softplus(smt_ref[s, SMALL_A:SMALL_A + GDN_HEADS, :] + dtbc_ref[...])
        if masked:
            g_col = jnp.where(valid_c, g_col, 0.0)
            bt = jnp.where(valid_c, bt, 0.0)
            g_row = jnp.where(valid_r, g_row, 0.0)
        beta_col.append(bt)
        b_col.append(jnp.dot(tri_c, g_col, precision=HIGHEST, preferred_element_type=F32))
        b_row.append(jnp.dot(g_row, tri_r, precision=HIGHEST, preferred_element_type=F32))
    yield

    def conv_silu(s, c0):
        cs = slice(c0, c0 + LANES)
        y = xp_scr[s, pad:pad + chunk, cs] * wc_ref[GDN_CONV - 1:GDN_CONV, cs]
        for j in range(1, GDN_CONV):
            y = y + xp_scr[s, pad - j:pad - j + chunk, cs] * wc_ref[GDN_CONV - 1 - j:GDN_CONV - j, cs]
        return _silu(y)

    bdot = lambda a, b: jnp.dot(a.astype(BF16), b.astype(BF16), preferred_element_type=F32)
    q, k, v, kb, dec, bc, beta = [], [], [], [], [], [], []
    for u, (s, h) in enumerate(units):
        qh = conv_silu(s, h * GDN_DK)
        kh = conv_silu(s, GDN_QK + h * GDN_DK)
        v.append(conv_silu(s, 2 * GDN_QK + h * GDN_DV))
        q.append(qh * lax.rsqrt(jnp.sum(qh * qh, axis=-1, keepdims=True) + L2_EPS) * (GDN_DK ** -0.5))
        k.append(kh * lax.rsqrt(jnp.sum(kh * kh, axis=-1, keepdims=True) + L2_EPS))
        bc.append(b_col[s][:, h:h + 1])
        beta.append(beta_col[s][:, h:h + 1])
        dec.append(jnp.where(row >= col, jnp.exp(jnp.minimum(bc[u] - b_row[s][h:h + 1, :], 0.0)), 0.0))
        kb.append(k[u] * beta[u])
        if u % GDN_HEADS == GDN_HEADS - 1:
            yield
    kq = [lax.dot_general(jnp.concatenate([kb[u], q[u]], axis=0).astype(BF16), k[u].astype(BF16), NT_DIMS,
                          preferred_element_type=F32) for u in nu]
    yield
    blk = min(GDN_INV_BLOCK, chunk)
    nblk = chunk // blk
    same_blk = (row // blk) == (col // blk)
    lmat = [jnp.where(row > col, kq[u][:chunk] * dec[u], 0.0) for u in nu]
    p = [jnp.where(same_blk, -lmat[u], 0.0) for u in nu]
    r = p
    for _ in range(int(math.log2(blk)) - 1):
        p = [bdot(p[u], p[u]) for u in nu]
        r = [r[u] + p[u] + bdot(r[u], p[u]) for u in nu]
        yield
    if nblk > 1:
        lo = [jnp.where(same_blk, 0.0, lmat[u]) for u in nu]
        p = [-(lo[u] + bdot(r[u], lo[u])) for u in nu]
        qm = p
        yield
        for _ in range(int(math.log2(nblk)) - 1):
            p = [bdot(p[u], p[u]) for u in nu]
            qm = [qm[u] + p[u] + bdot(qm[u], p[u]) for u in nu]
            yield
        r = [r[u] + qm[u] + bdot(qm[u], r[u]) for u in nu]
    rhs = [jnp.concatenate([v[u] * beta[u], kb[u] * jnp.exp(bc[u])], axis=-1) for u in nu]
    uw = [rhs[u] + bdot(r[u], rhs[u]) for u in nu]
    yield
    st = [s_scr[u] for u in nu]
    ws = [bdot(jnp.concatenate([uw[u][:, GDN_DV:], q[u] * jnp.exp(bc[u])], axis=0), st[u]) for u in nu]
    v_new = [uw[u][:, :GDN_DV] - ws[u][:chunk] for u in nu]
    yield
    o = [ws[u][chunk:] + bdot(kq[u][chunk:] * dec[u], v_new[u]) for u in nu]
    for u in nu:
        bl = bc[u][chunk - 1:chunk, :]
        s_scr[u] = jnp.exp(bl) * st[u] + lax.dot_general((k[u] * jnp.exp(bl - bc[u])).astype(BF16),
                                                         v_new[u].astype(BF16), TN_DIMS, preferred_element_type=F32)
    yield
    for u, (s, h) in enumerate(units):
        sl = slice(h * GDN_DV, (h + 1) * GDN_DV)
        out[s, h] = (_rms(o[u], nw_ref[...]) * _silu(zb_ref[s, :, sl].astype(F32))
                     * _sigmoid(gt_ref[s, :, sl].astype(F32)))

    @pl.when(n == nchunks - 1)
    def _():
        last = t_valid - (nchunks - 1) * chunk
        for s in range(nseq):
            cout_ref[s] = xp_scr[s, pad + last - (GDN_CONV - 1):pad + last, :]
        for u, (s, h) in enumerate(units):
            sout_ref[s, h] = s_scr[u]

    for s in range(nseq):
        xp_scr[s, 0:pad, :] = xp_scr[s, chunk:chunk + pad, :]


N_GLA_IN, N_GDN_IN = 8, 11


def _mixer_kernel(*refs, chunk, sub, nchunks, nseq, has_state, t_valid):
    gla_in, refs = refs[:N_GLA_IN], refs[N_GLA_IN:]
    gdn_in, refs = refs[:N_GDN_IN], refs[N_GDN_IN:]
    if has_state:
        (sa0_ref, sb0_ref, c0_ref), refs = refs[:3], refs[3:]
    else:
        sa0_ref = sb0_ref = c0_ref = None
    o_ref, sa_out, sb_out, c_out, sa_scr, sb_scr, xp_scr = refs
    out_a, out_b = {}, {}
    gla = _gla_stages(gla_in, sa0_ref, sa_out, sa_scr, out_a,
                      chunk=chunk, sub=sub, nchunks=nchunks, nseq=nseq, t_valid=t_valid)
    gdn = _gdn_stages(gdn_in, sb0_ref, c0_ref, sb_out, c_out, sb_scr, xp_scr, out_b,
                      chunk=chunk, nchunks=nchunks, nseq=nseq, t_valid=t_valid)
    live = [gdn, gla]
    while live:
        for g in list(live):
            if next(g, StopIteration) is StopIteration:
                live.remove(g)
    per = GLA_DV // GDN_DV
    for (s, h), ob in out_b.items():
        oa = out_a[s, h // per][:, (h % per) * GDN_DV:(h % per + 1) * GDN_DV]
        o_ref[s, :, h * GDN_DV:(h + 1) * GDN_DV] = (oa + ob).astype(o_ref.dtype)


def _mixer(proj3, small3, small_t4, w, s_gla, s_gdn, s_conv, *, chunk, sub, nseq, t_valid, out_dtype):
    nb, t_pad, _ = proj3.shape
    nchunks = t_pad // chunk
    has_state = s_gla is not None
    blk = lambda width, c: pl.BlockSpec((nseq, chunk, width), lambda b, n, c=c: (b, n, c))
    full = lambda a: pl.BlockSpec(a.shape, lambda b, n: (0,) * a.ndim)
    alog_r, dtb_r = w["a_log"].reshape(1, GDN_HEADS), w["dt_bias"].reshape(1, GDN_HEADS)
    alog_c, dtb_c = w["a_log"].reshape(GDN_HEADS, 1), w["dt_bias"].reshape(GDN_HEADS, 1)
    small_spec = blk(LANES, 0)
    gla_consts = [w["wa2"], w["ba"], w["gla_norm"]]
    gdn_consts = [w["w_conv"], alog_r, dtb_r, alog_c, dtb_c, w["gdn_norm"]]
    in_specs = ([blk(1024, 0), blk(1024, 1), blk(1024, 2), blk(1024, 7), small_spec] + [full(a) for a in gla_consts]
                + [blk(GDN_CONV_DIM, 1), blk(1024, 6), blk(1024, 8), small_spec,
                   pl.BlockSpec((nseq, None, SMALL_ROWS, chunk), lambda b, n: (b, n, 0, 0))]
                + [full(a) for a in gdn_consts])
    args = [proj3] * 4 + [small3] + gla_consts + [proj3] * 3 + [small3, small_t4] + gdn_consts
    assert len(in_specs) == N_GLA_IN + N_GDN_IN
    sa_spec = pl.BlockSpec((nseq, GLA_HEADS, GLA_DK, GLA_DV), lambda b, n: (b, 0, 0, 0))
    sb_spec = pl.BlockSpec((nseq, GDN_HEADS, GDN_DK, GDN_DV), lambda b, n: (b, 0, 0, 0))
    conv_spec = pl.BlockSpec((nseq, GDN_CONV - 1, GDN_CONV_DIM), lambda b, n: (b, 0, 0))
    if has_state:
        in_specs += [sa_spec, sb_spec, conv_spec]
        args += [s_gla, s_gdn, s_conv]
    return pl.pallas_call(
        functools.partial(_mixer_kernel, chunk=chunk, sub=sub, nchunks=nchunks, nseq=nseq, has_state=has_state,
                          t_valid=t_valid),
        out_shape=(jax.ShapeDtypeStruct((nb, t_pad, D_MODEL), out_dtype),
                   jax.ShapeDtypeStruct((nb, GLA_HEADS, GLA_DK, GLA_DV), F32),
                   jax.ShapeDtypeStruct((nb, GDN_HEADS, GDN_DK, GDN_DV), F32),
                   jax.ShapeDtypeStruct((nb, GDN_CONV - 1, GDN_CONV_DIM), F32)),
        grid=(nb // nseq, nchunks),
        in_specs=in_specs,
        out_specs=(blk(D_MODEL, 0), sa_spec, sb_spec, conv_spec),
        scratch_shapes=[pltpu.VMEM((nseq * GLA_HEADS, GLA_DV, GLA_DK), F32),
                        pltpu.VMEM((nseq * GDN_HEADS, GDN_DK, GDN_DV), F32),
                        pltpu.VMEM((nseq, chunk + 2 * SUBLANES, GDN_CONV_DIM), F32)],
        compiler_params=_params("parallel", "arbitrary"),
        name="mixer",
    )(*args)


def _post_kernel(mg_ref, x_ref, gate_ref, shift_ref, scale_ref, wo_ref, nw_ref, wrt_ref, brt_ref,
                 x1_ref, hx_ref, gid_ref):
    y = jnp.dot(mg_ref[...].astype(BF16), wo_ref[...], preferred_element_type=F32)
    x1 = x_ref[...] + gate_ref[...] * y
    x1_ref[...] = x1
    h2 = _rms(x1, nw_ref[...]) * (1.0 + scale_ref[...]) + shift_ref[...]
    hx_ref[:, 0:D_MODEL] = h2
    lt = lax.dot_general(wrt_ref[...], h2.astype(BF16), NT_DIMS, preferred_element_type=F32) + brt_ref[...]
    tm = lt.shape[1]
    gl = lt[0:N_GROUPS]
    gidx = lax.broadcasted_iota(jnp.int32, (N_GROUPS, tm), 0)
    gmax = jnp.max(gl, axis=0, keepdims=True)
    g_w = 1.0 / jnp.sum(jnp.exp(gl - gmax), axis=0, keepdims=True)
    g_sel = jnp.min(jnp.where(gl == gmax, gidx, N_GROUPS), axis=0, keepdims=True)
    el = jnp.zeros((EXPERTS_PER_GROUP, tm), F32)
    for g in range(N_GROUPS):
        r = ROUTER_GROUP_STRIDE * (1 + g)
        el = el + jnp.where(g_sel == g, lt[r:r + EXPERTS_PER_GROUP], 0.0)
    eidx = lax.broadcasted_iota(jnp.int32, (EXPERTS_PER_GROUP, tm), 0)
    m1 = jnp.max(el, axis=0, keepdims=True)
    i1 = jnp.min(jnp.where(el == m1, eidx, EXPERTS_PER_GROUP), axis=0, keepdims=True)
    el2 = jnp.where(eidx == i1, -jnp.inf, el)
    m2 = jnp.max(el2, axis=0, keepdims=True)
    i2 = jnp.min(jnp.where(el2 == m2, eidx, EXPERTS_PER_GROUP), axis=0, keepdims=True)
    r21 = jnp.exp(m2 - m1)
    w1 = 1.0 / (1.0 + r21)
    w2 = r21 / (1.0 + r21)
    comb_t = g_w * (jnp.where(eidx == i1, w1, 0.0) + jnp.where(eidx == i2, w2, 0.0))
    ident = (lax.broadcasted_iota(jnp.int32, (EXPERTS_PER_GROUP, LANES), 0)
             == lax.broadcasted_iota(jnp.int32, (EXPERTS_PER_GROUP, LANES), 1)).astype(F32)
    hx_ref[:, D_MODEL:] = lax.dot_general(comb_t, ident, TN_DIMS, precision=HIGHEST, preferred_element_type=F32)
    lo, hi = jnp.minimum(i1, i2), jnp.maximum(i1, i2)
    pair = jnp.right_shift(lo * (2 * EXPERTS_PER_GROUP - 1 - lo), 1) + hi - lo - 1
    rid = lax.broadcasted_iota(jnp.int32, gid_ref.shape, 0)
    gid_ref[...] = jnp.where(rid == 0, g_sel, g_sel * len(EXPERT_PAIRS) + pair)


def _post(merged, x2d, gate, shift, scale, mod_specs, wo, nw, wrt, brt, tm):
    m, d = x2d.shape
    row = lambda: pl.BlockSpec((tm, d), lambda i: (i, 0))
    full = lambda a: pl.BlockSpec(a.shape, lambda i: (0,) * a.ndim)
    return pl.pallas_call(
        _post_kernel,
        out_shape=(jax.ShapeDtypeStruct((m, d), F32),
                   jax.ShapeDtypeStruct((m, HX_WIDTH), F32),
                   jax.ShapeDtypeStruct((SUBLANES, m), jnp.int32)),
        grid=(m // tm,),
        in_specs=[row(), row(), mod_specs[0], mod_specs[1], mod_specs[2],
                  full(wo), full(nw), full(wrt), full(brt)],
        out_specs=(row(), pl.BlockSpec((tm, HX_WIDTH), lambda i: (i, 0)),
                   pl.BlockSpec((SUBLANES, tm), lambda i: (0, i))),
        compiler_params=_params("parallel"),
        name="post_mixer",
    )(merged, x2d, gate, shift, scale, wo, nw, wrt, brt)


def _route(cls, ncls, tm):
    m = cls.shape[0]
    ntiles_max = m // tm + ncls - 1
    onehot = (cls[:, None] == jnp.arange(ncls, dtype=jnp.int32)[None, :]).astype(jnp.int32)
    incl = jnp.cumsum(onehot, axis=0)
    tiles = (incl[-1] + tm - 1) // tm
    tile_end = jnp.cumsum(tiles)
    pos = jnp.sum(onehot * ((tile_end - tiles) * tm + incl - onehot), axis=1)
    src = jnp.zeros((ntiles_max * tm,), jnp.int32).at[pos].set(jnp.arange(m, dtype=jnp.int32))
    t = jnp.arange(ntiles_max, dtype=jnp.int32)
    tile_class = jnp.minimum(jnp.sum((t[:, None] >= tile_end[None, :]).astype(jnp.int32), axis=1), ncls - 1)
    return pos, src, tile_class, tile_end[-1:]


def _row_gather(idx_ref, src_hbm, buf, sem, slot, rows):
    for r in range(rows):
        pltpu.make_async_copy(src_hbm.at[pl.ds(idx_ref[0, r], 1), :], buf.at[slot, pl.ds(r, 1), :],
                              sem.at[slot]).start()


def _row_gather_wait(src_hbm, buf, sem, slot, rows):
    pltpu.make_async_copy(src_hbm.at[pl.ds(0, rows), :], buf.at[slot], sem.at[slot]).wait()


def _experts_kernel(meta_ref, nt_ref, src_ref, srcn_ref, hx_hbm, *rest, tm, nslot):
    w_refs, (y_ref, xbuf, sem) = rest[:3 * nslot], rest[3 * nslot:]
    t = pl.program_id(0)
    nt = nt_ref[0]
    slot = t % 2

    @pl.when(t == 0)
    def _():
        _row_gather(src_ref, hx_hbm, xbuf, sem, 0, tm)

    @pl.when(t < nt)
    def _():
        _row_gather_wait(hx_hbm, xbuf, sem, slot, tm)
        _row_gather(srcn_ref, hx_hbm, xbuf, sem, 1 - slot, tm)
        x = xbuf[slot, :, 0:D_MODEL].astype(BF16)
        acc = jnp.zeros((tm, D_MODEL), F32)
        for j in range(nslot):
            wg_ref, wu_ref, wd_ref = w_refs[3 * j:3 * j + 3]
            e = meta_ref[t * (1 + nslot) + 1 + j]
            a = jnp.dot(x, wg_ref[0, 0], preferred_element_type=F32)
            u = jnp.dot(x, wu_ref[0, 0], preferred_element_type=F32)
            cw = jnp.zeros((tm, 1), F32)
            for c in range(EXPERTS_PER_GROUP):
                cw = cw + jnp.where(e == c, xbuf[slot, :, D_MODEL + c:D_MODEL + c + 1], 0.0)
            acc = acc + jnp.dot((_silu(a) * u * cw).astype(BF16), wd_ref[0, 0], preferred_element_type=F32)
        y_ref[...] = acc

        @pl.when(t == nt - 1)
        def _():
            _row_gather_wait(hx_hbm, xbuf, sem, 1 - slot, tm)

    @pl.when(t >= nt)
    def _():
        y_ref[...] = jnp.zeros_like(y_ref)


def _experts(hx, src, tile_class, ntiles, class_table, wg, wu, wd, tm):
    ntiles_max = tile_class.shape[0]
    nslot = class_table.shape[1] - 1
    d, f = D_MODEL, D_EXPERT
    src3 = src.reshape(ntiles_max, 1, tm)
    meta = jnp.asarray(class_table, jnp.int32)[tile_class].reshape(-1)
    stride = 1 + nslot
    wspec = lambda j, shape: pl.BlockSpec(
        (1, 1) + shape, lambda t, meta, nt, j=j: (meta[t * stride], meta[t * stride + 1 + j], 0, 0))
    w_specs, w_args = [], []
    for j in range(nslot):
        w_specs += [wspec(j, (d, f)), wspec(j, (d, f)), wspec(j, (f, d))]
        w_args += [wg, wu, wd]
    return pl.pallas_call(
        functools.partial(_experts_kernel, tm=tm, nslot=nslot),
        out_shape=jax.ShapeDtypeStruct((ntiles_max * tm, d), F32),
        grid_spec=pltpu.PrefetchScalarGridSpec(
            num_scalar_prefetch=2,
            grid=(ntiles_max,),
            in_specs=[pl.BlockSpec((None, 1, tm), lambda t, meta, nt: (t, 0, 0), memory_space=pltpu.SMEM),
                      pl.BlockSpec((None, 1, tm),
                                   lambda t, meta, nt: (jnp.maximum(jnp.minimum(t + 1, nt[0] - 1), 0), 0, 0),
                                   memory_space=pltpu.SMEM),
                      pl.BlockSpec(memory_space=pl.ANY)] + w_specs,
            out_specs=pl.BlockSpec((tm, d), lambda t, meta, nt: (t, 0)),
            scratch_shapes=[pltpu.VMEM((2, tm, HX_WIDTH), F32), pltpu.SemaphoreType.DMA((2,))]),
        compiler_params=_params("arbitrary"),
        name="experts",
    )(meta, ntiles, src3, src3, hx, *w_args)


def _final_kernel(pos_ref, posn_ref, ys_hbm, x1_ref, gate_ref, nw_ref, o_ref, ybuf, sem, *, tm, nsteps):
    i = pl.program_id(0)
    slot = i % 2

    @pl.when(i == 0)
    def _():
        _row_gather(pos_ref, ys_hbm, ybuf, sem, 0, tm)

    _row_gather(posn_ref, ys_hbm, ybuf, sem, 1 - slot, tm)
    _row_gather_wait(ys_hbm, ybuf, sem, slot, tm)
    o_ref[...] = _rms(x1_ref[...] + gate_ref[...] * ybuf[slot], nw_ref[...])

    @pl.when(i == nsteps - 1)
    def _():
        _row_gather_wait(ys_hbm, ybuf, sem, 1 - slot, tm)


def _final(ys, pos, x1, gate, gate_spec, nw, tm):
    m, d = x1.shape
    nsteps = m // tm
    pos3 = pos.reshape(nsteps, 1, tm)
    return pl.pallas_call(
        functools.partial(_final_kernel, tm=tm, nsteps=nsteps),
        out_shape=jax.ShapeDtypeStruct((m, d), F32),
        grid=(nsteps,),
        in_specs=[pl.BlockSpec((None, 1, tm), lambda i: (i, 0, 0), memory_space=pltpu.SMEM),
                  pl.BlockSpec((None, 1, tm), lambda i: (jnp.minimum(i + 1, nsteps - 1), 0, 0),
                               memory_space=pltpu.SMEM),
                  pl.BlockSpec(memory_space=pl.ANY),
                  pl.BlockSpec((tm, d), lambda i: (i, 0)),
                  gate_spec,
                  pl.BlockSpec((1, d), lambda i: (0, 0))],
        out_specs=pl.BlockSpec((tm, d), lambda i: (i, 0)),
        scratch_shapes=[pltpu.VMEM((2, tm, d), F32), pltpu.SemaphoreType.DMA((2,))],
        compiler_params=_params("arbitrary"),
        name="final",
    )(pos3, pos3, ys, x1, gate, nw)


def _mod_spec_batch(idx, tm, t, ngrid):
    per = t // tm
    if ngrid == 2:
        return pl.BlockSpec((None, None, 1, D_MODEL), lambda i, j: (i // per, idx, 0, 0))
    return pl.BlockSpec((None, None, 1, D_MODEL), lambda i: (i // per, idx, 0, 0))


def _mod_spec_rows(tm, ngrid):
    if ngrid == 2:
        return pl.BlockSpec((tm, D_MODEL), lambda i, j: (i, 0))
    return pl.BlockSpec((tm, D_MODEL), lambda i: (i, 0))


def _trunk(x, mod, s_gla, s_gdn, s_conv, w, *, chunk, sub, nseq, t_valid, tm, tm_moe, pair_classes, act_dtype):
    nb, t_pad, d = x.shape
    m = nb * t_pad
    x2d = x.reshape(m, d)
    if t_pad % tm == 0:
        mod4 = mod.reshape(nb, N_MOD, 1, d)
        mods = [mod4] * N_MOD
        spec = lambda idx, ngrid: _mod_spec_batch(idx, tm, t_pad, ngrid)
    else:
        mods = [jnp.repeat(mod[:, i], t_pad, axis=0) for i in range(N_MOD)]
        spec = lambda idx, ngrid: _mod_spec_rows(tm, ngrid)

    proj, small, small_t = _inproj(x2d, mods[0], mods[1], (spec(0, 2), spec(1, 2)), w["norm1"],
                                   w["w_main"], w["w_small"], w["w_small_t"], tm, act_dtype)
    proj3 = proj.reshape(nb, t_pad, PROJ_MAIN)
    small3 = small.reshape(nb, t_pad, LANES)
    small_t4 = small_t.reshape(SMALL_ROWS, nb, t_pad // chunk, chunk).transpose(1, 2, 0, 3)
    merged, new_gla, new_gdn, new_conv = _mixer(proj3, small3, small_t4, w, s_gla, s_gdn, s_conv, chunk=chunk,
                                                sub=sub, nseq=nseq, t_valid=t_valid, out_dtype=act_dtype)
    x1, hx, gid = _post(merged.reshape(m, d), x2d, mods[2], mods[3], mods[4],
                        (spec(2, 1), spec(3, 1), spec(4, 1)),
                        w["w_out"], w["norm2"], w["w_router_t"], w["b_router_t"], tm)
    class_table = PAIR_CLASSES if pair_classes else GROUP_CLASSES
    pos, src, tile_class, ntiles = _route(gid[1 if pair_classes else 0], class_table.shape[0], tm_moe)
    ys = _experts(hx, src, tile_class, ntiles, class_table, w["w_gate"], w["w_up"], w["w_down"], tm_moe)
    tm_fin = min(tm, FINAL_TILE)
    gate2_spec = _mod_spec_batch(5, tm_fin, t_pad, 1) if t_pad % tm == 0 else _mod_spec_rows(tm_fin, 1)
    y = _final(ys, pos, x1, mods[5], gate2_spec, w["final_norm"], tm_fin)
    return y.reshape(nb, t_pad, d), new_gla, new_gdn, new_conv


def _prep_weights(w_in, w_gla_a2, b_gla_a, gla_norm_w, w_conv, gdn_A_log, gdn_dt_bias, gdn_norm_w, w_out,
                  norm1_w, norm2_w, w_group_router, b_group_router, w_expert_router, b_expert_router,
                  w_exp_gate, w_exp_up, w_exp_down, final_norm_w):
    d = D_MODEL
    o = 0
    cols = {}
    for name, width in (("gla", 2 * GLA_QK + 2 * GLA_VW), ("ra", GLA_GATE_RANK), ("qkv", GDN_CONV_DIM),
                        ("zb", GDN_VW), ("beta", GDN_HEADS), ("a", GDN_HEADS), ("gates", 2 * D_MODEL)):
        cols[name] = w_in[:, o:o + width]
        o += width
    w_main = jnp.concatenate([cols[c].astype(BF16) for c in ("gla", "qkv", "zb", "gates")], axis=1)
    small = jnp.concatenate([cols["ra"], cols["beta"], cols["a"]], axis=1)
    w_small = jnp.pad(small, ((0, 0), (0, LANES - small.shape[1]))).astype(BF16)
    w_small_t = small.T.astype(BF16)
    wr_t = jnp.zeros((ROUTER_ROWS, d), F32)
    wr_t = wr_t.at[0:N_GROUPS].set(w_group_router.T)
    br_t = jnp.zeros((ROUTER_ROWS, 1), F32)
    br_t = br_t.at[0:N_GROUPS, 0].set(b_group_router)
    for g in range(N_GROUPS):
        r = ROUTER_GROUP_STRIDE * (1 + g)
        es = slice(g * EXPERTS_PER_GROUP, (g + 1) * EXPERTS_PER_GROUP)
        wr_t = wr_t.at[r:r + EXPERTS_PER_GROUP].set(w_expert_router[:, es].T)
        br_t = br_t.at[r:r + EXPERTS_PER_GROUP, 0].set(b_expert_router[es])
    return dict(
        w_main=w_main, w_small=w_small, w_small_t=w_small_t,
        norm1=norm1_w.reshape(1, d), norm2=norm2_w.reshape(1, d), final_norm=final_norm_w.reshape(1, d),
        wa2=w_gla_a2, ba=b_gla_a.reshape(1, GLA_QK), gla_norm=gla_norm_w.reshape(1, GLA_DV),
        w_conv=w_conv, a_log=gdn_A_log, dt_bias=gdn_dt_bias, gdn_norm=gdn_norm_w.reshape(1, GDN_DV),
        w_out=w_out.astype(BF16), w_router_t=wr_t.astype(BF16), b_router_t=br_t,
        w_gate=w_exp_gate.astype(BF16), w_up=w_exp_up.astype(BF16), w_down=w_exp_down.astype(BF16),
    )


def kernel(x_prompt, x_sample, c_prompt, c_sample, state_gla, state_gdn, state_conv, w_ada, b_ada, norm1_w, w_in, w_gla_a2, b_gla_a, gla_norm_w, w_conv, gdn_A_log, gdn_dt_bias, gdn_norm_w, w_out, norm2_w, w_group_router, b_group_router, w_expert_router, b_expert_router, w_exp_gate, w_exp_up, w_exp_down, final_norm_w):
    assert w_ada.shape[0] == 1, "single layer"
    bp, tp, d = x_prompt.shape
    bs, ts, _ = x_sample.shape
    w = _prep_weights(w_in[0], w_gla_a2[0], b_gla_a[0], gla_norm_w[0], w_conv[0], gdn_A_log[0], gdn_dt_bias[0],
                      gdn_norm_w[0], w_out[0], norm1_w[0], norm2_w[0], w_group_router[0], b_group_router[0],
                      w_expert_router[0], b_expert_router[0], w_exp_gate[0], w_exp_up[0], w_exp_down[0],
                      final_norm_w)
    mod = _ada_mod(jnp.concatenate([c_prompt, c_sample], axis=0), w_ada[0], b_ada[0]).reshape(bp + bs, N_MOD, d)

    y_p, gla_p, gdn_p, conv_p = _trunk(x_prompt, mod[:bp], None, None, None, w,
                                       chunk=64, sub=GLA_SUBCHUNK, nseq=2, t_valid=tp, tm=min(1024, tp),
                                       tm_moe=min(512, tp), pair_classes=False, act_dtype=BF16)
    ts_pad = SUBLANES
    xs = jnp.pad(x_sample, ((0, 0), (0, ts_pad - ts), (0, 0)))
    y_s, gla_s, gdn_s, conv_s = _trunk(xs, mod[bp:], state_gla[0], state_gdn[0], state_conv[0], w,
                                       chunk=ts_pad, sub=ts_pad, nseq=4, t_valid=ts, tm=min(512, bs * ts_pad),
                                       tm_moe=min(256, bs * ts_pad), pair_classes=False, act_dtype=F32)
    return (y_p, y_s[:, :ts], gla_p[None], gdn_p[None], conv_p[None], gla_s[None], gdn_s[None], conv_s[None])
```

```python
import functools
import itertools
import math

import jax
import numpy as np
import jax.numpy as jnp
from jax import lax
from jax.experimental import pallas as pl
from jax.experimental.pallas import tpu as pltpu

F32 = jnp.float32
BF16 = jnp.bfloat16
HIGHEST = lax.Precision.HIGHEST

D_MODEL = 1024
GLA_HEADS = 4
GLA_DK = 128
GLA_DV = 256
GLA_QK = GLA_HEADS * GLA_DK
GLA_VW = GLA_HEADS * GLA_DV
GLA_GATE_RANK = 16
GLA_TAU = 16.0
GLA_SUBCHUNK = 16
GDN_HEADS = 8
GDN_DK = 128
GDN_DV = 128
GDN_QK = GDN_HEADS * GDN_DK
GDN_VW = GDN_HEADS * GDN_DV
GDN_CONV = 4
GDN_CONV_DIM = 2 * GDN_QK + GDN_VW
GDN_INV_BLOCK = 16
N_GROUPS = 4
EXPERTS_PER_GROUP = 4
N_EXPERTS = N_GROUPS * EXPERTS_PER_GROUP
D_EXPERT = D_MODEL // 2
N_MOD = 6
NORM_EPS = 1e-6
L2_EPS = 1e-6

LANES = 128
SUBLANES = 8
VMEM_LIMIT = 56 * 1024 * 1024

PROJ_MAIN = 2 * GLA_QK + 2 * GLA_VW + GDN_CONV_DIM + GDN_VW + 2 * D_MODEL
PROJ_PART = 1024
INPROJ_TN = 3 * PROJ_PART
COL_QK, COL_V, COL_GA = 0, 3, 6
COL_CQ, COL_CK, COL_CV = 1, 4, 7
COL_ZB, COL_GATE_A, COL_GATE_B = 2, 5, 8
SMALL_RA, SMALL_BETA, SMALL_A = 0, GLA_GATE_RANK, GLA_GATE_RANK + GDN_HEADS
SMALL_ROWS = 32
ROUTER_GROUP_STRIDE = 8
ROUTER_ROWS = ROUTER_GROUP_STRIDE * (1 + N_GROUPS)

EXPERT_PAIRS = tuple(itertools.combinations(range(EXPERTS_PER_GROUP), 2))
GROUP_CLASSES = np.array([[g] + list(range(EXPERTS_PER_GROUP)) for g in range(N_GROUPS)], np.int32)
PAIR_CLASSES = np.array([[g, a, b] for g in range(N_GROUPS) for a, b in EXPERT_PAIRS], np.int32)

HX_WIDTH = D_MODEL + LANES
FINAL_TILE = 256

NT_DIMS = (((1,), (1,)), ((), ()))
TN_DIMS = (((0,), (0,)), ((), ()))


def _sigmoid(x):
    return 0.5 * jnp.tanh(0.5 * x) + 0.5


def _silu(x):
    return x * _sigmoid(x)


def _softplus(x):
    return jnp.maximum(x, 0.0) + jnp.log1p(jnp.exp(-jnp.abs(x)))


def _rms(x, w):
    return x * lax.rsqrt(jnp.mean(x * x, axis=-1, keepdims=True) + NORM_EPS) * w


def _params(*sem):
    return pltpu.CompilerParams(dimension_semantics=sem, vmem_limit_bytes=VMEM_LIMIT)


def _ada_kernel(c_ref, w_ref, b_ref, o_ref):
    cs = _silu(c_ref[...])
    o_ref[...] = jnp.dot(cs.astype(BF16), w_ref[...].astype(BF16), preferred_element_type=F32) + b_ref[...]


def _ada_mod(c_all, w_ada, b_ada):
    rows, d = c_all.shape
    n = w_ada.shape[1]
    tn = 1024
    return pl.pallas_call(
        _ada_kernel,
        out_shape=jax.ShapeDtypeStruct((rows, n), F32),
        grid=(n // tn,),
        in_specs=[pl.BlockSpec((rows, d), lambda j: (0, 0)),
                  pl.BlockSpec((d, tn), lambda j: (0, j)),
                  pl.BlockSpec((1, tn), lambda j: (0, j))],
        out_specs=pl.BlockSpec((rows, tn), lambda j: (0, j)),
        compiler_params=_params("arbitrary"),
        name="ada_mod",
    )(c_all, w_ada, b_ada.reshape(1, n))


def _inproj_kernel(x_ref, shift_ref, scale_ref, nw_ref, w_ref, ws_ref, wst_ref,
                   o_ref, os_ref, ost_ref, h_scr):
    @pl.when(pl.program_id(1) == 0)
    def _():
        h = _rms(x_ref[...], nw_ref[...]) * (1.0 + scale_ref[...]) + shift_ref[...]
        hb = h.astype(BF16)
        h_scr[...] = hb
        os_ref[...] = jnp.dot(hb, ws_ref[...], preferred_element_type=F32)
        ost_ref[...] = lax.dot_general(wst_ref[...], hb, NT_DIMS, preferred_element_type=F32)

    o_ref[...] = jnp.dot(h_scr[...], w_ref[...], preferred_element_type=F32).astype(o_ref.dtype)


def _inproj(x2d, shift, scale, mod_specs, nw, w_main, w_small, w_small_t, tm, out_dtype):
    m, d = x2d.shape
    n = w_main.shape[1]
    tn = INPROJ_TN
    return pl.pallas_call(
        _inproj_kernel,
        out_shape=(jax.ShapeDtypeStruct((m, n), out_dtype),
                   jax.ShapeDtypeStruct((m, LANES), F32),
                   jax.ShapeDtypeStruct((SMALL_ROWS, m), F32)),
        grid=(m // tm, n // tn),
        in_specs=[pl.BlockSpec((tm, d), lambda i, j: (i, 0)),
                  mod_specs[0], mod_specs[1],
                  pl.BlockSpec((1, d), lambda i, j: (0, 0)),
                  pl.BlockSpec((d, tn), lambda i, j: (0, j)),
                  pl.BlockSpec((d, LANES), lambda i, j: (0, 0)),
                  pl.BlockSpec((SMALL_ROWS, d), lambda i, j: (0, 0))],
        out_specs=(pl.BlockSpec((tm, tn), lambda i, j: (i, j)),
                   pl.BlockSpec((tm, LANES), lambda i, j: (i, 0)),
                   pl.BlockSpec((SMALL_ROWS, tm), lambda i, j: (0, i))),
        scratch_shapes=[pltpu.VMEM((tm, d), BF16)],
        compiler_params=_params("parallel", "arbitrary"),
        name="inproj",
    )(x2d, shift, scale, nw, w_main, w_small, w_small_t)


def _gla_stages(ins, s0_ref, sout_ref, st_scr, out, *, chunk, sub, nchunks, nseq, t_valid):
    qk_ref, v_ref, ga_ref, gt_ref, sm_ref, wa2_ref, ba_ref, nw_ref = ins
    has_state = s0_ref is not None
    n = pl.program_id(1)
    masked = t_valid < chunk * nchunks
    units = [(s, h) for s in range(nseq) for h in range(GLA_HEADS)]

    @pl.when(n == 0)
    def _():
        for u, (s, h) in enumerate(units):
            st_scr[u] = s0_ref[s, h].T if has_state else jnp.zeros((GLA_DV, GLA_DK), F32)

    row = lax.broadcasted_iota(jnp.int32, (chunk, chunk), 0)
    col = lax.broadcasted_iota(jnp.int32, (chunk, chunk), 1)
    rowi = lax.broadcasted_iota(jnp.int32, (chunk, 1), 0)
    valid = (rowi + n * chunk) < t_valid
    tri = (row >= col).astype(F32)
    wa2 = wa2_ref[...].astype(BF16)
    b_all = []
    for s in range(nseq):
        ra = sm_ref[s, :, SMALL_RA:SMALL_RA + GLA_GATE_RANK]
        x = jnp.dot(ra.astype(BF16), wa2, preferred_element_type=F32) + ba_ref[...]
        g = (jnp.minimum(x, 0.0) - jnp.log1p(jnp.exp(-jnp.abs(x)))) * (1.0 / GLA_TAU)
        if masked:
            g = jnp.where(valid, g, 0.0)
        b_all.append(jnp.dot(tri, g, precision=HIGHEST, preferred_element_type=F32))
    yield

    q, k, v, b = [], [], [], []
    for s, h in units:
        q.append(qk_ref[s, :, h * GLA_DK:(h + 1) * GLA_DK].astype(F32) * (GLA_DK ** -0.5))
        kh = qk_ref[s, :, GLA_QK + h * GLA_DK:GLA_QK + (h + 1) * GLA_DK].astype(F32)
        k.append(jnp.where(valid, kh, 0.0) if masked else kh)
        v.append(v_ref[s, :, h * GLA_DV:(h + 1) * GLA_DV].astype(BF16))
        b.append(b_all[s][:, h * GLA_DK:(h + 1) * GLA_DK])
    nu = range(len(units))
    st = [st_scr[u] for u in nu]
    o = [lax.dot_general((q[u] * jnp.exp(b[u])).astype(BF16), st[u].astype(BF16), NT_DIMS,
                         preferred_element_type=F32) for u in nu]
    yield
    blocks = [[] for _ in nu]
    for i in range(chunk // sub):
        r0, r1 = i * sub, (i + 1) * sub
        for u in nu:
            bref = b[u][r0 - 1:r0] if i > 0 else jnp.zeros((1, GLA_DK), F32)
            qt = (q[u][r0:r1] * jnp.exp(b[u][r0:r1] - bref)).astype(BF16)
            expo = bref - b[u]
            if r1 < chunk:
                expo = jnp.where(rowi < r1, expo, 0.0)
            kt = (k[u] * jnp.exp(expo)).astype(BF16)
            blocks[u].append(lax.dot_general(qt, kt, NT_DIMS, preferred_element_type=F32))
        yield
    for u in nu:
        a = blocks[u][0] if len(blocks[u]) == 1 else jnp.concatenate(blocks[u], axis=0)
        a = jnp.where(col <= row, a, 0.0)
        o[u] = o[u] + jnp.dot(a.astype(BF16), v[u], preferred_element_type=F32)
    yield
    for u in nu:
        bl = b[u][chunk - 1:chunk]
        kt = (k[u] * jnp.exp(bl - b[u])).astype(BF16)
        st_scr[u] = st[u] * jnp.exp(bl) + lax.dot_general(v[u], kt, TN_DIMS, preferred_element_type=F32)
    yield
    for u, (s, h) in enumerate(units):
        sl = slice(h * GLA_DV, (h + 1) * GLA_DV)
        out[s, h] = (_rms(o[u], nw_ref[...]) * _silu(ga_ref[s, :, sl].astype(F32))
                     * _sigmoid(gt_ref[s, :, sl].astype(F32)))

    @pl.when(n == nchunks - 1)
    def _():
        for u, (s, h) in enumerate(units):
            sout_ref[s, h] = st_scr[u].T


def _gdn_stages(ins, s0_ref, c0_ref, sout_ref, cout_ref, s_scr, xp_scr, out, *, chunk, nchunks, nseq, t_valid):
    (cq_ref, ck_ref, cv_ref, zb_ref, gt_ref, sm_ref, smt_ref, wc_ref, alog_ref, dtb_ref, alogc_ref, dtbc_ref,
     nw_ref) = ins
    has_state = s0_ref is not None
    n = pl.program_id(1)
    masked = t_valid < chunk * nchunks
    pad = SUBLANES
    units = [(s, h) for s in range(nseq) for h in range(GDN_HEADS)]
    nu = range(len(units))

    @pl.when(n == 0)
    def _():
        for u, (s, h) in enumerate(units):
            s_scr[u] = s0_ref[s, h] if has_state else jnp.zeros((GDN_DK, GDN_DV), F32)
        for s in range(nseq):
            xp_scr[s, 0:pad, :] = jnp.zeros((pad, GDN_CONV_DIM), F32)
            if has_state:
                xp_scr[s, pad - (GDN_CONV - 1):pad, :] = c0_ref[s]

    row = lax.broadcasted_iota(jnp.int32, (chunk, chunk), 0)
    col = lax.broadcasted_iota(jnp.int32, (chunk, chunk), 1)
    valid_c = (lax.broadcasted_iota(jnp.int32, (chunk, 1), 0) + n * chunk) < t_valid
    valid_r = (lax.broadcasted_iota(jnp.int32, (1, chunk), 1) + n * chunk) < t_valid
    tri_c = (row >= col).astype(F32)
    tri_r = (row <= col).astype(F32)
    b_col, b_row, beta_col = [], [], []
    for s in range(nseq):
        for c, ref in enumerate((cq_ref, ck_ref, cv_ref)):
            xp_scr[s, pad:pad + chunk, c * PROJ_PART:(c + 1) * PROJ_PART] = ref[s].astype(F32)
        g_col = -jnp.exp(alog_ref[...]) * _softplus(sm_ref[s, :, SMALL_A:SMALL_A + GDN_HEADS] + dtb_ref[...])
        bt = _sigmoid(sm_ref[s, :, SMALL_BETA:SMALL_BETA + GDN_HEADS])
        g_row = -jnp.exp(alogc_ref[...]) * _softplus(smt_ref[s, SMALL_A:SMALL_A + GDN_HEADS, :] + dtbc_ref[...])
        if masked:
            g_col = jnp.where(valid_c, g_col, 0.0)
            bt = jnp.where(valid_c, bt, 0.0)
            g_row = jnp.where(valid_r, g_row, 0.0)
        beta_col.append(bt)
        b_col.append(jnp.dot(tri_c, g_col, precision=HIGHEST, preferred_element_type=F32))
        b_row.append(jnp.dot(g_row, tri_r, precision=HIGHEST, preferred_element_type=F32))
    yield

    def conv_silu(s, c0):
        cs = slice(c0, c0 + LANES)
        y = xp_scr[s, pad:pad + chunk, cs] * wc_ref[GDN_CONV - 1:GDN_CONV, cs]
        for j in range(1, GDN_CONV):
            y = y + xp_scr[s, pad - j:pad - j + chunk, cs] * wc_ref[GDN_CONV - 1 - j:GDN_CONV - j, cs]
        return _silu(y)

    bdot = lambda a, b: jnp.dot(a.astype(BF16), b.astype(BF16), preferred_element_type=F32)
    q, k, v, kb, dec, bc, beta = [], [], [], [], [], [], []
    for u, (s, h) in enumerate(units):
        qh = conv_silu(s, h * GDN_DK)
        kh = conv_silu(s, GDN_QK + h * GDN_DK)
        v.append(conv_silu(s, 2 * GDN_QK + h * GDN_DV))
        q.append(qh * lax.rsqrt(jnp.sum(qh * qh, axis=-1, keepdims=True) + L2_EPS) * (GDN_DK ** -0.5))
        k.append(kh * lax.rsqrt(jnp.sum(kh * kh, axis=-1, keepdims=True) + L2_EPS))
        bc.append(b_col[s][:, h:h + 1])
        beta.append(beta_col[s][:, h:h + 1])
        dec.append(jnp.where(row >= col, jnp.exp(jnp.minimum(bc[u] - b_row[s][h:h + 1, :], 0.0)), 0.0))
        kb.append(k[u] * beta[u])
        if u % GDN_HEADS == GDN_HEADS - 1:
            yield
    kq = [lax.dot_general(jnp.concatenate([kb[u], q[u]], axis=0).astype(BF16), k[u].astype(BF16), NT_DIMS,
                          preferred_element_type=F32) for u in nu]
    yield
    blk = min(GDN_INV_BLOCK, chunk)
    nblk = chunk // blk
    same_blk = (row // blk) == (col // blk)
    lmat = [jnp.where(row > col, kq[u][:chunk] * dec[u], 0.0) for u in nu]
    p = [jnp.where(same_blk, -lmat[u], 0.0) for u in nu]
    r = p
    for _ in range(int(math.log2(blk)) - 1):
        p = [bdot(p[u], p[u]) for u in nu]
        r = [r[u] + p[u] + bdot(r[u], p[u]) for u in nu]
        yield
    if nblk > 1:
        lo = [jnp.where(same_blk, 0.0, lmat[u]) for u in nu]
        p = [-(lo[u] + bdot(r[u], lo[u])) for u in nu]
        qm = p
        yield
        for _ in range(int(math.log2(nblk)) - 1):
            p = [bdot(p[u], p[u]) for u in nu]
            qm = [qm[u] + p[u] + bdot(qm[u], p[u]) for u in nu]
            yield
        r = [r[u] + qm[u] + bdot(qm[u], r[u]) for u in nu]
    rhs = [jnp.concatenate([v[u] * beta[u], kb[u] * jnp.exp(bc[u])], axis=-1) for u in nu]
    uw = [rhs[u] + bdot(r[u], rhs[u]) for u in nu]
    yield
    st = [s_scr[u] for u in nu]
    ws = [bdot(jnp.concatenate([uw[u][:, GDN_DV:], q[u] * jnp.exp(bc[u])], axis=0), st[u]) for u in nu]
    v_new = [uw[u][:, :GDN_DV] - ws[u][:chunk] for u in nu]
    yield
    o = [ws[u][chunk:] + bdot(kq[u][chunk:] * dec[u], v_new[u]) for u in nu]
    for u in nu:
        bl = bc[u][chunk - 1:chunk, :]
        s_scr[u] = jnp.exp(bl) * st[u] + lax.dot_general((k[u] * jnp.exp(bl - bc[u])).astype(BF16),
                                                         v_new[u].astype(BF16), TN_DIMS, preferred_element_type=F32)
    yield
    for u, (s, h) in enumerate(units):
        sl = slice(h * GDN_DV, (h + 1) * GDN_DV)
        out[s, h] = (_rms(o[u], nw_ref[...]) * _silu(zb_ref[s, :, sl].astype(F32))
                     * _sigmoid(gt_ref[s, :, sl].astype(F32)))

    @pl.when(n == nchunks - 1)
    def _():
        last = t_valid - (nchunks - 1) * chunk
        for s in range(nseq):
            cout_ref[s] = xp_scr[s, pad + last - (GDN_CONV - 1):pad + last, :]
        for u, (s, h) in enumerate(units):
            sout_ref[s, h] = s_scr[u]

    for s in range(nseq):
        xp_scr[s, 0:pad, :] = xp_scr[s, chunk:chunk + pad, :]


N_GLA_IN, N_GDN_IN = 8, 13


def _mixer_kernel(*refs, chunk, sub, nchunks, nseq, has_state, t_valid):
    gla_in, refs = refs[:N_GLA_IN], refs[N_GLA_IN:]
    gdn_in, refs = refs[:N_GDN_IN], refs[N_GDN_IN:]
    if has_state:
        (sa0_ref, sb0_ref, c0_ref), refs = refs[:3], refs[3:]
    else:
        sa0_ref = sb0_ref = c0_ref = None
    o_ref, sa_out, sb_out, c_out, sa_scr, sb_scr, xp_scr = refs
    out_a, out_b = {}, {}
    gla = _gla_stages(gla_in, sa0_ref, sa_out, sa_scr, out_a,
                      chunk=chunk, sub=sub, nchunks=nchunks, nseq=nseq, t_valid=t_valid)
    gdn = _gdn_stages(gdn_in, sb0_ref, c0_ref, sb_out, c_out, sb_scr, xp_scr, out_b,
                      chunk=chunk, nchunks=nchunks, nseq=nseq, t_valid=t_valid)
    live = [gdn, gla]
    while live:
        for g in list(live):
            if next(g, StopIteration) is StopIteration:
                live.remove(g)
    per = GLA_DV // GDN_DV
    for (s, h), ob in out_b.items():
        oa = out_a[s, h // per][:, (h % per) * GDN_DV:(h % per + 1) * GDN_DV]
        o_ref[s, :, h * GDN_DV:(h + 1) * GDN_DV] = (oa + ob).astype(o_ref.dtype)


def _mixer(proj3, small3, small_t4, w, s_gla, s_gdn, s_conv, *, chunk, sub, nseq, t_valid, out_dtype):
    nb, t_pad, _ = proj3.shape
    nchunks = t_pad // chunk
    has_state = s_gla is not None
    blk = lambda width, c: pl.BlockSpec((nseq, chunk, width), lambda b, n, c=c: (b, n, c))
    col = lambda c: blk(PROJ_PART, c)
    full = lambda a: pl.BlockSpec(a.shape, lambda b, n: (0,) * a.ndim)
    alog_r, dtb_r = w["a_log"].reshape(1, GDN_HEADS), w["dt_bias"].reshape(1, GDN_HEADS)
    alog_c, dtb_c = w["a_log"].reshape(GDN_HEADS, 1), w["dt_bias"].reshape(GDN_HEADS, 1)
    small_spec = blk(LANES, 0)
    gla_consts = [w["wa2"], w["ba"], w["gla_norm"]]
    gdn_consts = [w["w_conv"], alog_r, dtb_r, alog_c, dtb_c, w["gdn_norm"]]
    in_specs = ([col(COL_QK), col(COL_V), col(COL_GA), col(COL_GATE_A), small_spec] + [full(a) for a in gla_consts]
                + [col(COL_CQ), col(COL_CK), col(COL_CV), col(COL_ZB), col(COL_GATE_B), small_spec,
                   pl.BlockSpec((nseq, None, SMALL_ROWS, chunk), lambda b, n: (b, n, 0, 0))]
                + [full(a) for a in gdn_consts])
    args = [proj3] * 4 + [small3] + gla_consts + [proj3] * 5 + [small3, small_t4] + gdn_consts
    assert len(in_specs) == N_GLA_IN + N_GDN_IN
    sa_spec = pl.BlockSpec((nseq, GLA_HEADS, GLA_DK, GLA_DV), lambda b, n: (b, 0, 0, 0))
    sb_spec = pl.BlockSpec((nseq, GDN_HEADS, GDN_DK, GDN_DV), lambda b, n: (b, 0, 0, 0))
    conv_spec = pl.BlockSpec((nseq, GDN_CONV - 1, GDN_CONV_DIM), lambda b, n: (b, 0, 0))
    if has_state:
        in_specs += [sa_spec, sb_spec, conv_spec]
        args += [s_gla, s_gdn, s_conv]
    return pl.pallas_call(
        functools.partial(_mixer_kernel, chunk=chunk, sub=sub, nchunks=nchunks, nseq=nseq, has_state=has_state,
                          t_valid=t_valid),
        out_shape=(jax.ShapeDtypeStruct((nb, t_pad, D_MODEL), out_dtype),
                   jax.ShapeDtypeStruct((nb, GLA_HEADS, GLA_DK, GLA_DV), F32),
                   jax.ShapeDtypeStruct((nb, GDN_HEADS, GDN_DK, GDN_DV), F32),
                   jax.ShapeDtypeStruct((nb, GDN_CONV - 1, GDN_CONV_DIM), F32)),
        grid=(nb // nseq, nchunks),
        in_specs=in_specs,
        out_specs=(blk(D_MODEL, 0), sa_spec, sb_spec, conv_spec),
        scratch_shapes=[pltpu.VMEM((nseq * GLA_HEADS, GLA_DV, GLA_DK), F32),
                        pltpu.VMEM((nseq * GDN_HEADS, GDN_DK, GDN_DV), F32),
                        pltpu.VMEM((nseq, chunk + 2 * SUBLANES, GDN_CONV_DIM), F32)],
        compiler_params=_params("parallel", "arbitrary"),
        name="mixer",
    )(*args)


def _post_kernel(mg_ref, x_ref, gate_ref, shift_ref, scale_ref, wo_ref, nw_ref, wrt_ref, brt_ref,
                 x1_ref, hx_ref, gid_ref):
    y = jnp.dot(mg_ref[...].astype(BF16), wo_ref[...], preferred_element_type=F32)
    x1 = x_ref[...] + gate_ref[...] * y
    x1_ref[...] = x1
    h2 = _rms(x1, nw_ref[...]) * (1.0 + scale_ref[...]) + shift_ref[...]
    hx_ref[:, 0:D_MODEL] = h2
    lt = lax.dot_general(wrt_ref[...], h2.astype(BF16), NT_DIMS, preferred_element_type=F32) + brt_ref[...]
    tm = lt.shape[1]
    gl = lt[0:N_GROUPS]
    gidx = lax.broadcasted_iota(jnp.int32, (N_GROUPS, tm), 0)
    gmax = jnp.max(gl, axis=0, keepdims=True)
    g_w = 1.0 / jnp.sum(jnp.exp(gl - gmax), axis=0, keepdims=True)
    g_sel = jnp.min(jnp.where(gl == gmax, gidx, N_GROUPS), axis=0, keepdims=True)
    el = jnp.zeros((EXPERTS_PER_GROUP, tm), F32)
    for g in range(N_GROUPS):
        r = ROUTER_GROUP_STRIDE * (1 + g)
        el = el + jnp.where(g_sel == g, lt[r:r + EXPERTS_PER_GROUP], 0.0)
    eidx = lax.broadcasted_iota(jnp.int32, (EXPERTS_PER_GROUP, tm), 0)
    m1 = jnp.max(el, axis=0, keepdims=True)
    i1 = jnp.min(jnp.where(el == m1, eidx, EXPERTS_PER_GROUP), axis=0, keepdims=True)
    el2 = jnp.where(eidx == i1, -jnp.inf, el)
    m2 = jnp.max(el2, axis=0, keepdims=True)
    i2 = jnp.min(jnp.where(el2 == m2, eidx, EXPERTS_PER_GROUP), axis=0, keepdims=True)
    r21 = jnp.exp(m2 - m1)
    w1 = 1.0 / (1.0 + r21)
    w2 = r21 / (1.0 + r21)
    comb_t = g_w * (jnp.where(eidx == i1, w1, 0.0) + jnp.where(eidx == i2, w2, 0.0))
    ident = (lax.broadcasted_iota(jnp.int32, (EXPERTS_PER_GROUP, LANES), 0)
             == lax.broadcasted_iota(jnp.int32, (EXPERTS_PER_GROUP, LANES), 1)).astype(F32)
    hx_ref[:, D_MODEL:] = lax.dot_general(comb_t, ident, TN_DIMS, precision=HIGHEST, preferred_element_type=F32)
    lo, hi = jnp.minimum(i1, i2), jnp.maximum(i1, i2)
    pair = jnp.right_shift(lo * (2 * EXPERTS_PER_GROUP - 1 - lo), 1) + hi - lo - 1
    rid = lax.broadcasted_iota(jnp.int32, gid_ref.shape, 0)
    gid_ref[...] = jnp.where(rid == 0, g_sel, g_sel * len(EXPERT_PAIRS) + pair)


def _post(merged, x2d, gate, shift, scale, mod_specs, wo, nw, wrt, brt, tm):
    m, d = x2d.shape
    row = lambda: pl.BlockSpec((tm, d), lambda i: (i, 0))
    full = lambda a: pl.BlockSpec(a.shape, lambda i: (0,) * a.ndim)
    return pl.pallas_call(
        _post_kernel,
        out_shape=(jax.ShapeDtypeStruct((m, d), F32),
                   jax.ShapeDtypeStruct((m, HX_WIDTH), F32),
                   jax.ShapeDtypeStruct((SUBLANES, m), jnp.int32)),
        grid=(m // tm,),
        in_specs=[row(), row(), mod_specs[0], mod_specs[1], mod_specs[2],
                  full(wo), full(nw), full(wrt), full(brt)],
        out_specs=(row(), pl.BlockSpec((tm, HX_WIDTH), lambda i: (i, 0)),
                   pl.BlockSpec((SUBLANES, tm), lambda i: (0, i))),
        compiler_params=_params("parallel"),
        name="post_mixer",
    )(merged, x2d, gate, shift, scale, wo, nw, wrt, brt)


def _route(cls, ncls, tm):
    m = cls.shape[0]
    ntiles_max = m // tm + ncls - 1
    onehot = (cls[:, None] == jnp.arange(ncls, dtype=jnp.int32)[None, :]).astype(jnp.int32)
    incl = jnp.cumsum(onehot, axis=0)
    tiles = (incl[-1] + tm - 1) // tm
    tile_end = jnp.cumsum(tiles)
    pos = jnp.sum(onehot * ((tile_end - tiles) * tm + incl - onehot), axis=1)
    src = jnp.zeros((ntiles_max * tm,), jnp.int32).at[pos].set(jnp.arange(m, dtype=jnp.int32))
    t = jnp.arange(ntiles_max, dtype=jnp.int32)
    tile_class = jnp.minimum(jnp.sum((t[:, None] >= tile_end[None, :]).astype(jnp.int32), axis=1), ncls - 1)
    return pos, src, tile_class, tile_end[-1:]


def _row_gather(idx_ref, src_hbm, buf, sem, slot, rows):
    for r in range(rows):
        pltpu.make_async_copy(src_hbm.at[pl.ds(idx_ref[0, r], 1), :], buf.at[slot, pl.ds(r, 1), :],
                              sem.at[slot]).start()


def _row_gather_wait(src_hbm, buf, sem, slot, rows):
    pltpu.make_async_copy(src_hbm.at[pl.ds(0, rows), :], buf.at[slot], sem.at[slot]).wait()


def _experts_kernel(meta_ref, nt_ref, src_ref, srcn_ref, hx_hbm, *rest, tm, nslot):
    w_refs, (y_ref, xbuf, sem) = rest[:3 * nslot], rest[3 * nslot:]
    t = pl.program_id(0)
    nt = nt_ref[0]
    slot = t % 2

    @pl.when(t == 0)
    def _():
        _row_gather(src_ref, hx_hbm, xbuf, sem, 0, tm)

    @pl.when(t < nt)
    def _():
        _row_gather_wait(hx_hbm, xbuf, sem, slot, tm)
        _row_gather(srcn_ref, hx_hbm, xbuf, sem, 1 - slot, tm)
        x = xbuf[slot, :, 0:D_MODEL].astype(BF16)
        acc = jnp.zeros((tm, D_MODEL), F32)
        for j in range(nslot):
            wg_ref, wu_ref, wd_ref = w_refs[3 * j:3 * j + 3]
            e = meta_ref[t * (1 + nslot) + 1 + j]
            a = jnp.dot(x, wg_ref[0, 0], preferred_element_type=F32)
            u = jnp.dot(x, wu_ref[0, 0], preferred_element_type=F32)
            cw = jnp.zeros((tm, 1), F32)
            for c in range(EXPERTS_PER_GROUP):
                cw = cw + jnp.where(e == c, xbuf[slot, :, D_MODEL + c:D_MODEL + c + 1], 0.0)
            acc = acc + jnp.dot((_silu(a) * u * cw).astype(BF16), wd_ref[0, 0], preferred_element_type=F32)
        y_ref[...] = acc

        @pl.when(t == nt - 1)
        def _():
            _row_gather_wait(hx_hbm, xbuf, sem, 1 - slot, tm)

    @pl.when(t >= nt)
    def _():
        y_ref[...] = jnp.zeros_like(y_ref)


def _experts(hx, src, tile_class, ntiles, class_table, wg, wu, wd, tm):
    ntiles_max = tile_class.shape[0]
    nslot = class_table.shape[1] - 1
    d, f = D_MODEL, D_EXPERT
    src3 = src.reshape(ntiles_max, 1, tm)
    meta = jnp.asarray(class_table, jnp.int32)[tile_class].reshape(-1)
    stride = 1 + nslot
    wspec = lambda j, shape: pl.BlockSpec(
        (1, 1) + shape, lambda t, meta, nt, j=j: (meta[t * stride], meta[t * stride + 1 + j], 0, 0))
    w_specs, w_args = [], []
    for j in range(nslot):
        w_specs += [wspec(j, (d, f)), wspec(j, (d, f)), wspec(j, (f, d))]
        w_args += [wg, wu, wd]
    return pl.pallas_call(
        functools.partial(_experts_kernel, tm=tm, nslot=nslot),
        out_shape=jax.ShapeDtypeStruct((ntiles_max * tm, d), F32),
        grid_spec=pltpu.PrefetchScalarGridSpec(
            num_scalar_prefetch=2,
            grid=(ntiles_max,),
            in_specs=[pl.BlockSpec((None, 1, tm), lambda t, meta, nt: (t, 0, 0), memory_space=pltpu.SMEM),
                      pl.BlockSpec((None, 1, tm),
                                   lambda t, meta, nt: (jnp.maximum(jnp.minimum(t + 1, nt[0] - 1), 0), 0, 0),
                                   memory_space=pltpu.SMEM),
                      pl.BlockSpec(memory_space=pl.ANY)] + w_specs,
            out_specs=pl.BlockSpec((tm, d), lambda t, meta, nt: (t, 0)),
            scratch_shapes=[pltpu.VMEM((2, tm, HX_WIDTH), F32), pltpu.SemaphoreType.DMA((2,))]),
        compiler_params=_params("arbitrary"),
        name="experts",
    )(meta, ntiles, src3, src3, hx, *w_args)


def _final_kernel(pos_ref, posn_ref, ys_hbm, x1_ref, gate_ref, nw_ref, o_ref, ybuf, sem, *, tm, nsteps):
    i = pl.program_id(0)
    slot = i % 2

    @pl.when(i == 0)
    def _():
        _row_gather(pos_ref, ys_hbm, ybuf, sem, 0, tm)

    _row_gather(posn_ref, ys_hbm, ybuf, sem, 1 - slot, tm)
    _row_gather_wait(ys_hbm, ybuf, sem, slot, tm)
    o_ref[...] = _rms(x1_ref[...] + gate_ref[...] * ybuf[slot], nw_ref[...])

    @pl.when(i == nsteps - 1)
    def _():
        _row_gather_wait(ys_hbm, ybuf, sem, 1 - slot, tm)


def _final(ys, pos, x1, gate, gate_spec, nw, tm):
    m, d = x1.shape
    nsteps = m // tm
    pos3 = pos.reshape(nsteps, 1, tm)
    return pl.pallas_call(
        functools.partial(_final_kernel, tm=tm, nsteps=nsteps),
        out_shape=jax.ShapeDtypeStruct((m, d), F32),
        grid=(nsteps,),
        in_specs=[pl.BlockSpec((None, 1, tm), lambda i: (i, 0, 0), memory_space=pltpu.SMEM),
                  pl.BlockSpec((None, 1, tm), lambda i: (jnp.minimum(i + 1, nsteps - 1), 0, 0),
                               memory_space=pltpu.SMEM),
                  pl.BlockSpec(memory_space=pl.ANY),
                  pl.BlockSpec((tm, d), lambda i: (i, 0)),
                  gate_spec,
                  pl.BlockSpec((1, d), lambda i: (0, 0))],
        out_specs=pl.BlockSpec((tm, d), lambda i: (i, 0)),
        scratch_shapes=[pltpu.VMEM((2, tm, d), F32), pltpu.SemaphoreType.DMA((2,))],
        compiler_params=_params("arbitrary"),
        name="final",
    )(pos3, pos3, ys, x1, gate, nw)


def _mod_spec_batch(idx, tm, t, ngrid):
    per = t // tm
    if ngrid == 2:
        return pl.BlockSpec((None, None, 1, D_MODEL), lambda i, j: (i // per, idx, 0, 0))
    return pl.BlockSpec((None, None, 1, D_MODEL), lambda i: (i // per, idx, 0, 0))


def _mod_spec_rows(tm, ngrid):
    if ngrid == 2:
        return pl.BlockSpec((tm, D_MODEL), lambda i, j: (i, 0))
    return pl.BlockSpec((tm, D_MODEL), lambda i: (i, 0))


def _trunk(x, mod, s_gla, s_gdn, s_conv, w, *, chunk, sub, nseq, t_valid, tm, tm_moe, pair_classes, act_dtype):
    nb, t_pad, d = x.shape
    m = nb * t_pad
    x2d = x.reshape(m, d)
    if t_pad % tm == 0:
        mod4 = mod.reshape(nb, N_MOD, 1, d)
        mods = [mod4] * N_MOD
        spec = lambda idx, ngrid: _mod_spec_batch(idx, tm, t_pad, ngrid)
    else:
        mods = [jnp.repeat(mod[:, i], t_pad, axis=0) for i in range(N_MOD)]
        spec = lambda idx, ngrid: _mod_spec_rows(tm, ngrid)

    proj, small, small_t = _inproj(x2d, mods[0], mods[1], (spec(0, 2), spec(1, 2)), w["norm1"],
                                   w["w_main"], w["w_small"], w["w_small_t"], tm, act_dtype)
    proj3 = proj.reshape(nb, t_pad, PROJ_MAIN)
    small3 = small.reshape(nb, t_pad, LANES)
    small_t4 = small_t.reshape(SMALL_ROWS, nb, t_pad // chunk, chunk).transpose(1, 2, 0, 3)
    merged, new_gla, new_gdn, new_conv = _mixer(proj3, small3, small_t4, w, s_gla, s_gdn, s_conv, chunk=chunk,
                                                sub=sub, nseq=nseq, t_valid=t_valid, out_dtype=act_dtype)
    x1, hx, gid = _post(merged.reshape(m, d), x2d, mods[2], mods[3], mods[4],
                        (spec(2, 1), spec(3, 1), spec(4, 1)),
                        w["w_out"], w["norm2"], w["w_router_t"], w["b_router_t"], tm)
    class_table = PAIR_CLASSES if pair_classes else GROUP_CLASSES
    pos, src, tile_class, ntiles = _route(gid[1 if pair_classes else 0], class_table.shape[0], tm_moe)
    ys = _experts(hx, src, tile_class, ntiles, class_table, w["w_gate"], w["w_up"], w["w_down"], tm_moe)
    tm_fin = min(tm, FINAL_TILE)
    gate2_spec = _mod_spec_batch(5, tm_fin, t_pad, 1) if t_pad % tm == 0 else _mod_spec_rows(tm_fin, 1)
    y = _final(ys, pos, x1, mods[5], gate2_spec, w["final_norm"], tm_fin)
    return y.reshape(nb, t_pad, d), new_gla, new_gdn, new_conv


def _prep_weights(w_in, w_gla_a2, b_gla_a, gla_norm_w, w_conv, gdn_A_log, gdn_dt_bias, gdn_norm_w, w_out,
                  norm1_w, norm2_w, w_group_router, b_group_router, w_expert_router, b_expert_router,
                  w_exp_gate, w_exp_up, w_exp_down, final_norm_w):
    d = D_MODEL
    o = 0
    cols = {}
    for name, width in (("gla", 2 * GLA_QK + 2 * GLA_VW), ("ra", GLA_GATE_RANK), ("qkv", GDN_CONV_DIM),
                        ("zb", GDN_VW), ("beta", GDN_HEADS), ("a", GDN_HEADS), ("gates", 2 * D_MODEL)):
        cols[name] = w_in[:, o:o + width]
        o += width
    parts = [cols["gla"], cols["qkv"], jnp.concatenate([cols["zb"], cols["gates"]], axis=1)]
    w_main = jnp.concatenate([p[:, j * PROJ_PART:(j + 1) * PROJ_PART].astype(BF16)
                              for j in range(3) for p in parts], axis=1)
    small = jnp.concatenate([cols["ra"], cols["beta"], cols["a"]], axis=1)
    w_small = jnp.pad(small, ((0, 0), (0, LANES - small.shape[1]))).astype(BF16)
    w_small_t = small.T.astype(BF16)
    wr_t = jnp.zeros((ROUTER_ROWS, d), F32)
    wr_t = wr_t.at[0:N_GROUPS].set(w_group_router.T)
    br_t = jnp.zeros((ROUTER_ROWS, 1), F32)
    br_t = br_t.at[0:N_GROUPS, 0].set(b_group_router)
    for g in range(N_GROUPS):
        r = ROUTER_GROUP_STRIDE * (1 + g)
        es = slice(g * EXPERTS_PER_GROUP, (g + 1) * EXPERTS_PER_GROUP)
        wr_t = wr_t.at[r:r + EXPERTS_PER_GROUP].set(w_expert_router[:, es].T)
        br_t = br_t.at[r:r + EXPERTS_PER_GROUP, 0].set(b_expert_router[es])
    return dict(
        w_main=w_main, w_small=w_small, w_small_t=w_small_t,
        norm1=norm1_w.reshape(1, d), norm2=norm2_w.reshape(1, d), final_norm=final_norm_w.reshape(1, d),
        wa2=w_gla_a2, ba=b_gla_a.reshape(1, GLA_QK), gla_norm=gla_norm_w.reshape(1, GLA_DV),
        w_conv=w_conv, a_log=gdn_A_log, dt_bias=gdn_dt_bias, gdn_norm=gdn_norm_w.reshape(1, GDN_DV),
        w_out=w_out.astype(BF16), w_router_t=wr_t.astype(BF16), b_router_t=br_t,
        w_gate=w_exp_gate.astype(BF16), w_up=w_exp_up.astype(BF16), w_down=w_exp_down.astype(BF16),
    )


def kernel(x_prompt, x_sample, c_prompt, c_sample, state_gla, state_gdn, state_conv, w_ada, b_ada, norm1_w, w_in, w_gla_a2, b_gla_a, gla_norm_w, w_conv, gdn_A_log, gdn_dt_bias, gdn_norm_w, w_out, norm2_w, w_group_router, b_group_router, w_expert_router, b_expert_router, w_exp_gate, w_exp_up, w_exp_down, final_norm_w):
    assert w_ada.shape[0] == 1, "single layer"
    bp, tp, d = x_prompt.shape
    bs, ts, _ = x_sample.shape
    w = _prep_weights(w_in[0], w_gla_a2[0], b_gla_a[0], gla_norm_w[0], w_conv[0], gdn_A_log[0], gdn_dt_bias[0],
                      gdn_norm_w[0], w_out[0], norm1_w[0], norm2_w[0], w_group_router[0], b_group_router[0],
                      w_expert_router[0], b_expert_router[0], w_exp_gate[0], w_exp_up[0], w_exp_down[0],
                      final_norm_w)
    mod = _ada_mod(jnp.concatenate([c_prompt, c_sample], axis=0), w_ada[0], b_ada[0]).reshape(bp + bs, N_MOD, d)

    y_p, gla_p, gdn_p, conv_p = _trunk(x_prompt, mod[:bp], None, None, None, w,
                                       chunk=64, sub=GLA_SUBCHUNK, nseq=4, t_valid=tp, tm=min(1024, tp),
                                       tm_moe=min(512, tp), pair_classes=False, act_dtype=BF16)
    ts_pad = SUBLANES
    xs = jnp.pad(x_sample, ((0, 0), (0, ts_pad - ts), (0, 0)))
    y_s, gla_s, gdn_s, conv_s = _trunk(xs, mod[bp:], state_gla[0], state_gdn[0], state_conv[0], w,
                                       chunk=ts_pad, sub=ts_pad, nseq=4, t_valid=ts, tm=min(512, bs * ts_pad),
                                       tm_moe=min(256, bs * ts_pad), pair_classes=False, act_dtype=F32)
    return (y_p, y_s[:, :ts], gla_p[None], gdn_p[None], conv_p[None], gla_s[None], gdn_s[None], conv_s[None])
```

```python
import functools
import itertools
import math

import jax
import numpy as np
import jax.numpy as jnp
from jax import lax
from jax.experimental import pallas as pl
from jax.experimental.pallas import tpu as pltpu

F32 = jnp.float32
BF16 = jnp.bfloat16
HIGHEST = lax.Precision.HIGHEST

D_MODEL = 1024
GLA_HEADS = 4
GLA_DK = 128
GLA_DV = 256
GLA_QK = GLA_HEADS * GLA_DK
GLA_VW = GLA_HEADS * GLA_DV
GLA_GATE_RANK = 16
GLA_TAU = 16.0
GLA_SUBCHUNK = 16
GDN_HEADS = 8
GDN_DK = 128
GDN_DV = 128
GDN_QK = GDN_HEADS * GDN_DK
GDN_VW = GDN_HEADS * GDN_DV
GDN_CONV = 4
GDN_CONV_DIM = 2 * GDN_QK + GDN_VW
GDN_INV_BLOCK = 16
N_GROUPS = 4
EXPERTS_PER_GROUP = 4
N_EXPERTS = N_GROUPS * EXPERTS_PER_GROUP
D_EXPERT = D_MODEL // 2
N_MOD = 6
NORM_EPS = 1e-6
L2_EPS = 1e-6

LANES = 128
SUBLANES = 8
VMEM_LIMIT = 56 * 1024 * 1024

PROJ_MAIN = 2 * GLA_QK + 2 * GLA_VW + GDN_CONV_DIM + GDN_VW + 2 * D_MODEL
PROJ_PART = 1024
INPROJ_TN = 3 * PROJ_PART
COL_QK, COL_V, COL_GA = 0, 3, 6
COL_CQ, COL_CK, COL_CV = 1, 4, 7
COL_ZB, COL_GATE_A, COL_GATE_B = 2, 5, 8
SMALL_RA, SMALL_BETA, SMALL_A = 0, GLA_GATE_RANK, GLA_GATE_RANK + GDN_HEADS
SMALL_ROWS = 32
ROUTER_GROUP_STRIDE = 8
ROUTER_ROWS = ROUTER_GROUP_STRIDE * (1 + N_GROUPS)

EXPERT_PAIRS = tuple(itertools.combinations(range(EXPERTS_PER_GROUP), 2))
GROUP_CLASSES = np.array([[g] + list(range(EXPERTS_PER_GROUP)) for g in range(N_GROUPS)], np.int32)
PAIR_CLASSES = np.array([[g, a, b] for g in range(N_GROUPS) for a, b in EXPERT_PAIRS], np.int32)

HX_WIDTH = D_MODEL + LANES
SCALAR_UNROLL = 16
FINAL_TILE = 256

NT_DIMS = (((1,), (1,)), ((), ()))
TN_DIMS = (((0,), (0,)), ((), ()))


def _sigmoid(x):
    return 0.5 * jnp.tanh(0.5 * x) + 0.5


def _silu(x):
    return x * _sigmoid(x)


def _softplus(x):
    return jnp.maximum(x, 0.0) + jnp.log1p(jnp.exp(-jnp.abs(x)))


def _rms(x, w):
    return x * lax.rsqrt(jnp.mean(x * x, axis=-1, keepdims=True) + NORM_EPS) * w


def _mod(ref, rows):
    v = ref[...]
    if v.ndim == 3:
        v = jnp.broadcast_to(v, (v.shape[0], rows // v.shape[0], v.shape[2])).reshape(rows, v.shape[2])
    return v


def _params(*sem):
    return pltpu.CompilerParams(dimension_semantics=sem, vmem_limit_bytes=VMEM_LIMIT)


def _ada_kernel(c_ref, w_ref, b_ref, o_ref):
    cs = _silu(c_ref[...])
    o_ref[...] = jnp.dot(cs.astype(BF16), w_ref[...].astype(BF16), preferred_element_type=F32) + b_ref[...]


def _ada_mod(c_all, w_ada, b_ada):
    rows, d = c_all.shape
    n = w_ada.shape[1]
    tn = 1024
    return pl.pallas_call(
        _ada_kernel,
        out_shape=jax.ShapeDtypeStruct((rows, n), F32),
        grid=(n // tn,),
        in_specs=[pl.BlockSpec((rows, d), lambda j: (0, 0)),
                  pl.BlockSpec((d, tn), lambda j: (0, j)),
                  pl.BlockSpec((1, tn), lambda j: (0, j))],
        out_specs=pl.BlockSpec((rows, tn), lambda j: (0, j)),
        compiler_params=_params("arbitrary"),
        name="ada_mod",
    )(c_all, w_ada, b_ada.reshape(1, n))


def _inproj_kernel(x_ref, shift_ref, scale_ref, nw_ref, w_ref, ws_ref, wst_ref,
                   o_ref, os_ref, ost_ref, h_scr):
    @pl.when(pl.program_id(1) == 0)
    def _():
        rows = x_ref.shape[0]
        h = _rms(x_ref[...], nw_ref[...]) * (1.0 + _mod(scale_ref, rows)) + _mod(shift_ref, rows)
        hb = h.astype(BF16)
        h_scr[...] = hb
        os_ref[...] = jnp.dot(hb, ws_ref[...], preferred_element_type=F32)
        ost_ref[...] = lax.dot_general(wst_ref[...], hb, NT_DIMS, preferred_element_type=F32)

    o_ref[...] = jnp.dot(h_scr[...], w_ref[...], preferred_element_type=F32).astype(o_ref.dtype)


def _inproj(x2d, shift, scale, mod_specs, nw, w_main, w_small, w_small_t, tm, out_dtype):
    m, d = x2d.shape
    n = w_main.shape[1]
    tn = INPROJ_TN
    return pl.pallas_call(
        _inproj_kernel,
        out_shape=(jax.ShapeDtypeStruct((m, n), out_dtype),
                   jax.ShapeDtypeStruct((m, LANES), F32),
                   jax.ShapeDtypeStruct((SMALL_ROWS, m), F32)),
        grid=(m // tm, n // tn),
        in_specs=[pl.BlockSpec((tm, d), lambda i, j: (i, 0)),
                  mod_specs[0], mod_specs[1],
                  pl.BlockSpec((1, d), lambda i, j: (0, 0)),
                  pl.BlockSpec((d, tn), lambda i, j: (0, j)),
                  pl.BlockSpec((d, LANES), lambda i, j: (0, 0)),
                  pl.BlockSpec((SMALL_ROWS, d), lambda i, j: (0, 0))],
        out_specs=(pl.BlockSpec((tm, tn), lambda i, j: (i, j)),
                   pl.BlockSpec((tm, LANES), lambda i, j: (i, 0)),
                   pl.BlockSpec((SMALL_ROWS, tm), lambda i, j: (0, i))),
        scratch_shapes=[pltpu.VMEM((tm, d), BF16)],
        compiler_params=_params("parallel", "arbitrary"),
        name="inproj",
    )(x2d, shift, scale, nw, w_main, w_small, w_small_t)


def _gla_stages(ins, s0_ref, sout_ref, st_scr, out, *, chunk, sub, nchunks, nseq, t_valid):
    qk_ref, v_ref, ga_ref, gt_ref, sm_ref, wa2_ref, ba_ref, nw_ref = ins
    has_state = s0_ref is not None
    n = pl.program_id(1)
    masked = t_valid < chunk * nchunks
    units = [(s, h) for s in range(nseq) for h in range(GLA_HEADS)]

    @pl.when(n == 0)
    def _():
        for u, (s, h) in enumerate(units):
            st_scr[u] = s0_ref[s, h].T if has_state else jnp.zeros((GLA_DV, GLA_DK), F32)

    row = lax.broadcasted_iota(jnp.int32, (chunk, chunk), 0)
    col = lax.broadcasted_iota(jnp.int32, (chunk, chunk), 1)
    rowi = lax.broadcasted_iota(jnp.int32, (chunk, 1), 0)
    valid = (rowi + n * chunk) < t_valid
    tri = (row >= col).astype(F32)
    wa2 = wa2_ref[...].astype(BF16)
    b_all = []
    for s in range(nseq):
        ra = sm_ref[s, :, SMALL_RA:SMALL_RA + GLA_GATE_RANK]
        x = jnp.dot(ra.astype(BF16), wa2, preferred_element_type=F32) + ba_ref[...]
        g = (jnp.minimum(x, 0.0) - jnp.log1p(jnp.exp(-jnp.abs(x)))) * (1.0 / GLA_TAU)
        if masked:
            g = jnp.where(valid, g, 0.0)
        b_all.append(jnp.dot(tri, g, precision=HIGHEST, preferred_element_type=F32))
    yield

    q, k, v, b = [], [], [], []
    for s, h in units:
        q.append(qk_ref[s, :, h * GLA_DK:(h + 1) * GLA_DK].astype(F32) * (GLA_DK ** -0.5))
        kh = qk_ref[s, :, GLA_QK + h * GLA_DK:GLA_QK + (h + 1) * GLA_DK].astype(F32)
        k.append(jnp.where(valid, kh, 0.0) if masked else kh)
        v.append(v_ref[s, :, h * GLA_DV:(h + 1) * GLA_DV].astype(BF16))
        b.append(b_all[s][:, h * GLA_DK:(h + 1) * GLA_DK])
    nu = range(len(units))
    st = [st_scr[u] for u in nu]
    o = [lax.dot_general((q[u] * jnp.exp(b[u])).astype(BF16), st[u].astype(BF16), NT_DIMS,
                         preferred_element_type=F32) for u in nu]
    yield
    blocks = [[] for _ in nu]
    for i in range(chunk // sub):
        r0, r1 = i * sub, (i + 1) * sub
        for u in nu:
            bref = b[u][r0 - 1:r0] if i > 0 else jnp.zeros((1, GLA_DK), F32)
            qt = (q[u][r0:r1] * jnp.exp(b[u][r0:r1] - bref)).astype(BF16)
            expo = bref - b[u]
            if r1 < chunk:
                expo = jnp.where(rowi < r1, expo, 0.0)
            kt = (k[u] * jnp.exp(expo)).astype(BF16)
            blocks[u].append(lax.dot_general(qt, kt, NT_DIMS, preferred_element_type=F32))
        yield
    for u in nu:
        a = blocks[u][0] if len(blocks[u]) == 1 else jnp.concatenate(blocks[u], axis=0)
        a = jnp.where(col <= row, a, 0.0)
        o[u] = o[u] + jnp.dot(a.astype(BF16), v[u], preferred_element_type=F32)
    yield
    for u in nu:
        bl = b[u][chunk - 1:chunk]
        kt = (k[u] * jnp.exp(bl - b[u])).astype(BF16)
        st_scr[u] = st[u] * jnp.exp(bl) + lax.dot_general(v[u], kt, TN_DIMS, preferred_element_type=F32)
    yield
    for u, (s, h) in enumerate(units):
        sl = slice(h * GLA_DV, (h + 1) * GLA_DV)
        out[s, h] = (_rms(o[u], nw_ref[...]) * _silu(ga_ref[s, :, sl].astype(F32))
                     * _sigmoid(gt_ref[s, :, sl].astype(F32)))

    @pl.when(n == nchunks - 1)
    def _():
        for u, (s, h) in enumerate(units):
            sout_ref[s, h] = st_scr[u].T


def _gdn_stages(ins, s0_ref, c0_ref, sout_ref, cout_ref, s_scr, xp_scr, out, *, chunk, nchunks, nseq, t_valid):
    (cq_ref, ck_ref, cv_ref, zb_ref, gt_ref, sm_ref, smt_ref, wc_ref, alog_ref, dtb_ref, alogc_ref, dtbc_ref,
     nw_ref) = ins
    has_state = s0_ref is not None
    n = pl.program_id(1)
    masked = t_valid < chunk * nchunks
    pad = SUBLANES
    units = [(s, h) for s in range(nseq) for h in range(GDN_HEADS)]
    nu = range(len(units))

    @pl.when(n == 0)
    def _():
        for u, (s, h) in enumerate(units):
            s_scr[u] = s0_ref[s, h] if has_state else jnp.zeros((GDN_DK, GDN_DV), F32)
        for s in range(nseq):
            xp_scr[s, 0:pad, :] = jnp.zeros((pad, GDN_CONV_DIM), F32)
            if has_state:
                xp_scr[s, pad - (GDN_CONV - 1):pad, :] = c0_ref[s]

    row = lax.broadcasted_iota(jnp.int32, (chunk, chunk), 0)
    col = lax.broadcasted_iota(jnp.int32, (chunk, chunk), 1)
    valid_c = (lax.broadcasted_iota(jnp.int32, (chunk, 1), 0) + n * chunk) < t_valid
    valid_r = (lax.broadcasted_iota(jnp.int32, (1, chunk), 1) + n * chunk) < t_valid
    tri_c = (row >= col).astype(F32)
    tri_r = (row <= col).astype(F32)
    b_col, b_row, beta_col = [], [], []
    for s in range(nseq):
        for c, ref in enumerate((cq_ref, ck_ref, cv_ref)):
            xp_scr[s, pad:pad + chunk, c * PROJ_PART:(c + 1) * PROJ_PART] = ref[s].astype(F32)
        g_col = -jnp.exp(alog_ref[...]) * _softplus(sm_ref[s, :, SMALL_A:SMALL_A + GDN_HEADS] + dtb_ref[...])
        bt = _sigmoid(sm_ref[s, :, SMALL_BETA:SMALL_BETA + GDN_HEADS])
        g_row = -jnp.exp(alogc_ref[...]) * _softplus(smt_ref[s, SMALL_A:SMALL_A + GDN_HEADS, :] + dtbc_ref[...])
        if masked:
            g_col = jnp.where(valid_c, g_col, 0.0)
            bt = jnp.where(valid_c, bt, 0.0)
            g_row = jnp.where(valid_r, g_row, 0.0)
        beta_col.append(bt)
        b_col.append(jnp.dot(tri_c, g_col, precision=HIGHEST, preferred_element_type=F32))
        b_row.append(jnp.dot(g_row, tri_r, precision=HIGHEST, preferred_element_type=F32))
    yield

    def conv_silu(s, c0):
        cs = slice(c0, c0 + LANES)
        y = xp_scr[s, pad:pad + chunk, cs] * wc_ref[GDN_CONV - 1:GDN_CONV, cs]
        for j in range(1, GDN_CONV):
            y = y + xp_scr[s, pad - j:pad - j + chunk, cs] * wc_ref[GDN_CONV - 1 - j:GDN_CONV - j, cs]
        return _silu(y)

    bdot = lambda a, b: jnp.dot(a.astype(BF16), b.astype(BF16), preferred_element_type=F32)
    q, k, v, kb, dec, bc, beta = [], [], [], [], [], [], []
    for u, (s, h) in enumerate(units):
        qh = conv_silu(s, h * GDN_DK)
        kh = conv_silu(s, GDN_QK + h * GDN_DK)
        v.append(conv_silu(s, 2 * GDN_QK + h * GDN_DV))
        q.append(qh * lax.rsqrt(jnp.sum(qh * qh, axis=-1, keepdims=True) + L2_EPS) * (GDN_DK ** -0.5))
        k.append(kh * lax.rsqrt(jnp.sum(kh * kh, axis=-1, keepdims=True) + L2_EPS))
        bc.append(b_col[s][:, h:h + 1])
        beta.append(beta_col[s][:, h:h + 1])
        dec.append(jnp.where(row >= col, jnp.exp(jnp.minimum(bc[u] - b_row[s][h:h + 1, :], 0.0)), 0.0))
        kb.append(k[u] * beta[u])
        if u % GDN_HEADS == GDN_HEADS - 1:
            yield
    kq = [lax.dot_general(jnp.concatenate([kb[u], q[u]], axis=0).astype(BF16), k[u].astype(BF16), NT_DIMS,
                          preferred_element_type=F32) for u in nu]
    yield
    blk = min(GDN_INV_BLOCK, chunk)
    nblk = chunk // blk
    same_blk = (row // blk) == (col // blk)
    lmat = [jnp.where(row > col, kq[u][:chunk] * dec[u], 0.0) for u in nu]
    p = [jnp.where(same_blk, -lmat[u], 0.0) for u in nu]
    r = p
    for _ in range(int(math.log2(blk)) - 1):
        p = [bdot(p[u], p[u]) for u in nu]
        r = [r[u] + p[u] + bdot(r[u], p[u]) for u in nu]
        yield
    if nblk > 1:
        lo = [jnp.where(same_blk, 0.0, lmat[u]) for u in nu]
        p = [-(lo[u] + bdot(r[u], lo[u])) for u in nu]
        qm = p
        yield
        for _ in range(int(math.log2(nblk)) - 1):
            p = [bdot(p[u], p[u]) for u in nu]
            qm = [qm[u] + p[u] + bdot(qm[u], p[u]) for u in nu]
            yield
        r = [r[u] + qm[u] + bdot(qm[u], r[u]) for u in nu]
    rhs = [jnp.concatenate([v[u] * beta[u], kb[u] * jnp.exp(bc[u])], axis=-1) for u in nu]
    uw = [rhs[u] + bdot(r[u], rhs[u]) for u in nu]
    yield
    st = [s_scr[u] for u in nu]
    ws = [bdot(jnp.concatenate([uw[u][:, GDN_DV:], q[u] * jnp.exp(bc[u])], axis=0), st[u]) for u in nu]
    v_new = [uw[u][:, :GDN_DV] - ws[u][:chunk] for u in nu]
    yield
    o = [ws[u][chunk:] + bdot(kq[u][chunk:] * dec[u], v_new[u]) for u in nu]
    for u in nu:
        bl = bc[u][chunk - 1:chunk, :]
        s_scr[u] = jnp.exp(bl) * st[u] + lax.dot_general((k[u] * jnp.exp(bl - bc[u])).astype(BF16),
                                                         v_new[u].astype(BF16), TN_DIMS, preferred_element_type=F32)
    yield
    for u, (s, h) in enumerate(units):
        sl = slice(h * GDN_DV, (h + 1) * GDN_DV)
        out[s, h] = (_rms(o[u], nw_ref[...]) * _silu(zb_ref[s, :, sl].astype(F32))
                     * _sigmoid(gt_ref[s, :, sl].astype(F32)))

    @pl.when(n == nchunks - 1)
    def _():
        last = t_valid - (nchunks - 1) * chunk
        for s in range(nseq):
            cout_ref[s] = xp_scr[s, pad + last - (GDN_CONV - 1):pad + last, :]
        for u, (s, h) in enumerate(units):
            sout_ref[s, h] = s_scr[u]

    for s in range(nseq):
        xp_scr[s, 0:pad, :] = xp_scr[s, chunk:chunk + pad, :]


N_GLA_IN, N_GDN_IN = 8, 13


def _mixer_kernel(*refs, chunk, sub, nchunks, nseq, has_state, t_valid):
    gla_in, refs = refs[:N_GLA_IN], refs[N_GLA_IN:]
    gdn_in, refs = refs[:N_GDN_IN], refs[N_GDN_IN:]
    if has_state:
        (sa0_ref, sb0_ref, c0_ref), refs = refs[:3], refs[3:]
    else:
        sa0_ref = sb0_ref = c0_ref = None
    o_ref, sa_out, sb_out, c_out, sa_scr, sb_scr, xp_scr = refs
    out_a, out_b = {}, {}
    gla = _gla_stages(gla_in, sa0_ref, sa_out, sa_scr, out_a,
                      chunk=chunk, sub=sub, nchunks=nchunks, nseq=nseq, t_valid=t_valid)
    gdn = _gdn_stages(gdn_in, sb0_ref, c0_ref, sb_out, c_out, sb_scr, xp_scr, out_b,
                      chunk=chunk, nchunks=nchunks, nseq=nseq, t_valid=t_valid)
    live = [gdn, gla]
    while live:
        for g in list(live):
            if next(g, StopIteration) is StopIteration:
                live.remove(g)
    per = GLA_DV // GDN_DV
    for (s, h), ob in out_b.items():
        oa = out_a[s, h // per][:, (h % per) * GDN_DV:(h % per + 1) * GDN_DV]
        o_ref[s, :, h * GDN_DV:(h + 1) * GDN_DV] = (oa + ob).astype(o_ref.dtype)


def _mixer(proj3, small3, small_t4, w, s_gla, s_gdn, s_conv, *, chunk, sub, nseq, t_valid, out_dtype):
    nb, t_pad, _ = proj3.shape
    nchunks = t_pad // chunk
    has_state = s_gla is not None
    blk = lambda width, c: pl.BlockSpec((nseq, chunk, width), lambda b, n, c=c: (b, n, c))
    col = lambda c: blk(PROJ_PART, c)
    full = lambda a: pl.BlockSpec(a.shape, lambda b, n: (0,) * a.ndim)
    alog_r, dtb_r = w["a_log"].reshape(1, GDN_HEADS), w["dt_bias"].reshape(1, GDN_HEADS)
    alog_c, dtb_c = w["a_log"].reshape(GDN_HEADS, 1), w["dt_bias"].reshape(GDN_HEADS, 1)
    small_spec = blk(LANES, 0)
    gla_consts = [w["wa2"], w["ba"], w["gla_norm"]]
    gdn_consts = [w["w_conv"], alog_r, dtb_r, alog_c, dtb_c, w["gdn_norm"]]
    in_specs = ([col(COL_QK), col(COL_V), col(COL_GA), col(COL_GATE_A), small_spec] + [full(a) for a in gla_consts]
                + [col(COL_CQ), col(COL_CK), col(COL_CV), col(COL_ZB), col(COL_GATE_B), small_spec,
                   pl.BlockSpec((nseq, None, SMALL_ROWS, chunk), lambda b, n: (b, n, 0, 0))]
                + [full(a) for a in gdn_consts])
    args = [proj3] * 4 + [small3] + gla_consts + [proj3] * 5 + [small3, small_t4] + gdn_consts
    assert len(in_specs) == N_GLA_IN + N_GDN_IN
    sa_spec = pl.BlockSpec((nseq, GLA_HEADS, GLA_DK, GLA_DV), lambda b, n: (b, 0, 0, 0))
    sb_spec = pl.BlockSpec((nseq, GDN_HEADS, GDN_DK, GDN_DV), lambda b, n: (b, 0, 0, 0))
    conv_spec = pl.BlockSpec((nseq, GDN_CONV - 1, GDN_CONV_DIM), lambda b, n: (b, 0, 0))
    if has_state:
        in_specs += [sa_spec, sb_spec, conv_spec]
        args += [s_gla, s_gdn, s_conv]
    return pl.pallas_call(
        functools.partial(_mixer_kernel, chunk=chunk, sub=sub, nchunks=nchunks, nseq=nseq, has_state=has_state,
                          t_valid=t_valid),
        out_shape=(jax.ShapeDtypeStruct((nb, t_pad, D_MODEL), out_dtype),
                   jax.ShapeDtypeStruct((nb, GLA_HEADS, GLA_DK, GLA_DV), F32),
                   jax.ShapeDtypeStruct((nb, GDN_HEADS, GDN_DK, GDN_DV), F32),
                   jax.ShapeDtypeStruct((nb, GDN_CONV - 1, GDN_CONV_DIM), F32)),
        grid=(nb // nseq, nchunks),
        in_specs=in_specs,
        out_specs=(blk(D_MODEL, 0), sa_spec, sb_spec, conv_spec),
        scratch_shapes=[pltpu.VMEM((nseq * GLA_HEADS, GLA_DV, GLA_DK), F32),
                        pltpu.VMEM((nseq * GDN_HEADS, GDN_DK, GDN_DV), F32),
                        pltpu.VMEM((nseq, chunk + 2 * SUBLANES, GDN_CONV_DIM), F32)],
        compiler_params=_params("parallel", "arbitrary"),
        name="mixer",
    )(*args)


def _post_kernel(mg_ref, x_ref, gate_ref, shift_ref, scale_ref, wo_ref, nw_ref, wrt_ref, brt_ref,
                 x1_ref, hx_ref, gid_ref):
    y = jnp.dot(mg_ref[...].astype(BF16), wo_ref[...], preferred_element_type=F32)
    rows = x_ref.shape[0]
    x1 = x_ref[...] + _mod(gate_ref, rows) * y
    x1_ref[...] = x1
    h2 = _rms(x1, nw_ref[...]) * (1.0 + _mod(scale_ref, rows)) + _mod(shift_ref, rows)
    hx_ref[:, 0:D_MODEL] = h2
    lt = lax.dot_general(wrt_ref[...], h2.astype(BF16), NT_DIMS, preferred_element_type=F32) + brt_ref[...]
    tm = lt.shape[1]
    gl = lt[0:N_GROUPS]
    gidx = lax.broadcasted_iota(jnp.int32, (N_GROUPS, tm), 0)
    gmax = jnp.max(gl, axis=0, keepdims=True)
    g_w = 1.0 / jnp.sum(jnp.exp(gl - gmax), axis=0, keepdims=True)
    g_sel = jnp.min(jnp.where(gl == gmax, gidx, N_GROUPS), axis=0, keepdims=True)
    el = jnp.zeros((EXPERTS_PER_GROUP, tm), F32)
    for g in range(N_GROUPS):
        r = ROUTER_GROUP_STRIDE * (1 + g)
        el = el + jnp.where(g_sel == g, lt[r:r + EXPERTS_PER_GROUP], 0.0)
    eidx = lax.broadcasted_iota(jnp.int32, (EXPERTS_PER_GROUP, tm), 0)
    m1 = jnp.max(el, axis=0, keepdims=True)
    i1 = jnp.min(jnp.where(el == m1, eidx, EXPERTS_PER_GROUP), axis=0, keepdims=True)
    el2 = jnp.where(eidx == i1, -jnp.inf, el)
    m2 = jnp.max(el2, axis=0, keepdims=True)
    i2 = jnp.min(jnp.where(el2 == m2, eidx, EXPERTS_PER_GROUP), axis=0, keepdims=True)
    r21 = jnp.exp(m2 - m1)
    w1 = 1.0 / (1.0 + r21)
    w2 = r21 / (1.0 + r21)
    comb_t = g_w * (jnp.where(eidx == i1, w1, 0.0) + jnp.where(eidx == i2, w2, 0.0))
    ident = (lax.broadcasted_iota(jnp.int32, (EXPERTS_PER_GROUP, LANES), 0)
             == lax.broadcasted_iota(jnp.int32, (EXPERTS_PER_GROUP, LANES), 1)).astype(F32)
    hx_ref[:, D_MODEL:] = lax.dot_general(comb_t, ident, TN_DIMS, precision=HIGHEST, preferred_element_type=F32)
    lo, hi = jnp.minimum(i1, i2), jnp.maximum(i1, i2)
    pair = jnp.right_shift(lo * (2 * EXPERTS_PER_GROUP - 1 - lo), 1) + hi - lo - 1
    rid = lax.broadcasted_iota(jnp.int32, gid_ref.shape, 0)
    gid_ref[...] = jnp.where(rid == 0, g_sel, g_sel * len(EXPERT_PAIRS) + pair)


def _post(merged, x2d, gate, shift, scale, mod_specs, wo, nw, wrt, brt, tm):
    m, d = x2d.shape
    row = lambda: pl.BlockSpec((tm, d), lambda i: (i, 0))
    full = lambda a: pl.BlockSpec(a.shape, lambda i: (0,) * a.ndim)
    return pl.pallas_call(
        _post_kernel,
        out_shape=(jax.ShapeDtypeStruct((m, d), F32),
                   jax.ShapeDtypeStruct((m, HX_WIDTH), F32),
                   jax.ShapeDtypeStruct((SUBLANES, m), jnp.int32)),
        grid=(m // tm,),
        in_specs=[row(), row(), mod_specs[0], mod_specs[1], mod_specs[2],
                  full(wo), full(nw), full(wrt), full(brt)],
        out_specs=(row(), pl.BlockSpec((tm, HX_WIDTH), lambda i: (i, 0)),
                   pl.BlockSpec((SUBLANES, tm), lambda i: (0, i))),
        compiler_params=_params("parallel"),
        name="post_mixer",
    )(merged, x2d, gate, shift, scale, wo, nw, wrt, brt)


def _invert_kernel(pos_ref, src_ref):
    def clear(i, carry):
        src_ref[i] = 0
        return carry

    def place(t, carry):
        src_ref[pos_ref[t]] = t
        return carry

    lax.fori_loop(0, src_ref.shape[0], clear, 0, unroll=SCALAR_UNROLL)
    lax.fori_loop(0, pos_ref.shape[0], place, 0, unroll=SCALAR_UNROLL)


def _invert_rows(pos, rows):
    smem = pl.BlockSpec(memory_space=pltpu.SMEM)
    return pl.pallas_call(
        _invert_kernel,
        out_shape=jax.ShapeDtypeStruct((rows,), jnp.int32),
        in_specs=[smem],
        out_specs=smem,
        name="invert_rows",
    )(pos)


def _route(cls, ncls, tm):
    m = cls.shape[0]
    ntiles_max = m // tm + ncls - 1
    onehot = (cls[:, None] == jnp.arange(ncls, dtype=jnp.int32)[None, :]).astype(jnp.int32)
    incl = jnp.cumsum(onehot, axis=0)
    tiles = (incl[-1] + tm - 1) // tm
    tile_end = jnp.cumsum(tiles)
    pos = jnp.sum(onehot * ((tile_end - tiles) * tm + incl - onehot), axis=1)
    src = _invert_rows(pos, ntiles_max * tm)
    t = jnp.arange(ntiles_max, dtype=jnp.int32)
    tile_class = jnp.minimum(jnp.sum((t[:, None] >= tile_end[None, :]).astype(jnp.int32), axis=1), ncls - 1)
    return pos, src, tile_class, tile_end[-1:]


def _row_gather(idx_ref, src_hbm, buf, sem, slot, rows, first=0):
    for r in range(first, first + rows):
        pltpu.make_async_copy(src_hbm.at[pl.ds(idx_ref[0, r], 1), :], buf.at[slot, pl.ds(r, 1), :],
                              sem.at[slot]).start()


def _row_gather_wait(src_hbm, buf, sem, slot, rows):
    pltpu.make_async_copy(src_hbm.at[pl.ds(0, rows), :], buf.at[slot], sem.at[slot]).wait()


def _experts_kernel(meta_ref, nt_ref, src_ref, srcn_ref, hx_hbm, *rest, tm, nslot):
    w_refs, (y_ref, xbuf, sem) = rest[:3 * nslot], rest[3 * nslot:]
    t = pl.program_id(0)
    nt = nt_ref[0]
    slot = t % 2

    @pl.when(t == 0)
    def _():
        _row_gather(src_ref, hx_hbm, xbuf, sem, 0, tm)

    @pl.when(t < nt)
    def _():
        _row_gather_wait(hx_hbm, xbuf, sem, slot, tm)
        x = xbuf[slot, :, 0:D_MODEL].astype(BF16)
        acc = jnp.zeros((tm, D_MODEL), F32)
        for j in range(nslot):
            wg_ref, wu_ref, wd_ref = w_refs[3 * j:3 * j + 3]
            e = meta_ref[t * (1 + nslot) + 1 + j]
            share = tm // (3 * nslot)
            a = jnp.dot(x, wg_ref[0, 0], preferred_element_type=F32)
            _row_gather(srcn_ref, hx_hbm, xbuf, sem, 1 - slot, share, first=(3 * j) * share)
            u = jnp.dot(x, wu_ref[0, 0], preferred_element_type=F32)
            _row_gather(srcn_ref, hx_hbm, xbuf, sem, 1 - slot, share, first=(3 * j + 1) * share)
            cw = jnp.zeros((tm, 1), F32)
            for c in range(EXPERTS_PER_GROUP):
                cw = cw + jnp.where(e == c, xbuf[slot, :, D_MODEL + c:D_MODEL + c + 1], 0.0)
            acc = acc + jnp.dot((_silu(a) * u * cw).astype(BF16), wd_ref[0, 0], preferred_element_type=F32)
            last = tm - (3 * nslot - 1) * share if j == nslot - 1 else share
            _row_gather(srcn_ref, hx_hbm, xbuf, sem, 1 - slot, last, first=(3 * j + 2) * share)
        y_ref[...] = acc

        @pl.when(t == nt - 1)
        def _():
            _row_gather_wait(hx_hbm, xbuf, sem, 1 - slot, tm)

    @pl.when(t >= nt)
    def _():
        y_ref[...] = jnp.zeros_like(y_ref)


def _experts(hx, src, tile_class, ntiles, class_table, wg, wu, wd, tm):
    ntiles_max = tile_class.shape[0]
    nslot = class_table.shape[1] - 1
    d, f = D_MODEL, D_EXPERT
    src3 = src.reshape(ntiles_max, 1, tm)
    meta = jnp.asarray(class_table, jnp.int32)[tile_class].reshape(-1)
    stride = 1 + nslot
    wspec = lambda j, shape: pl.BlockSpec(
        (1, 1) + shape, lambda t, meta, nt, j=j: (meta[t * stride], meta[t * stride + 1 + j], 0, 0))
    w_specs, w_args = [], []
    for j in range(nslot):
        w_specs += [wspec(j, (d, f)), wspec(j, (d, f)), wspec(j, (f, d))]
        w_args += [wg, wu, wd]
    return pl.pallas_call(
        functools.partial(_experts_kernel, tm=tm, nslot=nslot),
        out_shape=jax.ShapeDtypeStruct((ntiles_max * tm, d), F32),
        grid_spec=pltpu.PrefetchScalarGridSpec(
            num_scalar_prefetch=2,
            grid=(ntiles_max,),
            in_specs=[pl.BlockSpec((None, 1, tm), lambda t, meta, nt: (t, 0, 0), memory_space=pltpu.SMEM),
                      pl.BlockSpec((None, 1, tm),
                                   lambda t, meta, nt: (jnp.maximum(jnp.minimum(t + 1, nt[0] - 1), 0), 0, 0),
                                   memory_space=pltpu.SMEM),
                      pl.BlockSpec(memory_space=pl.ANY)] + w_specs,
            out_specs=pl.BlockSpec((tm, d), lambda t, meta, nt: (t, 0)),
            scratch_shapes=[pltpu.VMEM((2, tm, HX_WIDTH), F32), pltpu.SemaphoreType.DMA((2,))]),
        compiler_params=_params("arbitrary"),
        name="experts",
    )(meta, ntiles, src3, src3, hx, *w_args)


def _final_kernel(pos_ref, posn_ref, ys_hbm, x1_ref, gate_ref, nw_ref, o_ref, ybuf, sem, *, tm, nsteps):
    i = pl.program_id(0)
    slot = i % 2

    @pl.when(i == 0)
    def _():
        _row_gather(pos_ref, ys_hbm, ybuf, sem, 0, tm)

    _row_gather(posn_ref, ys_hbm, ybuf, sem, 1 - slot, tm)
    _row_gather_wait(ys_hbm, ybuf, sem, slot, tm)
    o_ref[...] = _rms(x1_ref[...] + _mod(gate_ref, tm) * ybuf[slot], nw_ref[...])

    @pl.when(i == nsteps - 1)
    def _():
        _row_gather_wait(ys_hbm, ybuf, sem, 1 - slot, tm)


def _final(ys, pos, x1, gate, gate_spec, nw, tm):
    m, d = x1.shape
    nsteps = m // tm
    pos3 = pos.reshape(nsteps, 1, tm)
    return pl.pallas_call(
        functools.partial(_final_kernel, tm=tm, nsteps=nsteps),
        out_shape=jax.ShapeDtypeStruct((m, d), F32),
        grid=(nsteps,),
        in_specs=[pl.BlockSpec((None, 1, tm), lambda i: (i, 0, 0), memory_space=pltpu.SMEM),
                  pl.BlockSpec((None, 1, tm), lambda i: (jnp.minimum(i + 1, nsteps - 1), 0, 0),
                               memory_space=pltpu.SMEM),
                  pl.BlockSpec(memory_space=pl.ANY),
                  pl.BlockSpec((tm, d), lambda i: (i, 0)),
                  gate_spec,
                  pl.BlockSpec((1, d), lambda i: (0, 0))],
        out_specs=pl.BlockSpec((tm, d), lambda i: (i, 0)),
        scratch_shapes=[pltpu.VMEM((2, tm, d), F32), pltpu.SemaphoreType.DMA((2,))],
        compiler_params=_params("arbitrary"),
        name="final",
    )(pos3, pos3, ys, x1, gate, nw)


def _mod_spec(idx, tm, t, ngrid):
    if t % tm == 0:
        per = t // tm
        shape, index = (None, None, 1, D_MODEL), lambda i: (i // per, idx, 0, 0)
    else:
        shape, index = (tm // t, None, 1, D_MODEL), lambda i: (i, idx, 0, 0)
    if ngrid == 2:
        return pl.BlockSpec(shape, lambda i, j: index(i))
    return pl.BlockSpec(shape, index)


def _trunk(x, mod, s_gla, s_gdn, s_conv, w, *, chunk, sub, nseq, t_valid, tm, tm_moe, pair_classes, act_dtype):
    nb, t_pad, d = x.shape
    m = nb * t_pad
    x2d = x.reshape(m, d)
    mods = [mod.reshape(nb, N_MOD, 1, d)] * N_MOD
    spec = lambda idx, ngrid: _mod_spec(idx, tm, t_pad, ngrid)

    proj, small, small_t = _inproj(x2d, mods[0], mods[1], (spec(0, 2), spec(1, 2)), w["norm1"],
                                   w["w_main"], w["w_small"], w["w_small_t"], tm, act_dtype)
    proj3 = proj.reshape(nb, t_pad, PROJ_MAIN)
    small3 = small.reshape(nb, t_pad, LANES)
    small_t4 = small_t.reshape(SMALL_ROWS, nb, t_pad // chunk, chunk).transpose(1, 2, 0, 3)
    merged, new_gla, new_gdn, new_conv = _mixer(proj3, small3, small_t4, w, s_gla, s_gdn, s_conv, chunk=chunk,
                                                sub=sub, nseq=nseq, t_valid=t_valid, out_dtype=act_dtype)
    x1, hx, gid = _post(merged.reshape(m, d), x2d, mods[2], mods[3], mods[4],
                        (spec(2, 1), spec(3, 1), spec(4, 1)),
                        w["w_out"], w["norm2"], w["w_router_t"], w["b_router_t"], tm)
    class_table = PAIR_CLASSES if pair_classes else GROUP_CLASSES
    pos, src, tile_class, ntiles = _route(gid[1 if pair_classes else 0], class_table.shape[0], tm_moe)
    ys = _experts(hx, src, tile_class, ntiles, class_table, w["w_gate"], w["w_up"], w["w_down"], tm_moe)
    tm_fin = min(tm, FINAL_TILE)
    y = _final(ys, pos, x1, mods[5], _mod_spec(5, tm_fin, t_pad, 1), w["final_norm"], tm_fin)
    return y.reshape(nb, t_pad, d), new_gla, new_gdn, new_conv


def _prep_weights(w_in, w_gla_a2, b_gla_a, gla_norm_w, w_conv, gdn_A_log, gdn_dt_bias, gdn_norm_w, w_out,
                  norm1_w, norm2_w, w_group_router, b_group_router, w_expert_router, b_expert_router,
                  w_exp_gate, w_exp_up, w_exp_down, final_norm_w):
    d = D_MODEL
    o = 0
    cols = {}
    for name, width in (("gla", 2 * GLA_QK + 2 * GLA_VW), ("ra", GLA_GATE_RANK), ("qkv", GDN_CONV_DIM),
                        ("zb", GDN_VW), ("beta", GDN_HEADS), ("a", GDN_HEADS), ("gates", 2 * D_MODEL)):
        cols[name] = w_in[:, o:o + width]
        o += width
    parts = [cols["gla"], cols["qkv"], jnp.concatenate([cols["zb"], cols["gates"]], axis=1)]
    w_main = jnp.concatenate([p[:, j * PROJ_PART:(j + 1) * PROJ_PART].astype(BF16)
                              for j in range(3) for p in parts], axis=1)
    small = jnp.concatenate([cols["ra"], cols["beta"], cols["a"]], axis=1)
    w_small = jnp.pad(small, ((0, 0), (0, LANES - small.shape[1]))).astype(BF16)
    w_small_t = small.T.astype(BF16)
    assert N_GROUPS == EXPERTS_PER_GROUP
    stride_pad = ((0, 0), (0, 0), (0, ROUTER_GROUP_STRIDE - EXPERTS_PER_GROUP))

    def router_rows(group_part, expert_part):
        both = jnp.concatenate([group_part, expert_part], axis=1).reshape(-1, 1 + N_GROUPS, EXPERTS_PER_GROUP)
        return jnp.pad(both, stride_pad).reshape(-1, ROUTER_ROWS)

    wr_t = router_rows(w_group_router, w_expert_router).T
    br_t = router_rows(b_group_router[None], b_expert_router[None]).T
    return dict(
        w_main=w_main, w_small=w_small, w_small_t=w_small_t,
        norm1=norm1_w.reshape(1, d), norm2=norm2_w.reshape(1, d), final_norm=final_norm_w.reshape(1, d),
        wa2=w_gla_a2, ba=b_gla_a.reshape(1, GLA_QK), gla_norm=gla_norm_w.reshape(1, GLA_DV),
        w_conv=w_conv, a_log=gdn_A_log, dt_bias=gdn_dt_bias, gdn_norm=gdn_norm_w.reshape(1, GDN_DV),
        w_out=w_out.astype(BF16), w_router_t=wr_t.astype(BF16), b_router_t=br_t,
        w_gate=w_exp_gate.astype(BF16), w_up=w_exp_up.astype(BF16), w_down=w_exp_down.astype(BF16),
    )


def kernel(x_prompt, x_sample, c_prompt, c_sample, state_gla, state_gdn, state_conv, w_ada, b_ada, norm1_w, w_in, w_gla_a2, b_gla_a, gla_norm_w, w_conv, gdn_A_log, gdn_dt_bias, gdn_norm_w, w_out, norm2_w, w_group_router, b_group_router, w_expert_router, b_expert_router, w_exp_gate, w_exp_up, w_exp_down, final_norm_w):
    assert w_ada.shape[0] == 1, "single layer"
    bp, tp, d = x_prompt.shape
    bs, ts, _ = x_sample.shape
    w = _prep_weights(w_in[0], w_gla_a2[0], b_gla_a[0], gla_norm_w[0], w_conv[0], gdn_A_log[0], gdn_dt_bias[0],
                      gdn_norm_w[0], w_out[0], norm1_w[0], norm2_w[0], w_group_router[0], b_group_router[0],
                      w_expert_router[0], b_expert_router[0], w_exp_gate[0], w_exp_up[0], w_exp_down[0],
                      final_norm_w)
    mod = _ada_mod(jnp.concatenate([c_prompt, c_sample], axis=0), w_ada[0], b_ada[0]).reshape(bp + bs, N_MOD, d)

    y_p, gla_p, gdn_p, conv_p = _trunk(x_prompt, mod[:bp], None, None, None, w,
                                       chunk=64, sub=GLA_SUBCHUNK, nseq=4, t_valid=tp, tm=min(1024, tp),
                                       tm_moe=min(512, tp), pair_classes=False, act_dtype=BF16)
    ts_pad = SUBLANES
    xs = jnp.pad(x_sample, ((0, 0), (0, ts_pad - ts), (0, 0)))
    y_s, gla_s, gdn_s, conv_s = _trunk(xs, mod[bp:], state_gla[0], state_gdn[0], state_conv[0], w,
                                       chunk=ts_pad, sub=ts_pad, nseq=4, t_valid=ts, tm=min(512, bs * ts_pad),
                                       tm_moe=min(256, bs * ts_pad), pair_classes=False, act_dtype=F32)
    return (y_p, y_s[:, :ts], gla_p[None], gdn_p[None], conv_p[None], gla_s[None], gdn_s[None], conv_s[None])
```

```python
import functools
import itertools
import math

import jax
import numpy as np
import jax.numpy as jnp
from jax import lax
from jax.experimental import pallas as pl
from jax.experimental.pallas import tpu as pltpu

F32 = jnp.float32
BF16 = jnp.bfloat16
HIGHEST = lax.Precision.HIGHEST

D_MODEL = 1024
GLA_HEADS = 4
GLA_DK = 128
GLA_DV = 256
GLA_QK = GLA_HEADS * GLA_DK
GLA_VW = GLA_HEADS * GLA_DV
GLA_GATE_RANK = 16
GLA_TAU = 16.0
GLA_SUBCHUNK = 16
GDN_HEADS = 8
GDN_DK = 128
GDN_DV = 128
GDN_QK = GDN_HEADS * GDN_DK
GDN_VW = GDN_HEADS * GDN_DV
GDN_CONV = 4
GDN_CONV_DIM = 2 * GDN_QK + GDN_VW
GDN_INV_BLOCK = 16
N_GROUPS = 4
EXPERTS_PER_GROUP = 4
N_EXPERTS = N_GROUPS * EXPERTS_PER_GROUP
D_EXPERT = D_MODEL // 2
N_MOD = 6
NORM_EPS = 1e-6
L2_EPS = 1e-6

LANES = 128
SUBLANES = 8
VMEM_LIMIT = 56 * 1024 * 1024

PROJ_MAIN = 2 * GLA_QK + 2 * GLA_VW + GDN_CONV_DIM + GDN_VW + 2 * D_MODEL
PROJ_PART = 1024
INPROJ_TN = 3 * PROJ_PART
COL_QK, COL_V, COL_GA = 0, 3, 6
COL_CQ, COL_CK, COL_CV = 1, 4, 7
COL_ZB, COL_GATE_A, COL_GATE_B = 2, 5, 8
SMALL_RA, SMALL_BETA, SMALL_A = 0, GLA_GATE_RANK, GLA_GATE_RANK + GDN_HEADS
SMALL_ROWS = 32
ROUTER_GROUP_STRIDE = 8
ROUTER_ROWS = ROUTER_GROUP_STRIDE * (1 + N_GROUPS)

EXPERT_PAIRS = tuple(itertools.combinations(range(EXPERTS_PER_GROUP), 2))
GROUP_CLASSES = np.array([[g] + list(range(EXPERTS_PER_GROUP)) for g in range(N_GROUPS)], np.int32)
PAIR_CLASSES = np.array([[g, a, b] for g in range(N_GROUPS) for a, b in EXPERT_PAIRS], np.int32)

HX_WIDTH = D_MODEL + LANES
SCALAR_UNROLL = 16
FINAL_TILE = 256

NT_DIMS = (((1,), (1,)), ((), ()))
TN_DIMS = (((0,), (0,)), ((), ()))


def _sigmoid(x):
    return 0.5 * jnp.tanh(0.5 * x) + 0.5


def _silu(x):
    return x * _sigmoid(x)


def _softplus(x):
    return jnp.maximum(x, 0.0) + jnp.log1p(jnp.exp(-jnp.abs(x)))


def _rms(x, w):
    return x * lax.rsqrt(jnp.mean(x * x, axis=-1, keepdims=True) + NORM_EPS) * w


def _mod(ref, rows):
    v = ref[...]
    if v.ndim == 3:
        v = jnp.broadcast_to(v, (v.shape[0], rows // v.shape[0], v.shape[2])).reshape(rows, v.shape[2])
    return v


def _params(*sem):
    return pltpu.CompilerParams(dimension_semantics=sem, vmem_limit_bytes=VMEM_LIMIT)


def _ada_kernel(c_ref, w_ref, b_ref, o_ref):
    cs = _silu(c_ref[...])
    o_ref[...] = jnp.dot(cs.astype(BF16), w_ref[...].astype(BF16), preferred_element_type=F32) + b_ref[...]


def _ada_mod(c_all, w_ada, b_ada):
    rows, d = c_all.shape
    n = w_ada.shape[1]
    tn = 1024
    return pl.pallas_call(
        _ada_kernel,
        out_shape=jax.ShapeDtypeStruct((rows, n), F32),
        grid=(n // tn,),
        in_specs=[pl.BlockSpec((rows, d), lambda j: (0, 0)),
                  pl.BlockSpec((d, tn), lambda j: (0, j)),
                  pl.BlockSpec((1, tn), lambda j: (0, j))],
        out_specs=pl.BlockSpec((rows, tn), lambda j: (0, j)),
        compiler_params=_params("arbitrary"),
        name="ada_mod",
    )(c_all, w_ada, b_ada.reshape(1, n))


def _inproj_kernel(x_ref, shift_ref, scale_ref, nw_ref, w_ref, ws_ref, wst_ref,
                   o_ref, os_ref, ost_ref, h_scr):
    @pl.when(pl.program_id(1) == 0)
    def _():
        rows = x_ref.shape[0]
        h = _rms(x_ref[...], nw_ref[...]) * (1.0 + _mod(scale_ref, rows)) + _mod(shift_ref, rows)
        hb = h.astype(BF16)
        h_scr[...] = hb
        os_ref[...] = jnp.dot(hb, ws_ref[...], preferred_element_type=F32)
        ost_ref[...] = lax.dot_general(wst_ref[...], hb, NT_DIMS, preferred_element_type=F32)

    o_ref[...] = jnp.dot(h_scr[...], w_ref[...], preferred_element_type=F32).astype(o_ref.dtype)


def _inproj(x2d, shift, scale, mod_specs, nw, w_main, w_small, w_small_t, tm, out_dtype):
    m, d = x2d.shape
    n = w_main.shape[1]
    tn = INPROJ_TN
    return pl.pallas_call(
        _inproj_kernel,
        out_shape=(jax.ShapeDtypeStruct((m, n), out_dtype),
                   jax.ShapeDtypeStruct((m, LANES), F32),
                   jax.ShapeDtypeStruct((SMALL_ROWS, m), F32)),
        grid=(m // tm, n // tn),
        in_specs=[pl.BlockSpec((tm, d), lambda i, j: (i, 0)),
                  mod_specs[0], mod_specs[1],
                  pl.BlockSpec((1, d), lambda i, j: (0, 0)),
                  pl.BlockSpec((d, tn), lambda i, j: (0, j)),
                  pl.BlockSpec((d, LANES), lambda i, j: (0, 0)),
                  pl.BlockSpec((SMALL_ROWS, d), lambda i, j: (0, 0))],
        out_specs=(pl.BlockSpec((tm, tn), lambda i, j: (i, j)),
                   pl.BlockSpec((tm, LANES), lambda i, j: (i, 0)),
                   pl.BlockSpec((SMALL_ROWS, tm), lambda i, j: (0, i))),
        scratch_shapes=[pltpu.VMEM((tm, d), BF16)],
        compiler_params=_params("parallel", "arbitrary"),
        name="inproj",
    )(x2d, shift, scale, nw, w_main, w_small, w_small_t)


def _gla_stages(ins, s0_ref, sout_ref, st_scr, out, *, chunk, sub, nchunks, nseq, t_valid):
    qk_ref, v_ref, ga_ref, gt_ref, sm_ref, wa2_ref, ba_ref, nw_ref = ins
    has_state = s0_ref is not None
    n = pl.program_id(1)
    masked = t_valid < chunk * nchunks
    units = [(s, h) for s in range(nseq) for h in range(GLA_HEADS)]

    @pl.when(n == 0)
    def _():
        for u, (s, h) in enumerate(units):
            st_scr[u] = s0_ref[s, h].T if has_state else jnp.zeros((GLA_DV, GLA_DK), F32)

    row = lax.broadcasted_iota(jnp.int32, (chunk, chunk), 0)
    col = lax.broadcasted_iota(jnp.int32, (chunk, chunk), 1)
    rowi = lax.broadcasted_iota(jnp.int32, (chunk, 1), 0)
    valid = (rowi + n * chunk) < t_valid
    tri = (row >= col).astype(F32)
    wa2 = wa2_ref[...].astype(BF16)
    b_all = []
    for s in range(nseq):
        ra = sm_ref[s, :, SMALL_RA:SMALL_RA + GLA_GATE_RANK]
        x = jnp.dot(ra.astype(BF16), wa2, preferred_element_type=F32) + ba_ref[...]
        g = (jnp.minimum(x, 0.0) - jnp.log1p(jnp.exp(-jnp.abs(x)))) * (1.0 / GLA_TAU)
        if masked:
            g = jnp.where(valid, g, 0.0)
        b_all.append(jnp.dot(tri, g, precision=HIGHEST, preferred_element_type=F32))
    yield

    q, k, v, b = [], [], [], []
    for s, h in units:
        q.append(qk_ref[s, :, h * GLA_DK:(h + 1) * GLA_DK].astype(F32) * (GLA_DK ** -0.5))
        kh = qk_ref[s, :, GLA_QK + h * GLA_DK:GLA_QK + (h + 1) * GLA_DK].astype(F32)
        k.append(jnp.where(valid, kh, 0.0) if masked else kh)
        v.append(v_ref[s, :, h * GLA_DV:(h + 1) * GLA_DV].astype(BF16))
        b.append(b_all[s][:, h * GLA_DK:(h + 1) * GLA_DK])
    nu = range(len(units))
    st = [st_scr[u] for u in nu]
    o = [lax.dot_general((q[u] * jnp.exp(b[u])).astype(BF16), st[u].astype(BF16), NT_DIMS,
                         preferred_element_type=F32) for u in nu]
    yield
    blocks = [[] for _ in nu]
    for i in range(chunk // sub):
        r0, r1 = i * sub, (i + 1) * sub
        for u in nu:
            bref = b[u][r0 - 1:r0] if i > 0 else jnp.zeros((1, GLA_DK), F32)
            qt = (q[u][r0:r1] * jnp.exp(b[u][r0:r1] - bref)).astype(BF16)
            expo = bref - b[u]
            if r1 < chunk:
                expo = jnp.where(rowi < r1, expo, 0.0)
            kt = (k[u] * jnp.exp(expo)).astype(BF16)
            blocks[u].append(lax.dot_general(qt, kt, NT_DIMS, preferred_element_type=F32))
        yield
    for u in nu:
        a = blocks[u][0] if len(blocks[u]) == 1 else jnp.concatenate(blocks[u], axis=0)
        a = jnp.where(col <= row, a, 0.0)
        o[u] = o[u] + jnp.dot(a.astype(BF16), v[u], preferred_element_type=F32)
    yield
    for u in nu:
        bl = b[u][chunk - 1:chunk]
        kt = (k[u] * jnp.exp(bl - b[u])).astype(BF16)
        st_scr[u] = st[u] * jnp.exp(bl) + lax.dot_general(v[u], kt, TN_DIMS, preferred_element_type=F32)
    yield
    for u, (s, h) in enumerate(units):
        sl = slice(h * GLA_DV, (h + 1) * GLA_DV)
        out[s, h] = (_rms(o[u], nw_ref[...]) * _silu(ga_ref[s, :, sl].astype(F32))
                     * _sigmoid(gt_ref[s, :, sl].astype(F32)))

    @pl.when(n == nchunks - 1)
    def _():
        for u, (s, h) in enumerate(units):
            sout_ref[s, h] = st_scr[u].T


def _gdn_stages(ins, s0_ref, c0_ref, sout_ref, cout_ref, s_scr, xp_scr, out, *, chunk, nchunks, nseq, t_valid):
    (cq_ref, ck_ref, cv_ref, zb_ref, gt_ref, sm_ref, smt_ref, wc_ref, alog_ref, dtb_ref, alogc_ref, dtbc_ref,
     nw_ref) = ins
    has_state = s0_ref is not None
    n = pl.program_id(1)
    masked = t_valid < chunk * nchunks
    pad = SUBLANES
    units = [(s, h) for s in range(nseq) for h in range(GDN_HEADS)]
    nu = range(len(units))

    @pl.when(n == 0)
    def _():
        for u, (s, h) in enumerate(units):
            s_scr[u] = s0_ref[s, h] if has_state else jnp.zeros((GDN_DK, GDN_DV), F32)
        for s in range(nseq):
            xp_scr[s, 0:pad, :] = jnp.zeros((pad, GDN_CONV_DIM), F32)
            if has_state:
                xp_scr[s, pad - (GDN_CONV - 1):pad, :] = c0_ref[s]

    row = lax.broadcasted_iota(jnp.int32, (chunk, chunk), 0)
    col = lax.broadcasted_iota(jnp.int32, (chunk, chunk), 1)
    valid_c = (lax.broadcasted_iota(jnp.int32, (chunk, 1), 0) + n * chunk) < t_valid
    valid_r = (lax.broadcasted_iota(jnp.int32, (1, chunk), 1) + n * chunk) < t_valid
    tri_c = (row >= col).astype(F32)
    tri_r = (row <= col).astype(F32)
    b_col, b_row, beta_col = [], [], []
    for s in range(nseq):
        for c, ref in enumerate((cq_ref, ck_ref, cv_ref)):
            xp_scr[s, pad:pad + chunk, c * PROJ_PART:(c + 1) * PROJ_PART] = ref[s].astype(F32)
        g_col = -jnp.exp(alog_ref[...]) * _softplus(sm_ref[s, :, SMALL_A:SMALL_A + GDN_HEADS] + dtb_ref[...])
        bt = _sigmoid(sm_ref[s, :, SMALL_BETA:SMALL_BETA + GDN_HEADS])
        g_row = -jnp.exp(alogc_ref[...]) * _softplus(smt_ref[s, SMALL_A:SMALL_A + GDN_HEADS, :] + dtbc_ref[...])
        if masked:
            g_col = jnp.where(valid_c, g_col, 0.0)
            bt = jnp.where(valid_c, bt, 0.0)
            g_row = jnp.where(valid_r, g_row, 0.0)
        beta_col.append(bt)
        b_col.append(jnp.dot(tri_c, g_col, precision=HIGHEST, preferred_element_type=F32))
        b_row.append(jnp.dot(g_row, tri_r, precision=HIGHEST, preferred_element_type=F32))
    yield

    def conv_silu(s, c0):
        cs = slice(c0, c0 + LANES)
        y = xp_scr[s, pad:pad + chunk, cs] * wc_ref[GDN_CONV - 1:GDN_CONV, cs]
        for j in range(1, GDN_CONV):
            y = y + xp_scr[s, pad - j:pad - j + chunk, cs] * wc_ref[GDN_CONV - 1 - j:GDN_CONV - j, cs]
        return _silu(y)

    bdot = lambda a, b: jnp.dot(a.astype(BF16), b.astype(BF16), preferred_element_type=F32)
    q, k, v, kb, dec, bc, beta = [], [], [], [], [], [], []
    for u, (s, h) in enumerate(units):
        qh = conv_silu(s, h * GDN_DK)
        kh = conv_silu(s, GDN_QK + h * GDN_DK)
        v.append(conv_silu(s, 2 * GDN_QK + h * GDN_DV))
        q.append(qh * lax.rsqrt(jnp.sum(qh * qh, axis=-1, keepdims=True) + L2_EPS) * (GDN_DK ** -0.5))
        k.append(kh * lax.rsqrt(jnp.sum(kh * kh, axis=-1, keepdims=True) + L2_EPS))
        bc.append(b_col[s][:, h:h + 1])
        beta.append(beta_col[s][:, h:h + 1])
        dec.append(jnp.where(row >= col, jnp.exp(jnp.minimum(bc[u] - b_row[s][h:h + 1, :], 0.0)), 0.0))
        kb.append(k[u] * beta[u])
        if u % GDN_HEADS == GDN_HEADS - 1:
            yield
    kq = [lax.dot_general(jnp.concatenate([kb[u], q[u]], axis=0).astype(BF16), k[u].astype(BF16), NT_DIMS,
                          preferred_element_type=F32) for u in nu]
    yield
    blk = min(GDN_INV_BLOCK, chunk)
    nblk = chunk // blk
    same_blk = (row // blk) == (col // blk)
    lmat = [jnp.where(row > col, kq[u][:chunk] * dec[u], 0.0) for u in nu]
    p = [jnp.where(same_blk, -lmat[u], 0.0) for u in nu]
    r = p
    for _ in range(int(math.log2(blk)) - 1):
        p = [bdot(p[u], p[u]) for u in nu]
        r = [r[u] + p[u] + bdot(r[u], p[u]) for u in nu]
        yield
    if nblk > 1:
        lo = [jnp.where(same_blk, 0.0, lmat[u]) for u in nu]
        p = [-(lo[u] + bdot(r[u], lo[u])) for u in nu]
        qm = p
        yield
        for _ in range(int(math.log2(nblk)) - 1):
            p = [bdot(p[u], p[u]) for u in nu]
            qm = [qm[u] + p[u] + bdot(qm[u], p[u]) for u in nu]
            yield
        r = [r[u] + qm[u] + bdot(qm[u], r[u]) for u in nu]
    rhs = [jnp.concatenate([v[u] * beta[u], kb[u] * jnp.exp(bc[u])], axis=-1) for u in nu]
    uw = [rhs[u] + bdot(r[u], rhs[u]) for u in nu]
    yield
    st = [s_scr[u] for u in nu]
    ws = [bdot(jnp.concatenate([uw[u][:, GDN_DV:], q[u] * jnp.exp(bc[u])], axis=0), st[u]) for u in nu]
    v_new = [uw[u][:, :GDN_DV] - ws[u][:chunk] for u in nu]
    yield
    o = [ws[u][chunk:] + bdot(kq[u][chunk:] * dec[u], v_new[u]) for u in nu]
    for u in nu:
        bl = bc[u][chunk - 1:chunk, :]
        s_scr[u] = jnp.exp(bl) * st[u] + lax.dot_general((k[u] * jnp.exp(bl - bc[u])).astype(BF16),
                                                         v_new[u].astype(BF16), TN_DIMS, preferred_element_type=F32)
    yield
    for u, (s, h) in enumerate(units):
        sl = slice(h * GDN_DV, (h + 1) * GDN_DV)
        out[s, h] = (_rms(o[u], nw_ref[...]) * _silu(zb_ref[s, :, sl].astype(F32))
                     * _sigmoid(gt_ref[s, :, sl].astype(F32)))

    @pl.when(n == nchunks - 1)
    def _():
        last = t_valid - (nchunks - 1) * chunk
        for s in range(nseq):
            cout_ref[s] = xp_scr[s, pad + last - (GDN_CONV - 1):pad + last, :]
        for u, (s, h) in enumerate(units):
            sout_ref[s, h] = s_scr[u]

    for s in range(nseq):
        xp_scr[s, 0:pad, :] = xp_scr[s, chunk:chunk + pad, :]


N_GLA_IN, N_GDN_IN = 8, 13


def _mixer_kernel(*refs, chunk, sub, nchunks, nseq, has_state, t_valid):
    gla_in, refs = refs[:N_GLA_IN], refs[N_GLA_IN:]
    gdn_in, refs = refs[:N_GDN_IN], refs[N_GDN_IN:]
    if has_state:
        (sa0_ref, sb0_ref, c0_ref), refs = refs[:3], refs[3:]
    else:
        sa0_ref = sb0_ref = c0_ref = None
    o_ref, sa_out, sb_out, c_out, sa_scr, sb_scr, xp_scr = refs
    out_a, out_b = {}, {}
    gla = _gla_stages(gla_in, sa0_ref, sa_out, sa_scr, out_a,
                      chunk=chunk, sub=sub, nchunks=nchunks, nseq=nseq, t_valid=t_valid)
    gdn = _gdn_stages(gdn_in, sb0_ref, c0_ref, sb_out, c_out, sb_scr, xp_scr, out_b,
                      chunk=chunk, nchunks=nchunks, nseq=nseq, t_valid=t_valid)
    live = [gdn, gla]
    while live:
        for g in list(live):
            if next(g, StopIteration) is StopIteration:
                live.remove(g)
    per = GLA_DV // GDN_DV
    for (s, h), ob in out_b.items():
        oa = out_a[s, h // per][:, (h % per) * GDN_DV:(h % per + 1) * GDN_DV]
        o_ref[s, :, h * GDN_DV:(h + 1) * GDN_DV] = (oa + ob).astype(o_ref.dtype)


def _mixer(proj3, small3, small_t4, w, s_gla, s_gdn, s_conv, *, chunk, sub, nseq, t_valid, out_dtype):
    nb, t_pad, _ = proj3.shape
    nchunks = t_pad // chunk
    has_state = s_gla is not None
    blk = lambda width, c: pl.BlockSpec((nseq, chunk, width), lambda b, n, c=c: (b, n, c))
    col = lambda c: blk(PROJ_PART, c)
    full = lambda a: pl.BlockSpec(a.shape, lambda b, n: (0,) * a.ndim)
    alog_r, dtb_r = w["a_log"].reshape(1, GDN_HEADS), w["dt_bias"].reshape(1, GDN_HEADS)
    alog_c, dtb_c = w["a_log"].reshape(GDN_HEADS, 1), w["dt_bias"].reshape(GDN_HEADS, 1)
    small_spec = blk(LANES, 0)
    gla_consts = [w["wa2"], w["ba"], w["gla_norm"]]
    gdn_consts = [w["w_conv"], alog_r, dtb_r, alog_c, dtb_c, w["gdn_norm"]]
    in_specs = ([col(COL_QK), col(COL_V), col(COL_GA), col(COL_GATE_A), small_spec] + [full(a) for a in gla_consts]
                + [col(COL_CQ), col(COL_CK), col(COL_CV), col(COL_ZB), col(COL_GATE_B), small_spec,
                   pl.BlockSpec((nseq, None, SMALL_ROWS, chunk), lambda b, n: (b, n, 0, 0))]
                + [full(a) for a in gdn_consts])
    args = [proj3] * 4 + [small3] + gla_consts + [proj3] * 5 + [small3, small_t4] + gdn_consts
    assert len(in_specs) == N_GLA_IN + N_GDN_IN
    sa_spec = pl.BlockSpec((nseq, GLA_HEADS, GLA_DK, GLA_DV), lambda b, n: (b, 0, 0, 0))
    sb_spec = pl.BlockSpec((nseq, GDN_HEADS, GDN_DK, GDN_DV), lambda b, n: (b, 0, 0, 0))
    conv_spec = pl.BlockSpec((nseq, GDN_CONV - 1, GDN_CONV_DIM), lambda b, n: (b, 0, 0))
    if has_state:
        in_specs += [sa_spec, sb_spec, conv_spec]
        args += [s_gla, s_gdn, s_conv]
    return pl.pallas_call(
        functools.partial(_mixer_kernel, chunk=chunk, sub=sub, nchunks=nchunks, nseq=nseq, has_state=has_state,
                          t_valid=t_valid),
        out_shape=(jax.ShapeDtypeStruct((nb, t_pad, D_MODEL), out_dtype),
                   jax.ShapeDtypeStruct((nb, GLA_HEADS, GLA_DK, GLA_DV), F32),
                   jax.ShapeDtypeStruct((nb, GDN_HEADS, GDN_DK, GDN_DV), F32),
                   jax.ShapeDtypeStruct((nb, GDN_CONV - 1, GDN_CONV_DIM), F32)),
        grid=(nb // nseq, nchunks),
        in_specs=in_specs,
        out_specs=(blk(D_MODEL, 0), sa_spec, sb_spec, conv_spec),
        scratch_shapes=[pltpu.VMEM((nseq * GLA_HEADS, GLA_DV, GLA_DK), F32),
                        pltpu.VMEM((nseq * GDN_HEADS, GDN_DK, GDN_DV), F32),
                        pltpu.VMEM((nseq, chunk + 2 * SUBLANES, GDN_CONV_DIM), F32)],
        compiler_params=_params("parallel", "arbitrary"),
        name="mixer",
    )(*args)


def _post_kernel(mg_ref, x_ref, gate_ref, shift_ref, scale_ref, wo_ref, nw_ref, wrt_ref, brt_ref,
                 x1_ref, hx_ref, gid_ref):
    y = jnp.dot(mg_ref[...].astype(BF16), wo_ref[...], preferred_element_type=F32)
    rows = x_ref.shape[0]
    x1 = x_ref[...] + _mod(gate_ref, rows) * y
    x1_ref[...] = x1
    h2 = _rms(x1, nw_ref[...]) * (1.0 + _mod(scale_ref, rows)) + _mod(shift_ref, rows)
    hx_ref[:, 0:D_MODEL] = h2
    lt = lax.dot_general(wrt_ref[...], h2.astype(BF16), NT_DIMS, preferred_element_type=F32) + brt_ref[...]
    tm = lt.shape[1]
    gl = lt[0:N_GROUPS]
    gidx = lax.broadcasted_iota(jnp.int32, (N_GROUPS, tm), 0)
    gmax = jnp.max(gl, axis=0, keepdims=True)
    g_w = 1.0 / jnp.sum(jnp.exp(gl - gmax), axis=0, keepdims=True)
    g_sel = jnp.min(jnp.where(gl == gmax, gidx, N_GROUPS), axis=0, keepdims=True)
    el = jnp.zeros((EXPERTS_PER_GROUP, tm), F32)
    for g in range(N_GROUPS):
        r = ROUTER_GROUP_STRIDE * (1 + g)
        el = el + jnp.where(g_sel == g, lt[r:r + EXPERTS_PER_GROUP], 0.0)
    eidx = lax.broadcasted_iota(jnp.int32, (EXPERTS_PER_GROUP, tm), 0)
    m1 = jnp.max(el, axis=0, keepdims=True)
    i1 = jnp.min(jnp.where(el == m1, eidx, EXPERTS_PER_GROUP), axis=0, keepdims=True)
    el2 = jnp.where(eidx == i1, -jnp.inf, el)
    m2 = jnp.max(el2, axis=0, keepdims=True)
    i2 = jnp.min(jnp.where(el2 == m2, eidx, EXPERTS_PER_GROUP), axis=0, keepdims=True)
    r21 = jnp.exp(m2 - m1)
    w1 = 1.0 / (1.0 + r21)
    w2 = r21 / (1.0 + r21)
    comb_t = g_w * (jnp.where(eidx == i1, w1, 0.0) + jnp.where(eidx == i2, w2, 0.0))
    ident = (lax.broadcasted_iota(jnp.int32, (EXPERTS_PER_GROUP, LANES), 0)
             == lax.broadcasted_iota(jnp.int32, (EXPERTS_PER_GROUP, LANES), 1)).astype(F32)
    hx_ref[:, D_MODEL:] = lax.dot_general(comb_t, ident, TN_DIMS, precision=HIGHEST, preferred_element_type=F32)
    lo, hi = jnp.minimum(i1, i2), jnp.maximum(i1, i2)
    pair = jnp.right_shift(lo * (2 * EXPERTS_PER_GROUP - 1 - lo), 1) + hi - lo - 1
    rid = lax.broadcasted_iota(jnp.int32, gid_ref.shape, 0)
    gid_ref[...] = jnp.where(rid == 0, g_sel, g_sel * len(EXPERT_PAIRS) + pair)


def _post(merged, x2d, gate, shift, scale, mod_specs, wo, nw, wrt, brt, tm):
    m, d = x2d.shape
    row = lambda: pl.BlockSpec((tm, d), lambda i: (i, 0))
    full = lambda a: pl.BlockSpec(a.shape, lambda i: (0,) * a.ndim)
    return pl.pallas_call(
        _post_kernel,
        out_shape=(jax.ShapeDtypeStruct((m, d), F32),
                   jax.ShapeDtypeStruct((m, HX_WIDTH), F32),
                   jax.ShapeDtypeStruct((SUBLANES, m), jnp.int32)),
        grid=(m // tm,),
        in_specs=[row(), row(), mod_specs[0], mod_specs[1], mod_specs[2],
                  full(wo), full(nw), full(wrt), full(brt)],
        out_specs=(row(), pl.BlockSpec((tm, HX_WIDTH), lambda i: (i, 0)),
                   pl.BlockSpec((SUBLANES, tm), lambda i: (0, i))),
        compiler_params=_params("parallel"),
        name="post_mixer",
    )(merged, x2d, gate, shift, scale, wo, nw, wrt, brt)


def _invert_kernel(pos_ref, src_ref):
    def clear(i, carry):
        src_ref[i] = 0
        return carry

    def place(t, carry):
        src_ref[pos_ref[t]] = t
        return carry

    lax.fori_loop(0, src_ref.shape[0], clear, 0, unroll=SCALAR_UNROLL)
    lax.fori_loop(0, pos_ref.shape[0], place, 0, unroll=SCALAR_UNROLL)


def _invert_rows(pos, rows):
    smem = pl.BlockSpec(memory_space=pltpu.SMEM)
    return pl.pallas_call(
        _invert_kernel,
        out_shape=jax.ShapeDtypeStruct((rows,), jnp.int32),
        in_specs=[smem],
        out_specs=smem,
        name="invert_rows",
    )(pos)


def _route(cls, ncls, tm):
    m = cls.shape[0]
    ntiles_max = m // tm + ncls - 1
    onehot = (cls[:, None] == jnp.arange(ncls, dtype=jnp.int32)[None, :]).astype(jnp.int32)
    incl = jnp.cumsum(onehot, axis=0)
    tiles = (incl[-1] + tm - 1) // tm
    tile_end = jnp.cumsum(tiles)
    pos = jnp.sum(onehot * ((tile_end - tiles) * tm + incl - onehot), axis=1)
    src = _invert_rows(pos, ntiles_max * tm)
    t = jnp.arange(ntiles_max, dtype=jnp.int32)
    tile_class = jnp.minimum(jnp.sum((t[:, None] >= tile_end[None, :]).astype(jnp.int32), axis=1), ncls - 1)
    return pos, src, tile_class, tile_end[-1:]


def _row_gather(idx_ref, src_hbm, buf, sem, rows, first=0):
    for r in range(first, first + rows):
        pltpu.make_async_copy(src_hbm.at[pl.ds(idx_ref[0, r], 1), :], buf.at[pl.ds(r, 1), :],
                              sem).start(priority=r % 2)


def _row_gather_wait(src_hbm, buf, sem):
    pltpu.make_async_copy(src_hbm.at[pl.ds(0, buf.shape[0]), :], buf, sem).wait()


def _experts_kernel(meta_ref, nt_ref, src_ref, srcn_ref, hx_hbm, *rest, tm, nslot):
    w_refs, (y_ref, xbuf, sem) = rest[:3 * nslot], rest[3 * nslot:]
    t = pl.program_id(0)
    nt = nt_ref[0]
    slot = t % 2

    @pl.when(t == 0)
    def _():
        _row_gather(src_ref, hx_hbm, xbuf.at[0], sem.at[0], tm)

    @pl.when(t < nt)
    def _():
        cur, nxt, sem_cur, sem_nxt = xbuf.at[slot], xbuf.at[1 - slot], sem.at[slot], sem.at[1 - slot]
        _row_gather_wait(hx_hbm, cur, sem_cur)
        x = cur[:, 0:D_MODEL].astype(BF16)
        acc = jnp.zeros((tm, D_MODEL), F32)
        for j in range(nslot):
            wg_ref, wu_ref, wd_ref = w_refs[3 * j:3 * j + 3]
            e = meta_ref[t * (1 + nslot) + 1 + j]
            share = tm // (3 * nslot)
            a = jnp.dot(x, wg_ref[0, 0], preferred_element_type=F32)
            _row_gather(srcn_ref, hx_hbm, nxt, sem_nxt, share, first=(3 * j) * share)
            u = jnp.dot(x, wu_ref[0, 0], preferred_element_type=F32)
            _row_gather(srcn_ref, hx_hbm, nxt, sem_nxt, share, first=(3 * j + 1) * share)
            cw = jnp.zeros((tm, 1), F32)
            for c in range(EXPERTS_PER_GROUP):
                cw = cw + jnp.where(e == c, cur[:, D_MODEL + c:D_MODEL + c + 1], 0.0)
            acc = acc + jnp.dot((_silu(a) * u * cw).astype(BF16), wd_ref[0, 0], preferred_element_type=F32)
            last = tm - (3 * nslot - 1) * share if j == nslot - 1 else share
            _row_gather(srcn_ref, hx_hbm, nxt, sem_nxt, last, first=(3 * j + 2) * share)
        y_ref[...] = acc

        @pl.when(t == nt - 1)
        def _():
            _row_gather_wait(hx_hbm, nxt, sem_nxt)

    @pl.when(t >= nt)
    def _():
        y_ref[...] = jnp.zeros_like(y_ref)


def _experts(hx, src, tile_class, ntiles, class_table, wg, wu, wd, tm):
    ntiles_max = tile_class.shape[0]
    nslot = class_table.shape[1] - 1
    d, f = D_MODEL, D_EXPERT
    src3 = src.reshape(ntiles_max, 1, tm)
    meta = jnp.asarray(class_table, jnp.int32)[tile_class].reshape(-1)
    stride = 1 + nslot
    wspec = lambda j, shape: pl.BlockSpec(
        (1, 1) + shape, lambda t, meta, nt, j=j: (meta[t * stride], meta[t * stride + 1 + j], 0, 0))
    w_specs, w_args = [], []
    for j in range(nslot):
        w_specs += [wspec(j, (d, f)), wspec(j, (d, f)), wspec(j, (f, d))]
        w_args += [wg, wu, wd]
    return pl.pallas_call(
        functools.partial(_experts_kernel, tm=tm, nslot=nslot),
        out_shape=jax.ShapeDtypeStruct((ntiles_max * tm, d), F32),
        grid_spec=pltpu.PrefetchScalarGridSpec(
            num_scalar_prefetch=2,
            grid=(ntiles_max,),
            in_specs=[pl.BlockSpec((None, 1, tm), lambda t, meta, nt: (t, 0, 0), memory_space=pltpu.SMEM),
                      pl.BlockSpec((None, 1, tm),
                                   lambda t, meta, nt: (jnp.maximum(jnp.minimum(t + 1, nt[0] - 1), 0), 0, 0),
                                   memory_space=pltpu.SMEM),
                      pl.BlockSpec(memory_space=pl.ANY)] + w_specs,
            out_specs=pl.BlockSpec((tm, d), lambda t, meta, nt: (t, 0)),
            scratch_shapes=[pltpu.VMEM((2, tm, HX_WIDTH), F32), pltpu.SemaphoreType.DMA((2,))]),
        compiler_params=_params("arbitrary"),
        name="experts",
    )(meta, ntiles, src3, src3, hx, *w_args)


def _final_kernel(pos_ref, posn_ref, ys_hbm, x1_ref, gate_ref, nw_ref, o_ref, ybuf, sem, *, tm, nsteps):
    i = pl.program_id(0)
    slot = i % 2

    @pl.when(i == 0)
    def _():
        _row_gather(pos_ref, ys_hbm, ybuf.at[0], sem.at[0], tm)

    _row_gather(posn_ref, ys_hbm, ybuf.at[1 - slot], sem.at[1 - slot], tm)
    _row_gather_wait(ys_hbm, ybuf.at[slot], sem.at[slot])
    o_ref[...] = _rms(x1_ref[...] + _mod(gate_ref, tm) * ybuf[slot], nw_ref[...])

    @pl.when(i == nsteps - 1)
    def _():
        _row_gather_wait(ys_hbm, ybuf.at[1 - slot], sem.at[1 - slot])


def _final(ys, pos, x1, gate, gate_spec, nw, tm):
    m, d = x1.shape
    nsteps = m // tm
    pos3 = pos.reshape(nsteps, 1, tm)
    return pl.pallas_call(
        functools.partial(_final_kernel, tm=tm, nsteps=nsteps),
        out_shape=jax.ShapeDtypeStruct((m, d), F32),
        grid=(nsteps,),
        in_specs=[pl.BlockSpec((None, 1, tm), lambda i: (i, 0, 0), memory_space=pltpu.SMEM),
                  pl.BlockSpec((None, 1, tm), lambda i: (jnp.minimum(i + 1, nsteps - 1), 0, 0),
                               memory_space=pltpu.SMEM),
                  pl.BlockSpec(memory_space=pl.ANY),
                  pl.BlockSpec((tm, d), lambda i: (i, 0)),
                  gate_spec,
                  pl.BlockSpec((1, d), lambda i: (0, 0))],
        out_specs=pl.BlockSpec((tm, d), lambda i: (i, 0)),
        scratch_shapes=[pltpu.VMEM((2, tm, d), F32), pltpu.SemaphoreType.DMA((2,))],
        compiler_params=_params("arbitrary"),
        name="final",
    )(pos3, pos3, ys, x1, gate, nw)


def _mod_spec(idx, tm, t, ngrid):
    if t % tm == 0:
        per = t // tm
        shape, index = (None, None, 1, D_MODEL), lambda i: (i // per, idx, 0, 0)
    else:
        shape, index = (tm // t, None, 1, D_MODEL), lambda i: (i, idx, 0, 0)
    if ngrid == 2:
        return pl.BlockSpec(shape, lambda i, j: index(i))
    return pl.BlockSpec(shape, index)


def _trunk(x, mod, s_gla, s_gdn, s_conv, w, *, chunk, sub, nseq, t_valid, tm, tm_moe, pair_classes, act_dtype):
    nb, t_pad, d = x.shape
    m = nb * t_pad
    x2d = x.reshape(m, d)
    mods = [mod.reshape(nb, N_MOD, 1, d)] * N_MOD
    spec = lambda idx, ngrid: _mod_spec(idx, tm, t_pad, ngrid)

    proj, small, small_t = _inproj(x2d, mods[0], mods[1], (spec(0, 2), spec(1, 2)), w["norm1"],
                                   w["w_main"], w["w_small"], w["w_small_t"], tm, act_dtype)
    proj3 = proj.reshape(nb, t_pad, PROJ_MAIN)
    small3 = small.reshape(nb, t_pad, LANES)
    small_t4 = small_t.reshape(SMALL_ROWS, nb, t_pad // chunk, chunk).transpose(1, 2, 0, 3)
    merged, new_gla, new_gdn, new_conv = _mixer(proj3, small3, small_t4, w, s_gla, s_gdn, s_conv, chunk=chunk,
                                                sub=sub, nseq=nseq, t_valid=t_valid, out_dtype=act_dtype)
    x1, hx, gid = _post(merged.reshape(m, d), x2d, mods[2], mods[3], mods[4],
                        (spec(2, 1), spec(3, 1), spec(4, 1)),
                        w["w_out"], w["norm2"], w["w_router_t"], w["b_router_t"], tm)
    class_table = PAIR_CLASSES if pair_classes else GROUP_CLASSES
    pos, src, tile_class, ntiles = _route(gid[1 if pair_classes else 0], class_table.shape[0], tm_moe)
    ys = _experts(hx, src, tile_class, ntiles, class_table, w["w_gate"], w["w_up"], w["w_down"], tm_moe)
    tm_fin = min(tm, FINAL_TILE)
    y = _final(ys, pos, x1, mods[5], _mod_spec(5, tm_fin, t_pad, 1), w["final_norm"], tm_fin)
    return y.reshape(nb, t_pad, d), new_gla, new_gdn, new_conv


def _prep_weights(w_in, w_gla_a2, b_gla_a, gla_norm_w, w_conv, gdn_A_log, gdn_dt_bias, gdn_norm_w, w_out,
                  norm1_w, norm2_w, w_group_router, b_group_router, w_expert_router, b_expert_router,
                  w_exp_gate, w_exp_up, w_exp_down, final_norm_w):
    d = D_MODEL
    o = 0
    cols = {}
    for name, width in (("gla", 2 * GLA_QK + 2 * GLA_VW), ("ra", GLA_GATE_RANK), ("qkv", GDN_CONV_DIM),
                        ("zb", GDN_VW), ("beta", GDN_HEADS), ("a", GDN_HEADS), ("gates", 2 * D_MODEL)):
        cols[name] = w_in[:, o:o + width]
        o += width
    parts = [cols["gla"], cols["qkv"], jnp.concatenate([cols["zb"], cols["gates"]], axis=1)]
    w_main = jnp.concatenate([p[:, j * PROJ_PART:(j + 1) * PROJ_PART].astype(BF16)
                              for j in range(3) for p in parts], axis=1)
    small = jnp.concatenate([cols["ra"], cols["beta"], cols["a"]], axis=1)
    w_small = jnp.pad(small, ((0, 0), (0, LANES - small.shape[1]))).astype(BF16)
    w_small_t = small.T.astype(BF16)
    assert N_GROUPS == EXPERTS_PER_GROUP
    stride_pad = ((0, 0), (0, 0), (0, ROUTER_GROUP_STRIDE - EXPERTS_PER_GROUP))

    def router_rows(group_part, expert_part):
        both = jnp.concatenate([group_part, expert_part], axis=1).reshape(-1, 1 + N_GROUPS, EXPERTS_PER_GROUP)
        return jnp.pad(both, stride_pad).reshape(-1, ROUTER_ROWS)

    wr_t = router_rows(w_group_router, w_expert_router).T
    br_t = router_rows(b_group_router[None], b_expert_router[None]).T
    return dict(
        w_main=w_main, w_small=w_small, w_small_t=w_small_t,
        norm1=norm1_w.reshape(1, d), norm2=norm2_w.reshape(1, d), final_norm=final_norm_w.reshape(1, d),
        wa2=w_gla_a2, ba=b_gla_a.reshape(1, GLA_QK), gla_norm=gla_norm_w.reshape(1, GLA_DV),
        w_conv=w_conv, a_log=gdn_A_log, dt_bias=gdn_dt_bias, gdn_norm=gdn_norm_w.reshape(1, GDN_DV),
        w_out=w_out.astype(BF16), w_router_t=wr_t.astype(BF16), b_router_t=br_t,
        w_gate=w_exp_gate.astype(BF16), w_up=w_exp_up.astype(BF16), w_down=w_exp_down.astype(BF16),
    )


def kernel(x_prompt, x_sample, c_prompt, c_sample, state_gla, state_gdn, state_conv, w_ada, b_ada, norm1_w, w_in, w_gla_a2, b_gla_a, gla_norm_w, w_conv, gdn_A_log, gdn_dt_bias, gdn_norm_w, w_out, norm2_w, w_group_router, b_group_router, w_expert_router, b_expert_router, w_exp_gate, w_exp_up, w_exp_down, final_norm_w):
    assert w_ada.shape[0] == 1, "single layer"
    bp, tp, d = x_prompt.shape
    bs, ts, _ = x_sample.shape
    w = _prep_weights(w_in[0], w_gla_a2[0], b_gla_a[0], gla_norm_w[0], w_conv[0], gdn_A_log[0], gdn_dt_bias[0],
                      gdn_norm_w[0], w_out[0], norm1_w[0], norm2_w[0], w_group_router[0], b_group_router[0],
                      w_expert_router[0], b_expert_router[0], w_exp_gate[0], w_exp_up[0], w_exp_down[0],
                      final_norm_w)
    mod = _ada_mod(jnp.concatenate([c_prompt, c_sample], axis=0), w_ada[0], b_ada[0]).reshape(bp + bs, N_MOD, d)

    y_p, gla_p, gdn_p, conv_p = _trunk(x_prompt, mod[:bp], None, None, None, w,
                                       chunk=64, sub=GLA_SUBCHUNK, nseq=4, t_valid=tp, tm=min(1024, tp),
                                       tm_moe=min(512, tp), pair_classes=False, act_dtype=BF16)
    ts_pad = SUBLANES
    xs = jnp.pad(x_sample, ((0, 0), (0, ts_pad - ts), (0, 0)))
    y_s, gla_s, gdn_s, conv_s = _trunk(xs, mod[bp:], state_gla[0], state_gdn[0], state_conv[0], w,
                                       chunk=ts_pad, sub=ts_pad, nseq=4, t_valid=ts, tm=min(512, bs * ts_pad),
                                       tm_moe=min(256, bs * ts_pad), pair_classes=False, act_dtype=F32)
    return (y_p, y_s[:, :ts], gla_p[None], gdn_p[None], conv_p[None], gla_s[None], gdn_s[None], conv_s[None])
```

```python
import functools
import itertools
import math

import jax
import numpy as np
import jax.numpy as jnp
from jax import lax
from jax.experimental import pallas as pl
from jax.experimental.pallas import tpu as pltpu

F32 = jnp.float32
BF16 = jnp.bfloat16
HIGHEST = lax.Precision.HIGHEST

D_MODEL = 1024
GLA_HEADS = 4
GLA_DK = 128
GLA_DV = 256
GLA_QK = GLA_HEADS * GLA_DK
GLA_VW = GLA_HEADS * GLA_DV
GLA_GATE_RANK = 16
GLA_TAU = 16.0
GLA_SUBCHUNK = 16
GDN_HEADS = 8
GDN_DK = 128
GDN_DV = 128
GDN_QK = GDN_HEADS * GDN_DK
GDN_VW = GDN_HEADS * GDN_DV
GDN_CONV = 4
GDN_CONV_DIM = 2 * GDN_QK + GDN_VW
GDN_INV_BLOCK = 16
N_GROUPS = 4
EXPERTS_PER_GROUP = 4
N_EXPERTS = N_GROUPS * EXPERTS_PER_GROUP
D_EXPERT = D_MODEL // 2
N_MOD = 6
NORM_EPS = 1e-6
L2_EPS = 1e-6

LANES = 128
SUBLANES = 8
VMEM_LIMIT = 56 * 1024 * 1024

PROJ_MAIN = 2 * GLA_QK + 2 * GLA_VW + GDN_CONV_DIM + GDN_VW + 2 * D_MODEL
PROJ_PART = 1024
INPROJ_TN = 3 * PROJ_PART
COL_QK, COL_V, COL_GA = 0, 3, 6
COL_CQ, COL_CK, COL_CV = 1, 4, 7
COL_ZB, COL_GATE_A, COL_GATE_B = 2, 5, 8
SMALL_RA, SMALL_BETA, SMALL_A = 0, GLA_GATE_RANK, GLA_GATE_RANK + GDN_HEADS
SMALL_ROWS = 32
ROUTER_GROUP_STRIDE = 8
ROUTER_ROWS = ROUTER_GROUP_STRIDE * (1 + N_GROUPS)

EXPERT_PAIRS = tuple(itertools.combinations(range(EXPERTS_PER_GROUP), 2))
GROUP_CLASSES = np.array([[g] + list(range(EXPERTS_PER_GROUP)) for g in range(N_GROUPS)], np.int32)
PAIR_CLASSES = np.array([[g, a, b] for g in range(N_GROUPS) for a, b in EXPERT_PAIRS], np.int32)

HX_WIDTH = D_MODEL + LANES
SCALAR_UNROLL = 16
FINAL_TILE = 256

NT_DIMS = (((1,), (1,)), ((), ()))
TN_DIMS = (((0,), (0,)), ((), ()))


def _sigmoid(x):
    return 0.5 * jnp.tanh(0.5 * x) + 0.5


def _silu(x):
    return x * _sigmoid(x)


def _softplus(x):
    return jnp.maximum(x, 0.0) + jnp.log1p(jnp.exp(-jnp.abs(x)))


def _rms(x, w):
    return x * lax.rsqrt(jnp.mean(x * x, axis=-1, keepdims=True) + NORM_EPS) * w


def _mod(ref, rows):
    v = ref[...]
    if v.ndim == 3:
        v = jnp.broadcast_to(v, (v.shape[0], rows // v.shape[0], v.shape[2])).reshape(rows, v.shape[2])
    return v


def _params(*sem):
    return pltpu.CompilerParams(dimension_semantics=sem, vmem_limit_bytes=VMEM_LIMIT)


def _ada_kernel(c_ref, w_ref, b_ref, o_ref):
    cs = _silu(c_ref[...])
    o_ref[...] = jnp.dot(cs.astype(BF16), w_ref[...].astype(BF16), preferred_element_type=F32) + b_ref[...]


def _ada_mod(c_all, w_ada, b_ada):
    rows, d = c_all.shape
    n = w_ada.shape[1]
    tn = 1024
    return pl.pallas_call(
        _ada_kernel,
        out_shape=jax.ShapeDtypeStruct((rows, n), F32),
        grid=(n // tn,),
        in_specs=[pl.BlockSpec((rows, d), lambda j: (0, 0)),
                  pl.BlockSpec((d, tn), lambda j: (0, j)),
                  pl.BlockSpec((1, tn), lambda j: (0, j))],
        out_specs=pl.BlockSpec((rows, tn), lambda j: (0, j)),
        compiler_params=_params("arbitrary"),
        name="ada_mod",
    )(c_all, w_ada, b_ada.reshape(1, n))


def _inproj_kernel(x_ref, shift_ref, scale_ref, nw_ref, w_ref, ws_ref, wst_ref,
                   o_ref, os_ref, ost_ref, h_scr):
    @pl.when(pl.program_id(1) == 0)
    def _():
        rows = x_ref.shape[0]
        h = _rms(x_ref[...], nw_ref[...]) * (1.0 + _mod(scale_ref, rows)) + _mod(shift_ref, rows)
        hb = h.astype(BF16)
        h_scr[...] = hb
        os_ref[...] = jnp.dot(hb, ws_ref[...], preferred_element_type=F32)
        ost_ref[...] = lax.dot_general(wst_ref[...], hb, NT_DIMS, preferred_element_type=F32)

    o_ref[...] = jnp.dot(h_scr[...], w_ref[...], preferred_element_type=F32).astype(o_ref.dtype)


def _inproj(x2d, shift, scale, mod_specs, nw, w_main, w_small, w_small_t, tm, out_dtype):
    m, d = x2d.shape
    n = w_main.shape[1]
    tn = INPROJ_TN
    return pl.pallas_call(
        _inproj_kernel,
        out_shape=(jax.ShapeDtypeStruct((m, n), out_dtype),
                   jax.ShapeDtypeStruct((m, LANES), F32),
                   jax.ShapeDtypeStruct((SMALL_ROWS, m), F32)),
        grid=(m // tm, n // tn),
        in_specs=[pl.BlockSpec((tm, d), lambda i, j: (i, 0)),
                  mod_specs[0], mod_specs[1],
                  pl.BlockSpec((1, d), lambda i, j: (0, 0)),
                  pl.BlockSpec((d, tn), lambda i, j: (0, j)),
                  pl.BlockSpec((d, LANES), lambda i, j: (0, 0)),
                  pl.BlockSpec((SMALL_ROWS, d), lambda i, j: (0, 0))],
        out_specs=(pl.BlockSpec((tm, tn), lambda i, j: (i, j)),
                   pl.BlockSpec((tm, LANES), lambda i, j: (i, 0)),
                   pl.BlockSpec((SMALL_ROWS, tm), lambda i, j: (0, i))),
        scratch_shapes=[pltpu.VMEM((tm, d), BF16)],
        compiler_params=_params("parallel", "arbitrary"),
        name="inproj",
    )(x2d, shift, scale, nw, w_main, w_small, w_small_t)


def _gla_stages(ins, s0_ref, sout_ref, st_scr, out, *, chunk, sub, nchunks, nseq, t_valid):
    qk_ref, v_ref, ga_ref, gt_ref, sm_ref, wa2_ref, ba_ref, nw_ref = ins
    has_state = s0_ref is not None
    n = pl.program_id(1)
    masked = t_valid < chunk * nchunks
    units = [(s, h) for s in range(nseq) for h in range(GLA_HEADS)]

    @pl.when(n == 0)
    def _():
        for u, (s, h) in enumerate(units):
            st_scr[u] = s0_ref[s, h].T if has_state else jnp.zeros((GLA_DV, GLA_DK), F32)

    row = lax.broadcasted_iota(jnp.int32, (chunk, chunk), 0)
    col = lax.broadcasted_iota(jnp.int32, (chunk, chunk), 1)
    rowi = lax.broadcasted_iota(jnp.int32, (chunk, 1), 0)
    valid = (rowi + n * chunk) < t_valid
    tri = (row >= col).astype(F32)
    wa2 = wa2_ref[...].astype(BF16)
    b_all = []
    for s in range(nseq):
        ra = sm_ref[s, :, SMALL_RA:SMALL_RA + GLA_GATE_RANK]
        x = jnp.dot(ra.astype(BF16), wa2, preferred_element_type=F32) + ba_ref[...]
        g = (jnp.minimum(x, 0.0) - jnp.log1p(jnp.exp(-jnp.abs(x)))) * (1.0 / GLA_TAU)
        if masked:
            g = jnp.where(valid, g, 0.0)
        b_all.append(jnp.dot(tri, g, precision=HIGHEST, preferred_element_type=F32))
    yield

    q, k, v, b = [], [], [], []
    for s, h in units:
        q.append(qk_ref[s, :, h * GLA_DK:(h + 1) * GLA_DK].astype(F32) * (GLA_DK ** -0.5))
        kh = qk_ref[s, :, GLA_QK + h * GLA_DK:GLA_QK + (h + 1) * GLA_DK].astype(F32)
        k.append(jnp.where(valid, kh, 0.0) if masked else kh)
        v.append(v_ref[s, :, h * GLA_DV:(h + 1) * GLA_DV].astype(BF16))
        b.append(b_all[s][:, h * GLA_DK:(h + 1) * GLA_DK])
    nu = range(len(units))
    st = [st_scr[u] for u in nu]
    o = [lax.dot_general((q[u] * jnp.exp(b[u])).astype(BF16), st[u].astype(BF16), NT_DIMS,
                         preferred_element_type=F32) for u in nu]
    yield
    blocks = [[] for _ in nu]
    for i in range(chunk // sub):
        r0, r1 = i * sub, (i + 1) * sub
        for u in nu:
            bref = b[u][r0 - 1:r0] if i > 0 else jnp.zeros((1, GLA_DK), F32)
            qt = (q[u][r0:r1] * jnp.exp(b[u][r0:r1] - bref)).astype(BF16)
            expo = bref - b[u]
            if r1 < chunk:
                expo = jnp.where(rowi < r1, expo, 0.0)
            kt = (k[u] * jnp.exp(expo)).astype(BF16)
            blocks[u].append(lax.dot_general(qt, kt, NT_DIMS, preferred_element_type=F32))
        yield
    for u in nu:
        a = blocks[u][0] if len(blocks[u]) == 1 else jnp.concatenate(blocks[u], axis=0)
        a = jnp.where(col <= row, a, 0.0)
        o[u] = o[u] + jnp.dot(a.astype(BF16), v[u], preferred_element_type=F32)
    yield
    for u in nu:
        bl = b[u][chunk - 1:chunk]
        kt = (k[u] * jnp.exp(bl - b[u])).astype(BF16)
        st_scr[u] = st[u] * jnp.exp(bl) + lax.dot_general(v[u], kt, TN_DIMS, preferred_element_type=F32)
    yield
    for u, (s, h) in enumerate(units):
        sl = slice(h * GLA_DV, (h + 1) * GLA_DV)
        out[s, h] = (_rms(o[u], nw_ref[...]) * _silu(ga_ref[s, :, sl].astype(F32))
                     * _sigmoid(gt_ref[s, :, sl].astype(F32)))

    @pl.when(n == nchunks - 1)
    def _():
        for u, (s, h) in enumerate(units):
            sout_ref[s, h] = st_scr[u].T


def _gdn_stages(ins, s0_ref, c0_ref, sout_ref, cout_ref, s_scr, xp_scr, out, *, chunk, nchunks, nseq, t_valid):
    (cq_ref, ck_ref, cv_ref, zb_ref, gt_ref, sm_ref, smt_ref, wc_ref, alog_ref, dtb_ref, alogc_ref, dtbc_ref,
     nw_ref) = ins
    has_state = s0_ref is not None
    n = pl.program_id(1)
    masked = t_valid < chunk * nchunks
    pad = SUBLANES
    units = [(s, h) for s in range(nseq) for h in range(GDN_HEADS)]
    nu = range(len(units))

    @pl.when(n == 0)
    def _():
        for u, (s, h) in enumerate(units):
            s_scr[u] = s0_ref[s, h] if has_state else jnp.zeros((GDN_DK, GDN_DV), F32)
        for s in range(nseq):
            xp_scr[s, 0:pad, :] = jnp.zeros((pad, GDN_CONV_DIM), F32)
            if has_state:
                xp_scr[s, pad - (GDN_CONV - 1):pad, :] = c0_ref[s]

    row = lax.broadcasted_iota(jnp.int32, (chunk, chunk), 0)
    col = lax.broadcasted_iota(jnp.int32, (chunk, chunk), 1)
    valid_c = (lax.broadcasted_iota(jnp.int32, (chunk, 1), 0) + n * chunk) < t_valid
    valid_r = (lax.broadcasted_iota(jnp.int32, (1, chunk), 1) + n * chunk) < t_valid
    tri_c = (row >= col).astype(F32)
    tri_r = (row <= col).astype(F32)
    b_col, b_row, beta_col = [], [], []
    for s in range(nseq):
        for c, ref in enumerate((cq_ref, ck_ref, cv_ref)):
            xp_scr[s, pad:pad + chunk, c * PROJ_PART:(c + 1) * PROJ_PART] = ref[s].astype(F32)
        g_col = -jnp.exp(alog_ref[...]) * _softplus(sm_ref[s, :, SMALL_A:SMALL_A + GDN_HEADS] + dtb_ref[...])
        bt = _sigmoid(sm_ref[s, :, SMALL_BETA:SMALL_BETA + GDN_HEADS])
        g_row = -jnp.exp(alogc_ref[...]) * _softplus(smt_ref[s, SMALL_A:SMALL_A + GDN_HEADS, :] + dtbc_ref[...])
        if masked:
            g_col = jnp.where(valid_c, g_col, 0.0)
            bt = jnp.where(valid_c, bt, 0.0)
            g_row = jnp.where(valid_r, g_row, 0.0)
        beta_col.append(bt)
        b_col.append(jnp.dot(tri_c, g_col, precision=HIGHEST, preferred_element_type=F32))
        b_row.append(jnp.dot(g_row, tri_r, precision=HIGHEST, preferred_element_type=F32))
    yield

    def conv_silu(s, c0):
        cs = slice(c0, c0 + LANES)
        y = xp_scr[s, pad:pad + chunk, cs] * wc_ref[GDN_CONV - 1:GDN_CONV, cs]
        for j in range(1, GDN_CONV):
            y = y + xp_scr[s, pad - j:pad - j + chunk, cs] * wc_ref[GDN_CONV - 1 - j:GDN_CONV - j, cs]
        return _silu(y)

    bdot = lambda a, b: jnp.dot(a.astype(BF16), b.astype(BF16), preferred_element_type=F32)
    q, k, v, kb, dec, bc, beta = [], [], [], [], [], [], []
    for u, (s, h) in enumerate(units):
        qh = conv_silu(s, h * GDN_DK)
        kh = conv_silu(s, GDN_QK + h * GDN_DK)
        v.append(conv_silu(s, 2 * GDN_QK + h * GDN_DV))
        q.append(qh * lax.rsqrt(jnp.sum(qh * qh, axis=-1, keepdims=True) + L2_EPS) * (GDN_DK ** -0.5))
        k.append(kh * lax.rsqrt(jnp.sum(kh * kh, axis=-1, keepdims=True) + L2_EPS))
        bc.append(b_col[s][:, h:h + 1])
        beta.append(beta_col[s][:, h:h + 1])
        dec.append(jnp.where(row >= col, jnp.exp(jnp.minimum(bc[u] - b_row[s][h:h + 1, :], 0.0)), 0.0))
        kb.append(k[u] * beta[u])
        if u % GDN_HEADS == GDN_HEADS - 1:
            yield
    kq = [lax.dot_general(jnp.concatenate([kb[u], q[u]], axis=0).astype(BF16), k[u].astype(BF16), NT_DIMS,
                          preferred_element_type=F32) for u in nu]
    yield
    blk = min(GDN_INV_BLOCK, chunk)
    nblk = chunk // blk
    same_blk = (row // blk) == (col // blk)
    lmat = [jnp.where(row > col, kq[u][:chunk] * dec[u], 0.0) for u in nu]
    p = [jnp.where(same_blk, -lmat[u], 0.0) for u in nu]
    r = p
    for _ in range(int(math.log2(blk)) - 1):
        p = [bdot(p[u], p[u]) for u in nu]
        r = [r[u] + p[u] + bdot(r[u], p[u]) for u in nu]
        yield
    if nblk > 1:
        lo = [jnp.where(same_blk, 0.0, lmat[u]) for u in nu]
        p = [-(lo[u] + bdot(r[u], lo[u])) for u in nu]
        qm = p
        yield
        for _ in range(int(math.log2(nblk)) - 1):
            p = [bdot(p[u], p[u]) for u in nu]
            qm = [qm[u] + p[u] + bdot(qm[u], p[u]) for u in nu]
            yield
        r = [r[u] + qm[u] + bdot(qm[u], r[u]) for u in nu]
    rhs = [jnp.concatenate([v[u] * beta[u], kb[u] * jnp.exp(bc[u])], axis=-1) for u in nu]
    uw = [rhs[u] + bdot(r[u], rhs[u]) for u in nu]
    yield
    st = [s_scr[u] for u in nu]
    ws = [bdot(jnp.concatenate([uw[u][:, GDN_DV:], q[u] * jnp.exp(bc[u])], axis=0), st[u]) for u in nu]
    v_new = [uw[u][:, :GDN_DV] - ws[u][:chunk] for u in nu]
    yield
    o = [ws[u][chunk:] + bdot(kq[u][chunk:] * dec[u], v_new[u]) for u in nu]
    for u in nu:
        bl = bc[u][chunk - 1:chunk, :]
        s_scr[u] = jnp.exp(bl) * st[u] + lax.dot_general((k[u] * jnp.exp(bl - bc[u])).astype(BF16),
                                                         v_new[u].astype(BF16), TN_DIMS, preferred_element_type=F32)
    yield
    for u, (s, h) in enumerate(units):
        sl = slice(h * GDN_DV, (h + 1) * GDN_DV)
        out[s, h] = (_rms(o[u], nw_ref[...]) * _silu(zb_ref[s, :, sl].astype(F32))
                     * _sigmoid(gt_ref[s, :, sl].astype(F32)))

    @pl.when(n == nchunks - 1)
    def _():
        last = t_valid - (nchunks - 1) * chunk
        for s in range(nseq):
            cout_ref[s] = xp_scr[s, pad + last - (GDN_CONV - 1):pad + last, :]
        for u, (s, h) in enumerate(units):
            sout_ref[s, h] = s_scr[u]

    for s in range(nseq):
        xp_scr[s, 0:pad, :] = xp_scr[s, chunk:chunk + pad, :]


N_GLA_IN, N_GDN_IN = 8, 13


def _mixer_kernel(*refs, chunk, sub, nchunks, nseq, has_state, t_valid):
    gla_in, refs = refs[:N_GLA_IN], refs[N_GLA_IN:]
    gdn_in, refs = refs[:N_GDN_IN], refs[N_GDN_IN:]
    if has_state:
        (sa0_ref, sb0_ref, c0_ref), refs = refs[:3], refs[3:]
    else:
        sa0_ref = sb0_ref = c0_ref = None
    o_ref, sa_out, sb_out, c_out, sa_scr, sb_scr, xp_scr = refs
    out_a, out_b = {}, {}
    gla = _gla_stages(gla_in, sa0_ref, sa_out, sa_scr, out_a,
                      chunk=chunk, sub=sub, nchunks=nchunks, nseq=nseq, t_valid=t_valid)
    gdn = _gdn_stages(gdn_in, sb0_ref, c0_ref, sb_out, c_out, sb_scr, xp_scr, out_b,
                      chunk=chunk, nchunks=nchunks, nseq=nseq, t_valid=t_valid)
    live = [gdn, gla]
    while live:
        for g in list(live):
            if next(g, StopIteration) is StopIteration:
                live.remove(g)
    per = GLA_DV // GDN_DV
    for (s, h), ob in out_b.items():
        oa = out_a[s, h // per][:, (h % per) * GDN_DV:(h % per + 1) * GDN_DV]
        o_ref[s, :, h * GDN_DV:(h + 1) * GDN_DV] = (oa + ob).astype(o_ref.dtype)


def _mixer(proj3, small3, small_t4, w, s_gla, s_gdn, s_conv, *, chunk, sub, nseq, t_valid, out_dtype):
    nb, t_pad, _ = proj3.shape
    nchunks = t_pad // chunk
    has_state = s_gla is not None
    blk = lambda width, c: pl.BlockSpec((nseq, chunk, width), lambda b, n, c=c: (b, n, c))
    col = lambda c: blk(PROJ_PART, c)
    full = lambda a: pl.BlockSpec(a.shape, lambda b, n: (0,) * a.ndim)
    alog_r, dtb_r = w["a_log"].reshape(1, GDN_HEADS), w["dt_bias"].reshape(1, GDN_HEADS)
    alog_c, dtb_c = w["a_log"].reshape(GDN_HEADS, 1), w["dt_bias"].reshape(GDN_HEADS, 1)
    small_spec = blk(LANES, 0)
    gla_consts = [w["wa2"], w["ba"], w["gla_norm"]]
    gdn_consts = [w["w_conv"], alog_r, dtb_r, alog_c, dtb_c, w["gdn_norm"]]
    in_specs = ([col(COL_QK), col(COL_V), col(COL_GA), col(COL_GATE_A), small_spec] + [full(a) for a in gla_consts]
                + [col(COL_CQ), col(COL_CK), col(COL_CV), col(COL_ZB), col(COL_GATE_B), small_spec,
                   pl.BlockSpec((nseq, None, SMALL_ROWS, chunk), lambda b, n: (b, n, 0, 0))]
                + [full(a) for a in gdn_consts])
    args = [proj3] * 4 + [small3] + gla_consts + [proj3] * 5 + [small3, small_t4] + gdn_consts
    assert len(in_specs) == N_GLA_IN + N_GDN_IN
    sa_spec = pl.BlockSpec((nseq, GLA_HEADS, GLA_DK, GLA_DV), lambda b, n: (b, 0, 0, 0))
    sb_spec = pl.BlockSpec((nseq, GDN_HEADS, GDN_DK, GDN_DV), lambda b, n: (b, 0, 0, 0))
    conv_spec = pl.BlockSpec((nseq, GDN_CONV - 1, GDN_CONV_DIM), lambda b, n: (b, 0, 0))
    if has_state:
        in_specs += [sa_spec, sb_spec, conv_spec]
        args += [s_gla, s_gdn, s_conv]
    return pl.pallas_call(
        functools.partial(_mixer_kernel, chunk=chunk, sub=sub, nchunks=nchunks, nseq=nseq, has_state=has_state,
                          t_valid=t_valid),
        out_shape=(jax.ShapeDtypeStruct((nb, t_pad, D_MODEL), out_dtype),
                   jax.ShapeDtypeStruct((nb, GLA_HEADS, GLA_DK, GLA_DV), F32),
                   jax.ShapeDtypeStruct((nb, GDN_HEADS, GDN_DK, GDN_DV), F32),
                   jax.ShapeDtypeStruct((nb, GDN_CONV - 1, GDN_CONV_DIM), F32)),
        grid=(nb // nseq, nchunks),
        in_specs=in_specs,
        out_specs=(blk(D_MODEL, 0), sa_spec, sb_spec, conv_spec),
        scratch_shapes=[pltpu.VMEM((nseq * GLA_HEADS, GLA_DV, GLA_DK), F32),
                        pltpu.VMEM((nseq * GDN_HEADS, GDN_DK, GDN_DV), F32),
                        pltpu.VMEM((nseq, chunk + 2 * SUBLANES, GDN_CONV_DIM), F32)],
        compiler_params=_params("parallel", "arbitrary"),
        name="mixer",
    )(*args)


def _post_kernel(mg_ref, x_ref, gate_ref, shift_ref, scale_ref, wo_ref, nw_ref, wrt_ref, brt_ref,
                 x1_ref, hx_ref, gid_ref):
    y = jnp.dot(mg_ref[...].astype(BF16), wo_ref[...], preferred_element_type=F32)
    rows = x_ref.shape[0]
    x1 = x_ref[...] + _mod(gate_ref, rows) * y
    x1_ref[...] = x1
    h2 = _rms(x1, nw_ref[...]) * (1.0 + _mod(scale_ref, rows)) + _mod(shift_ref, rows)
    hx_ref[:, 0:D_MODEL] = h2
    lt = lax.dot_general(wrt_ref[...], h2.astype(BF16), NT_DIMS, preferred_element_type=F32) + brt_ref[...]
    tm = lt.shape[1]
    gl = lt[0:N_GROUPS]
    gidx = lax.broadcasted_iota(jnp.int32, (N_GROUPS, tm), 0)
    gmax = jnp.max(gl, axis=0, keepdims=True)
    g_w = 1.0 / jnp.sum(jnp.exp(gl - gmax), axis=0, keepdims=True)
    g_sel = jnp.min(jnp.where(gl == gmax, gidx, N_GROUPS), axis=0, keepdims=True)
    el = jnp.zeros((EXPERTS_PER_GROUP, tm), F32)
    for g in range(N_GROUPS):
        r = ROUTER_GROUP_STRIDE * (1 + g)
        el = el + jnp.where(g_sel == g, lt[r:r + EXPERTS_PER_GROUP], 0.0)
    eidx = lax.broadcasted_iota(jnp.int32, (EXPERTS_PER_GROUP, tm), 0)
    m1 = jnp.max(el, axis=0, keepdims=True)
    i1 = jnp.min(jnp.where(el == m1, eidx, EXPERTS_PER_GROUP), axis=0, keepdims=True)
    el2 = jnp.where(eidx == i1, -jnp.inf, el)
    m2 = jnp.max(el2, axis=0, keepdims=True)
    i2 = jnp.min(jnp.where(el2 == m2, eidx, EXPERTS_PER_GROUP), axis=0, keepdims=True)
    r21 = jnp.exp(m2 - m1)
    w1 = 1.0 / (1.0 + r21)
    w2 = r21 / (1.0 + r21)
    comb_t = g_w * (jnp.where(eidx == i1, w1, 0.0) + jnp.where(eidx == i2, w2, 0.0))
    ident = (lax.broadcasted_iota(jnp.int32, (EXPERTS_PER_GROUP, LANES), 0)
             == lax.broadcasted_iota(jnp.int32, (EXPERTS_PER_GROUP, LANES), 1)).astype(F32)
    hx_ref[:, D_MODEL:] = lax.dot_general(comb_t, ident, TN_DIMS, precision=HIGHEST, preferred_element_type=F32)
    lo, hi = jnp.minimum(i1, i2), jnp.maximum(i1, i2)
    pair = jnp.right_shift(lo * (2 * EXPERTS_PER_GROUP - 1 - lo), 1) + hi - lo - 1
    rid = lax.broadcasted_iota(jnp.int32, gid_ref.shape, 0)
    gid_ref[...] = jnp.where(rid == 0, g_sel, g_sel * len(EXPERT_PAIRS) + pair)


def _post(merged, x2d, gate, shift, scale, mod_specs, wo, nw, wrt, brt, tm):
    m, d = x2d.shape
    row = lambda: pl.BlockSpec((tm, d), lambda i: (i, 0))
    full = lambda a: pl.BlockSpec(a.shape, lambda i: (0,) * a.ndim)
    return pl.pallas_call(
        _post_kernel,
        out_shape=(jax.ShapeDtypeStruct((m, d), F32),
                   jax.ShapeDtypeStruct((m, HX_WIDTH), F32),
                   jax.ShapeDtypeStruct((SUBLANES, m), jnp.int32)),
        grid=(m // tm,),
        in_specs=[row(), row(), mod_specs[0], mod_specs[1], mod_specs[2],
                  full(wo), full(nw), full(wrt), full(brt)],
        out_specs=(row(), pl.BlockSpec((tm, HX_WIDTH), lambda i: (i, 0)),
                   pl.BlockSpec((SUBLANES, tm), lambda i: (0, i))),
        compiler_params=_params("parallel"),
        name="post_mixer",
    )(merged, x2d, gate, shift, scale, wo, nw, wrt, brt)


def _invert_kernel(pos_ref, src_ref):
    def clear(i, carry):
        src_ref[i] = 0
        return carry

    def place(t, carry):
        src_ref[pos_ref[t]] = t
        return carry

    lax.fori_loop(0, src_ref.shape[0], clear, 0, unroll=SCALAR_UNROLL)
    lax.fori_loop(0, pos_ref.shape[0], place, 0, unroll=SCALAR_UNROLL)


def _invert_rows(pos, rows):
    smem = pl.BlockSpec(memory_space=pltpu.SMEM)
    return pl.pallas_call(
        _invert_kernel,
        out_shape=jax.ShapeDtypeStruct((rows,), jnp.int32),
        in_specs=[smem],
        out_specs=smem,
        name="invert_rows",
    )(pos)


def _route(cls, ncls, tm):
    m = cls.shape[0]
    ntiles_max = m // tm + ncls - 1
    onehot = (cls[:, None] == jnp.arange(ncls, dtype=jnp.int32)[None, :]).astype(jnp.int32)
    incl = jnp.cumsum(onehot, axis=0)
    tiles = (incl[-1] + tm - 1) // tm
    tile_end = jnp.cumsum(tiles)
    pos = jnp.sum(onehot * ((tile_end - tiles) * tm + incl - onehot), axis=1)
    src = _invert_rows(pos, ntiles_max * tm)
    t = jnp.arange(ntiles_max, dtype=jnp.int32)
    tile_class = jnp.minimum(jnp.sum((t[:, None] >= tile_end[None, :]).astype(jnp.int32), axis=1), ncls - 1)
    return pos, src, tile_class, tile_end[-1:]


def _row_gather(idx_ref, src_hbm, buf, sem, rows, first=0):
    for r in range(first, first + rows):
        pltpu.make_async_copy(src_hbm.at[pl.ds(idx_ref[0, r], 1)], buf.at[pl.ds(r, 1)],
                              sem).start(priority=r % 2)


def _row_gather_wait(src_hbm, buf, sem):
    pltpu.make_async_copy(src_hbm.at[pl.ds(0, buf.shape[0])], buf, sem).wait()


def _experts_kernel(meta_ref, nt_ref, src_ref, srcn_ref, hx_hbm, *rest, tm, nslot):
    w_refs, (y_ref, xbuf, sem) = rest[:3 * nslot], rest[3 * nslot:]
    t = pl.program_id(0)
    nt = nt_ref[0]
    slot = t % 2

    @pl.when(t == 0)
    def _():
        _row_gather(src_ref, hx_hbm, xbuf.at[0], sem.at[0], tm)

    @pl.when(t < nt)
    def _():
        cur, nxt, sem_cur, sem_nxt = xbuf.at[slot], xbuf.at[1 - slot], sem.at[slot], sem.at[1 - slot]
        _row_gather_wait(hx_hbm, cur, sem_cur)
        x = cur[:, 0:D_MODEL].astype(BF16)
        acc = jnp.zeros((tm, D_MODEL), F32)
        for j in range(nslot):
            wg_ref, wu_ref, wd_ref = w_refs[3 * j:3 * j + 3]
            e = meta_ref[t * (1 + nslot) + 1 + j]
            share = tm // (3 * nslot)
            a = jnp.dot(x, wg_ref[0, 0], preferred_element_type=F32)
            _row_gather(srcn_ref, hx_hbm, nxt, sem_nxt, share, first=(3 * j) * share)
            u = jnp.dot(x, wu_ref[0, 0], preferred_element_type=F32)
            _row_gather(srcn_ref, hx_hbm, nxt, sem_nxt, share, first=(3 * j + 1) * share)
            cw = jnp.zeros((tm, 1), F32)
            for c in range(EXPERTS_PER_GROUP):
                cw = cw + jnp.where(e == c, cur[:, D_MODEL + c:D_MODEL + c + 1], 0.0)
            acc = acc + jnp.dot((_silu(a) * u * cw).astype(BF16), wd_ref[0, 0], preferred_element_type=F32)
            last = tm - (3 * nslot - 1) * share if j == nslot - 1 else share
            _row_gather(srcn_ref, hx_hbm, nxt, sem_nxt, last, first=(3 * j + 2) * share)
        y_ref[:, 0, :] = acc

        @pl.when(t == nt - 1)
        def _():
            _row_gather_wait(hx_hbm, nxt, sem_nxt)

    @pl.when(t >= nt)
    def _():
        y_ref[...] = jnp.zeros_like(y_ref)


def _experts(hx, src, tile_class, ntiles, class_table, wg, wu, wd, tm):
    ntiles_max = tile_class.shape[0]
    nslot = class_table.shape[1] - 1
    d, f = D_MODEL, D_EXPERT
    src3 = src.reshape(ntiles_max, 1, tm)
    meta = jnp.asarray(class_table, jnp.int32)[tile_class].reshape(-1)
    stride = 1 + nslot
    wspec = lambda j, shape: pl.BlockSpec(
        (1, 1) + shape, lambda t, meta, nt, j=j: (meta[t * stride], meta[t * stride + 1 + j], 0, 0))
    w_specs, w_args = [], []
    for j in range(nslot):
        w_specs += [wspec(j, (d, f)), wspec(j, (d, f)), wspec(j, (f, d))]
        w_args += [wg, wu, wd]
    return pl.pallas_call(
        functools.partial(_experts_kernel, tm=tm, nslot=nslot),
        out_shape=jax.ShapeDtypeStruct((ntiles_max * tm, 1, d), F32),
        grid_spec=pltpu.PrefetchScalarGridSpec(
            num_scalar_prefetch=2,
            grid=(ntiles_max,),
            in_specs=[pl.BlockSpec((None, 1, tm), lambda t, meta, nt: (t, 0, 0), memory_space=pltpu.SMEM),
                      pl.BlockSpec((None, 1, tm),
                                   lambda t, meta, nt: (jnp.maximum(jnp.minimum(t + 1, nt[0] - 1), 0), 0, 0),
                                   memory_space=pltpu.SMEM),
                      pl.BlockSpec(memory_space=pl.ANY)] + w_specs,
            out_specs=pl.BlockSpec((tm, 1, d), lambda t, meta, nt: (t, 0, 0)),
            scratch_shapes=[pltpu.VMEM((2, tm, HX_WIDTH), F32), pltpu.SemaphoreType.DMA((2,))]),
        compiler_params=_params("arbitrary"),
        name="experts",
    )(meta, ntiles, src3, src3, hx, *w_args)


def _final_kernel(pos_ref, posn_ref, ys_hbm, x1_ref, gate_ref, nw_ref, o_ref, ybuf, sem, *, tm, nsteps):
    i = pl.program_id(0)
    slot = i % 2

    @pl.when(i == 0)
    def _():
        _row_gather(pos_ref, ys_hbm, ybuf.at[0], sem.at[0], tm)

    _row_gather(posn_ref, ys_hbm, ybuf.at[1 - slot], sem.at[1 - slot], tm)
    _row_gather_wait(ys_hbm, ybuf.at[slot], sem.at[slot])
    o_ref[...] = _rms(x1_ref[...] + _mod(gate_ref, tm) * ybuf[slot, :, 0, :], nw_ref[...])

    @pl.when(i == nsteps - 1)
    def _():
        _row_gather_wait(ys_hbm, ybuf.at[1 - slot], sem.at[1 - slot])


def _final(ys, pos, x1, gate, gate_spec, nw, tm):
    m, d = x1.shape
    nsteps = m // tm
    pos3 = pos.reshape(nsteps, 1, tm)
    return pl.pallas_call(
        functools.partial(_final_kernel, tm=tm, nsteps=nsteps),
        out_shape=jax.ShapeDtypeStruct((m, d), F32),
        grid=(nsteps,),
        in_specs=[pl.BlockSpec((None, 1, tm), lambda i: (i, 0, 0), memory_space=pltpu.SMEM),
                  pl.BlockSpec((None, 1, tm), lambda i: (jnp.minimum(i + 1, nsteps - 1), 0, 0),
                               memory_space=pltpu.SMEM),
                  pl.BlockSpec(memory_space=pl.ANY),
                  pl.BlockSpec((tm, d), lambda i: (i, 0)),
                  gate_spec,
                  pl.BlockSpec((1, d), lambda i: (0, 0))],
        out_specs=pl.BlockSpec((tm, d), lambda i: (i, 0)),
        scratch_shapes=[pltpu.VMEM((2, tm, 1, d), F32), pltpu.SemaphoreType.DMA((2,))],
        compiler_params=_params("arbitrary"),
        name="final",
    )(pos3, pos3, ys, x1, gate, nw)


def _mod_spec(idx, tm, t, ngrid):
    if t % tm == 0:
        per = t // tm
        shape, index = (None, None, 1, D_MODEL), lambda i: (i // per, idx, 0, 0)
    else:
        shape, index = (tm // t, None, 1, D_MODEL), lambda i: (i, idx, 0, 0)
    if ngrid == 2:
        return pl.BlockSpec(shape, lambda i, j: index(i))
    return pl.BlockSpec(shape, index)


def _trunk(x, mod, s_gla, s_gdn, s_conv, w, *, chunk, sub, nseq, t_valid, tm, tm_moe, pair_classes, act_dtype):
    nb, t_pad, d = x.shape
    m = nb * t_pad
    x2d = x.reshape(m, d)
    mods = [mod.reshape(nb, N_MOD, 1, d)] * N_MOD
    spec = lambda idx, ngrid: _mod_spec(idx, tm, t_pad, ngrid)

    proj, small, small_t = _inproj(x2d, mods[0], mods[1], (spec(0, 2), spec(1, 2)), w["norm1"],
                                   w["w_main"], w["w_small"], w["w_small_t"], tm, act_dtype)
    proj3 = proj.reshape(nb, t_pad, PROJ_MAIN)
    small3 = small.reshape(nb, t_pad, LANES)
    small_t4 = small_t.reshape(SMALL_ROWS, nb, t_pad // chunk, chunk).transpose(1, 2, 0, 3)
    merged, new_gla, new_gdn, new_conv = _mixer(proj3, small3, small_t4, w, s_gla, s_gdn, s_conv, chunk=chunk,
                                                sub=sub, nseq=nseq, t_valid=t_valid, out_dtype=act_dtype)
    if t_valid < t_pad:
        merged, x = merged[:, :t_valid], x[:, :t_valid]
    m = nb * t_valid
    tm = min(tm, m)
    tm_moe = min(tm_moe, m)
    spec = lambda idx, ngrid: _mod_spec(idx, tm, t_valid, ngrid)
    x1, hx, gid = _post(merged.reshape(m, d), x.reshape(m, d), mods[2], mods[3], mods[4],
                        (spec(2, 1), spec(3, 1), spec(4, 1)),
                        w["w_out"], w["norm2"], w["w_router_t"], w["b_router_t"], tm)
    class_table = PAIR_CLASSES if pair_classes else GROUP_CLASSES
    pos, src, tile_class, ntiles = _route(gid[1 if pair_classes else 0], class_table.shape[0], tm_moe)
    ys = _experts(hx, src, tile_class, ntiles, class_table, w["w_gate"], w["w_up"], w["w_down"], tm_moe)
    tm_fin = min(tm, FINAL_TILE)
    y = _final(ys, pos, x1, mods[5], _mod_spec(5, tm_fin, t_valid, 1), w["final_norm"], tm_fin)
    return y.reshape(nb, t_valid, d), new_gla, new_gdn, new_conv


def _regroup_kernel(q_ref, r_ref, a_ref, b_ref, o_ref, *, shifts):
    window = jnp.concatenate([a_ref[...], b_ref[...]], axis=1)
    r = r_ref[pl.program_id(0)]
    for s in shifts:
        @pl.when(r == s)
        def _():
            o_ref[...] = window[:, s:s + PROJ_PART].astype(o_ref.dtype)


def _regroup_columns(w, starts):
    d = w.shape[0]
    q = np.array([s // PROJ_PART for s in starts], np.int32)
    r = np.array([s % PROJ_PART for s in starts], np.int32)
    assert r.max() <= LANES
    per = PROJ_PART // LANES
    return pl.pallas_call(
        functools.partial(_regroup_kernel, shifts=tuple(sorted(set(r.tolist())))),
        out_shape=jax.ShapeDtypeStruct((d, len(starts) * PROJ_PART), BF16),
        grid_spec=pltpu.PrefetchScalarGridSpec(
            num_scalar_prefetch=2,
            grid=(len(starts),),
            in_specs=[pl.BlockSpec((d, PROJ_PART), lambda i, q, r: (0, q[i])),
                      pl.BlockSpec((d, LANES), lambda i, q, r: (0, per * (q[i] + 1)))],
            out_specs=pl.BlockSpec((d, PROJ_PART), lambda i, q, r: (0, i))),
        compiler_params=_params("arbitrary"),
        name="regroup_w_in",
    )(jnp.asarray(q), jnp.asarray(r), w, w)


def _prep_weights(w_in, w_gla_a2, b_gla_a, gla_norm_w, w_conv, gdn_A_log, gdn_dt_bias, gdn_norm_w, w_out,
                  norm1_w, norm2_w, w_group_router, b_group_router, w_expert_router, b_expert_router,
                  w_exp_gate, w_exp_up, w_exp_down, final_norm_w):
    d = D_MODEL
    o = 0
    cols, start = {}, {}
    for name, width in (("gla", 2 * GLA_QK + 2 * GLA_VW), ("ra", GLA_GATE_RANK), ("qkv", GDN_CONV_DIM),
                        ("zb", GDN_VW), ("beta", GDN_HEADS), ("a", GDN_HEADS), ("gates", 2 * D_MODEL)):
        cols[name], start[name] = w_in[:, o:o + width], o
        o += width
    gate_starts = [start["zb"], start["gates"], start["gates"] + PROJ_PART]
    w_main = _regroup_columns(w_in, [s for j in range(3) for s in (start["gla"] + j * PROJ_PART,
                                                                   start["qkv"] + j * PROJ_PART, gate_starts[j])])
    small = jnp.concatenate([cols["ra"], cols["beta"], cols["a"]], axis=1)
    w_small = jnp.pad(small, ((0, 0), (0, LANES - small.shape[1]))).astype(BF16)
    w_small_t = small.T.astype(BF16)
    assert N_GROUPS == EXPERTS_PER_GROUP
    stride_pad = ((0, 0), (0, 0), (0, ROUTER_GROUP_STRIDE - EXPERTS_PER_GROUP))

    def router_rows(group_part, expert_part):
        both = jnp.concatenate([group_part, expert_part], axis=1).reshape(-1, 1 + N_GROUPS, EXPERTS_PER_GROUP)
        return jnp.pad(both, stride_pad).reshape(-1, ROUTER_ROWS)

    wr_t = router_rows(w_group_router, w_expert_router).T
    br_t = router_rows(b_group_router[None], b_expert_router[None]).T
    return dict(
        w_main=w_main, w_small=w_small, w_small_t=w_small_t,
        norm1=norm1_w.reshape(1, d), norm2=norm2_w.reshape(1, d), final_norm=final_norm_w.reshape(1, d),
        wa2=w_gla_a2, ba=b_gla_a.reshape(1, GLA_QK), gla_norm=gla_norm_w.reshape(1, GLA_DV),
        w_conv=w_conv, a_log=gdn_A_log, dt_bias=gdn_dt_bias, gdn_norm=gdn_norm_w.reshape(1, GDN_DV),
        w_out=w_out.astype(BF16), w_router_t=wr_t.astype(BF16), b_router_t=br_t,
        w_gate=w_exp_gate.astype(BF16), w_up=w_exp_up.astype(BF16), w_down=w_exp_down.astype(BF16),
    )


def kernel(x_prompt, x_sample, c_prompt, c_sample, state_gla, state_gdn, state_conv, w_ada, b_ada, norm1_w, w_in, w_gla_a2, b_gla_a, gla_norm_w, w_conv, gdn_A_log, gdn_dt_bias, gdn_norm_w, w_out, norm2_w, w_group_router, b_group_router, w_expert_router, b_expert_router, w_exp_gate, w_exp_up, w_exp_down, final_norm_w):
    assert w_ada.shape[0] == 1, "single layer"
    bp, tp, d = x_prompt.shape
    bs, ts, _ = x_sample.shape
    w = _prep_weights(w_in[0], w_gla_a2[0], b_gla_a[0], gla_norm_w[0], w_conv[0], gdn_A_log[0], gdn_dt_bias[0],
                      gdn_norm_w[0], w_out[0], norm1_w[0], norm2_w[0], w_group_router[0], b_group_router[0],
                      w_expert_router[0], b_expert_router[0], w_exp_gate[0], w_exp_up[0], w_exp_down[0],
                      final_norm_w)
    mod = _ada_mod(jnp.concatenate([c_prompt, c_sample], axis=0), w_ada[0], b_ada[0]).reshape(bp + bs, N_MOD, d)

    y_p, gla_p, gdn_p, conv_p = _trunk(x_prompt, mod[:bp], None, None, None, w,
                                       chunk=64, sub=GLA_SUBCHUNK, nseq=4, t_valid=tp, tm=min(1024, tp),
                                       tm_moe=min(512, tp), pair_classes=False, act_dtype=BF16)
    ts_pad = SUBLANES
    xs = jnp.pad(x_sample, ((0, 0), (0, ts_pad - ts), (0, 0)))
    y_s, gla_s, gdn_s, conv_s = _trunk(xs, mod[bp:], state_gla[0], state_gdn[0], state_conv[0], w,
                                       chunk=ts_pad, sub=ts_pad, nseq=4, t_valid=ts, tm=min(512, bs * ts_pad),
                                       tm_moe=128, pair_classes=False, act_dtype=F32)
    return (y_p, y_s, gla_p[None], gdn_p[None], conv_p[None], gla_s[None], gdn_s[None], conv_s[None])
```

```python
import functools
import itertools
import math

import jax
import numpy as np
import jax.numpy as jnp
from jax import lax
from jax.experimental import pallas as pl
from jax.experimental.pallas import tpu as pltpu

F32 = jnp.float32
BF16 = jnp.bfloat16
HIGHEST = lax.Precision.HIGHEST

D_MODEL = 1024
GLA_HEADS = 4
GLA_DK = 128
GLA_DV = 256
GLA_QK = GLA_HEADS * GLA_DK
GLA_VW = GLA_HEADS * GLA_DV
GLA_GATE_RANK = 16
GLA_TAU = 16.0
GLA_SUBCHUNK = 16
GDN_HEADS = 8
GDN_DK = 128
GDN_DV = 128
GDN_QK = GDN_HEADS * GDN_DK
GDN_VW = GDN_HEADS * GDN_DV
GDN_CONV = 4
GDN_CONV_DIM = 2 * GDN_QK + GDN_VW
GDN_INV_BLOCK = 16
N_GROUPS = 4
EXPERTS_PER_GROUP = 4
N_EXPERTS = N_GROUPS * EXPERTS_PER_GROUP
D_EXPERT = D_MODEL // 2
N_MOD = 6
NORM_EPS = 1e-6
L2_EPS = 1e-6

LANES = 128
SUBLANES = 8
VMEM_LIMIT = 56 * 1024 * 1024

PROJ_MAIN = 2 * GLA_QK + 2 * GLA_VW + GDN_CONV_DIM + GDN_VW + 2 * D_MODEL
PROJ_PART = 1024
INPROJ_TN = 3 * PROJ_PART
COL_QK, COL_V, COL_GA = 0, 3, 6
COL_CQ, COL_CK, COL_CV = 1, 4, 7
COL_ZB, COL_GATE_A, COL_GATE_B = 2, 5, 8
SMALL_RA, SMALL_BETA, SMALL_A = 0, GLA_GATE_RANK, GLA_GATE_RANK + GDN_HEADS
SMALL_ROWS = 32
ROUTER_GROUP_STRIDE = 8
ROUTER_ROWS = ROUTER_GROUP_STRIDE * (1 + N_GROUPS)

EXPERT_PAIRS = tuple(itertools.combinations(range(EXPERTS_PER_GROUP), 2))
GROUP_CLASSES = np.array([[g] + list(range(EXPERTS_PER_GROUP)) for g in range(N_GROUPS)], np.int32)
PAIR_CLASSES = np.array([[g, a, b] for g in range(N_GROUPS) for a, b in EXPERT_PAIRS], np.int32)

HX_WIDTH = D_MODEL + LANES
SCALAR_UNROLL = 16
FINAL_TILE = 256

NT_DIMS = (((1,), (1,)), ((), ()))
TN_DIMS = (((0,), (0,)), ((), ()))


def _sigmoid(x):
    return 0.5 * jnp.tanh(0.5 * x) + 0.5


def _silu(x):
    return x * _sigmoid(x)


def _softplus(x):
    return jnp.maximum(x, 0.0) + jnp.log1p(jnp.exp(-jnp.abs(x)))


def _rms(x, w):
    return x * lax.rsqrt(jnp.mean(x * x, axis=-1, keepdims=True) + NORM_EPS) * w


def _mod(ref, rows):
    v = ref[...]
    if v.ndim == 3:
        v = jnp.broadcast_to(v, (v.shape[0], rows // v.shape[0], v.shape[2])).reshape(rows, v.shape[2])
    return v


def _params(*sem):
    return pltpu.CompilerParams(dimension_semantics=sem, vmem_limit_bytes=VMEM_LIMIT)


def _ada_kernel(c_ref, w_ref, b_ref, o_ref):
    cs = _silu(c_ref[...])
    o_ref[...] = jnp.dot(cs.astype(BF16), w_ref[...].astype(BF16), preferred_element_type=F32) + b_ref[...]


def _ada_mod(c_all, w_ada, b_ada):
    rows, d = c_all.shape
    n = w_ada.shape[1]
    tn = 1024
    return pl.pallas_call(
        _ada_kernel,
        out_shape=jax.ShapeDtypeStruct((rows, n), F32),
        grid=(n // tn,),
        in_specs=[pl.BlockSpec((rows, d), lambda j: (0, 0)),
                  pl.BlockSpec((d, tn), lambda j: (0, j)),
                  pl.BlockSpec((1, tn), lambda j: (0, j))],
        out_specs=pl.BlockSpec((rows, tn), lambda j: (0, j)),
        compiler_params=_params("arbitrary"),
        name="ada_mod",
    )(c_all, w_ada, b_ada.reshape(1, n))


def _inproj_kernel(x_ref, shift_ref, scale_ref, nw_ref, w_ref, ws_ref, wst_ref,
                   o_ref, os_ref, ost_ref, h_scr):
    @pl.when(pl.program_id(1) == 0)
    def _():
        rows = x_ref.shape[0]
        h = _rms(x_ref[...], nw_ref[...]) * (1.0 + _mod(scale_ref, rows)) + _mod(shift_ref, rows)
        hb = h.astype(BF16)
        h_scr[...] = hb
        os_ref[...] = jnp.dot(hb, ws_ref[...], preferred_element_type=F32)
        ost_ref[...] = lax.dot_general(wst_ref[...], hb, NT_DIMS, preferred_element_type=F32)

    o_ref[...] = jnp.dot(h_scr[...], w_ref[...], preferred_element_type=F32).astype(o_ref.dtype)


def _inproj(x2d, shift, scale, mod_specs, nw, w_main, w_small, w_small_t, tm, out_dtype):
    m, d = x2d.shape
    n = PROJ_MAIN
    tn = INPROJ_TN
    return pl.pallas_call(
        _inproj_kernel,
        out_shape=(jax.ShapeDtypeStruct((m, n), out_dtype),
                   jax.ShapeDtypeStruct((m, LANES), F32),
                   jax.ShapeDtypeStruct((SMALL_ROWS, m), F32)),
        grid=(m // tm, n // tn),
        in_specs=[pl.BlockSpec((tm, d), lambda i, j: (i, 0)),
                  mod_specs[0], mod_specs[1],
                  pl.BlockSpec((1, d), lambda i, j: (0, 0)),
                  pl.BlockSpec((d, tn), lambda i, j: (0, j)),
                  pl.BlockSpec((d, LANES), lambda i, j: (0, 0)),
                  pl.BlockSpec((SMALL_ROWS, d), lambda i, j: (0, 0))],
        out_specs=(pl.BlockSpec((tm, tn), lambda i, j: (i, j)),
                   pl.BlockSpec((tm, LANES), lambda i, j: (i, 0)),
                   pl.BlockSpec((SMALL_ROWS, tm), lambda i, j: (0, i))),
        scratch_shapes=[pltpu.VMEM((tm, d), BF16)],
        compiler_params=_params("parallel", "arbitrary"),
        name="inproj",
    )(x2d, shift, scale, nw, w_main, w_small, w_small_t)


def _gla_stages(ins, s0_ref, sout_ref, st_scr, out, *, chunk, sub, nchunks, nseq, t_valid):
    qk_ref, v_ref, ga_ref, gt_ref, sm_ref, wa2_ref, ba_ref, nw_ref = ins
    has_state = s0_ref is not None
    n = pl.program_id(1)
    masked = t_valid < chunk * nchunks
    units = [(s, h) for s in range(nseq) for h in range(GLA_HEADS)]

    @pl.when(n == 0)
    def _():
        for u, (s, h) in enumerate(units):
            st_scr[u] = s0_ref[s, h].T if has_state else jnp.zeros((GLA_DV, GLA_DK), F32)

    row = lax.broadcasted_iota(jnp.int32, (chunk, chunk), 0)
    col = lax.broadcasted_iota(jnp.int32, (chunk, chunk), 1)
    rowi = lax.broadcasted_iota(jnp.int32, (chunk, 1), 0)
    valid = (rowi + n * chunk) < t_valid
    tri = (row >= col).astype(F32)
    wa2 = wa2_ref[...].astype(BF16)
    b_all = []
    for s in range(nseq):
        ra = sm_ref[s, :, SMALL_RA:SMALL_RA + GLA_GATE_RANK]
        x = jnp.dot(ra.astype(BF16), wa2, preferred_element_type=F32) + ba_ref[...]
        g = (jnp.minimum(x, 0.0) - jnp.log1p(jnp.exp(-jnp.abs(x)))) * (1.0 / GLA_TAU)
        if masked:
            g = jnp.where(valid, g, 0.0)
        b_all.append(jnp.dot(tri, g, precision=HIGHEST, preferred_element_type=F32))
    yield

    q, k, v, b = [], [], [], []
    for s, h in units:
        q.append(qk_ref[s, :, h * GLA_DK:(h + 1) * GLA_DK].astype(F32) * (GLA_DK ** -0.5))
        kh = qk_ref[s, :, GLA_QK + h * GLA_DK:GLA_QK + (h + 1) * GLA_DK].astype(F32)
        k.append(jnp.where(valid, kh, 0.0) if masked else kh)
        v.append(v_ref[s, :, h * GLA_DV:(h + 1) * GLA_DV].astype(BF16))
        b.append(b_all[s][:, h * GLA_DK:(h + 1) * GLA_DK])
    nu = range(len(units))
    st = [st_scr[u] for u in nu]
    o = [lax.dot_general((q[u] * jnp.exp(b[u])).astype(BF16), st[u].astype(BF16), NT_DIMS,
                         preferred_element_type=F32) for u in nu]
    yield
    blocks = [[] for _ in nu]
    for i in range(chunk // sub):
        r0, r1 = i * sub, (i + 1) * sub
        for u in nu:
            bref = b[u][r0 - 1:r0] if i > 0 else jnp.zeros((1, GLA_DK), F32)
            qt = (q[u][r0:r1] * jnp.exp(b[u][r0:r1] - bref)).astype(BF16)
            expo = bref - b[u]
            if r1 < chunk:
                expo = jnp.where(rowi < r1, expo, 0.0)
            kt = (k[u] * jnp.exp(expo)).astype(BF16)
            blocks[u].append(lax.dot_general(qt, kt, NT_DIMS, preferred_element_type=F32))
        yield
    for u in nu:
        a = blocks[u][0] if len(blocks[u]) == 1 else jnp.concatenate(blocks[u], axis=0)
        a = jnp.where(col <= row, a, 0.0)
        o[u] = o[u] + jnp.dot(a.astype(BF16), v[u], preferred_element_type=F32)
    yield
    for u in nu:
        bl = b[u][chunk - 1:chunk]
        kt = (k[u] * jnp.exp(bl - b[u])).astype(BF16)
        st_scr[u] = st[u] * jnp.exp(bl) + lax.dot_general(v[u], kt, TN_DIMS, preferred_element_type=F32)
    yield
    for u, (s, h) in enumerate(units):
        sl = slice(h * GLA_DV, (h + 1) * GLA_DV)
        out[s, h] = (_rms(o[u], nw_ref[...]) * _silu(ga_ref[s, :, sl].astype(F32))
                     * _sigmoid(gt_ref[s, :, sl].astype(F32)))

    @pl.when(n == nchunks - 1)
    def _():
        for u, (s, h) in enumerate(units):
            sout_ref[s, h] = st_scr[u].T


def _gdn_stages(ins, s0_ref, c0_ref, sout_ref, cout_ref, s_scr, xp_scr, out, *, chunk, nchunks, nseq, t_valid):
    (cq_ref, ck_ref, cv_ref, zb_ref, gt_ref, sm_ref, smt_ref, wc_ref, alog_ref, dtb_ref, alogc_ref, dtbc_ref,
     nw_ref) = ins
    has_state = s0_ref is not None
    n = pl.program_id(1)
    masked = t_valid < chunk * nchunks
    pad = SUBLANES
    units = [(s, h) for s in range(nseq) for h in range(GDN_HEADS)]
    nu = range(len(units))

    @pl.when(n == 0)
    def _():
        for u, (s, h) in enumerate(units):
            s_scr[u] = s0_ref[s, h] if has_state else jnp.zeros((GDN_DK, GDN_DV), F32)
        for s in range(nseq):
            xp_scr[s, 0:pad, :] = jnp.zeros((pad, GDN_CONV_DIM), F32)
            if has_state:
                xp_scr[s, pad - (GDN_CONV - 1):pad, :] = c0_ref[s]

    row = lax.broadcasted_iota(jnp.int32, (chunk, chunk), 0)
    col = lax.broadcasted_iota(jnp.int32, (chunk, chunk), 1)
    valid_c = (lax.broadcasted_iota(jnp.int32, (chunk, 1), 0) + n * chunk) < t_valid
    valid_r = (lax.broadcasted_iota(jnp.int32, (1, chunk), 1) + n * chunk) < t_valid
    tri_c = (row >= col).astype(F32)
    tri_r = (row <= col).astype(F32)
    b_col, b_row, beta_col = [], [], []
    for s in range(nseq):
        for c, ref in enumerate((cq_ref, ck_ref, cv_ref)):
            xp_scr[s, pad:pad + chunk, c * PROJ_PART:(c + 1) * PROJ_PART] = ref[s].astype(F32)
        g_col = -jnp.exp(alog_ref[...]) * _softplus(sm_ref[s, :, SMALL_A:SMALL_A + GDN_HEADS] + dtb_ref[...])
        bt = _sigmoid(sm_ref[s, :, SMALL_BETA:SMALL_BETA + GDN_HEADS])
        g_row = -jnp.exp(alogc_ref[...]) * _softplus(smt_ref[s, SMALL_A:SMALL_A + GDN_HEADS, :] + dtbc_ref[...])
        if masked:
            g_col = jnp.where(valid_c, g_col, 0.0)
            bt = jnp.where(valid_c, bt, 0.0)
            g_row = jnp.where(valid_r, g_row, 0.0)
        beta_col.append(bt)
        b_col.append(jnp.dot(tri_c, g_col, precision=HIGHEST, preferred_element_type=F32))
        b_row.append(jnp.dot(g_row, tri_r, precision=HIGHEST, preferred_element_type=F32))
    yield

    def conv_silu(s, c0):
        cs = slice(c0, c0 + LANES)
        y = xp_scr[s, pad:pad + chunk, cs] * wc_ref[GDN_CONV - 1:GDN_CONV, cs]
        for j in range(1, GDN_CONV):
            y = y + xp_scr[s, pad - j:pad - j + chunk, cs] * wc_ref[GDN_CONV - 1 - j:GDN_CONV - j, cs]
        return _silu(y)

    bdot = lambda a, b: jnp.dot(a.astype(BF16), b.astype(BF16), preferred_element_type=F32)
    q, k, v, kb, dec, bc, beta = [], [], [], [], [], [], []
    for u, (s, h) in enumerate(units):
        qh = conv_silu(s, h * GDN_DK)
        kh = conv_silu(s, GDN_QK + h * GDN_DK)
        v.append(conv_silu(s, 2 * GDN_QK + h * GDN_DV))
        q.append(qh * lax.rsqrt(jnp.sum(qh * qh, axis=-1, keepdims=True) + L2_EPS) * (GDN_DK ** -0.5))
        k.append(kh * lax.rsqrt(jnp.sum(kh * kh, axis=-1, keepdims=True) + L2_EPS))
        bc.append(b_col[s][:, h:h + 1])
        beta.append(beta_col[s][:, h:h + 1])
        dec.append(jnp.where(row >= col, jnp.exp(jnp.minimum(bc[u] - b_row[s][h:h + 1, :], 0.0)), 0.0))
        kb.append(k[u] * beta[u])
        if u % GDN_HEADS == GDN_HEADS - 1:
            yield
    kq = [lax.dot_general(jnp.concatenate([kb[u], q[u]], axis=0).astype(BF16), k[u].astype(BF16), NT_DIMS,
                          preferred_element_type=F32) for u in nu]
    yield
    blk = min(GDN_INV_BLOCK, chunk)
    nblk = chunk // blk
    same_blk = (row // blk) == (col // blk)
    lmat = [jnp.where(row > col, kq[u][:chunk] * dec[u], 0.0) for u in nu]
    p = [jnp.where(same_blk, -lmat[u], 0.0) for u in nu]
    r = p
    for _ in range(int(math.log2(blk)) - 1):
        p = [bdot(p[u], p[u]) for u in nu]
        r = [r[u] + p[u] + bdot(r[u], p[u]) for u in nu]
        yield
    if nblk > 1:
        lo = [jnp.where(same_blk, 0.0, lmat[u]) for u in nu]
        p = [-(lo[u] + bdot(r[u], lo[u])) for u in nu]
        qm = p
        yield
        for _ in range(int(math.log2(nblk)) - 1):
            p = [bdot(p[u], p[u]) for u in nu]
            qm = [qm[u] + p[u] + bdot(qm[u], p[u]) for u in nu]
            yield
        r = [r[u] + qm[u] + bdot(qm[u], r[u]) for u in nu]
    rhs = [jnp.concatenate([v[u] * beta[u], kb[u] * jnp.exp(bc[u])], axis=-1) for u in nu]
    uw = [rhs[u] + bdot(r[u], rhs[u]) for u in nu]
    yield
    st = [s_scr[u] for u in nu]
    ws = [bdot(jnp.concatenate([uw[u][:, GDN_DV:], q[u] * jnp.exp(bc[u])], axis=0), st[u]) for u in nu]
    v_new = [uw[u][:, :GDN_DV] - ws[u][:chunk] for u in nu]
    yield
    o = [ws[u][chunk:] + bdot(kq[u][chunk:] * dec[u], v_new[u]) for u in nu]
    for u in nu:
        bl = bc[u][chunk - 1:chunk, :]
        s_scr[u] = jnp.exp(bl) * st[u] + lax.dot_general((k[u] * jnp.exp(bl - bc[u])).astype(BF16),
                                                         v_new[u].astype(BF16), TN_DIMS, preferred_element_type=F32)
    yield
    for u, (s, h) in enumerate(units):
        sl = slice(h * GDN_DV, (h + 1) * GDN_DV)
        out[s, h] = (_rms(o[u], nw_ref[...]) * _silu(zb_ref[s, :, sl].astype(F32))
                     * _sigmoid(gt_ref[s, :, sl].astype(F32)))

    @pl.when(n == nchunks - 1)
    def _():
        last = t_valid - (nchunks - 1) * chunk
        for s in range(nseq):
            cout_ref[s] = xp_scr[s, pad + last - (GDN_CONV - 1):pad + last, :]
        for u, (s, h) in enumerate(units):
            sout_ref[s, h] = s_scr[u]

    for s in range(nseq):
        xp_scr[s, 0:pad, :] = xp_scr[s, chunk:chunk + pad, :]


N_GLA_IN, N_GDN_IN = 8, 13


def _mixer_kernel(*refs, chunk, sub, nchunks, nseq, has_state, t_valid):
    gla_in, refs = refs[:N_GLA_IN], refs[N_GLA_IN:]
    gdn_in, refs = refs[:N_GDN_IN], refs[N_GDN_IN:]
    if has_state:
        (sa0_ref, sb0_ref, c0_ref), refs = refs[:3], refs[3:]
    else:
        sa0_ref = sb0_ref = c0_ref = None
    o_ref, sa_out, sb_out, c_out, sa_scr, sb_scr, xp_scr = refs
    out_a, out_b = {}, {}
    gla = _gla_stages(gla_in, sa0_ref, sa_out, sa_scr, out_a,
                      chunk=chunk, sub=sub, nchunks=nchunks, nseq=nseq, t_valid=t_valid)
    gdn = _gdn_stages(gdn_in, sb0_ref, c0_ref, sb_out, c_out, sb_scr, xp_scr, out_b,
                      chunk=chunk, nchunks=nchunks, nseq=nseq, t_valid=t_valid)
    live = [gdn, gla]
    while live:
        for g in list(live):
            if next(g, StopIteration) is StopIteration:
                live.remove(g)
    per = GLA_DV // GDN_DV
    for (s, h), ob in out_b.items():
        oa = out_a[s, h // per][:, (h % per) * GDN_DV:(h % per + 1) * GDN_DV]
        o_ref[s, :, h * GDN_DV:(h + 1) * GDN_DV] = (oa + ob).astype(o_ref.dtype)


def _mixer(proj3, small3, small_t4, w, s_gla, s_gdn, s_conv, *, chunk, sub, nseq, t_valid, out_dtype):
    nb, t_pad, _ = proj3.shape
    nchunks = t_pad // chunk
    has_state = s_gla is not None
    blk = lambda width, c: pl.BlockSpec((nseq, chunk, width), lambda b, n, c=c: (b, n, c))
    col = lambda c: blk(PROJ_PART, c)
    full = lambda a: pl.BlockSpec(a.shape, lambda b, n: (0,) * a.ndim)
    alog_r, dtb_r = w["a_log"].reshape(1, GDN_HEADS), w["dt_bias"].reshape(1, GDN_HEADS)
    alog_c, dtb_c = w["a_log"].reshape(GDN_HEADS, 1), w["dt_bias"].reshape(GDN_HEADS, 1)
    small_spec = blk(LANES, 0)
    gla_consts = [w["wa2"], w["ba"], w["gla_norm"]]
    gdn_consts = [w["w_conv"], alog_r, dtb_r, alog_c, dtb_c, w["gdn_norm"]]
    in_specs = ([col(COL_QK), col(COL_V), col(COL_GA), col(COL_GATE_A), small_spec] + [full(a) for a in gla_consts]
                + [col(COL_CQ), col(COL_CK), col(COL_CV), col(COL_ZB), col(COL_GATE_B), small_spec,
                   pl.BlockSpec((nseq, None, SMALL_ROWS, chunk), lambda b, n: (b, n, 0, 0))]
                + [full(a) for a in gdn_consts])
    args = [proj3] * 4 + [small3] + gla_consts + [proj3] * 5 + [small3, small_t4] + gdn_consts
    assert len(in_specs) == N_GLA_IN + N_GDN_IN
    sa_spec = pl.BlockSpec((nseq, GLA_HEADS, GLA_DK, GLA_DV), lambda b, n: (b, 0, 0, 0))
    sb_spec = pl.BlockSpec((nseq, GDN_HEADS, GDN_DK, GDN_DV), lambda b, n: (b, 0, 0, 0))
    conv_spec = pl.BlockSpec((nseq, GDN_CONV - 1, GDN_CONV_DIM), lambda b, n: (b, 0, 0))
    if has_state:
        in_specs += [sa_spec, sb_spec, conv_spec]
        args += [s_gla, s_gdn, s_conv]
    return pl.pallas_call(
        functools.partial(_mixer_kernel, chunk=chunk, sub=sub, nchunks=nchunks, nseq=nseq, has_state=has_state,
                          t_valid=t_valid),
        out_shape=(jax.ShapeDtypeStruct((nb, t_pad, D_MODEL), out_dtype),
                   jax.ShapeDtypeStruct((nb, GLA_HEADS, GLA_DK, GLA_DV), F32),
                   jax.ShapeDtypeStruct((nb, GDN_HEADS, GDN_DK, GDN_DV), F32),
                   jax.ShapeDtypeStruct((nb, GDN_CONV - 1, GDN_CONV_DIM), F32)),
        grid=(nb // nseq, nchunks),
        in_specs=in_specs,
        out_specs=(blk(D_MODEL, 0), sa_spec, sb_spec, conv_spec),
        scratch_shapes=[pltpu.VMEM((nseq * GLA_HEADS, GLA_DV, GLA_DK), F32),
                        pltpu.VMEM((nseq * GDN_HEADS, GDN_DK, GDN_DV), F32),
                        pltpu.VMEM((nseq, chunk + 2 * SUBLANES, GDN_CONV_DIM), F32)],
        compiler_params=_params("parallel", "arbitrary"),
        name="mixer",
    )(*args)


def _post_kernel(mg_ref, x_ref, gate_ref, shift_ref, scale_ref, wo_ref, nw_ref, wrt_ref, brt_ref,
                 x1_ref, hx_ref, gid_ref):
    y = jnp.dot(mg_ref[...].astype(BF16), wo_ref[...], preferred_element_type=F32)
    rows = x_ref.shape[0]
    x1 = x_ref[...] + _mod(gate_ref, rows) * y
    x1_ref[...] = x1
    h2 = _rms(x1, nw_ref[...]) * (1.0 + _mod(scale_ref, rows)) + _mod(shift_ref, rows)
    hx_ref[:, 0:D_MODEL] = h2
    lt = lax.dot_general(wrt_ref[...], h2.astype(BF16), NT_DIMS, preferred_element_type=F32) + brt_ref[...]
    tm = lt.shape[1]
    gl = lt[0:N_GROUPS]
    gidx = lax.broadcasted_iota(jnp.int32, (N_GROUPS, tm), 0)
    gmax = jnp.max(gl, axis=0, keepdims=True)
    g_w = 1.0 / jnp.sum(jnp.exp(gl - gmax), axis=0, keepdims=True)
    g_sel = jnp.min(jnp.where(gl == gmax, gidx, N_GROUPS), axis=0, keepdims=True)
    el = jnp.zeros((EXPERTS_PER_GROUP, tm), F32)
    for g in range(N_GROUPS):
        r = ROUTER_GROUP_STRIDE * (1 + g)
        el = el + jnp.where(g_sel == g, lt[r:r + EXPERTS_PER_GROUP], 0.0)
    eidx = lax.broadcasted_iota(jnp.int32, (EXPERTS_PER_GROUP, tm), 0)
    m1 = jnp.max(el, axis=0, keepdims=True)
    i1 = jnp.min(jnp.where(el == m1, eidx, EXPERTS_PER_GROUP), axis=0, keepdims=True)
    el2 = jnp.where(eidx == i1, -jnp.inf, el)
    m2 = jnp.max(el2, axis=0, keepdims=True)
    i2 = jnp.min(jnp.where(el2 == m2, eidx, EXPERTS_PER_GROUP), axis=0, keepdims=True)
    r21 = jnp.exp(m2 - m1)
    w1 = 1.0 / (1.0 + r21)
    w2 = r21 / (1.0 + r21)
    comb_t = g_w * (jnp.where(eidx == i1, w1, 0.0) + jnp.where(eidx == i2, w2, 0.0))
    ident = (lax.broadcasted_iota(jnp.int32, (EXPERTS_PER_GROUP, LANES), 0)
             == lax.broadcasted_iota(jnp.int32, (EXPERTS_PER_GROUP, LANES), 1)).astype(F32)
    hx_ref[:, D_MODEL:] = lax.dot_general(comb_t, ident, TN_DIMS, precision=HIGHEST, preferred_element_type=F32)
    lo, hi = jnp.minimum(i1, i2), jnp.maximum(i1, i2)
    pair = jnp.right_shift(lo * (2 * EXPERTS_PER_GROUP - 1 - lo), 1) + hi - lo - 1
    rid = lax.broadcasted_iota(jnp.int32, gid_ref.shape, 0)
    gid_ref[...] = jnp.where(rid == 0, g_sel, g_sel * len(EXPERT_PAIRS) + pair)


def _post(merged, x2d, gate, shift, scale, mod_specs, wo, nw, wrt, brt, tm):
    m, d = x2d.shape
    row = lambda: pl.BlockSpec((tm, d), lambda i: (i, 0))
    full = lambda a: pl.BlockSpec(a.shape, lambda i: (0,) * a.ndim)
    return pl.pallas_call(
        _post_kernel,
        out_shape=(jax.ShapeDtypeStruct((m, d), F32),
                   jax.ShapeDtypeStruct((m, HX_WIDTH), F32),
                   jax.ShapeDtypeStruct((SUBLANES, m), jnp.int32)),
        grid=(m // tm,),
        in_specs=[row(), row(), mod_specs[0], mod_specs[1], mod_specs[2],
                  full(wo), full(nw), full(wrt), full(brt)],
        out_specs=(row(), pl.BlockSpec((tm, HX_WIDTH), lambda i: (i, 0)),
                   pl.BlockSpec((SUBLANES, tm), lambda i: (0, i))),
        compiler_params=_params("parallel"),
        name="post_mixer",
    )(merged, x2d, gate, shift, scale, wo, nw, wrt, brt)


def _invert_kernel(pos_ref, src_ref):
    def clear(i, carry):
        src_ref[i] = 0
        return carry

    def place(t, carry):
        src_ref[pos_ref[t]] = t
        return carry

    lax.fori_loop(0, src_ref.shape[0], clear, 0, unroll=SCALAR_UNROLL)
    lax.fori_loop(0, pos_ref.shape[0], place, 0, unroll=SCALAR_UNROLL)


def _invert_rows(pos, rows):
    smem = pl.BlockSpec(memory_space=pltpu.SMEM)
    return pl.pallas_call(
        _invert_kernel,
        out_shape=jax.ShapeDtypeStruct((rows,), jnp.int32),
        in_specs=[smem],
        out_specs=smem,
        name="invert_rows",
    )(pos)


def _route(cls, ncls, tm):
    m = cls.shape[0]
    ntiles_max = m // tm + ncls - 1
    onehot = (cls[:, None] == jnp.arange(ncls, dtype=jnp.int32)[None, :]).astype(jnp.int32)
    incl = jnp.cumsum(onehot, axis=0)
    tiles = (incl[-1] + tm - 1) // tm
    tile_end = jnp.cumsum(tiles)
    pos = jnp.sum(onehot * ((tile_end - tiles) * tm + incl - onehot), axis=1)
    src = _invert_rows(pos, ntiles_max * tm)
    t = jnp.arange(ntiles_max, dtype=jnp.int32)
    tile_class = jnp.minimum(jnp.sum((t[:, None] >= tile_end[None, :]).astype(jnp.int32), axis=1), ncls - 1)
    return pos, src, tile_class, tile_end[-1:]


def _row_gather(idx_ref, src_hbm, buf, sem, rows, first=0):
    for r in range(first, first + rows):
        pltpu.make_async_copy(src_hbm.at[pl.ds(idx_ref[0, r], 1)], buf.at[pl.ds(r, 1)],
                              sem).start(priority=r % 2)


def _row_gather_wait(src_hbm, buf, sem):
    pltpu.make_async_copy(src_hbm.at[pl.ds(0, buf.shape[0])], buf, sem).wait()


def _experts_kernel(meta_ref, nt_ref, src_ref, srcn_ref, hx_hbm, *rest, tm, nslot):
    w_refs, (y_ref, xbuf, sem) = rest[:3 * nslot], rest[3 * nslot:]
    t = pl.program_id(0)
    nt = nt_ref[0]
    slot = t % 2

    @pl.when(t == 0)
    def _():
        _row_gather(src_ref, hx_hbm, xbuf.at[0], sem.at[0], tm)

    @pl.when(t < nt)
    def _():
        cur, nxt, sem_cur, sem_nxt = xbuf.at[slot], xbuf.at[1 - slot], sem.at[slot], sem.at[1 - slot]
        _row_gather_wait(hx_hbm, cur, sem_cur)
        x = cur[:, 0:D_MODEL].astype(BF16)
        acc = jnp.zeros((tm, D_MODEL), F32)
        for j in range(nslot):
            wg_ref, wu_ref, wd_ref = w_refs[3 * j:3 * j + 3]
            e = meta_ref[t * (1 + nslot) + 1 + j]
            share = tm // (3 * nslot)
            a = jnp.dot(x, wg_ref[0, 0], preferred_element_type=F32)
            _row_gather(srcn_ref, hx_hbm, nxt, sem_nxt, share, first=(3 * j) * share)
            u = jnp.dot(x, wu_ref[0, 0], preferred_element_type=F32)
            _row_gather(srcn_ref, hx_hbm, nxt, sem_nxt, share, first=(3 * j + 1) * share)
            cw = jnp.zeros((tm, 1), F32)
            for c in range(EXPERTS_PER_GROUP):
                cw = cw + jnp.where(e == c, cur[:, D_MODEL + c:D_MODEL + c + 1], 0.0)
            acc = acc + jnp.dot((_silu(a) * u * cw).astype(BF16), wd_ref[0, 0], preferred_element_type=F32)
            last = tm - (3 * nslot - 1) * share if j == nslot - 1 else share
            _row_gather(srcn_ref, hx_hbm, nxt, sem_nxt, last, first=(3 * j + 2) * share)
        y_ref[:, 0, :] = acc

        @pl.when(t == nt - 1)
        def _():
            _row_gather_wait(hx_hbm, nxt, sem_nxt)

    @pl.when(t >= nt)
    def _():
        y_ref[...] = jnp.zeros_like(y_ref)


def _experts(hx, src, tile_class, ntiles, class_table, wg, wu, wd, tm):
    ntiles_max = tile_class.shape[0]
    nslot = class_table.shape[1] - 1
    d, f = D_MODEL, D_EXPERT
    src3 = src.reshape(ntiles_max, 1, tm)
    meta = jnp.asarray(class_table, jnp.int32)[tile_class].reshape(-1)
    stride = 1 + nslot
    wspec = lambda j, shape: pl.BlockSpec(
        (1, 1) + shape, lambda t, meta, nt, j=j: (meta[t * stride], meta[t * stride + 1 + j], 0, 0))
    w_specs, w_args = [], []
    for j in range(nslot):
        w_specs += [wspec(j, (d, f)), wspec(j, (d, f)), wspec(j, (f, d))]
        w_args += [wg, wu, wd]
    return pl.pallas_call(
        functools.partial(_experts_kernel, tm=tm, nslot=nslot),
        out_shape=jax.ShapeDtypeStruct((ntiles_max * tm, 1, d), F32),
        grid_spec=pltpu.PrefetchScalarGridSpec(
            num_scalar_prefetch=2,
            grid=(ntiles_max,),
            in_specs=[pl.BlockSpec((None, 1, tm), lambda t, meta, nt: (t, 0, 0), memory_space=pltpu.SMEM),
                      pl.BlockSpec((None, 1, tm),
                                   lambda t, meta, nt: (jnp.maximum(jnp.minimum(t + 1, nt[0] - 1), 0), 0, 0),
                                   memory_space=pltpu.SMEM),
                      pl.BlockSpec(memory_space=pl.ANY)] + w_specs,
            out_specs=pl.BlockSpec((tm, 1, d), lambda t, meta, nt: (t, 0, 0)),
            scratch_shapes=[pltpu.VMEM((2, tm, HX_WIDTH), F32), pltpu.SemaphoreType.DMA((2,))]),
        compiler_params=_params("arbitrary"),
        name="experts",
    )(meta, ntiles, src3, src3, hx, *w_args)


def _final_kernel(pos_ref, posn_ref, ys_hbm, x1_ref, gate_ref, nw_ref, o_ref, ybuf, sem, *, tm, nsteps):
    i = pl.program_id(0)
    slot = i % 2

    @pl.when(i == 0)
    def _():
        _row_gather(pos_ref, ys_hbm, ybuf.at[0], sem.at[0], tm)

    _row_gather(posn_ref, ys_hbm, ybuf.at[1 - slot], sem.at[1 - slot], tm)
    _row_gather_wait(ys_hbm, ybuf.at[slot], sem.at[slot])
    o_ref[...] = _rms(x1_ref[...] + _mod(gate_ref, tm) * ybuf[slot, :, 0, :], nw_ref[...])

    @pl.when(i == nsteps - 1)
    def _():
        _row_gather_wait(ys_hbm, ybuf.at[1 - slot], sem.at[1 - slot])


def _final(ys, pos, x1, gate, gate_spec, nw, tm):
    m, d = x1.shape
    nsteps = m // tm
    pos3 = pos.reshape(nsteps, 1, tm)
    return pl.pallas_call(
        functools.partial(_final_kernel, tm=tm, nsteps=nsteps),
        out_shape=jax.ShapeDtypeStruct((m, d), F32),
        grid=(nsteps,),
        in_specs=[pl.BlockSpec((None, 1, tm), lambda i: (i, 0, 0), memory_space=pltpu.SMEM),
                  pl.BlockSpec((None, 1, tm), lambda i: (jnp.minimum(i + 1, nsteps - 1), 0, 0),
                               memory_space=pltpu.SMEM),
                  pl.BlockSpec(memory_space=pl.ANY),
                  pl.BlockSpec((tm, d), lambda i: (i, 0)),
                  gate_spec,
                  pl.BlockSpec((1, d), lambda i: (0, 0))],
        out_specs=pl.BlockSpec((tm, d), lambda i: (i, 0)),
        scratch_shapes=[pltpu.VMEM((2, tm, 1, d), F32), pltpu.SemaphoreType.DMA((2,))],
        compiler_params=_params("arbitrary"),
        name="final",
    )(pos3, pos3, ys, x1, gate, nw)


def _mod_spec(idx, tm, t, ngrid):
    if t % tm == 0:
        per = t // tm
        shape, index = (None, None, 1, D_MODEL), lambda i: (i // per, idx, 0, 0)
    else:
        shape, index = (tm // t, None, 1, D_MODEL), lambda i: (i, idx, 0, 0)
    if ngrid == 2:
        return pl.BlockSpec(shape, lambda i, j: index(i))
    return pl.BlockSpec(shape, index)


def _trunk(x, mod, s_gla, s_gdn, s_conv, w, *, chunk, sub, nseq, t_valid, tm, tm_moe, pair_classes, act_dtype):
    nb, t_pad, d = x.shape
    m = nb * t_pad
    x2d = x.reshape(m, d)
    mods = [mod.reshape(nb, N_MOD, 1, d)] * N_MOD
    spec = lambda idx, ngrid: _mod_spec(idx, tm, t_pad, ngrid)

    proj, small, small_t = _inproj(x2d, mods[0], mods[1], (spec(0, 2), spec(1, 2)), w["norm1"],
                                   w["w_main"], w["w_small"], w["w_small_t"], tm, act_dtype)
    proj3 = proj.reshape(nb, t_pad, PROJ_MAIN)
    small3 = small.reshape(nb, t_pad, LANES)
    small_t4 = small_t.reshape(SMALL_ROWS, nb, t_pad // chunk, chunk).transpose(1, 2, 0, 3)
    merged, new_gla, new_gdn, new_conv = _mixer(proj3, small3, small_t4, w, s_gla, s_gdn, s_conv, chunk=chunk,
                                                sub=sub, nseq=nseq, t_valid=t_valid, out_dtype=act_dtype)
    if t_valid < t_pad:
        merged, x = merged[:, :t_valid], x[:, :t_valid]
    m = nb * t_valid
    tm = min(tm, m)
    tm_moe = min(tm_moe, m)
    spec = lambda idx, ngrid: _mod_spec(idx, tm, t_valid, ngrid)
    x1, hx, gid = _post(merged.reshape(m, d), x.reshape(m, d), mods[2], mods[3], mods[4],
                        (spec(2, 1), spec(3, 1), spec(4, 1)),
                        w["w_out"], w["norm2"], w["w_router_t"], w["b_router_t"], tm)
    class_table = PAIR_CLASSES if pair_classes else GROUP_CLASSES
    pos, src, tile_class, ntiles = _route(gid[1 if pair_classes else 0], class_table.shape[0], tm_moe)
    ys = _experts(hx, src, tile_class, ntiles, class_table, w["w_gate"], w["w_up"], w["w_down"], tm_moe)
    tm_fin = min(tm, FINAL_TILE)
    y = _final(ys, pos, x1, mods[5], _mod_spec(5, tm_fin, t_valid, 1), w["final_norm"], tm_fin)
    return y.reshape(nb, t_valid, d), new_gla, new_gdn, new_conv


def _regroup_kernel(q_ref, r_ref, a_ref, b_ref, o_ref, *, shifts):
    window = jnp.concatenate([a_ref[...], b_ref[...]], axis=0)
    r = r_ref[pl.program_id(0)]
    for s in shifts:
        @pl.when(r == s)
        def _():
            o_ref[...] = window[s:s + PROJ_PART, :].T.astype(o_ref.dtype)


def _regroup_columns(w, starts):
    d = w.shape[1]
    assert d == PROJ_PART and all(s % SUBLANES == 0 for s in starts)
    w = jnp.swapaxes(w, 1, 2)
    q = np.array([s // PROJ_PART for s in starts], np.int32)
    r = np.array([s % PROJ_PART for s in starts], np.int32)
    assert r.max() <= LANES
    per = PROJ_PART // LANES
    return pl.pallas_call(
        functools.partial(_regroup_kernel, shifts=tuple(sorted(set(r.tolist())))),
        out_shape=jax.ShapeDtypeStruct((d, len(starts) * PROJ_PART), BF16),
        grid_spec=pltpu.PrefetchScalarGridSpec(
            num_scalar_prefetch=2,
            grid=(len(starts),),
            in_specs=[pl.BlockSpec((None, PROJ_PART, d), lambda i, q, r: (0, q[i], 0)),
                      pl.BlockSpec((None, LANES, d), lambda i, q, r: (0, per * (q[i] + 1), 0))],
            out_specs=pl.BlockSpec((d, PROJ_PART), lambda i, q, r: (0, i))),
        compiler_params=_params("arbitrary"),
        name="regroup_w_in",
    )(jnp.asarray(q), jnp.asarray(r), w, w)


def _prep_weights(w_in, w_gla_a2, b_gla_a, gla_norm_w, w_conv, gdn_A_log, gdn_dt_bias, gdn_norm_w, w_out,
                  norm1_w, norm2_w, w_group_router, b_group_router, w_expert_router, b_expert_router,
                  w_exp_gate, w_exp_up, w_exp_down, final_norm_w):
    d = D_MODEL
    o = 0
    start = {}
    for name, width in (("gla", 2 * GLA_QK + 2 * GLA_VW), ("ra", GLA_GATE_RANK), ("qkv", GDN_CONV_DIM),
                        ("zb", GDN_VW), ("beta", GDN_HEADS), ("a", GDN_HEADS), ("gates", 2 * D_MODEL)):
        start[name] = o
        o += width
    gate_starts = [start["zb"], start["gates"], start["gates"] + PROJ_PART]
    main_starts = [s for j in range(3) for s in (start["gla"] + j * PROJ_PART, start["qkv"] + j * PROJ_PART,
                                                 gate_starts[j])]
    assert start["a"] == start["beta"] + GDN_HEADS
    w_main = _regroup_columns(w_in, main_starts + [start["ra"], start["beta"]])
    small = jnp.concatenate([w_main[:, PROJ_MAIN:PROJ_MAIN + GLA_GATE_RANK],
                             w_main[:, PROJ_MAIN + PROJ_PART:PROJ_MAIN + PROJ_PART + 2 * GDN_HEADS]], axis=1)
    w_small = jnp.pad(small, ((0, 0), (0, LANES - small.shape[1])))
    w_small_t = small.T
    assert N_GROUPS == EXPERTS_PER_GROUP
    stride_pad = ((0, 0), (0, 0), (0, ROUTER_GROUP_STRIDE - EXPERTS_PER_GROUP))

    def router_rows(group_part, expert_part):
        both = jnp.concatenate([group_part, expert_part], axis=1).reshape(-1, 1 + N_GROUPS, EXPERTS_PER_GROUP)
        return jnp.pad(both, stride_pad).reshape(-1, ROUTER_ROWS)

    wr_t = router_rows(w_group_router, w_expert_router).T
    br_t = router_rows(b_group_router[None], b_expert_router[None]).T
    return dict(
        w_main=w_main, w_small=w_small, w_small_t=w_small_t,
        norm1=norm1_w.reshape(1, d), norm2=norm2_w.reshape(1, d), final_norm=final_norm_w.reshape(1, d),
        wa2=w_gla_a2, ba=b_gla_a.reshape(1, GLA_QK), gla_norm=gla_norm_w.reshape(1, GLA_DV),
        w_conv=w_conv, a_log=gdn_A_log, dt_bias=gdn_dt_bias, gdn_norm=gdn_norm_w.reshape(1, GDN_DV),
        w_out=w_out.astype(BF16), w_router_t=wr_t.astype(BF16), b_router_t=br_t,
        w_gate=w_exp_gate.astype(BF16), w_up=w_exp_up.astype(BF16), w_down=w_exp_down.astype(BF16),
    )


def kernel(x_prompt, x_sample, c_prompt, c_sample, state_gla, state_gdn, state_conv, w_ada, b_ada, norm1_w, w_in, w_gla_a2, b_gla_a, gla_norm_w, w_conv, gdn_A_log, gdn_dt_bias, gdn_norm_w, w_out, norm2_w, w_group_router, b_group_router, w_expert_router, b_expert_router, w_exp_gate, w_exp_up, w_exp_down, final_norm_w):
    assert w_ada.shape[0] == 1, "single layer"
    bp, tp, d = x_prompt.shape
    bs, ts, _ = x_sample.shape
    w = _prep_weights(w_in, w_gla_a2[0], b_gla_a[0], gla_norm_w[0], w_conv[0], gdn_A_log[0], gdn_dt_bias[0],
                      gdn_norm_w[0], w_out[0], norm1_w[0], norm2_w[0], w_group_router[0], b_group_router[0],
                      w_expert_router[0], b_expert_router[0], w_exp_gate[0], w_exp_up[0], w_exp_down[0],
                      final_norm_w)
    mod = _ada_mod(jnp.concatenate([c_prompt, c_sample], axis=0), w_ada[0], b_ada[0]).reshape(bp + bs, N_MOD, d)

    y_p, gla_p, gdn_p, conv_p = _trunk(x_prompt, mod[:bp], None, None, None, w,
                                       chunk=64, sub=GLA_SUBCHUNK, nseq=4, t_valid=tp, tm=min(1024, tp),
                                       tm_moe=min(512, tp), pair_classes=False, act_dtype=BF16)
    ts_pad = SUBLANES
    xs = jnp.pad(x_sample, ((0, 0), (0, ts_pad - ts), (0, 0)))
    y_s, gla_s, gdn_s, conv_s = _trunk(xs, mod[bp:], state_gla[0], state_gdn[0], state_conv[0], w,
                                       chunk=ts_pad, sub=ts_pad, nseq=4, t_valid=ts, tm=min(512, bs * ts_pad),
                                       tm_moe=128, pair_classes=False, act_dtype=F32)
    return (y_p, y_s, gla_p[None], gdn_p[None], conv_p[None], gla_s[None], gdn_s[None], conv_s[None])
```

```python
import functools
import math

import jax
import numpy as np
import jax.numpy as jnp
from jax import lax
from jax.experimental import pallas as pl
from jax.experimental.pallas import tpu as pltpu

F32 = jnp.float32
BF16 = jnp.bfloat16
HIGHEST = lax.Precision.HIGHEST

D_MODEL = 1024
GLA_HEADS = 4
GLA_DK = 128
GLA_DV = 256
GLA_QK = GLA_HEADS * GLA_DK
GLA_VW = GLA_HEADS * GLA_DV
GLA_GATE_RANK = 16
GLA_TAU = 16.0
GLA_SUBCHUNK = 16
GDN_HEADS = 8
GDN_DK = 128
GDN_DV = 128
GDN_QK = GDN_HEADS * GDN_DK
GDN_VW = GDN_HEADS * GDN_DV
GDN_CONV = 4
GDN_CONV_DIM = 2 * GDN_QK + GDN_VW
GDN_INV_BLOCK = 16
N_GROUPS = 4
EXPERTS_PER_GROUP = 4
N_EXPERTS = N_GROUPS * EXPERTS_PER_GROUP
D_EXPERT = D_MODEL // 2
N_MOD = 6
NORM_EPS = 1e-6
L2_EPS = 1e-6

LANES = 128
SUBLANES = 8
VMEM_LIMIT = 56 * 1024 * 1024

PROJ_MAIN = 2 * GLA_QK + 2 * GLA_VW + GDN_CONV_DIM + GDN_VW + 2 * D_MODEL
PROJ_PART = 1024
INPROJ_TN = 3 * PROJ_PART
COL_QK, COL_V, COL_GA = 0, 3, 6
COL_CQ, COL_CK, COL_CV = 1, 4, 7
COL_ZB, COL_GATE_A, COL_GATE_B = 2, 5, 8
SMALL_RA, SMALL_BETA, SMALL_A = 0, GLA_GATE_RANK, GLA_GATE_RANK + GDN_HEADS
SMALL_ROWS = 32
ROUTER_GROUP_STRIDE = 8
ROUTER_ROWS = ROUTER_GROUP_STRIDE * (1 + N_GROUPS)

GROUP_CLASSES = np.array([[g] + list(range(EXPERTS_PER_GROUP)) for g in range(N_GROUPS)], np.int32)

HX_WIDTH = D_MODEL + LANES
SCALAR_UNROLL = 16
FINAL_TILE = 256

NT_DIMS = (((1,), (1,)), ((), ()))
TN_DIMS = (((0,), (0,)), ((), ()))


def _sigmoid(x):
    return 0.5 * jnp.tanh(0.5 * x) + 0.5


def _silu(x):
    return x * _sigmoid(x)


def _softplus(x):
    return jnp.maximum(x, 0.0) + jnp.log1p(jnp.exp(-jnp.abs(x)))


def _rms(x, w):
    return x * lax.rsqrt(jnp.mean(x * x, axis=-1, keepdims=True) + NORM_EPS) * w


def _mod(ref, rows):
    v = ref[...]
    if v.ndim == 3:
        v = jnp.broadcast_to(v, (v.shape[0], rows // v.shape[0], v.shape[2])).reshape(rows, v.shape[2])
    return v


def _params(*sem):
    return pltpu.CompilerParams(dimension_semantics=sem, vmem_limit_bytes=VMEM_LIMIT)


def _ada_kernel(c_ref, w_ref, b_ref, o_ref):
    cs = _silu(c_ref[...])
    o_ref[...] = jnp.dot(cs.astype(BF16), w_ref[...].astype(BF16), preferred_element_type=F32) + b_ref[...]


def _ada_mod(c_all, w_ada, b_ada):
    rows, d = c_all.shape
    n = w_ada.shape[1]
    tn = 1024
    return pl.pallas_call(
        _ada_kernel,
        out_shape=jax.ShapeDtypeStruct((rows, n), F32),
        grid=(n // tn,),
        in_specs=[pl.BlockSpec((rows, d), lambda j: (0, 0)),
                  pl.BlockSpec((d, tn), lambda j: (0, j)),
                  pl.BlockSpec((1, tn), lambda j: (0, j))],
        out_specs=pl.BlockSpec((rows, tn), lambda j: (0, j)),
        compiler_params=_params("arbitrary"),
        name="ada_mod",
    )(c_all, w_ada, b_ada.reshape(1, n))


def _inproj_kernel(x_ref, shift_ref, scale_ref, nw_ref, w_ref, ws_ref, wst_ref,
                   o_ref, os_ref, ost_ref, h_scr):
    @pl.when(pl.program_id(1) == 0)
    def _():
        rows = x_ref.shape[0]
        h = _rms(x_ref[...], nw_ref[...]) * (1.0 + _mod(scale_ref, rows)) + _mod(shift_ref, rows)
        hb = h.astype(BF16)
        h_scr[...] = hb
        os_ref[...] = jnp.dot(hb, ws_ref[...], preferred_element_type=F32)
        ost_ref[...] = lax.dot_general(wst_ref[...], hb, NT_DIMS, preferred_element_type=F32)

    o_ref[...] = jnp.dot(h_scr[...], w_ref[...], preferred_element_type=F32).astype(o_ref.dtype)


def _inproj(x2d, shift, scale, mod_specs, nw, w_main, w_small, w_small_t, tm, out_dtype):
    m, d = x2d.shape
    n = PROJ_MAIN
    tn = INPROJ_TN
    return pl.pallas_call(
        _inproj_kernel,
        out_shape=(jax.ShapeDtypeStruct((m, n), out_dtype),
                   jax.ShapeDtypeStruct((m, LANES), F32),
                   jax.ShapeDtypeStruct((SMALL_ROWS, m), F32)),
        grid=(m // tm, n // tn),
        in_specs=[pl.BlockSpec((tm, d), lambda i, j: (i, 0)),
                  mod_specs[0], mod_specs[1],
                  pl.BlockSpec((1, d), lambda i, j: (0, 0)),
                  pl.BlockSpec((d, tn), lambda i, j: (0, j)),
                  pl.BlockSpec((d, LANES), lambda i, j: (0, 0)),
                  pl.BlockSpec((SMALL_ROWS, d), lambda i, j: (0, 0))],
        out_specs=(pl.BlockSpec((tm, tn), lambda i, j: (i, j)),
                   pl.BlockSpec((tm, LANES), lambda i, j: (i, 0)),
                   pl.BlockSpec((SMALL_ROWS, tm), lambda i, j: (0, i))),
        scratch_shapes=[pltpu.VMEM((tm, d), BF16)],
        compiler_params=_params("parallel", "arbitrary"),
        name="inproj",
    )(x2d, shift, scale, nw, w_main, w_small, w_small_t)


def _gla_stages(ins, s0_ref, sout_ref, st_scr, out, *, chunk, sub, nchunks, nseq, t_valid):
    qk_ref, v_ref, ga_ref, gt_ref, sm_ref, wa2_ref, ba_ref, nw_ref = ins
    has_state = s0_ref is not None
    n = pl.program_id(1)
    masked = t_valid < chunk * nchunks
    units = [(s, h) for s in range(nseq) for h in range(GLA_HEADS)]

    @pl.when(n == 0)
    def _():
        for u, (s, h) in enumerate(units):
            st_scr[u] = s0_ref[s, h].T if has_state else jnp.zeros((GLA_DV, GLA_DK), F32)

    row = lax.broadcasted_iota(jnp.int32, (chunk, chunk), 0)
    col = lax.broadcasted_iota(jnp.int32, (chunk, chunk), 1)
    rowi = lax.broadcasted_iota(jnp.int32, (chunk, 1), 0)
    valid = (rowi + n * chunk) < t_valid
    tri = (row >= col).astype(F32)
    wa2 = wa2_ref[...].astype(BF16)
    b_all = []
    for s in range(nseq):
        ra = sm_ref[s, :, SMALL_RA:SMALL_RA + GLA_GATE_RANK]
        x = jnp.dot(ra.astype(BF16), wa2, preferred_element_type=F32) + ba_ref[...]
        g = (jnp.minimum(x, 0.0) - jnp.log1p(jnp.exp(-jnp.abs(x)))) * (1.0 / GLA_TAU)
        if masked:
            g = jnp.where(valid, g, 0.0)
        b_all.append(jnp.dot(tri, g, precision=HIGHEST, preferred_element_type=F32))
    yield

    q, k, v, b = [], [], [], []
    for s, h in units:
        q.append(qk_ref[s, :, h * GLA_DK:(h + 1) * GLA_DK].astype(F32) * (GLA_DK ** -0.5))
        kh = qk_ref[s, :, GLA_QK + h * GLA_DK:GLA_QK + (h + 1) * GLA_DK].astype(F32)
        k.append(jnp.where(valid, kh, 0.0) if masked else kh)
        v.append(v_ref[s, :, h * GLA_DV:(h + 1) * GLA_DV].astype(BF16))
        b.append(b_all[s][:, h * GLA_DK:(h + 1) * GLA_DK])
    nu = range(len(units))
    st = [st_scr[u] for u in nu]
    o = [lax.dot_general((q[u] * jnp.exp(b[u])).astype(BF16), st[u].astype(BF16), NT_DIMS,
                         preferred_element_type=F32) for u in nu]
    yield
    blocks = [[] for _ in nu]
    for i in range(chunk // sub):
        r0, r1 = i * sub, (i + 1) * sub
        for u in nu:
            bref = b[u][r0 - 1:r0] if i > 0 else jnp.zeros((1, GLA_DK), F32)
            qt = (q[u][r0:r1] * jnp.exp(b[u][r0:r1] - bref)).astype(BF16)
            expo = bref - b[u]
            if r1 < chunk:
                expo = jnp.where(rowi < r1, expo, 0.0)
            kt = (k[u] * jnp.exp(expo)).astype(BF16)
            blocks[u].append(lax.dot_general(qt, kt, NT_DIMS, preferred_element_type=F32))
        yield
    for u in nu:
        a = blocks[u][0] if len(blocks[u]) == 1 else jnp.concatenate(blocks[u], axis=0)
        a = jnp.where(col <= row, a, 0.0)
        o[u] = o[u] + jnp.dot(a.astype(BF16), v[u], preferred_element_type=F32)
    yield
    for u in nu:
        bl = b[u][chunk - 1:chunk]
        kt = (k[u] * jnp.exp(bl - b[u])).astype(BF16)
        st_scr[u] = st[u] * jnp.exp(bl) + lax.dot_general(v[u], kt, TN_DIMS, preferred_element_type=F32)
    yield
    for u, (s, h) in enumerate(units):
        sl = slice(h * GLA_DV, (h + 1) * GLA_DV)
        out[s, h] = (_rms(o[u], nw_ref[...]) * _silu(ga_ref[s, :, sl].astype(F32))
                     * _sigmoid(gt_ref[s, :, sl].astype(F32)))

    @pl.when(n == nchunks - 1)
    def _():
        for u, (s, h) in enumerate(units):
            sout_ref[s, h] = st_scr[u].T


def _gdn_stages(ins, s0_ref, c0_ref, sout_ref, cout_ref, s_scr, xp_scr, out, *, chunk, nchunks, nseq, t_valid):
    (cq_ref, ck_ref, cv_ref, zb_ref, gt_ref, sm_ref, smt_ref, wc_ref, alog_ref, dtb_ref, alogc_ref, dtbc_ref,
     nw_ref) = ins
    has_state = s0_ref is not None
    n = pl.program_id(1)
    masked = t_valid < chunk * nchunks
    pad = SUBLANES
    units = [(s, h) for s in range(nseq) for h in range(GDN_HEADS)]
    nu = range(len(units))

    @pl.when(n == 0)
    def _():
        for u, (s, h) in enumerate(units):
            s_scr[u] = s0_ref[s, h] if has_state else jnp.zeros((GDN_DK, GDN_DV), F32)
        for s in range(nseq):
            xp_scr[s, 0:pad, :] = jnp.zeros((pad, GDN_CONV_DIM), F32)
            if has_state:
                xp_scr[s, pad - (GDN_CONV - 1):pad, :] = c0_ref[s]

    row = lax.broadcasted_iota(jnp.int32, (chunk, chunk), 0)
    col = lax.broadcasted_iota(jnp.int32, (chunk, chunk), 1)
    valid_c = (lax.broadcasted_iota(jnp.int32, (chunk, 1), 0) + n * chunk) < t_valid
    valid_r = (lax.broadcasted_iota(jnp.int32, (1, chunk), 1) + n * chunk) < t_valid
    tri_c = (row >= col).astype(F32)
    tri_r = (row <= col).astype(F32)
    b_col, b_row, beta_col = [], [], []
    for s in range(nseq):
        for c, ref in enumerate((cq_ref, ck_ref, cv_ref)):
            xp_scr[s, pad:pad + chunk, c * PROJ_PART:(c + 1) * PROJ_PART] = ref[s].astype(F32)
        g_col = -jnp.exp(alog_ref[...]) * _softplus(sm_ref[s, :, SMALL_A:SMALL_A + GDN_HEADS] + dtb_ref[...])
        bt = _sigmoid(sm_ref[s, :, SMALL_BETA:SMALL_BETA + GDN_HEADS])
        g_row = -jnp.exp(alogc_ref[...]) * _softplus(smt_ref[s, SMALL_A:SMALL_A + GDN_HEADS, :] + dtbc_ref[...])
        if masked:
            g_col = jnp.where(valid_c, g_col, 0.0)
            bt = jnp.where(valid_c, bt, 0.0)
            g_row = jnp.where(valid_r, g_row, 0.0)
        beta_col.append(bt)
        b_col.append(jnp.dot(tri_c, g_col, precision=HIGHEST, preferred_element_type=F32))
        b_row.append(jnp.dot(g_row, tri_r, precision=HIGHEST, preferred_element_type=F32))
    yield

    def conv_silu(s, c0):
        cs = slice(c0, c0 + LANES)
        y = xp_scr[s, pad:pad + chunk, cs] * wc_ref[GDN_CONV - 1:GDN_CONV, cs]
        for j in range(1, GDN_CONV):
            y = y + xp_scr[s, pad - j:pad - j + chunk, cs] * wc_ref[GDN_CONV - 1 - j:GDN_CONV - j, cs]
        return _silu(y)

    bdot = lambda a, b: jnp.dot(a.astype(BF16), b.astype(BF16), preferred_element_type=F32)
    q, k, v, kb, dec, bc, beta = [], [], [], [], [], [], []
    for u, (s, h) in enumerate(units):
        qh = conv_silu(s, h * GDN_DK)
        kh = conv_silu(s, GDN_QK + h * GDN_DK)
        v.append(conv_silu(s, 2 * GDN_QK + h * GDN_DV))
        q.append(qh * lax.rsqrt(jnp.sum(qh * qh, axis=-1, keepdims=True) + L2_EPS) * (GDN_DK ** -0.5))
        k.append(kh * lax.rsqrt(jnp.sum(kh * kh, axis=-1, keepdims=True) + L2_EPS))
        bc.append(b_col[s][:, h:h + 1])
        beta.append(beta_col[s][:, h:h + 1])
        dec.append(jnp.where(row >= col, jnp.exp(jnp.minimum(bc[u] - b_row[s][h:h + 1, :], 0.0)), 0.0))
        kb.append(k[u] * beta[u])
        if u % GDN_HEADS == GDN_HEADS - 1:
            yield
    kq = [lax.dot_general(jnp.concatenate([kb[u], q[u]], axis=0).astype(BF16), k[u].astype(BF16), NT_DIMS,
                          preferred_element_type=F32) for u in nu]
    yield
    blk = min(GDN_INV_BLOCK, chunk)
    nblk = chunk // blk
    same_blk = (row // blk) == (col // blk)
    lmat = [jnp.where(row > col, kq[u][:chunk] * dec[u], 0.0) for u in nu]
    p = [jnp.where(same_blk, -lmat[u], 0.0) for u in nu]
    r = p
    for _ in range(int(math.log2(blk)) - 1):
        p = [bdot(p[u], p[u]) for u in nu]
        r = [r[u] + p[u] + bdot(r[u], p[u]) for u in nu]
        yield
    if nblk > 1:
        lo = [jnp.where(same_blk, 0.0, lmat[u]) for u in nu]
        p = [-(lo[u] + bdot(r[u], lo[u])) for u in nu]
        qm = p
        yield
        for _ in range(int(math.log2(nblk)) - 1):
            p = [bdot(p[u], p[u]) for u in nu]
            qm = [qm[u] + p[u] + bdot(qm[u], p[u]) for u in nu]
            yield
        r = [r[u] + qm[u] + bdot(qm[u], r[u]) for u in nu]
    rhs = [jnp.concatenate([v[u] * beta[u], kb[u] * jnp.exp(bc[u])], axis=-1) for u in nu]
    uw = [rhs[u] + bdot(r[u], rhs[u]) for u in nu]
    yield
    st = [s_scr[u] for u in nu]
    ws = [bdot(jnp.concatenate([uw[u][:, GDN_DV:], q[u] * jnp.exp(bc[u])], axis=0), st[u]) for u in nu]
    v_new = [uw[u][:, :GDN_DV] - ws[u][:chunk] for u in nu]
    yield
    o = [ws[u][chunk:] + bdot(kq[u][chunk:] * dec[u], v_new[u]) for u in nu]
    for u in nu:
        bl = bc[u][chunk - 1:chunk, :]
        s_scr[u] = jnp.exp(bl) * st[u] + lax.dot_general((k[u] * jnp.exp(bl - bc[u])).astype(BF16),
                                                         v_new[u].astype(BF16), TN_DIMS, preferred_element_type=F32)
    yield
    for u, (s, h) in enumerate(units):
        sl = slice(h * GDN_DV, (h + 1) * GDN_DV)
        out[s, h] = (_rms(o[u], nw_ref[...]) * _silu(zb_ref[s, :, sl].astype(F32))
                     * _sigmoid(gt_ref[s, :, sl].astype(F32)))

    @pl.when(n == nchunks - 1)
    def _():
        last = t_valid - (nchunks - 1) * chunk
        for s in range(nseq):
            cout_ref[s] = xp_scr[s, pad + last - (GDN_CONV - 1):pad + last, :]
        for u, (s, h) in enumerate(units):
            sout_ref[s, h] = s_scr[u]

    for s in range(nseq):
        xp_scr[s, 0:pad, :] = xp_scr[s, chunk:chunk + pad, :]


N_GLA_IN, N_GDN_IN = 8, 13


def _mixer_kernel(*refs, chunk, sub, nchunks, nseq, has_state, t_valid):
    gla_in, refs = refs[:N_GLA_IN], refs[N_GLA_IN:]
    gdn_in, refs = refs[:N_GDN_IN], refs[N_GDN_IN:]
    if has_state:
        (sa0_ref, sb0_ref, c0_ref), refs = refs[:3], refs[3:]
    else:
        sa0_ref = sb0_ref = c0_ref = None
    o_ref, sa_out, sb_out, c_out, sa_scr, sb_scr, xp_scr = refs
    out_a, out_b = {}, {}
    gla = _gla_stages(gla_in, sa0_ref, sa_out, sa_scr, out_a,
                      chunk=chunk, sub=sub, nchunks=nchunks, nseq=nseq, t_valid=t_valid)
    gdn = _gdn_stages(gdn_in, sb0_ref, c0_ref, sb_out, c_out, sb_scr, xp_scr, out_b,
                      chunk=chunk, nchunks=nchunks, nseq=nseq, t_valid=t_valid)
    live = [gdn, gla]
    while live:
        for g in list(live):
            if next(g, StopIteration) is StopIteration:
                live.remove(g)
    per = GLA_DV // GDN_DV
    for (s, h), ob in out_b.items():
        oa = out_a[s, h // per][:, (h % per) * GDN_DV:(h % per + 1) * GDN_DV]
        o_ref[s, :, h * GDN_DV:(h + 1) * GDN_DV] = (oa + ob).astype(o_ref.dtype)


def _mixer(proj3, small3, small_t4, w, s_gla, s_gdn, s_conv, *, chunk, sub, nseq, t_valid, out_dtype):
    nb, t_pad, _ = proj3.shape
    nchunks = t_pad // chunk
    has_state = s_gla is not None
    blk = lambda width, c: pl.BlockSpec((nseq, chunk, width), lambda b, n, c=c: (b, n, c))
    col = lambda c: blk(PROJ_PART, c)
    full = lambda a: pl.BlockSpec(a.shape, lambda b, n: (0,) * a.ndim)
    alog_r, dtb_r = w["a_log"].reshape(1, GDN_HEADS), w["dt_bias"].reshape(1, GDN_HEADS)
    alog_c, dtb_c = w["a_log"].reshape(GDN_HEADS, 1), w["dt_bias"].reshape(GDN_HEADS, 1)
    small_spec = blk(LANES, 0)
    gla_consts = [w["wa2"], w["ba"], w["gla_norm"]]
    gdn_consts = [w["w_conv"], alog_r, dtb_r, alog_c, dtb_c, w["gdn_norm"]]
    in_specs = ([col(COL_QK), col(COL_V), col(COL_GA), col(COL_GATE_A), small_spec] + [full(a) for a in gla_consts]
                + [col(COL_CQ), col(COL_CK), col(COL_CV), col(COL_ZB), col(COL_GATE_B), small_spec,
                   pl.BlockSpec((nseq, None, SMALL_ROWS, chunk), lambda b, n: (b, n, 0, 0))]
                + [full(a) for a in gdn_consts])
    args = [proj3] * 4 + [small3] + gla_consts + [proj3] * 5 + [small3, small_t4] + gdn_consts
    assert len(in_specs) == N_GLA_IN + N_GDN_IN
    sa_spec = pl.BlockSpec((nseq, GLA_HEADS, GLA_DK, GLA_DV), lambda b, n: (b, 0, 0, 0))
    sb_spec = pl.BlockSpec((nseq, GDN_HEADS, GDN_DK, GDN_DV), lambda b, n: (b, 0, 0, 0))
    conv_spec = pl.BlockSpec((nseq, GDN_CONV - 1, GDN_CONV_DIM), lambda b, n: (b, 0, 0))
    if has_state:
        in_specs += [sa_spec, sb_spec, conv_spec]
        args += [s_gla, s_gdn, s_conv]
    return pl.pallas_call(
        functools.partial(_mixer_kernel, chunk=chunk, sub=sub, nchunks=nchunks, nseq=nseq, has_state=has_state,
                          t_valid=t_valid),
        out_shape=(jax.ShapeDtypeStruct((nb, t_pad, D_MODEL), out_dtype),
                   jax.ShapeDtypeStruct((nb, GLA_HEADS, GLA_DK, GLA_DV), F32),
                   jax.ShapeDtypeStruct((nb, GDN_HEADS, GDN_DK, GDN_DV), F32),
                   jax.ShapeDtypeStruct((nb, GDN_CONV - 1, GDN_CONV_DIM), F32)),
        grid=(nb // nseq, nchunks),
        in_specs=in_specs,
        out_specs=(blk(D_MODEL, 0), sa_spec, sb_spec, conv_spec),
        scratch_shapes=[pltpu.VMEM((nseq * GLA_HEADS, GLA_DV, GLA_DK), F32),
                        pltpu.VMEM((nseq * GDN_HEADS, GDN_DK, GDN_DV), F32),
                        pltpu.VMEM((nseq, chunk + 2 * SUBLANES, GDN_CONV_DIM), F32)],
        compiler_params=_params("parallel", "arbitrary"),
        name="mixer",
    )(*args)


def _post_kernel(mg_ref, x_ref, gate_ref, shift_ref, scale_ref, wo_ref, nw_ref, wrt_ref, brt_ref,
                 x1_ref, hx_ref, gid_ref):
    y = jnp.dot(mg_ref[...].astype(BF16), wo_ref[...], preferred_element_type=F32)
    rows = x_ref.shape[0]
    x1 = x_ref[...] + _mod(gate_ref, rows) * y
    x1_ref[...] = x1
    h2 = _rms(x1, nw_ref[...]) * (1.0 + _mod(scale_ref, rows)) + _mod(shift_ref, rows)
    hx_ref[:, 0:D_MODEL] = h2
    lt = lax.dot_general(wrt_ref[...], h2.astype(BF16), NT_DIMS, preferred_element_type=F32) + brt_ref[...]
    tm = lt.shape[1]
    gl = lt[0:N_GROUPS]
    gidx = lax.broadcasted_iota(jnp.int32, (N_GROUPS, tm), 0)
    gmax = jnp.max(gl, axis=0, keepdims=True)
    g_w = 1.0 / jnp.sum(jnp.exp(gl - gmax), axis=0, keepdims=True)
    g_sel = jnp.min(jnp.where(gl == gmax, gidx, N_GROUPS), axis=0, keepdims=True)
    el = jnp.zeros((EXPERTS_PER_GROUP, tm), F32)
    for g in range(N_GROUPS):
        r = ROUTER_GROUP_STRIDE * (1 + g)
        el = el + jnp.where(g_sel == g, lt[r:r + EXPERTS_PER_GROUP], 0.0)
    eidx = lax.broadcasted_iota(jnp.int32, (EXPERTS_PER_GROUP, tm), 0)
    m1 = jnp.max(el, axis=0, keepdims=True)
    i1 = jnp.min(jnp.where(el == m1, eidx, EXPERTS_PER_GROUP), axis=0, keepdims=True)
    el2 = jnp.where(eidx == i1, -jnp.inf, el)
    m2 = jnp.max(el2, axis=0, keepdims=True)
    i2 = jnp.min(jnp.where(el2 == m2, eidx, EXPERTS_PER_GROUP), axis=0, keepdims=True)
    r21 = jnp.exp(m2 - m1)
    w1 = 1.0 / (1.0 + r21)
    w2 = r21 / (1.0 + r21)
    comb_t = g_w * (jnp.where(eidx == i1, w1, 0.0) + jnp.where(eidx == i2, w2, 0.0))
    ident = (lax.broadcasted_iota(jnp.int32, (EXPERTS_PER_GROUP, LANES), 0)
             == lax.broadcasted_iota(jnp.int32, (EXPERTS_PER_GROUP, LANES), 1)).astype(F32)
    hx_ref[:, D_MODEL:] = lax.dot_general(comb_t, ident, TN_DIMS, precision=HIGHEST, preferred_element_type=F32)
    gid_ref[...] = jnp.broadcast_to(g_sel, gid_ref.shape)


def _post(merged, x2d, gate, shift, scale, mod_specs, wo, nw, wrt, brt, tm):
    m, d = x2d.shape
    row = lambda: pl.BlockSpec((tm, d), lambda i: (i, 0))
    full = lambda a: pl.BlockSpec(a.shape, lambda i: (0,) * a.ndim)
    return pl.pallas_call(
        _post_kernel,
        out_shape=(jax.ShapeDtypeStruct((m, d), F32),
                   jax.ShapeDtypeStruct((m, HX_WIDTH), F32),
                   jax.ShapeDtypeStruct((SUBLANES, m), jnp.int32)),
        grid=(m // tm,),
        in_specs=[row(), row(), mod_specs[0], mod_specs[1], mod_specs[2],
                  full(wo), full(nw), full(wrt), full(brt)],
        out_specs=(row(), pl.BlockSpec((tm, HX_WIDTH), lambda i: (i, 0)),
                   pl.BlockSpec((SUBLANES, tm), lambda i: (0, i))),
        compiler_params=_params("parallel"),
        name="post_mixer",
    )(merged, x2d, gate, shift, scale, wo, nw, wrt, brt)


def _invert_kernel(pos_ref, src_ref):
    def clear(i, carry):
        src_ref[i] = 0
        return carry

    def place(t, carry):
        src_ref[pos_ref[t]] = t
        return carry

    lax.fori_loop(0, src_ref.shape[0], clear, 0, unroll=SCALAR_UNROLL)
    lax.fori_loop(0, pos_ref.shape[0], place, 0, unroll=SCALAR_UNROLL)


def _invert_rows(pos, rows):
    smem = pl.BlockSpec(memory_space=pltpu.SMEM)
    return pl.pallas_call(
        _invert_kernel,
        out_shape=jax.ShapeDtypeStruct((rows,), jnp.int32),
        in_specs=[smem],
        out_specs=smem,
        name="invert_rows",
    )(pos)


def _route(cls, ncls, tm):
    m = cls.shape[0]
    ntiles_max = m // tm + ncls - 1
    onehot = (cls[:, None] == jnp.arange(ncls, dtype=jnp.int32)[None, :]).astype(jnp.int32)
    incl = jnp.cumsum(onehot, axis=0)
    tiles = (incl[-1] + tm - 1) // tm
    tile_end = jnp.cumsum(tiles)
    pos = jnp.sum(onehot * ((tile_end - tiles) * tm + incl - onehot), axis=1)
    src = _invert_rows(pos, ntiles_max * tm)
    t = jnp.arange(ntiles_max, dtype=jnp.int32)
    tile_class = jnp.minimum(jnp.sum((t[:, None] >= tile_end[None, :]).astype(jnp.int32), axis=1), ncls - 1)
    return pos, src, tile_class, tile_end[-1:]


def _row_gather(idx_ref, src_hbm, buf, sem, rows, first=0):
    for r in range(first, first + rows):
        pltpu.make_async_copy(src_hbm.at[pl.ds(idx_ref[0, r], 1)], buf.at[pl.ds(r, 1)],
                              sem).start(priority=r % 2)


def _row_gather_wait(src_hbm, buf, sem):
    pltpu.make_async_copy(src_hbm.at[pl.ds(0, buf.shape[0])], buf, sem).wait()


def _experts_kernel(meta_ref, nt_ref, src_ref, srcn_ref, hx_hbm, *rest, tm, nslot):
    w_refs, (y_ref, xbuf, sem) = rest[:3 * nslot], rest[3 * nslot:]
    t = pl.program_id(0)
    nt = nt_ref[0]
    slot = t % 2

    @pl.when(t == 0)
    def _():
        _row_gather(src_ref, hx_hbm, xbuf.at[0], sem.at[0], tm)

    @pl.when(t < nt)
    def _():
        cur, nxt, sem_cur, sem_nxt = xbuf.at[slot], xbuf.at[1 - slot], sem.at[slot], sem.at[1 - slot]
        _row_gather_wait(hx_hbm, cur, sem_cur)
        x = cur[:, 0:D_MODEL].astype(BF16)
        acc = jnp.zeros((tm, D_MODEL), F32)
        for j in range(nslot):
            wg_ref, wu_ref, wd_ref = w_refs[3 * j:3 * j + 3]
            e = meta_ref[t * (1 + nslot) + 1 + j]
            share = tm // (3 * nslot)
            a = jnp.dot(x, wg_ref[0, 0], preferred_element_type=F32)
            _row_gather(srcn_ref, hx_hbm, nxt, sem_nxt, share, first=(3 * j) * share)
            u = jnp.dot(x, wu_ref[0, 0], preferred_element_type=F32)
            _row_gather(srcn_ref, hx_hbm, nxt, sem_nxt, share, first=(3 * j + 1) * share)
            cw = jnp.zeros((tm, 1), F32)
            for c in range(EXPERTS_PER_GROUP):
                cw = cw + jnp.where(e == c, cur[:, D_MODEL + c:D_MODEL + c + 1], 0.0)
            acc = acc + jnp.dot((_silu(a) * u * cw).astype(BF16), wd_ref[0, 0], preferred_element_type=F32)
            last = tm - (3 * nslot - 1) * share if j == nslot - 1 else share
            _row_gather(srcn_ref, hx_hbm, nxt, sem_nxt, last, first=(3 * j + 2) * share)
        y_ref[:, 0, :] = acc

        @pl.when(t == nt - 1)
        def _():
            _row_gather_wait(hx_hbm, nxt, sem_nxt)

    @pl.when(t >= nt)
    def _():
        y_ref[...] = jnp.zeros_like(y_ref)


def _experts(hx, src, tile_class, ntiles, class_table, wg, wu, wd, tm):
    ntiles_max = tile_class.shape[0]
    nslot = class_table.shape[1] - 1
    d, f = D_MODEL, D_EXPERT
    src3 = src.reshape(ntiles_max, 1, tm)
    meta = jnp.asarray(class_table, jnp.int32)[tile_class].reshape(-1)
    stride = 1 + nslot
    wspec = lambda j, shape: pl.BlockSpec(
        (1, 1) + shape, lambda t, meta, nt, j=j: (meta[t * stride], meta[t * stride + 1 + j], 0, 0))
    w_specs, w_args = [], []
    for j in range(nslot):
        w_specs += [wspec(j, (d, f)), wspec(j, (d, f)), wspec(j, (f, d))]
        w_args += [wg, wu, wd]
    return pl.pallas_call(
        functools.partial(_experts_kernel, tm=tm, nslot=nslot),
        out_shape=jax.ShapeDtypeStruct((ntiles_max * tm, 1, d), F32),
        grid_spec=pltpu.PrefetchScalarGridSpec(
            num_scalar_prefetch=2,
            grid=(ntiles_max,),
            in_specs=[pl.BlockSpec((None, 1, tm), lambda t, meta, nt: (t, 0, 0), memory_space=pltpu.SMEM),
                      pl.BlockSpec((None, 1, tm),
                                   lambda t, meta, nt: (jnp.maximum(jnp.minimum(t + 1, nt[0] - 1), 0), 0, 0),
                                   memory_space=pltpu.SMEM),
                      pl.BlockSpec(memory_space=pl.ANY)] + w_specs,
            out_specs=pl.BlockSpec((tm, 1, d), lambda t, meta, nt: (t, 0, 0)),
            scratch_shapes=[pltpu.VMEM((2, tm, HX_WIDTH), F32), pltpu.SemaphoreType.DMA((2,))]),
        compiler_params=_params("arbitrary"),
        name="experts",
    )(meta, ntiles, src3, src3, hx, *w_args)


def _final_kernel(pos_ref, posn_ref, ys_hbm, x1_ref, gate_ref, nw_ref, o_ref, ybuf, sem, *, tm, nsteps):
    i = pl.program_id(0)
    slot = i % 2

    @pl.when(i == 0)
    def _():
        _row_gather(pos_ref, ys_hbm, ybuf.at[0], sem.at[0], tm)

    _row_gather(posn_ref, ys_hbm, ybuf.at[1 - slot], sem.at[1 - slot], tm)
    _row_gather_wait(ys_hbm, ybuf.at[slot], sem.at[slot])
    o_ref[...] = _rms(x1_ref[...] + _mod(gate_ref, tm) * ybuf[slot, :, 0, :], nw_ref[...])

    @pl.when(i == nsteps - 1)
    def _():
        _row_gather_wait(ys_hbm, ybuf.at[1 - slot], sem.at[1 - slot])


def _final(ys, pos, x1, gate, gate_spec, nw, tm):
    m, d = x1.shape
    nsteps = m // tm
    pos3 = pos.reshape(nsteps, 1, tm)
    return pl.pallas_call(
        functools.partial(_final_kernel, tm=tm, nsteps=nsteps),
        out_shape=jax.ShapeDtypeStruct((m, d), F32),
        grid=(nsteps,),
        in_specs=[pl.BlockSpec((None, 1, tm), lambda i: (i, 0, 0), memory_space=pltpu.SMEM),
                  pl.BlockSpec((None, 1, tm), lambda i: (jnp.minimum(i + 1, nsteps - 1), 0, 0),
                               memory_space=pltpu.SMEM),
                  pl.BlockSpec(memory_space=pl.ANY),
                  pl.BlockSpec((tm, d), lambda i: (i, 0)),
                  gate_spec,
                  pl.BlockSpec((1, d), lambda i: (0, 0))],
        out_specs=pl.BlockSpec((tm, d), lambda i: (i, 0)),
        scratch_shapes=[pltpu.VMEM((2, tm, 1, d), F32), pltpu.SemaphoreType.DMA((2,))],
        compiler_params=_params("arbitrary"),
        name="final",
    )(pos3, pos3, ys, x1, gate, nw)


def _mod_spec(idx, tm, t, ngrid):
    if t % tm == 0:
        per = t // tm
        shape, index = (None, None, 1, D_MODEL), lambda i: (i // per, idx, 0, 0)
    else:
        shape, index = (tm // t, None, 1, D_MODEL), lambda i: (i, idx, 0, 0)
    if ngrid == 2:
        return pl.BlockSpec(shape, lambda i, j: index(i))
    return pl.BlockSpec(shape, index)


def _trunk(x, mod, s_gla, s_gdn, s_conv, w, *, chunk, sub, nseq, t_valid, tm, tm_moe, act_dtype):
    nb, t_pad, d = x.shape
    m = nb * t_pad
    x2d = x.reshape(m, d)
    mods = [mod.reshape(nb, N_MOD, 1, d)] * N_MOD
    spec = lambda idx, ngrid: _mod_spec(idx, tm, t_pad, ngrid)

    proj, small, small_t = _inproj(x2d, mods[0], mods[1], (spec(0, 2), spec(1, 2)), w["norm1"],
                                   w["w_main"], w["w_small"], w["w_small_t"], tm, act_dtype)
    proj3 = proj.reshape(nb, t_pad, PROJ_MAIN)
    small3 = small.reshape(nb, t_pad, LANES)
    small_t4 = small_t.reshape(SMALL_ROWS, nb, t_pad // chunk, chunk).transpose(1, 2, 0, 3)
    merged, new_gla, new_gdn, new_conv = _mixer(proj3, small3, small_t4, w, s_gla, s_gdn, s_conv, chunk=chunk,
                                                sub=sub, nseq=nseq, t_valid=t_valid, out_dtype=act_dtype)
    if t_valid < t_pad:
        merged, x = merged[:, :t_valid], x[:, :t_valid]
    m = nb * t_valid
    tm = min(tm, m)
    tm_moe = min(tm_moe, m)
    spec = lambda idx, ngrid: _mod_spec(idx, tm, t_valid, ngrid)
    x1, hx, gid = _post(merged.reshape(m, d), x.reshape(m, d), mods[2], mods[3], mods[4],
                        (spec(2, 1), spec(3, 1), spec(4, 1)),
                        w["w_out"], w["norm2"], w["w_router_t"], w["b_router_t"], tm)
    class_table = GROUP_CLASSES
    pos, src, tile_class, ntiles = _route(gid[0], class_table.shape[0], tm_moe)
    ys = _experts(hx, src, tile_class, ntiles, class_table, w["w_gate"], w["w_up"], w["w_down"], tm_moe)
    tm_fin = min(tm, FINAL_TILE)
    y = _final(ys, pos, x1, mods[5], _mod_spec(5, tm_fin, t_valid, 1), w["final_norm"], tm_fin)
    return y.reshape(nb, t_valid, d), new_gla, new_gdn, new_conv


def _regroup_kernel(q_ref, r_ref, a_ref, b_ref, o_ref, *, shifts):
    window = jnp.concatenate([a_ref[...], b_ref[...]], axis=0)
    r = r_ref[pl.program_id(0)]
    for s in shifts:
        @pl.when(r == s)
        def _():
            o_ref[...] = window[s:s + PROJ_PART, :].T.astype(o_ref.dtype)


def _regroup_columns(w, starts):
    d = w.shape[1]
    assert d == PROJ_PART and all(s % SUBLANES == 0 for s in starts)
    w = jnp.swapaxes(w, 1, 2)
    q = np.array([s // PROJ_PART for s in starts], np.int32)
    r = np.array([s % PROJ_PART for s in starts], np.int32)
    assert r.max() <= LANES
    per = PROJ_PART // LANES
    return pl.pallas_call(
        functools.partial(_regroup_kernel, shifts=tuple(sorted(set(r.tolist())))),
        out_shape=jax.ShapeDtypeStruct((d, len(starts) * PROJ_PART), BF16),
        grid_spec=pltpu.PrefetchScalarGridSpec(
            num_scalar_prefetch=2,
            grid=(len(starts),),
            in_specs=[pl.BlockSpec((None, PROJ_PART, d), lambda i, q, r: (0, q[i], 0)),
                      pl.BlockSpec((None, LANES, d), lambda i, q, r: (0, per * (q[i] + 1), 0))],
            out_specs=pl.BlockSpec((d, PROJ_PART), lambda i, q, r: (0, i))),
        compiler_params=_params("arbitrary"),
        name="regroup_w_in",
    )(jnp.asarray(q), jnp.asarray(r), w, w)


def _prep_weights(w_in, w_gla_a2, b_gla_a, gla_norm_w, w_conv, gdn_A_log, gdn_dt_bias, gdn_norm_w, w_out,
                  norm1_w, norm2_w, w_group_router, b_group_router, w_expert_router, b_expert_router,
                  w_exp_gate, w_exp_up, w_exp_down, final_norm_w):
    d = D_MODEL
    o = 0
    start = {}
    for name, width in (("gla", 2 * GLA_QK + 2 * GLA_VW), ("ra", GLA_GATE_RANK), ("qkv", GDN_CONV_DIM),
                        ("zb", GDN_VW), ("beta", GDN_HEADS), ("a", GDN_HEADS), ("gates", 2 * D_MODEL)):
        start[name] = o
        o += width
    gate_starts = [start["zb"], start["gates"], start["gates"] + PROJ_PART]
    main_starts = [s for j in range(3) for s in (start["gla"] + j * PROJ_PART, start["qkv"] + j * PROJ_PART,
                                                 gate_starts[j])]
    assert start["a"] == start["beta"] + GDN_HEADS
    w_main = _regroup_columns(w_in, main_starts + [start["ra"], start["beta"]])
    small = jnp.concatenate([w_main[:, PROJ_MAIN:PROJ_MAIN + GLA_GATE_RANK],
                             w_main[:, PROJ_MAIN + PROJ_PART:PROJ_MAIN + PROJ_PART + 2 * GDN_HEADS]], axis=1)
    w_small = jnp.pad(small, ((0, 0), (0, LANES - small.shape[1])))
    w_small_t = small.T
    assert N_GROUPS == EXPERTS_PER_GROUP
    stride_pad = ((0, 0), (0, 0), (0, ROUTER_GROUP_STRIDE - EXPERTS_PER_GROUP))

    def router_rows(group_part, expert_part):
        both = jnp.concatenate([group_part, expert_part], axis=1).reshape(-1, 1 + N_GROUPS, EXPERTS_PER_GROUP)
        return jnp.pad(both, stride_pad).reshape(-1, ROUTER_ROWS)

    wr_t = router_rows(w_group_router, w_expert_router).T
    br_t = router_rows(b_group_router[None], b_expert_router[None]).T
    return dict(
        w_main=w_main, w_small=w_small, w_small_t=w_small_t,
        norm1=norm1_w.reshape(1, d), norm2=norm2_w.reshape(1, d), final_norm=final_norm_w.reshape(1, d),
        wa2=w_gla_a2, ba=b_gla_a.reshape(1, GLA_QK), gla_norm=gla_norm_w.reshape(1, GLA_DV),
        w_conv=w_conv, a_log=gdn_A_log, dt_bias=gdn_dt_bias, gdn_norm=gdn_norm_w.reshape(1, GDN_DV),
        w_out=w_out.astype(BF16), w_router_t=wr_t.astype(BF16), b_router_t=br_t,
        w_gate=w_exp_gate.astype(BF16), w_up=w_exp_up.astype(BF16), w_down=w_exp_down.astype(BF16),
    )


def kernel(x_prompt, x_sample, c_prompt, c_sample, state_gla, state_gdn, state_conv, w_ada, b_ada, norm1_w, w_in, w_gla_a2, b_gla_a, gla_norm_w, w_conv, gdn_A_log, gdn_dt_bias, gdn_norm_w, w_out, norm2_w, w_group_router, b_group_router, w_expert_router, b_expert_router, w_exp_gate, w_exp_up, w_exp_down, final_norm_w):
    assert w_ada.shape[0] == 1, "single layer"
    bp, tp, d = x_prompt.shape
    bs, ts, _ = x_sample.shape
    w = _prep_weights(w_in, w_gla_a2[0], b_gla_a[0], gla_norm_w[0], w_conv[0], gdn_A_log[0], gdn_dt_bias[0],
                      gdn_norm_w[0], w_out[0], norm1_w[0], norm2_w[0], w_group_router[0], b_group_router[0],
                      w_expert_router[0], b_expert_router[0], w_exp_gate[0], w_exp_up[0], w_exp_down[0],
                      final_norm_w)
    mod = _ada_mod(jnp.concatenate([c_prompt, c_sample], axis=0), w_ada[0], b_ada[0]).reshape(bp + bs, N_MOD, d)

    y_p, gla_p, gdn_p, conv_p = _trunk(x_prompt, mod[:bp], None, None, None, w,
                                       chunk=64, sub=GLA_SUBCHUNK, nseq=4, t_valid=tp, tm=min(1024, tp),
                                       tm_moe=min(512, tp), act_dtype=BF16)
    ts_pad = SUBLANES
    xs = jnp.pad(x_sample, ((0, 0), (0, ts_pad - ts), (0, 0)))
    y_s, gla_s, gdn_s, conv_s = _trunk(xs, mod[bp:], state_gla[0], state_gdn[0], state_conv[0], w,
                                       chunk=ts_pad, sub=ts_pad, nseq=4, t_valid=ts, tm=min(512, bs * ts_pad),
                                       tm_moe=128, act_dtype=F32)
    return (y_p, y_s, gla_p[None], gdn_p[None], conv_p[None], gla_s[None], gdn_s[None], conv_s[None])
```

```python
import functools
import math

import jax
import numpy as np
import jax.numpy as jnp
from jax import lax
from jax.experimental import pallas as pl
from jax.experimental.pallas import tpu as pltpu

F32 = jnp.float32
BF16 = jnp.bfloat16
HIGHEST = lax.Precision.HIGHEST

D_MODEL = 1024
GLA_HEADS = 4
GLA_DK = 128
GLA_DV = 256
GLA_QK = GLA_HEADS * GLA_DK
GLA_VW = GLA_HEADS * GLA_DV
GLA_GATE_RANK = 16
GLA_TAU = 16.0
GLA_SUBCHUNK = 16
GDN_HEADS = 8
GDN_DK = 128
GDN_DV = 128
GDN_QK = GDN_HEADS * GDN_DK
GDN_VW = GDN_HEADS * GDN_DV
GDN_CONV = 4
GDN_CONV_DIM = 2 * GDN_QK + GDN_VW
GDN_INV_BLOCK = 16
N_GROUPS = 4
EXPERTS_PER_GROUP = 4
N_EXPERTS = N_GROUPS * EXPERTS_PER_GROUP
D_EXPERT = D_MODEL // 2
N_MOD = 6
NORM_EPS = 1e-6
L2_EPS = 1e-6

LANES = 128
SUBLANES = 8
VMEM_LIMIT = 56 * 1024 * 1024

PROJ_MAIN = 2 * GLA_QK + 2 * GLA_VW + GDN_CONV_DIM + GDN_VW + 2 * D_MODEL
PROJ_PART = 1024
INPROJ_TN = 3 * PROJ_PART
COL_QK, COL_V, COL_GA = 0, 3, 6
COL_CQ, COL_CK, COL_CV = 1, 4, 7
COL_ZB, COL_GATE_A, COL_GATE_B = 2, 5, 8
SMALL_RA, SMALL_BETA, SMALL_A = 0, GLA_GATE_RANK, GLA_GATE_RANK + GDN_HEADS
SMALL_ROWS = 32
ROUTER_GROUP_STRIDE = 8
ROUTER_ROWS = ROUTER_GROUP_STRIDE * (1 + N_GROUPS)

GROUP_CLASSES = np.array([[g] + list(range(EXPERTS_PER_GROUP)) for g in range(N_GROUPS)], np.int32)

HX_WIDTH = D_MODEL + LANES
SCALAR_UNROLL = 16
FINAL_TILE = 512

NT_DIMS = (((1,), (1,)), ((), ()))
TN_DIMS = (((0,), (0,)), ((), ()))


def _sigmoid(x):
    return 0.5 * jnp.tanh(0.5 * x) + 0.5


def _silu(x):
    return x * _sigmoid(x)


def _softplus(x):
    return jnp.maximum(x, 0.0) + jnp.log1p(jnp.exp(-jnp.abs(x)))


def _rms(x, w):
    return x * lax.rsqrt(jnp.mean(x * x, axis=-1, keepdims=True) + NORM_EPS) * w


def _mod(ref, rows):
    v = ref[...]
    if v.ndim == 3:
        v = jnp.broadcast_to(v, (v.shape[0], rows // v.shape[0], v.shape[2])).reshape(rows, v.shape[2])
    return v


def _params(*sem):
    return pltpu.CompilerParams(dimension_semantics=sem, vmem_limit_bytes=VMEM_LIMIT)


def _ada_kernel(c_ref, w_ref, b_ref, o_ref):
    cs = _silu(c_ref[...])
    o_ref[...] = jnp.dot(cs.astype(BF16), w_ref[...].astype(BF16), preferred_element_type=F32) + b_ref[...]


def _ada_mod(c_all, w_ada, b_ada):
    rows, d = c_all.shape
    n = w_ada.shape[1]
    tn = 1024
    return pl.pallas_call(
        _ada_kernel,
        out_shape=jax.ShapeDtypeStruct((rows, n), F32),
        grid=(n // tn,),
        in_specs=[pl.BlockSpec((rows, d), lambda j: (0, 0)),
                  pl.BlockSpec((d, tn), lambda j: (0, j)),
                  pl.BlockSpec((1, tn), lambda j: (0, j))],
        out_specs=pl.BlockSpec((rows, tn), lambda j: (0, j)),
        compiler_params=_params("arbitrary"),
        name="ada_mod",
    )(c_all, w_ada, b_ada.reshape(1, n))


def _inproj_kernel(x_ref, shift_ref, scale_ref, nw_ref, w_ref, ws_ref, wst_ref,
                   o_ref, os_ref, ost_ref, h_scr):
    @pl.when(pl.program_id(1) == 0)
    def _():
        rows = x_ref.shape[0]
        h = _rms(x_ref[...], nw_ref[...]) * (1.0 + _mod(scale_ref, rows)) + _mod(shift_ref, rows)
        hb = h.astype(BF16)
        h_scr[...] = hb
        os_ref[...] = jnp.dot(hb, ws_ref[...], preferred_element_type=F32)
        ost_ref[...] = lax.dot_general(wst_ref[...], hb, NT_DIMS, preferred_element_type=F32)

    o_ref[...] = jnp.dot(h_scr[...], w_ref[...], preferred_element_type=F32).astype(o_ref.dtype)


def _inproj(x2d, shift, scale, mod_specs, nw, w_main, w_small, w_small_t, tm, out_dtype):
    m, d = x2d.shape
    n = PROJ_MAIN
    tn = INPROJ_TN
    return pl.pallas_call(
        _inproj_kernel,
        out_shape=(jax.ShapeDtypeStruct((m, n), out_dtype),
                   jax.ShapeDtypeStruct((m, LANES), F32),
                   jax.ShapeDtypeStruct((SMALL_ROWS, m), F32)),
        grid=(m // tm, n // tn),
        in_specs=[pl.BlockSpec((tm, d), lambda i, j: (i, 0)),
                  mod_specs[0], mod_specs[1],
                  pl.BlockSpec((1, d), lambda i, j: (0, 0)),
                  pl.BlockSpec((d, tn), lambda i, j: (0, j)),
                  pl.BlockSpec((d, LANES), lambda i, j: (0, 0)),
                  pl.BlockSpec((SMALL_ROWS, d), lambda i, j: (0, 0))],
        out_specs=(pl.BlockSpec((tm, tn), lambda i, j: (i, j)),
                   pl.BlockSpec((tm, LANES), lambda i, j: (i, 0)),
                   pl.BlockSpec((SMALL_ROWS, tm), lambda i, j: (0, i))),
        scratch_shapes=[pltpu.VMEM((tm, d), BF16)],
        compiler_params=_params("parallel", "arbitrary"),
        name="inproj",
    )(x2d, shift, scale, nw, w_main, w_small, w_small_t)


def _gla_stages(ins, s0_ref, sout_ref, st_scr, out, *, chunk, sub, nchunks, nseq, t_valid):
    qk_ref, v_ref, ga_ref, gt_ref, sm_ref, wa2_ref, ba_ref, nw_ref = ins
    has_state = s0_ref is not None
    n = pl.program_id(1)
    masked = t_valid < chunk * nchunks
    units = [(s, h) for s in range(nseq) for h in range(GLA_HEADS)]

    @pl.when(n == 0)
    def _():
        for u, (s, h) in enumerate(units):
            st_scr[u] = s0_ref[s, h].T if has_state else jnp.zeros((GLA_DV, GLA_DK), F32)

    row = lax.broadcasted_iota(jnp.int32, (chunk, chunk), 0)
    col = lax.broadcasted_iota(jnp.int32, (chunk, chunk), 1)
    rowi = lax.broadcasted_iota(jnp.int32, (chunk, 1), 0)
    valid = (rowi + n * chunk) < t_valid
    tri = (row >= col).astype(F32)
    wa2 = wa2_ref[...].astype(BF16)
    b_all = []
    for s in range(nseq):
        ra = sm_ref[s, :, SMALL_RA:SMALL_RA + GLA_GATE_RANK]
        x = jnp.dot(ra.astype(BF16), wa2, preferred_element_type=F32) + ba_ref[...]
        g = (jnp.minimum(x, 0.0) - jnp.log1p(jnp.exp(-jnp.abs(x)))) * (1.0 / GLA_TAU)
        if masked:
            g = jnp.where(valid, g, 0.0)
        b_all.append(jnp.dot(tri, g, precision=HIGHEST, preferred_element_type=F32))
    yield

    q, k, v, b = [], [], [], []
    for s, h in units:
        q.append(qk_ref[s, :, h * GLA_DK:(h + 1) * GLA_DK].astype(F32) * (GLA_DK ** -0.5))
        kh = qk_ref[s, :, GLA_QK + h * GLA_DK:GLA_QK + (h + 1) * GLA_DK].astype(F32)
        k.append(jnp.where(valid, kh, 0.0) if masked else kh)
        v.append(v_ref[s, :, h * GLA_DV:(h + 1) * GLA_DV].astype(BF16))
        b.append(b_all[s][:, h * GLA_DK:(h + 1) * GLA_DK])
    nu = range(len(units))
    st = [st_scr[u] for u in nu]
    o = [lax.dot_general((q[u] * jnp.exp(b[u])).astype(BF16), st[u].astype(BF16), NT_DIMS,
                         preferred_element_type=F32) for u in nu]
    yield
    blocks = [[] for _ in nu]
    for i in range(chunk // sub):
        r0, r1 = i * sub, (i + 1) * sub
        for u in nu:
            bref = b[u][r0 - 1:r0] if i > 0 else jnp.zeros((1, GLA_DK), F32)
            qt = (q[u][r0:r1] * jnp.exp(b[u][r0:r1] - bref)).astype(BF16)
            expo = bref - b[u]
            if r1 < chunk:
                expo = jnp.where(rowi < r1, expo, 0.0)
            kt = (k[u] * jnp.exp(expo)).astype(BF16)
            blocks[u].append(lax.dot_general(qt, kt, NT_DIMS, preferred_element_type=F32))
        yield
    for u in nu:
        a = blocks[u][0] if len(blocks[u]) == 1 else jnp.concatenate(blocks[u], axis=0)
        a = jnp.where(col <= row, a, 0.0)
        o[u] = o[u] + jnp.dot(a.astype(BF16), v[u], preferred_element_type=F32)
    yield
    for u in nu:
        bl = b[u][chunk - 1:chunk]
        kt = (k[u] * jnp.exp(bl - b[u])).astype(BF16)
        st_scr[u] = st[u] * jnp.exp(bl) + lax.dot_general(v[u], kt, TN_DIMS, preferred_element_type=F32)
    yield
    for u, (s, h) in enumerate(units):
        sl = slice(h * GLA_DV, (h + 1) * GLA_DV)
        out[s, h] = (_rms(o[u], nw_ref[...]) * _silu(ga_ref[s, :, sl].astype(F32))
                     * _sigmoid(gt_ref[s, :, sl].astype(F32)))

    @pl.when(n == nchunks - 1)
    def _():
        for u, (s, h) in enumerate(units):
            sout_ref[s, h] = st_scr[u].T


def _gdn_stages(ins, s0_ref, c0_ref, sout_ref, cout_ref, s_scr, xp_scr, out, *, chunk, nchunks, nseq, t_valid):
    (cq_ref, ck_ref, cv_ref, zb_ref, gt_ref, sm_ref, smt_ref, wc_ref, alog_ref, dtb_ref, alogc_ref, dtbc_ref,
     nw_ref) = ins
    has_state = s0_ref is not None
    n = pl.program_id(1)
    masked = t_valid < chunk * nchunks
    pad = SUBLANES
    units = [(s, h) for s in range(nseq) for h in range(GDN_HEADS)]
    nu = range(len(units))

    @pl.when(n == 0)
    def _():
        for u, (s, h) in enumerate(units):
            s_scr[u] = s0_ref[s, h] if has_state else jnp.zeros((GDN_DK, GDN_DV), F32)
        for s in range(nseq):
            xp_scr[s, 0:pad, :] = jnp.zeros((pad, GDN_CONV_DIM), F32)
            if has_state:
                xp_scr[s, pad - (GDN_CONV - 1):pad, :] = c0_ref[s]

    row = lax.broadcasted_iota(jnp.int32, (chunk, chunk), 0)
    col = lax.broadcasted_iota(jnp.int32, (chunk, chunk), 1)
    valid_c = (lax.broadcasted_iota(jnp.int32, (chunk, 1), 0) + n * chunk) < t_valid
    valid_r = (lax.broadcasted_iota(jnp.int32, (1, chunk), 1) + n * chunk) < t_valid
    tri_c = (row >= col).astype(F32)
    tri_r = (row <= col).astype(F32)
    b_col, b_row, beta_col = [], [], []
    for s in range(nseq):
        for c, ref in enumerate((cq_ref, ck_ref, cv_ref)):
            xp_scr[s, pad:pad + chunk, c * PROJ_PART:(c + 1) * PROJ_PART] = ref[s].astype(F32)
        g_col = -jnp.exp(alog_ref[...]) * _softplus(sm_ref[s, :, SMALL_A:SMALL_A + GDN_HEADS] + dtb_ref[...])
        bt = _sigmoid(sm_ref[s, :, SMALL_BETA:SMALL_BETA + GDN_HEADS])
        g_row = -jnp.exp(alogc_ref[...]) * _softplus(smt_ref[s, SMALL_A:SMALL_A + GDN_HEADS, :] + dtbc_ref[...])
        if masked:
            g_col = jnp.where(valid_c, g_col, 0.0)
            bt = jnp.where(valid_c, bt, 0.0)
            g_row = jnp.where(valid_r, g_row, 0.0)
        beta_col.append(bt)
        b_col.append(jnp.dot(tri_c, g_col, precision=HIGHEST, preferred_element_type=F32))
        b_row.append(jnp.dot(g_row, tri_r, precision=HIGHEST, preferred_element_type=F32))
    yield

    def conv_silu(s, c0):
        cs = slice(c0, c0 + LANES)
        y = xp_scr[s, pad:pad + chunk, cs] * wc_ref[GDN_CONV - 1:GDN_CONV, cs]
        for j in range(1, GDN_CONV):
            y = y + xp_scr[s, pad - j:pad - j + chunk, cs] * wc_ref[GDN_CONV - 1 - j:GDN_CONV - j, cs]
        return _silu(y)

    bdot = lambda a, b: jnp.dot(a.astype(BF16), b.astype(BF16), preferred_element_type=F32)
    q, k, v, kb, dec, bc, beta = [], [], [], [], [], [], []
    for u, (s, h) in enumerate(units):
        qh = conv_silu(s, h * GDN_DK)
        kh = conv_silu(s, GDN_QK + h * GDN_DK)
        v.append(conv_silu(s, 2 * GDN_QK + h * GDN_DV))
        q.append(qh * lax.rsqrt(jnp.sum(qh * qh, axis=-1, keepdims=True) + L2_EPS) * (GDN_DK ** -0.5))
        k.append(kh * lax.rsqrt(jnp.sum(kh * kh, axis=-1, keepdims=True) + L2_EPS))
        bc.append(b_col[s][:, h:h + 1])
        beta.append(beta_col[s][:, h:h + 1])
        dec.append(jnp.where(row >= col, jnp.exp(jnp.minimum(bc[u] - b_row[s][h:h + 1, :], 0.0)), 0.0))
        kb.append(k[u] * beta[u])
        if u % GDN_HEADS == GDN_HEADS - 1:
            yield
    kq = [lax.dot_general(jnp.concatenate([kb[u], q[u]], axis=0).astype(BF16), k[u].astype(BF16), NT_DIMS,
                          preferred_element_type=F32) for u in nu]
    yield
    blk = min(GDN_INV_BLOCK, chunk)
    nblk = chunk // blk
    same_blk = (row // blk) == (col // blk)
    lmat = [jnp.where(row > col, kq[u][:chunk] * dec[u], 0.0) for u in nu]
    p = [jnp.where(same_blk, -lmat[u], 0.0) for u in nu]
    r = p
    for _ in range(int(math.log2(blk)) - 1):
        p = [bdot(p[u], p[u]) for u in nu]
        r = [r[u] + p[u] + bdot(r[u], p[u]) for u in nu]
        yield
    if nblk > 1:
        lo = [jnp.where(same_blk, 0.0, lmat[u]) for u in nu]
        p = [-(lo[u] + bdot(r[u], lo[u])) for u in nu]
        qm = p
        yield
        for _ in range(int(math.log2(nblk)) - 1):
            p = [bdot(p[u], p[u]) for u in nu]
            qm = [qm[u] + p[u] + bdot(qm[u], p[u]) for u in nu]
            yield
        r = [r[u] + qm[u] + bdot(qm[u], r[u]) for u in nu]
    rhs = [jnp.concatenate([v[u] * beta[u], kb[u] * jnp.exp(bc[u])], axis=-1) for u in nu]
    uw = [rhs[u] + bdot(r[u], rhs[u]) for u in nu]
    yield
    st = [s_scr[u] for u in nu]
    ws = [bdot(jnp.concatenate([uw[u][:, GDN_DV:], q[u] * jnp.exp(bc[u])], axis=0), st[u]) for u in nu]
    v_new = [uw[u][:, :GDN_DV] - ws[u][:chunk] for u in nu]
    yield
    o = [ws[u][chunk:] + bdot(kq[u][chunk:] * dec[u], v_new[u]) for u in nu]
    for u in nu:
        bl = bc[u][chunk - 1:chunk, :]
        s_scr[u] = jnp.exp(bl) * st[u] + lax.dot_general((k[u] * jnp.exp(bl - bc[u])).astype(BF16),
                                                         v_new[u].astype(BF16), TN_DIMS, preferred_element_type=F32)
    yield
    for u, (s, h) in enumerate(units):
        sl = slice(h * GDN_DV, (h + 1) * GDN_DV)
        out[s, h] = (_rms(o[u], nw_ref[...]) * _silu(zb_ref[s, :, sl].astype(F32))
                     * _sigmoid(gt_ref[s, :, sl].astype(F32)))

    @pl.when(n == nchunks - 1)
    def _():
        last = t_valid - (nchunks - 1) * chunk
        for s in range(nseq):
            cout_ref[s] = xp_scr[s, pad + last - (GDN_CONV - 1):pad + last, :]
        for u, (s, h) in enumerate(units):
            sout_ref[s, h] = s_scr[u]

    for s in range(nseq):
        xp_scr[s, 0:pad, :] = xp_scr[s, chunk:chunk + pad, :]


N_GLA_IN, N_GDN_IN = 8, 13


def _mixer_kernel(*refs, chunk, sub, nchunks, nseq, has_state, t_valid):
    gla_in, refs = refs[:N_GLA_IN], refs[N_GLA_IN:]
    gdn_in, refs = refs[:N_GDN_IN], refs[N_GDN_IN:]
    if has_state:
        (sa0_ref, sb0_ref, c0_ref), refs = refs[:3], refs[3:]
    else:
        sa0_ref = sb0_ref = c0_ref = None
    o_ref, sa_out, sb_out, c_out, sa_scr, sb_scr, xp_scr = refs
    out_a, out_b = {}, {}
    gla = _gla_stages(gla_in, sa0_ref, sa_out, sa_scr, out_a,
                      chunk=chunk, sub=sub, nchunks=nchunks, nseq=nseq, t_valid=t_valid)
    gdn = _gdn_stages(gdn_in, sb0_ref, c0_ref, sb_out, c_out, sb_scr, xp_scr, out_b,
                      chunk=chunk, nchunks=nchunks, nseq=nseq, t_valid=t_valid)
    live = [gdn, gla]
    while live:
        for g in list(live):
            if next(g, StopIteration) is StopIteration:
                live.remove(g)
    per = GLA_DV // GDN_DV
    for (s, h), ob in out_b.items():
        oa = out_a[s, h // per][:, (h % per) * GDN_DV:(h % per + 1) * GDN_DV]
        o_ref[s, :, h * GDN_DV:(h + 1) * GDN_DV] = (oa + ob).astype(o_ref.dtype)


def _mixer(proj3, small3, small_t4, w, s_gla, s_gdn, s_conv, *, chunk, sub, nseq, t_valid, out_dtype):
    nb, t_pad, _ = proj3.shape
    nchunks = t_pad // chunk
    has_state = s_gla is not None
    blk = lambda width, c: pl.BlockSpec((nseq, chunk, width), lambda b, n, c=c: (b, n, c))
    col = lambda c: blk(PROJ_PART, c)
    full = lambda a: pl.BlockSpec(a.shape, lambda b, n: (0,) * a.ndim)
    alog_r, dtb_r = w["a_log"].reshape(1, GDN_HEADS), w["dt_bias"].reshape(1, GDN_HEADS)
    alog_c, dtb_c = w["a_log"].reshape(GDN_HEADS, 1), w["dt_bias"].reshape(GDN_HEADS, 1)
    small_spec = blk(LANES, 0)
    gla_consts = [w["wa2"], w["ba"], w["gla_norm"]]
    gdn_consts = [w["w_conv"], alog_r, dtb_r, alog_c, dtb_c, w["gdn_norm"]]
    in_specs = ([col(COL_QK), col(COL_V), col(COL_GA), col(COL_GATE_A), small_spec] + [full(a) for a in gla_consts]
                + [col(COL_CQ), col(COL_CK), col(COL_CV), col(COL_ZB), col(COL_GATE_B), small_spec,
                   pl.BlockSpec((nseq, None, SMALL_ROWS, chunk), lambda b, n: (b, n, 0, 0))]
                + [full(a) for a in gdn_consts])
    args = [proj3] * 4 + [small3] + gla_consts + [proj3] * 5 + [small3, small_t4] + gdn_consts
    assert len(in_specs) == N_GLA_IN + N_GDN_IN
    sa_spec = pl.BlockSpec((nseq, GLA_HEADS, GLA_DK, GLA_DV), lambda b, n: (b, 0, 0, 0))
    sb_spec = pl.BlockSpec((nseq, GDN_HEADS, GDN_DK, GDN_DV), lambda b, n: (b, 0, 0, 0))
    conv_spec = pl.BlockSpec((nseq, GDN_CONV - 1, GDN_CONV_DIM), lambda b, n: (b, 0, 0))
    if has_state:
        in_specs += [sa_spec, sb_spec, conv_spec]
        args += [s_gla, s_gdn, s_conv]
    return pl.pallas_call(
        functools.partial(_mixer_kernel, chunk=chunk, sub=sub, nchunks=nchunks, nseq=nseq, has_state=has_state,
                          t_valid=t_valid),
        out_shape=(jax.ShapeDtypeStruct((nb, t_pad, D_MODEL), out_dtype),
                   jax.ShapeDtypeStruct((nb, GLA_HEADS, GLA_DK, GLA_DV), F32),
                   jax.ShapeDtypeStruct((nb, GDN_HEADS, GDN_DK, GDN_DV), F32),
                   jax.ShapeDtypeStruct((nb, GDN_CONV - 1, GDN_CONV_DIM), F32)),
        grid=(nb // nseq, nchunks),
        in_specs=in_specs,
        out_specs=(blk(D_MODEL, 0), sa_spec, sb_spec, conv_spec),
        scratch_shapes=[pltpu.VMEM((nseq * GLA_HEADS, GLA_DV, GLA_DK), F32),
                        pltpu.VMEM((nseq * GDN_HEADS, GDN_DK, GDN_DV), F32),
                        pltpu.VMEM((nseq, chunk + 2 * SUBLANES, GDN_CONV_DIM), F32)],
        compiler_params=_params("parallel", "arbitrary"),
        name="mixer",
    )(*args)


def _post_kernel(mg_ref, x_ref, gate_ref, shift_ref, scale_ref, wo_ref, nw_ref, wrt_ref, brt_ref,
                 x1_ref, hx_ref, gid_ref):
    y = jnp.dot(mg_ref[...].astype(BF16), wo_ref[...], preferred_element_type=F32)
    rows = x_ref.shape[0]
    x1 = x_ref[...] + _mod(gate_ref, rows) * y
    x1_ref[...] = x1
    h2 = _rms(x1, nw_ref[...]) * (1.0 + _mod(scale_ref, rows)) + _mod(shift_ref, rows)
    hx_ref[:, 0:D_MODEL] = h2
    lt = lax.dot_general(wrt_ref[...], h2.astype(BF16), NT_DIMS, preferred_element_type=F32) + brt_ref[...]
    tm = lt.shape[1]
    gl = lt[0:N_GROUPS]
    gidx = lax.broadcasted_iota(jnp.int32, (N_GROUPS, tm), 0)
    gmax = jnp.max(gl, axis=0, keepdims=True)
    g_w = 1.0 / jnp.sum(jnp.exp(gl - gmax), axis=0, keepdims=True)
    g_sel = jnp.min(jnp.where(gl == gmax, gidx, N_GROUPS), axis=0, keepdims=True)
    el = jnp.zeros((EXPERTS_PER_GROUP, tm), F32)
    for g in range(N_GROUPS):
        r = ROUTER_GROUP_STRIDE * (1 + g)
        el = el + jnp.where(g_sel == g, lt[r:r + EXPERTS_PER_GROUP], 0.0)
    eidx = lax.broadcasted_iota(jnp.int32, (EXPERTS_PER_GROUP, tm), 0)
    m1 = jnp.max(el, axis=0, keepdims=True)
    i1 = jnp.min(jnp.where(el == m1, eidx, EXPERTS_PER_GROUP), axis=0, keepdims=True)
    el2 = jnp.where(eidx == i1, -jnp.inf, el)
    m2 = jnp.max(el2, axis=0, keepdims=True)
    i2 = jnp.min(jnp.where(el2 == m2, eidx, EXPERTS_PER_GROUP), axis=0, keepdims=True)
    r21 = jnp.exp(m2 - m1)
    w1 = 1.0 / (1.0 + r21)
    w2 = r21 / (1.0 + r21)
    comb_t = g_w * (jnp.where(eidx == i1, w1, 0.0) + jnp.where(eidx == i2, w2, 0.0))
    ident = (lax.broadcasted_iota(jnp.int32, (EXPERTS_PER_GROUP, LANES), 0)
             == lax.broadcasted_iota(jnp.int32, (EXPERTS_PER_GROUP, LANES), 1)).astype(F32)
    hx_ref[:, D_MODEL:] = lax.dot_general(comb_t, ident, TN_DIMS, precision=HIGHEST, preferred_element_type=F32)
    gid_ref[...] = jnp.broadcast_to(g_sel, gid_ref.shape)


def _post(merged, x2d, gate, shift, scale, mod_specs, wo, nw, wrt, brt, tm):
    m, d = x2d.shape
    row = lambda: pl.BlockSpec((tm, d), lambda i: (i, 0))
    full = lambda a: pl.BlockSpec(a.shape, lambda i: (0,) * a.ndim)
    return pl.pallas_call(
        _post_kernel,
        out_shape=(jax.ShapeDtypeStruct((m, d), F32),
                   jax.ShapeDtypeStruct((m, HX_WIDTH), F32),
                   jax.ShapeDtypeStruct((SUBLANES, m), jnp.int32)),
        grid=(m // tm,),
        in_specs=[row(), row(), mod_specs[0], mod_specs[1], mod_specs[2],
                  full(wo), full(nw), full(wrt), full(brt)],
        out_specs=(row(), pl.BlockSpec((tm, HX_WIDTH), lambda i: (i, 0)),
                   pl.BlockSpec((SUBLANES, tm), lambda i: (0, i))),
        compiler_params=_params("parallel"),
        name="post_mixer",
    )(merged, x2d, gate, shift, scale, wo, nw, wrt, brt)


def _invert_kernel(pos_ref, src_ref):
    def clear(i, carry):
        src_ref[i] = 0
        return carry

    def place(t, carry):
        src_ref[pos_ref[t]] = t
        return carry

    lax.fori_loop(0, src_ref.shape[0], clear, 0, unroll=SCALAR_UNROLL)
    lax.fori_loop(0, pos_ref.shape[0], place, 0, unroll=SCALAR_UNROLL)


def _invert_rows(pos, rows):
    smem = pl.BlockSpec(memory_space=pltpu.SMEM)
    return pl.pallas_call(
        _invert_kernel,
        out_shape=jax.ShapeDtypeStruct((rows,), jnp.int32),
        in_specs=[smem],
        out_specs=smem,
        name="invert_rows",
    )(pos)


def _route(cls, ncls, tm):
    m = cls.shape[0]
    ntiles_max = m // tm + ncls - 1
    onehot = (cls[:, None] == jnp.arange(ncls, dtype=jnp.int32)[None, :]).astype(jnp.int32)
    incl = jnp.cumsum(onehot, axis=0)
    tiles = (incl[-1] + tm - 1) // tm
    tile_end = jnp.cumsum(tiles)
    pos = jnp.sum(onehot * ((tile_end - tiles) * tm + incl - onehot), axis=1)
    src = _invert_rows(pos, ntiles_max * tm)
    t = jnp.arange(ntiles_max, dtype=jnp.int32)
    tile_class = jnp.minimum(jnp.sum((t[:, None] >= tile_end[None, :]).astype(jnp.int32), axis=1), ncls - 1)
    return pos, src, tile_class, tile_end[-1:]


def _row_gather(idx_ref, src_hbm, buf, sem, rows, first=0):
    for r in range(first, first + rows):
        pltpu.make_async_copy(src_hbm.at[pl.ds(idx_ref[0, r], 1)], buf.at[pl.ds(r, 1)],
                              sem).start(priority=r % 2)


def _row_gather_wait(src_hbm, buf, sem):
    pltpu.make_async_copy(src_hbm.at[pl.ds(0, buf.shape[0])], buf, sem).wait()


def _experts_kernel(meta_ref, nt_ref, src_ref, srcn_ref, hx_hbm, *rest, tm, nslot):
    w_refs, (y_ref, xbuf, sem) = rest[:3 * nslot], rest[3 * nslot:]
    t = pl.program_id(0)
    nt = nt_ref[0]
    slot = t % 2

    @pl.when(t == 0)
    def _():
        _row_gather(src_ref, hx_hbm, xbuf.at[0], sem.at[0], tm)

    @pl.when(t < nt)
    def _():
        cur, nxt, sem_cur, sem_nxt = xbuf.at[slot], xbuf.at[1 - slot], sem.at[slot], sem.at[1 - slot]
        _row_gather_wait(hx_hbm, cur, sem_cur)
        x = cur[:, 0:D_MODEL].astype(BF16)
        acc = jnp.zeros((tm, D_MODEL), F32)
        for j in range(nslot):
            wg_ref, wu_ref, wd_ref = w_refs[3 * j:3 * j + 3]
            e = meta_ref[t * (1 + nslot) + 1 + j]
            share = tm // (3 * nslot)
            a = jnp.dot(x, wg_ref[0, 0], preferred_element_type=F32)
            _row_gather(srcn_ref, hx_hbm, nxt, sem_nxt, share, first=(3 * j) * share)
            u = jnp.dot(x, wu_ref[0, 0], preferred_element_type=F32)
            _row_gather(srcn_ref, hx_hbm, nxt, sem_nxt, share, first=(3 * j + 1) * share)
            cw = jnp.zeros((tm, 1), F32)
            for c in range(EXPERTS_PER_GROUP):
                cw = cw + jnp.where(e == c, cur[:, D_MODEL + c:D_MODEL + c + 1], 0.0)
            acc = acc + jnp.dot((_silu(a) * u * cw).astype(BF16), wd_ref[0, 0], preferred_element_type=F32)
            last = tm - (3 * nslot - 1) * share if j == nslot - 1 else share
            _row_gather(srcn_ref, hx_hbm, nxt, sem_nxt, last, first=(3 * j + 2) * share)
        y_ref[:, 0, :] = acc

        @pl.when(t == nt - 1)
        def _():
            _row_gather_wait(hx_hbm, nxt, sem_nxt)

    @pl.when(t >= nt)
    def _():
        y_ref[...] = jnp.zeros_like(y_ref)


def _experts(hx, src, tile_class, ntiles, class_table, wg, wu, wd, tm):
    ntiles_max = tile_class.shape[0]
    nslot = class_table.shape[1] - 1
    d, f = D_MODEL, D_EXPERT
    src3 = src.reshape(ntiles_max, 1, tm)
    meta = jnp.asarray(class_table, jnp.int32)[tile_class].reshape(-1)
    stride = 1 + nslot
    wspec = lambda j, shape: pl.BlockSpec(
        (1, 1) + shape, lambda t, meta, nt, j=j: (meta[t * stride], meta[t * stride + 1 + j], 0, 0))
    w_specs, w_args = [], []
    for j in range(nslot):
        w_specs += [wspec(j, (d, f)), wspec(j, (d, f)), wspec(j, (f, d))]
        w_args += [wg, wu, wd]
    return pl.pallas_call(
        functools.partial(_experts_kernel, tm=tm, nslot=nslot),
        out_shape=jax.ShapeDtypeStruct((ntiles_max * tm, 1, d), F32),
        grid_spec=pltpu.PrefetchScalarGridSpec(
            num_scalar_prefetch=2,
            grid=(ntiles_max,),
            in_specs=[pl.BlockSpec((None, 1, tm), lambda t, meta, nt: (t, 0, 0), memory_space=pltpu.SMEM),
                      pl.BlockSpec((None, 1, tm),
                                   lambda t, meta, nt: (jnp.maximum(jnp.minimum(t + 1, nt[0] - 1), 0), 0, 0),
                                   memory_space=pltpu.SMEM),
                      pl.BlockSpec(memory_space=pl.ANY)] + w_specs,
            out_specs=pl.BlockSpec((tm, 1, d), lambda t, meta, nt: (t, 0, 0)),
            scratch_shapes=[pltpu.VMEM((2, tm, HX_WIDTH), F32), pltpu.SemaphoreType.DMA((2,))]),
        compiler_params=_params("arbitrary"),
        name="experts",
    )(meta, ntiles, src3, src3, hx, *w_args)


def _final_kernel(pos_ref, posn_ref, ys_hbm, x1_ref, gate_ref, nw_ref, o_ref, ybuf, sem, *, tm, nsteps):
    i = pl.program_id(0)
    slot = i % 2

    @pl.when(i == 0)
    def _():
        _row_gather(pos_ref, ys_hbm, ybuf.at[0], sem.at[0], tm)

    _row_gather(posn_ref, ys_hbm, ybuf.at[1 - slot], sem.at[1 - slot], tm)
    _row_gather_wait(ys_hbm, ybuf.at[slot], sem.at[slot])
    o_ref[...] = _rms(x1_ref[...] + _mod(gate_ref, tm) * ybuf[slot, :, 0, :], nw_ref[...])

    @pl.when(i == nsteps - 1)
    def _():
        _row_gather_wait(ys_hbm, ybuf.at[1 - slot], sem.at[1 - slot])


def _final(ys, pos, x1, gate, gate_spec, nw, tm):
    m, d = x1.shape
    nsteps = m // tm
    pos3 = pos.reshape(nsteps, 1, tm)
    return pl.pallas_call(
        functools.partial(_final_kernel, tm=tm, nsteps=nsteps),
        out_shape=jax.ShapeDtypeStruct((m, d), F32),
        grid=(nsteps,),
        in_specs=[pl.BlockSpec((None, 1, tm), lambda i: (i, 0, 0), memory_space=pltpu.SMEM),
                  pl.BlockSpec((None, 1, tm), lambda i: (jnp.minimum(i + 1, nsteps - 1), 0, 0),
                               memory_space=pltpu.SMEM),
                  pl.BlockSpec(memory_space=pl.ANY),
                  pl.BlockSpec((tm, d), lambda i: (i, 0)),
                  gate_spec,
                  pl.BlockSpec((1, d), lambda i: (0, 0))],
        out_specs=pl.BlockSpec((tm, d), lambda i: (i, 0)),
        scratch_shapes=[pltpu.VMEM((2, tm, 1, d), F32), pltpu.SemaphoreType.DMA((2,))],
        compiler_params=_params("arbitrary"),
        name="final",
    )(pos3, pos3, ys, x1, gate, nw)


def _mod_spec(idx, tm, t, ngrid):
    if t % tm == 0:
        per = t // tm
        shape, index = (None, None, 1, D_MODEL), lambda i: (i // per, idx, 0, 0)
    else:
        shape, index = (tm // t, None, 1, D_MODEL), lambda i: (i, idx, 0, 0)
    if ngrid == 2:
        return pl.BlockSpec(shape, lambda i, j: index(i))
    return pl.BlockSpec(shape, index)


def _trunk(x, mod, s_gla, s_gdn, s_conv, w, *, chunk, sub, nseq, t_valid, tm, tm_moe, act_dtype):
    nb, t_pad, d = x.shape
    m = nb * t_pad
    x2d = x.reshape(m, d)
    mods = [mod.reshape(nb, N_MOD, 1, d)] * N_MOD
    spec = lambda idx, ngrid: _mod_spec(idx, tm, t_pad, ngrid)

    proj, small, small_t = _inproj(x2d, mods[0], mods[1], (spec(0, 2), spec(1, 2)), w["norm1"],
                                   w["w_main"], w["w_small"], w["w_small_t"], tm, act_dtype)
    proj3 = proj.reshape(nb, t_pad, PROJ_MAIN)
    small3 = small.reshape(nb, t_pad, LANES)
    small_t4 = small_t.reshape(SMALL_ROWS, nb, t_pad // chunk, chunk).transpose(1, 2, 0, 3)
    merged, new_gla, new_gdn, new_conv = _mixer(proj3, small3, small_t4, w, s_gla, s_gdn, s_conv, chunk=chunk,
                                                sub=sub, nseq=nseq, t_valid=t_valid, out_dtype=act_dtype)
    if t_valid < t_pad:
        merged, x = merged[:, :t_valid], x[:, :t_valid]
    m = nb * t_valid
    tm = min(tm, m)
    tm_moe = min(tm_moe, m)
    spec = lambda idx, ngrid: _mod_spec(idx, tm, t_valid, ngrid)
    x1, hx, gid = _post(merged.reshape(m, d), x.reshape(m, d), mods[2], mods[3], mods[4],
                        (spec(2, 1), spec(3, 1), spec(4, 1)),
                        w["w_out"], w["norm2"], w["w_router_t"], w["b_router_t"], tm)
    class_table = GROUP_CLASSES
    pos, src, tile_class, ntiles = _route(gid[0], class_table.shape[0], tm_moe)
    ys = _experts(hx, src, tile_class, ntiles, class_table, w["w_gate"], w["w_up"], w["w_down"], tm_moe)
    tm_fin = min(tm, FINAL_TILE)
    y = _final(ys, pos, x1, mods[5], _mod_spec(5, tm_fin, t_valid, 1), w["final_norm"], tm_fin)
    return y.reshape(nb, t_valid, d), new_gla, new_gdn, new_conv


def _regroup_kernel(q_ref, r_ref, a_ref, b_ref, o_ref, *, shifts):
    window = jnp.concatenate([a_ref[...], b_ref[...]], axis=0)
    r = r_ref[pl.program_id(0)]
    for s in shifts:
        @pl.when(r == s)
        def _():
            o_ref[...] = window[s:s + PROJ_PART, :].T.astype(o_ref.dtype)


def _regroup_columns(w, starts):
    d = w.shape[1]
    assert d == PROJ_PART and all(s % SUBLANES == 0 for s in starts)
    w = jnp.swapaxes(w, 1, 2)
    q = np.array([s // PROJ_PART for s in starts], np.int32)
    r = np.array([s % PROJ_PART for s in starts], np.int32)
    assert r.max() <= LANES
    per = PROJ_PART // LANES
    return pl.pallas_call(
        functools.partial(_regroup_kernel, shifts=tuple(sorted(set(r.tolist())))),
        out_shape=jax.ShapeDtypeStruct((d, len(starts) * PROJ_PART), BF16),
        grid_spec=pltpu.PrefetchScalarGridSpec(
            num_scalar_prefetch=2,
            grid=(len(starts),),
            in_specs=[pl.BlockSpec((None, PROJ_PART, d), lambda i, q, r: (0, q[i], 0)),
                      pl.BlockSpec((None, LANES, d), lambda i, q, r: (0, per * (q[i] + 1), 0))],
            out_specs=pl.BlockSpec((d, PROJ_PART), lambda i, q, r: (0, i))),
        compiler_params=_params("arbitrary"),
        name="regroup_w_in",
    )(jnp.asarray(q), jnp.asarray(r), w, w)


def _prep_weights(w_in, w_gla_a2, b_gla_a, gla_norm_w, w_conv, gdn_A_log, gdn_dt_bias, gdn_norm_w, w_out,
                  norm1_w, norm2_w, w_group_router, b_group_router, w_expert_router, b_expert_router,
                  w_exp_gate, w_exp_up, w_exp_down, final_norm_w):
    d = D_MODEL
    o = 0
    start = {}
    for name, width in (("gla", 2 * GLA_QK + 2 * GLA_VW), ("ra", GLA_GATE_RANK), ("qkv", GDN_CONV_DIM),
                        ("zb", GDN_VW), ("beta", GDN_HEADS), ("a", GDN_HEADS), ("gates", 2 * D_MODEL)):
        start[name] = o
        o += width
    gate_starts = [start["zb"], start["gates"], start["gates"] + PROJ_PART]
    main_starts = [s for j in range(3) for s in (start["gla"] + j * PROJ_PART, start["qkv"] + j * PROJ_PART,
                                                 gate_starts[j])]
    assert start["a"] == start["beta"] + GDN_HEADS
    w_main = _regroup_columns(w_in, main_starts + [start["ra"], start["beta"]])
    small = jnp.concatenate([w_main[:, PROJ_MAIN:PROJ_MAIN + GLA_GATE_RANK],
                             w_main[:, PROJ_MAIN + PROJ_PART:PROJ_MAIN + PROJ_PART + 2 * GDN_HEADS]], axis=1)
    w_small = jnp.pad(small, ((0, 0), (0, LANES - small.shape[1])))
    w_small_t = small.T
    assert N_GROUPS == EXPERTS_PER_GROUP
    stride_pad = ((0, 0), (0, 0), (0, ROUTER_GROUP_STRIDE - EXPERTS_PER_GROUP))

    def router_rows(group_part, expert_part):
        both = jnp.concatenate([group_part, expert_part], axis=1).reshape(-1, 1 + N_GROUPS, EXPERTS_PER_GROUP)
        return jnp.pad(both, stride_pad).reshape(-1, ROUTER_ROWS)

    wr_t = router_rows(w_group_router, w_expert_router).T
    br_t = router_rows(b_group_router[None], b_expert_router[None]).T
    return dict(
        w_main=w_main, w_small=w_small, w_small_t=w_small_t,
        norm1=norm1_w.reshape(1, d), norm2=norm2_w.reshape(1, d), final_norm=final_norm_w.reshape(1, d),
        wa2=w_gla_a2, ba=b_gla_a.reshape(1, GLA_QK), gla_norm=gla_norm_w.reshape(1, GLA_DV),
        w_conv=w_conv, a_log=gdn_A_log, dt_bias=gdn_dt_bias, gdn_norm=gdn_norm_w.reshape(1, GDN_DV),
        w_out=w_out.astype(BF16), w_router_t=wr_t.astype(BF16), b_router_t=br_t,
        w_gate=w_exp_gate.astype(BF16), w_up=w_exp_up.astype(BF16), w_down=w_exp_down.astype(BF16),
    )


def kernel(x_prompt, x_sample, c_prompt, c_sample, state_gla, state_gdn, state_conv, w_ada, b_ada, norm1_w, w_in, w_gla_a2, b_gla_a, gla_norm_w, w_conv, gdn_A_log, gdn_dt_bias, gdn_norm_w, w_out, norm2_w, w_group_router, b_group_router, w_expert_router, b_expert_router, w_exp_gate, w_exp_up, w_exp_down, final_norm_w):
    assert w_ada.shape[0] == 1, "single layer"
    bp, tp, d = x_prompt.shape
    bs, ts, _ = x_sample.shape
    w = _prep_weights(w_in, w_gla_a2[0], b_gla_a[0], gla_norm_w[0], w_conv[0], gdn_A_log[0], gdn_dt_bias[0],
                      gdn_norm_w[0], w_out[0], norm1_w[0], norm2_w[0], w_group_router[0], b_group_router[0],
                      w_expert_router[0], b_expert_router[0], w_exp_gate[0], w_exp_up[0], w_exp_down[0],
                      final_norm_w)
    mod = _ada_mod(jnp.concatenate([c_prompt, c_sample], axis=0), w_ada[0], b_ada[0]).reshape(bp + bs, N_MOD, d)

    y_p, gla_p, gdn_p, conv_p = _trunk(x_prompt, mod[:bp], None, None, None, w,
                                       chunk=64, sub=GLA_SUBCHUNK, nseq=4, t_valid=tp, tm=min(1024, tp),
                                       tm_moe=min(512, tp), act_dtype=BF16)
    ts_pad = SUBLANES
    xs = jnp.pad(x_sample, ((0, 0), (0, ts_pad - ts), (0, 0)))
    y_s, gla_s, gdn_s, conv_s = _trunk(xs, mod[bp:], state_gla[0], state_gdn[0], state_conv[0], w,
                                       chunk=ts_pad, sub=ts_pad, nseq=4, t_valid=ts, tm=min(512, bs * ts_pad),
                                       tm_moe=128, act_dtype=F32)
    return (y_p, y_s, gla_p[None], gdn_p[None], conv_p[None], gla_s[None], gdn_s[None], conv_s[None])
```

```python
import functools
import math

import jax
import numpy as np
import jax.numpy as jnp
from jax import lax
from jax.experimental import pallas as pl
from jax.experimental.pallas import tpu as pltpu

F32 = jnp.float32
BF16 = jnp.bfloat16
HIGHEST = lax.Precision.HIGHEST

D_MODEL = 1024
GLA_HEADS = 4
GLA_DK = 128
GLA_DV = 256
GLA_QK = GLA_HEADS * GLA_DK
GLA_VW = GLA_HEADS * GLA_DV
GLA_GATE_RANK = 16
GLA_TAU = 16.0
GLA_SUBCHUNK = 16
GDN_HEADS = 8
GDN_DK = 128
GDN_DV = 128
GDN_QK = GDN_HEADS * GDN_DK
GDN_VW = GDN_HEADS * GDN_DV
GDN_CONV = 4
GDN_CONV_DIM = 2 * GDN_QK + GDN_VW
GDN_INV_BLOCK = 16
N_GROUPS = 4
EXPERTS_PER_GROUP = 4
N_EXPERTS = N_GROUPS * EXPERTS_PER_GROUP
D_EXPERT = D_MODEL // 2
N_MOD = 6
NORM_EPS = 1e-6
L2_EPS = 1e-6

LANES = 128
SUBLANES = 8
VMEM_LIMIT = 56 * 1024 * 1024

PROJ_MAIN = 2 * GLA_QK + 2 * GLA_VW + GDN_CONV_DIM + GDN_VW + 2 * D_MODEL
PROJ_PART = 1024
INPROJ_TN = 3 * PROJ_PART
COL_QK, COL_V, COL_GA = 0, 3, 6
COL_CQ, COL_CK, COL_CV = 1, 4, 7
COL_ZB, COL_GATE_A, COL_GATE_B = 2, 5, 8
SMALL_RA, SMALL_BETA, SMALL_A = 0, GLA_GATE_RANK, GLA_GATE_RANK + GDN_HEADS
SMALL_ROWS = 32
ROUTER_GROUP_STRIDE = 8
ROUTER_ROWS = ROUTER_GROUP_STRIDE * (1 + N_GROUPS)

GROUP_CLASSES = np.array([[g] + list(range(EXPERTS_PER_GROUP)) for g in range(N_GROUPS)], np.int32)

HX_WIDTH = D_MODEL + LANES
SCALAR_UNROLL = 16
FINAL_TILE = 1024

NT_DIMS = (((1,), (1,)), ((), ()))
TN_DIMS = (((0,), (0,)), ((), ()))


def _sigmoid(x):
    return 0.5 * jnp.tanh(0.5 * x) + 0.5


def _silu(x):
    return x * _sigmoid(x)


def _softplus(x):
    return jnp.maximum(x, 0.0) + jnp.log1p(jnp.exp(-jnp.abs(x)))


def _rms(x, w):
    return x * lax.rsqrt(jnp.mean(x * x, axis=-1, keepdims=True) + NORM_EPS) * w


def _mod(ref, rows):
    v = ref[...]
    if v.ndim == 3:
        v = jnp.broadcast_to(v, (v.shape[0], rows // v.shape[0], v.shape[2])).reshape(rows, v.shape[2])
    return v


def _params(*sem):
    return pltpu.CompilerParams(dimension_semantics=sem, vmem_limit_bytes=VMEM_LIMIT)


def _ada_kernel(c_ref, w_ref, b_ref, o_ref):
    cs = _silu(c_ref[...])
    o_ref[...] = jnp.dot(cs.astype(BF16), w_ref[...].astype(BF16), preferred_element_type=F32) + b_ref[...]


def _ada_mod(c_all, w_ada, b_ada):
    rows, d = c_all.shape
    n = w_ada.shape[1]
    tn = 1024
    return pl.pallas_call(
        _ada_kernel,
        out_shape=jax.ShapeDtypeStruct((rows, n), F32),
        grid=(n // tn,),
        in_specs=[pl.BlockSpec((rows, d), lambda j: (0, 0)),
                  pl.BlockSpec((d, tn), lambda j: (0, j)),
                  pl.BlockSpec((1, tn), lambda j: (0, j))],
        out_specs=pl.BlockSpec((rows, tn), lambda j: (0, j)),
        compiler_params=_params("arbitrary"),
        name="ada_mod",
    )(c_all, w_ada, b_ada.reshape(1, n))


def _inproj_kernel(x_ref, shift_ref, scale_ref, nw_ref, w_ref, ws_ref, wst_ref,
                   o_ref, os_ref, ost_ref, h_scr):
    @pl.when(pl.program_id(1) == 0)
    def _():
        rows = x_ref.shape[0]
        h = _rms(x_ref[...], nw_ref[...]) * (1.0 + _mod(scale_ref, rows)) + _mod(shift_ref, rows)
        hb = h.astype(BF16)
        h_scr[...] = hb
        os_ref[...] = jnp.dot(hb, ws_ref[...], preferred_element_type=F32)
        ost_ref[...] = lax.dot_general(wst_ref[...], hb, NT_DIMS, preferred_element_type=F32)

    o_ref[...] = jnp.dot(h_scr[...], w_ref[...], preferred_element_type=F32).astype(o_ref.dtype)


def _inproj(x2d, shift, scale, mod_specs, nw, w_main, w_small, w_small_t, tm, out_dtype):
    m, d = x2d.shape
    n = PROJ_MAIN
    tn = INPROJ_TN
    return pl.pallas_call(
        _inproj_kernel,
        out_shape=(jax.ShapeDtypeStruct((m, n), out_dtype),
                   jax.ShapeDtypeStruct((m, LANES), F32),
                   jax.ShapeDtypeStruct((SMALL_ROWS, m), F32)),
        grid=(m // tm, n // tn),
        in_specs=[pl.BlockSpec((tm, d), lambda i, j: (i, 0)),
                  mod_specs[0], mod_specs[1],
                  pl.BlockSpec((1, d), lambda i, j: (0, 0)),
                  pl.BlockSpec((d, tn), lambda i, j: (0, j)),
                  pl.BlockSpec((d, LANES), lambda i, j: (0, 0)),
                  pl.BlockSpec((SMALL_ROWS, d), lambda i, j: (0, 0))],
        out_specs=(pl.BlockSpec((tm, tn), lambda i, j: (i, j)),
                   pl.BlockSpec((tm, LANES), lambda i, j: (i, 0)),
                   pl.BlockSpec((SMALL_ROWS, tm), lambda i, j: (0, i))),
        scratch_shapes=[pltpu.VMEM((tm, d), BF16)],
        compiler_params=_params("parallel", "arbitrary"),
        name="inproj",
    )(x2d, shift, scale, nw, w_main, w_small, w_small_t)


def _gla_stages(ins, s0_ref, sout_ref, st_scr, out, *, chunk, sub, nchunks, nseq, t_valid):
    qk_ref, v_ref, ga_ref, gt_ref, sm_ref, wa2_ref, ba_ref, nw_ref = ins
    has_state = s0_ref is not None
    n = pl.program_id(1)
    masked = t_valid < chunk * nchunks
    units = [(s, h) for s in range(nseq) for h in range(GLA_HEADS)]

    @pl.when(n == 0)
    def _():
        for u, (s, h) in enumerate(units):
            st_scr[u] = s0_ref[s, h].T if has_state else jnp.zeros((GLA_DV, GLA_DK), F32)

    row = lax.broadcasted_iota(jnp.int32, (chunk, chunk), 0)
    col = lax.broadcasted_iota(jnp.int32, (chunk, chunk), 1)
    rowi = lax.broadcasted_iota(jnp.int32, (chunk, 1), 0)
    valid = (rowi + n * chunk) < t_valid
    tri = (row >= col).astype(F32)
    wa2 = wa2_ref[...].astype(BF16)
    b_all = []
    for s in range(nseq):
        ra = sm_ref[s, :, SMALL_RA:SMALL_RA + GLA_GATE_RANK]
        x = jnp.dot(ra.astype(BF16), wa2, preferred_element_type=F32) + ba_ref[...]
        g = (jnp.minimum(x, 0.0) - jnp.log1p(jnp.exp(-jnp.abs(x)))) * (1.0 / GLA_TAU)
        if masked:
            g = jnp.where(valid, g, 0.0)
        b_all.append(jnp.dot(tri, g, precision=HIGHEST, preferred_element_type=F32))
    yield

    q, k, v, b = [], [], [], []
    for s, h in units:
        q.append(qk_ref[s, :, h * GLA_DK:(h + 1) * GLA_DK].astype(F32) * (GLA_DK ** -0.5))
        kh = qk_ref[s, :, GLA_QK + h * GLA_DK:GLA_QK + (h + 1) * GLA_DK].astype(F32)
        k.append(jnp.where(valid, kh, 0.0) if masked else kh)
        v.append(v_ref[s, :, h * GLA_DV:(h + 1) * GLA_DV].astype(BF16))
        b.append(b_all[s][:, h * GLA_DK:(h + 1) * GLA_DK])
    nu = range(len(units))
    st = [st_scr[u] for u in nu]
    o = [lax.dot_general((q[u] * jnp.exp(b[u])).astype(BF16), st[u].astype(BF16), NT_DIMS,
                         preferred_element_type=F32) for u in nu]
    yield
    blocks = [[] for _ in nu]
    for i in range(chunk // sub):
        r0, r1 = i * sub, (i + 1) * sub
        for u in nu:
            bref = b[u][r0 - 1:r0] if i > 0 else jnp.zeros((1, GLA_DK), F32)
            qt = (q[u][r0:r1] * jnp.exp(b[u][r0:r1] - bref)).astype(BF16)
            expo = bref - b[u]
            if r1 < chunk:
                expo = jnp.where(rowi < r1, expo, 0.0)
            kt = (k[u] * jnp.exp(expo)).astype(BF16)
            blocks[u].append(lax.dot_general(qt, kt, NT_DIMS, preferred_element_type=F32))
        yield
    for u in nu:
        a = blocks[u][0] if len(blocks[u]) == 1 else jnp.concatenate(blocks[u], axis=0)
        a = jnp.where(col <= row, a, 0.0)
        o[u] = o[u] + jnp.dot(a.astype(BF16), v[u], preferred_element_type=F32)
    yield
    for u in nu:
        bl = b[u][chunk - 1:chunk]
        kt = (k[u] * jnp.exp(bl - b[u])).astype(BF16)
        st_scr[u] = st[u] * jnp.exp(bl) + lax.dot_general(v[u], kt, TN_DIMS, preferred_element_type=F32)
    yield
    for u, (s, h) in enumerate(units):
        sl = slice(h * GLA_DV, (h + 1) * GLA_DV)
        out[s, h] = (_rms(o[u], nw_ref[...]) * _silu(ga_ref[s, :, sl].astype(F32))
                     * _sigmoid(gt_ref[s, :, sl].astype(F32)))

    @pl.when(n == nchunks - 1)
    def _():
        for u, (s, h) in enumerate(units):
            sout_ref[s, h] = st_scr[u].T


def _gdn_stages(ins, s0_ref, c0_ref, sout_ref, cout_ref, s_scr, xp_scr, out, *, chunk, nchunks, nseq, t_valid):
    (cq_ref, ck_ref, cv_ref, zb_ref, gt_ref, sm_ref, smt_ref, wc_ref, alog_ref, dtb_ref, alogc_ref, dtbc_ref,
     nw_ref) = ins
    has_state = s0_ref is not None
    n = pl.program_id(1)
    masked = t_valid < chunk * nchunks
    pad = SUBLANES
    units = [(s, h) for s in range(nseq) for h in range(GDN_HEADS)]
    nu = range(len(units))

    @pl.when(n == 0)
    def _():
        for u, (s, h) in enumerate(units):
            s_scr[u] = s0_ref[s, h] if has_state else jnp.zeros((GDN_DK, GDN_DV), F32)
        for s in range(nseq):
            xp_scr[s, 0:pad, :] = jnp.zeros((pad, GDN_CONV_DIM), F32)
            if has_state:
                xp_scr[s, pad - (GDN_CONV - 1):pad, :] = c0_ref[s]

    row = lax.broadcasted_iota(jnp.int32, (chunk, chunk), 0)
    col = lax.broadcasted_iota(jnp.int32, (chunk, chunk), 1)
    valid_c = (lax.broadcasted_iota(jnp.int32, (chunk, 1), 0) + n * chunk) < t_valid
    valid_r = (lax.broadcasted_iota(jnp.int32, (1, chunk), 1) + n * chunk) < t_valid
    tri_c = (row >= col).astype(F32)
    tri_r = (row <= col).astype(F32)
    b_col, b_row, beta_col = [], [], []
    for s in range(nseq):
        for c, ref in enumerate((cq_ref, ck_ref, cv_ref)):
            xp_scr[s, pad:pad + chunk, c * PROJ_PART:(c + 1) * PROJ_PART] = ref[s].astype(F32)
        g_col = -jnp.exp(alog_ref[...]) * _softplus(sm_ref[s, :, SMALL_A:SMALL_A + GDN_HEADS] + dtb_ref[...])
        bt = _sigmoid(sm_ref[s, :, SMALL_BETA:SMALL_BETA + GDN_HEADS])
        g_row = -jnp.exp(alogc_ref[...]) * _softplus(smt_ref[s, SMALL_A:SMALL_A + GDN_HEADS, :] + dtbc_ref[...])
        if masked:
            g_col = jnp.where(valid_c, g_col, 0.0)
            bt = jnp.where(valid_c, bt, 0.0)
            g_row = jnp.where(valid_r, g_row, 0.0)
        beta_col.append(bt)
        b_col.append(jnp.dot(tri_c, g_col, precision=HIGHEST, preferred_element_type=F32))
        b_row.append(jnp.dot(g_row, tri_r, precision=HIGHEST, preferred_element_type=F32))
    yield

    def conv_silu(s, c0):
        cs = slice(c0, c0 + LANES)
        y = xp_scr[s, pad:pad + chunk, cs] * wc_ref[GDN_CONV - 1:GDN_CONV, cs]
        for j in range(1, GDN_CONV):
            y = y + xp_scr[s, pad - j:pad - j + chunk, cs] * wc_ref[GDN_CONV - 1 - j:GDN_CONV - j, cs]
        return _silu(y)

    bdot = lambda a, b: jnp.dot(a.astype(BF16), b.astype(BF16), preferred_element_type=F32)
    q, k, v, kb, dec, bc, beta = [], [], [], [], [], [], []
    for u, (s, h) in enumerate(units):
        qh = conv_silu(s, h * GDN_DK)
        kh = conv_silu(s, GDN_QK + h * GDN_DK)
        v.append(conv_silu(s, 2 * GDN_QK + h * GDN_DV))
        q.append(qh * lax.rsqrt(jnp.sum(qh * qh, axis=-1, keepdims=True) + L2_EPS) * (GDN_DK ** -0.5))
        k.append(kh * lax.rsqrt(jnp.sum(kh * kh, axis=-1, keepdims=True) + L2_EPS))
        bc.append(b_col[s][:, h:h + 1])
        beta.append(beta_col[s][:, h:h + 1])
        dec.append(jnp.where(row >= col, jnp.exp(jnp.minimum(bc[u] - b_row[s][h:h + 1, :], 0.0)), 0.0))
        kb.append(k[u] * beta[u])
        if u % GDN_HEADS == GDN_HEADS - 1:
            yield
    kq = [lax.dot_general(jnp.concatenate([kb[u], q[u]], axis=0).astype(BF16), k[u].astype(BF16), NT_DIMS,
                          preferred_element_type=F32) for u in nu]
    yield
    blk = min(GDN_INV_BLOCK, chunk)
    nblk = chunk // blk
    same_blk = (row // blk) == (col // blk)
    lmat = [jnp.where(row > col, kq[u][:chunk] * dec[u], 0.0) for u in nu]
    p = [jnp.where(same_blk, -lmat[u], 0.0) for u in nu]
    r = p
    for _ in range(int(math.log2(blk)) - 1):
        p = [bdot(p[u], p[u]) for u in nu]
        r = [r[u] + p[u] + bdot(r[u], p[u]) for u in nu]
        yield
    if nblk > 1:
        lo = [jnp.where(same_blk, 0.0, lmat[u]) for u in nu]
        p = [-(lo[u] + bdot(r[u], lo[u])) for u in nu]
        qm = p
        yield
        for _ in range(int(math.log2(nblk)) - 1):
            p = [bdot(p[u], p[u]) for u in nu]
            qm = [qm[u] + p[u] + bdot(qm[u], p[u]) for u in nu]
            yield
        r = [r[u] + qm[u] + bdot(qm[u], r[u]) for u in nu]
    rhs = [jnp.concatenate([v[u] * beta[u], kb[u] * jnp.exp(bc[u])], axis=-1) for u in nu]
    uw = [rhs[u] + bdot(r[u], rhs[u]) for u in nu]
    yield
    st = [s_scr[u] for u in nu]
    ws = [bdot(jnp.concatenate([uw[u][:, GDN_DV:], q[u] * jnp.exp(bc[u])], axis=0), st[u]) for u in nu]
    v_new = [uw[u][:, :GDN_DV] - ws[u][:chunk] for u in nu]
    yield
    o = [ws[u][chunk:] + bdot(kq[u][chunk:] * dec[u], v_new[u]) for u in nu]
    for u in nu:
        bl = bc[u][chunk - 1:chunk, :]
        s_scr[u] = jnp.exp(bl) * st[u] + lax.dot_general((k[u] * jnp.exp(bl - bc[u])).astype(BF16),
                                                         v_new[u].astype(BF16), TN_DIMS, preferred_element_type=F32)
    yield
    for u, (s, h) in enumerate(units):
        sl = slice(h * GDN_DV, (h + 1) * GDN_DV)
        out[s, h] = (_rms(o[u], nw_ref[...]) * _silu(zb_ref[s, :, sl].astype(F32))
                     * _sigmoid(gt_ref[s, :, sl].astype(F32)))

    @pl.when(n == nchunks - 1)
    def _():
        last = t_valid - (nchunks - 1) * chunk
        for s in range(nseq):
            cout_ref[s] = xp_scr[s, pad + last - (GDN_CONV - 1):pad + last, :]
        for u, (s, h) in enumerate(units):
            sout_ref[s, h] = s_scr[u]

    for s in range(nseq):
        xp_scr[s, 0:pad, :] = xp_scr[s, chunk:chunk + pad, :]


N_GLA_IN, N_GDN_IN = 8, 13


def _mixer_kernel(*refs, chunk, sub, nchunks, nseq, has_state, t_valid):
    gla_in, refs = refs[:N_GLA_IN], refs[N_GLA_IN:]
    gdn_in, refs = refs[:N_GDN_IN], refs[N_GDN_IN:]
    if has_state:
        (sa0_ref, sb0_ref, c0_ref), refs = refs[:3], refs[3:]
    else:
        sa0_ref = sb0_ref = c0_ref = None
    o_ref, sa_out, sb_out, c_out, sa_scr, sb_scr, xp_scr = refs
    out_a, out_b = {}, {}
    gla = _gla_stages(gla_in, sa0_ref, sa_out, sa_scr, out_a,
                      chunk=chunk, sub=sub, nchunks=nchunks, nseq=nseq, t_valid=t_valid)
    gdn = _gdn_stages(gdn_in, sb0_ref, c0_ref, sb_out, c_out, sb_scr, xp_scr, out_b,
                      chunk=chunk, nchunks=nchunks, nseq=nseq, t_valid=t_valid)
    live = [gdn, gla]
    while live:
        for g in list(live):
            if next(g, StopIteration) is StopIteration:
                live.remove(g)
    per = GLA_DV // GDN_DV
    for (s, h), ob in out_b.items():
        oa = out_a[s, h // per][:, (h % per) * GDN_DV:(h % per + 1) * GDN_DV]
        o_ref[s, :, h * GDN_DV:(h + 1) * GDN_DV] = (oa + ob).astype(o_ref.dtype)


def _mixer(proj3, small3, small_t4, w, s_gla, s_gdn, s_conv, *, chunk, sub, nseq, t_valid, out_dtype):
    nb, t_pad, _ = proj3.shape
    nchunks = t_pad // chunk
    has_state = s_gla is not None
    blk = lambda width, c: pl.BlockSpec((nseq, chunk, width), lambda b, n, c=c: (b, n, c))
    col = lambda c: blk(PROJ_PART, c)
    full = lambda a: pl.BlockSpec(a.shape, lambda b, n: (0,) * a.ndim)
    alog_r, dtb_r = w["a_log"].reshape(1, GDN_HEADS), w["dt_bias"].reshape(1, GDN_HEADS)
    alog_c, dtb_c = w["a_log"].reshape(GDN_HEADS, 1), w["dt_bias"].reshape(GDN_HEADS, 1)
    small_spec = blk(LANES, 0)
    gla_consts = [w["wa2"], w["ba"], w["gla_norm"]]
    gdn_consts = [w["w_conv"], alog_r, dtb_r, alog_c, dtb_c, w["gdn_norm"]]
    in_specs = ([col(COL_QK), col(COL_V), col(COL_GA), col(COL_GATE_A), small_spec] + [full(a) for a in gla_consts]
                + [col(COL_CQ), col(COL_CK), col(COL_CV), col(COL_ZB), col(COL_GATE_B), small_spec,
                   pl.BlockSpec((nseq, None, SMALL_ROWS, chunk), lambda b, n: (b, n, 0, 0))]
                + [full(a) for a in gdn_consts])
    args = [proj3] * 4 + [small3] + gla_consts + [proj3] * 5 + [small3, small_t4] + gdn_consts
    assert len(in_specs) == N_GLA_IN + N_GDN_IN
    sa_spec = pl.BlockSpec((nseq, GLA_HEADS, GLA_DK, GLA_DV), lambda b, n: (b, 0, 0, 0))
    sb_spec = pl.BlockSpec((nseq, GDN_HEADS, GDN_DK, GDN_DV), lambda b, n: (b, 0, 0, 0))
    conv_spec = pl.BlockSpec((nseq, GDN_CONV - 1, GDN_CONV_DIM), lambda b, n: (b, 0, 0))
    if has_state:
        in_specs += [sa_spec, sb_spec, conv_spec]
        args += [s_gla, s_gdn, s_conv]
    return pl.pallas_call(
        functools.partial(_mixer_kernel, chunk=chunk, sub=sub, nchunks=nchunks, nseq=nseq, has_state=has_state,
                          t_valid=t_valid),
        out_shape=(jax.ShapeDtypeStruct((nb, t_pad, D_MODEL), out_dtype),
                   jax.ShapeDtypeStruct((nb, GLA_HEADS, GLA_DK, GLA_DV), F32),
                   jax.ShapeDtypeStruct((nb, GDN_HEADS, GDN_DK, GDN_DV), F32),
                   jax.ShapeDtypeStruct((nb, GDN_CONV - 1, GDN_CONV_DIM), F32)),
        grid=(nb // nseq, nchunks),
        in_specs=in_specs,
        out_specs=(blk(D_MODEL, 0), sa_spec, sb_spec, conv_spec),
        scratch_shapes=[pltpu.VMEM((nseq * GLA_HEADS, GLA_DV, GLA_DK), F32),
                        pltpu.VMEM((nseq * GDN_HEADS, GDN_DK, GDN_DV), F32),
                        pltpu.VMEM((nseq, chunk + 2 * SUBLANES, GDN_CONV_DIM), F32)],
        compiler_params=_params("parallel", "arbitrary"),
        name="mixer",
    )(*args)


def _post_kernel(mg_ref, x_ref, gate_ref, shift_ref, scale_ref, wo_ref, nw_ref, wrt_ref, brt_ref,
                 x1_ref, hx_ref, gid_ref):
    y = jnp.dot(mg_ref[...].astype(BF16), wo_ref[...], preferred_element_type=F32)
    rows = x_ref.shape[0]
    x1 = x_ref[...] + _mod(gate_ref, rows) * y
    x1_ref[...] = x1
    h2 = _rms(x1, nw_ref[...]) * (1.0 + _mod(scale_ref, rows)) + _mod(shift_ref, rows)
    hx_ref[:, 0:D_MODEL] = h2
    lt = lax.dot_general(wrt_ref[...], h2.astype(BF16), NT_DIMS, preferred_element_type=F32) + brt_ref[...]
    tm = lt.shape[1]
    gl = lt[0:N_GROUPS]
    gidx = lax.broadcasted_iota(jnp.int32, (N_GROUPS, tm), 0)
    gmax = jnp.max(gl, axis=0, keepdims=True)
    g_w = 1.0 / jnp.sum(jnp.exp(gl - gmax), axis=0, keepdims=True)
    g_sel = jnp.min(jnp.where(gl == gmax, gidx, N_GROUPS), axis=0, keepdims=True)
    el = jnp.zeros((EXPERTS_PER_GROUP, tm), F32)
    for g in range(N_GROUPS):
        r = ROUTER_GROUP_STRIDE * (1 + g)
        el = el + jnp.where(g_sel == g, lt[r:r + EXPERTS_PER_GROUP], 0.0)
    eidx = lax.broadcasted_iota(jnp.int32, (EXPERTS_PER_GROUP, tm), 0)
    m1 = jnp.max(el, axis=0, keepdims=True)
    i1 = jnp.min(jnp.where(el == m1, eidx, EXPERTS_PER_GROUP), axis=0, keepdims=True)
    el2 = jnp.where(eidx == i1, -jnp.inf, el)
    m2 = jnp.max(el2, axis=0, keepdims=True)
    i2 = jnp.min(jnp.where(el2 == m2, eidx, EXPERTS_PER_GROUP), axis=0, keepdims=True)
    r21 = jnp.exp(m2 - m1)
    w1 = 1.0 / (1.0 + r21)
    w2 = r21 / (1.0 + r21)
    comb_t = g_w * (jnp.where(eidx == i1, w1, 0.0) + jnp.where(eidx == i2, w2, 0.0))
    ident = (lax.broadcasted_iota(jnp.int32, (EXPERTS_PER_GROUP, LANES), 0)
             == lax.broadcasted_iota(jnp.int32, (EXPERTS_PER_GROUP, LANES), 1)).astype(F32)
    hx_ref[:, D_MODEL:] = lax.dot_general(comb_t, ident, TN_DIMS, precision=HIGHEST, preferred_element_type=F32)
    gid_ref[...] = jnp.broadcast_to(g_sel, gid_ref.shape)


def _post(merged, x2d, gate, shift, scale, mod_specs, wo, nw, wrt, brt, tm):
    m, d = x2d.shape
    row = lambda: pl.BlockSpec((tm, d), lambda i: (i, 0))
    full = lambda a: pl.BlockSpec(a.shape, lambda i: (0,) * a.ndim)
    return pl.pallas_call(
        _post_kernel,
        out_shape=(jax.ShapeDtypeStruct((m, d), F32),
                   jax.ShapeDtypeStruct((m, HX_WIDTH), F32),
                   jax.ShapeDtypeStruct((SUBLANES, m), jnp.int32)),
        grid=(m // tm,),
        in_specs=[row(), row(), mod_specs[0], mod_specs[1], mod_specs[2],
                  full(wo), full(nw), full(wrt), full(brt)],
        out_specs=(row(), pl.BlockSpec((tm, HX_WIDTH), lambda i: (i, 0)),
                   pl.BlockSpec((SUBLANES, tm), lambda i: (0, i))),
        compiler_params=_params("parallel"),
        name="post_mixer",
    )(merged, x2d, gate, shift, scale, wo, nw, wrt, brt)


def _invert_kernel(pos_ref, src_ref):
    def clear(i, carry):
        src_ref[i] = 0
        return carry

    def place(t, carry):
        src_ref[pos_ref[t]] = t
        return carry

    lax.fori_loop(0, src_ref.shape[0], clear, 0, unroll=SCALAR_UNROLL)
    lax.fori_loop(0, pos_ref.shape[0], place, 0, unroll=SCALAR_UNROLL)


def _invert_rows(pos, rows):
    smem = pl.BlockSpec(memory_space=pltpu.SMEM)
    return pl.pallas_call(
        _invert_kernel,
        out_shape=jax.ShapeDtypeStruct((rows,), jnp.int32),
        in_specs=[smem],
        out_specs=smem,
        name="invert_rows",
    )(pos)


def _route(cls, ncls, tm):
    m = cls.shape[0]
    ntiles_max = m // tm + ncls - 1
    onehot = (cls[:, None] == jnp.arange(ncls, dtype=jnp.int32)[None, :]).astype(jnp.int32)
    incl = jnp.cumsum(onehot, axis=0)
    tiles = (incl[-1] + tm - 1) // tm
    tile_end = jnp.cumsum(tiles)
    pos = jnp.sum(onehot * ((tile_end - tiles) * tm + incl - onehot), axis=1)
    src = _invert_rows(pos, ntiles_max * tm)
    t = jnp.arange(ntiles_max, dtype=jnp.int32)
    tile_class = jnp.minimum(jnp.sum((t[:, None] >= tile_end[None, :]).astype(jnp.int32), axis=1), ncls - 1)
    return pos, src, tile_class, tile_end[-1:]


def _row_gather(idx_ref, src_hbm, buf, sem, rows, first=0):
    for r in range(first, first + rows):
        pltpu.make_async_copy(src_hbm.at[pl.ds(idx_ref[0, r], 1)], buf.at[pl.ds(r, 1)],
                              sem).start(priority=r % 2)


def _row_gather_wait(src_hbm, buf, sem):
    pltpu.make_async_copy(src_hbm.at[pl.ds(0, buf.shape[0])], buf, sem).wait()


def _experts_kernel(meta_ref, nt_ref, src_ref, srcn_ref, hx_hbm, *rest, tm, nslot):
    w_refs, (y_ref, xbuf, sem) = rest[:3 * nslot], rest[3 * nslot:]
    t = pl.program_id(0)
    nt = nt_ref[0]
    slot = t % 2

    @pl.when(t == 0)
    def _():
        _row_gather(src_ref, hx_hbm, xbuf.at[0], sem.at[0], tm)

    @pl.when(t < nt)
    def _():
        cur, nxt, sem_cur, sem_nxt = xbuf.at[slot], xbuf.at[1 - slot], sem.at[slot], sem.at[1 - slot]
        _row_gather_wait(hx_hbm, cur, sem_cur)
        x = cur[:, 0:D_MODEL].astype(BF16)
        acc = jnp.zeros((tm, D_MODEL), F32)
        for j in range(nslot):
            wg_ref, wu_ref, wd_ref = w_refs[3 * j:3 * j + 3]
            e = meta_ref[t * (1 + nslot) + 1 + j]
            share = tm // (3 * nslot)
            a = jnp.dot(x, wg_ref[0, 0], preferred_element_type=F32)
            _row_gather(srcn_ref, hx_hbm, nxt, sem_nxt, share, first=(3 * j) * share)
            u = jnp.dot(x, wu_ref[0, 0], preferred_element_type=F32)
            _row_gather(srcn_ref, hx_hbm, nxt, sem_nxt, share, first=(3 * j + 1) * share)
            cw = jnp.zeros((tm, 1), F32)
            for c in range(EXPERTS_PER_GROUP):
                cw = cw + jnp.where(e == c, cur[:, D_MODEL + c:D_MODEL + c + 1], 0.0)
            acc = acc + jnp.dot((_silu(a) * u * cw).astype(BF16), wd_ref[0, 0], preferred_element_type=F32)
            last = tm - (3 * nslot - 1) * share if j == nslot - 1 else share
            _row_gather(srcn_ref, hx_hbm, nxt, sem_nxt, last, first=(3 * j + 2) * share)
        y_ref[:, 0, :] = acc

        @pl.when(t == nt - 1)
        def _():
            _row_gather_wait(hx_hbm, nxt, sem_nxt)

    @pl.when(t >= nt)
    def _():
        y_ref[...] = jnp.zeros_like(y_ref)


def _experts(hx, src, tile_class, ntiles, class_table, wg, wu, wd, tm):
    ntiles_max = tile_class.shape[0]
    nslot = class_table.shape[1] - 1
    d, f = D_MODEL, D_EXPERT
    src3 = src.reshape(ntiles_max, 1, tm)
    meta = jnp.asarray(class_table, jnp.int32)[tile_class].reshape(-1)
    stride = 1 + nslot
    wspec = lambda j, shape: pl.BlockSpec(
        (1, 1) + shape, lambda t, meta, nt, j=j: (meta[t * stride], meta[t * stride + 1 + j], 0, 0))
    w_specs, w_args = [], []
    for j in range(nslot):
        w_specs += [wspec(j, (d, f)), wspec(j, (d, f)), wspec(j, (f, d))]
        w_args += [wg, wu, wd]
    return pl.pallas_call(
        functools.partial(_experts_kernel, tm=tm, nslot=nslot),
        out_shape=jax.ShapeDtypeStruct((ntiles_max * tm, 1, d), F32),
        grid_spec=pltpu.PrefetchScalarGridSpec(
            num_scalar_prefetch=2,
            grid=(ntiles_max,),
            in_specs=[pl.BlockSpec((None, 1, tm), lambda t, meta, nt: (t, 0, 0), memory_space=pltpu.SMEM),
                      pl.BlockSpec((None, 1, tm),
                                   lambda t, meta, nt: (jnp.maximum(jnp.minimum(t + 1, nt[0] - 1), 0), 0, 0),
                                   memory_space=pltpu.SMEM),
                      pl.BlockSpec(memory_space=pl.ANY)] + w_specs,
            out_specs=pl.BlockSpec((tm, 1, d), lambda t, meta, nt: (t, 0, 0)),
            scratch_shapes=[pltpu.VMEM((2, tm, HX_WIDTH), F32), pltpu.SemaphoreType.DMA((2,))]),
        compiler_params=_params("arbitrary"),
        name="experts",
    )(meta, ntiles, src3, src3, hx, *w_args)


def _final_kernel(pos_ref, posn_ref, ys_hbm, x1_ref, gate_ref, nw_ref, o_ref, ybuf, sem, *, tm, nsteps):
    i = pl.program_id(0)
    slot = i % 2

    @pl.when(i == 0)
    def _():
        _row_gather(pos_ref, ys_hbm, ybuf.at[0], sem.at[0], tm)

    _row_gather(posn_ref, ys_hbm, ybuf.at[1 - slot], sem.at[1 - slot], tm)
    _row_gather_wait(ys_hbm, ybuf.at[slot], sem.at[slot])
    o_ref[...] = _rms(x1_ref[...] + _mod(gate_ref, tm) * ybuf[slot, :, 0, :], nw_ref[...])

    @pl.when(i == nsteps - 1)
    def _():
        _row_gather_wait(ys_hbm, ybuf.at[1 - slot], sem.at[1 - slot])


def _final(ys, pos, x1, gate, gate_spec, nw, tm):
    m, d = x1.shape
    nsteps = m // tm
    pos3 = pos.reshape(nsteps, 1, tm)
    return pl.pallas_call(
        functools.partial(_final_kernel, tm=tm, nsteps=nsteps),
        out_shape=jax.ShapeDtypeStruct((m, d), F32),
        grid=(nsteps,),
        in_specs=[pl.BlockSpec((None, 1, tm), lambda i: (i, 0, 0), memory_space=pltpu.SMEM),
                  pl.BlockSpec((None, 1, tm), lambda i: (jnp.minimum(i + 1, nsteps - 1), 0, 0),
                               memory_space=pltpu.SMEM),
                  pl.BlockSpec(memory_space=pl.ANY),
                  pl.BlockSpec((tm, d), lambda i: (i, 0)),
                  gate_spec,
                  pl.BlockSpec((1, d), lambda i: (0, 0))],
        out_specs=pl.BlockSpec((tm, d), lambda i: (i, 0)),
        scratch_shapes=[pltpu.VMEM((2, tm, 1, d), F32), pltpu.SemaphoreType.DMA((2,))],
        compiler_params=_params("arbitrary"),
        name="final",
    )(pos3, pos3, ys, x1, gate, nw)


def _mod_spec(idx, tm, t, ngrid):
    if t % tm == 0:
        per = t // tm
        shape, index = (None, None, 1, D_MODEL), lambda i: (i // per, idx, 0, 0)
    else:
        shape, index = (tm // t, None, 1, D_MODEL), lambda i: (i, idx, 0, 0)
    if ngrid == 2:
        return pl.BlockSpec(shape, lambda i, j: index(i))
    return pl.BlockSpec(shape, index)


def _trunk(x, mod, s_gla, s_gdn, s_conv, w, *, chunk, sub, nseq, t_valid, tm, tm_moe, act_dtype):
    nb, t_pad, d = x.shape
    m = nb * t_pad
    x2d = x.reshape(m, d)
    mods = [mod.reshape(nb, N_MOD, 1, d)] * N_MOD
    spec = lambda idx, ngrid: _mod_spec(idx, tm, t_pad, ngrid)

    proj, small, small_t = _inproj(x2d, mods[0], mods[1], (spec(0, 2), spec(1, 2)), w["norm1"],
                                   w["w_main"], w["w_small"], w["w_small_t"], tm, act_dtype)
    proj3 = proj.reshape(nb, t_pad, PROJ_MAIN)
    small3 = small.reshape(nb, t_pad, LANES)
    small_t4 = small_t.reshape(SMALL_ROWS, nb, t_pad // chunk, chunk).transpose(1, 2, 0, 3)
    merged, new_gla, new_gdn, new_conv = _mixer(proj3, small3, small_t4, w, s_gla, s_gdn, s_conv, chunk=chunk,
                                                sub=sub, nseq=nseq, t_valid=t_valid, out_dtype=act_dtype)
    if t_valid < t_pad:
        merged, x = merged[:, :t_valid], x[:, :t_valid]
    m = nb * t_valid
    tm = min(tm, m)
    tm_moe = min(tm_moe, m)
    spec = lambda idx, ngrid: _mod_spec(idx, tm, t_valid, ngrid)
    x1, hx, gid = _post(merged.reshape(m, d), x.reshape(m, d), mods[2], mods[3], mods[4],
                        (spec(2, 1), spec(3, 1), spec(4, 1)),
                        w["w_out"], w["norm2"], w["w_router_t"], w["b_router_t"], tm)
    class_table = GROUP_CLASSES
    pos, src, tile_class, ntiles = _route(gid[0], class_table.shape[0], tm_moe)
    ys = _experts(hx, src, tile_class, ntiles, class_table, w["w_gate"], w["w_up"], w["w_down"], tm_moe)
    tm_fin = min(tm, FINAL_TILE)
    y = _final(ys, pos, x1, mods[5], _mod_spec(5, tm_fin, t_valid, 1), w["final_norm"], tm_fin)
    return y.reshape(nb, t_valid, d), new_gla, new_gdn, new_conv


def _regroup_kernel(q_ref, r_ref, a_ref, b_ref, o_ref, *, shifts):
    window = jnp.concatenate([a_ref[...], b_ref[...]], axis=0)
    r = r_ref[pl.program_id(0)]
    for s in shifts:
        @pl.when(r == s)
        def _():
            o_ref[...] = window[s:s + PROJ_PART, :].T.astype(o_ref.dtype)


def _regroup_columns(w, starts):
    d = w.shape[1]
    assert d == PROJ_PART and all(s % SUBLANES == 0 for s in starts)
    w = jnp.swapaxes(w, 1, 2)
    q = np.array([s // PROJ_PART for s in starts], np.int32)
    r = np.array([s % PROJ_PART for s in starts], np.int32)
    assert r.max() <= LANES
    per = PROJ_PART // LANES
    return pl.pallas_call(
        functools.partial(_regroup_kernel, shifts=tuple(sorted(set(r.tolist())))),
        out_shape=jax.ShapeDtypeStruct((d, len(starts) * PROJ_PART), BF16),
        grid_spec=pltpu.PrefetchScalarGridSpec(
            num_scalar_prefetch=2,
            grid=(len(starts),),
            in_specs=[pl.BlockSpec((None, PROJ_PART, d), lambda i, q, r: (0, q[i], 0)),
                      pl.BlockSpec((None, LANES, d), lambda i, q, r: (0, per * (q[i] + 1), 0))],
            out_specs=pl.BlockSpec((d, PROJ_PART), lambda i, q, r: (0, i))),
        compiler_params=_params("arbitrary"),
        name="regroup_w_in",
    )(jnp.asarray(q), jnp.asarray(r), w, w)


def _prep_weights(w_in, w_gla_a2, b_gla_a, gla_norm_w, w_conv, gdn_A_log, gdn_dt_bias, gdn_norm_w, w_out,
                  norm1_w, norm2_w, w_group_router, b_group_router, w_expert_router, b_expert_router,
                  w_exp_gate, w_exp_up, w_exp_down, final_norm_w):
    d = D_MODEL
    o = 0
    start = {}
    for name, width in (("gla", 2 * GLA_QK + 2 * GLA_VW), ("ra", GLA_GATE_RANK), ("qkv", GDN_CONV_DIM),
                        ("zb", GDN_VW), ("beta", GDN_HEADS), ("a", GDN_HEADS), ("gates", 2 * D_MODEL)):
        start[name] = o
        o += width
    gate_starts = [start["zb"], start["gates"], start["gates"] + PROJ_PART]
    main_starts = [s for j in range(3) for s in (start["gla"] + j * PROJ_PART, start["qkv"] + j * PROJ_PART,
                                                 gate_starts[j])]
    assert start["a"] == start["beta"] + GDN_HEADS
    w_main = _regroup_columns(w_in, main_starts + [start["ra"], start["beta"]])
    small = jnp.concatenate([w_main[:, PROJ_MAIN:PROJ_MAIN + GLA_GATE_RANK],
                             w_main[:, PROJ_MAIN + PROJ_PART:PROJ_MAIN + PROJ_PART + 2 * GDN_HEADS]], axis=1)
    w_small = jnp.pad(small, ((0, 0), (0, LANES - small.shape[1])))
    w_small_t = small.T
    assert N_GROUPS == EXPERTS_PER_GROUP
    stride_pad = ((0, 0), (0, 0), (0, ROUTER_GROUP_STRIDE - EXPERTS_PER_GROUP))

    def router_rows(group_part, expert_part):
        both = jnp.concatenate([group_part, expert_part], axis=1).reshape(-1, 1 + N_GROUPS, EXPERTS_PER_GROUP)
        return jnp.pad(both, stride_pad).reshape(-1, ROUTER_ROWS)

    wr_t = router_rows(w_group_router, w_expert_router).T
    br_t = router_rows(b_group_router[None], b_expert_router[None]).T
    return dict(
        w_main=w_main, w_small=w_small, w_small_t=w_small_t,
        norm1=norm1_w.reshape(1, d), norm2=norm2_w.reshape(1, d), final_norm=final_norm_w.reshape(1, d),
        wa2=w_gla_a2, ba=b_gla_a.reshape(1, GLA_QK), gla_norm=gla_norm_w.reshape(1, GLA_DV),
        w_conv=w_conv, a_log=gdn_A_log, dt_bias=gdn_dt_bias, gdn_norm=gdn_norm_w.reshape(1, GDN_DV),
        w_out=w_out.astype(BF16), w_router_t=wr_t.astype(BF16), b_router_t=br_t,
        w_gate=w_exp_gate.astype(BF16), w_up=w_exp_up.astype(BF16), w_down=w_exp_down.astype(BF16),
    )


def kernel(x_prompt, x_sample, c_prompt, c_sample, state_gla, state_gdn, state_conv, w_ada, b_ada, norm1_w, w_in, w_gla_a2, b_gla_a, gla_norm_w, w_conv, gdn_A_log, gdn_dt_bias, gdn_norm_w, w_out, norm2_w, w_group_router, b_group_router, w_expert_router, b_expert_router, w_exp_gate, w_exp_up, w_exp_down, final_norm_w):
    assert w_ada.shape[0] == 1, "single layer"
    bp, tp, d = x_prompt.shape
    bs, ts, _ = x_sample.shape
    w = _prep_weights(w_in, w_gla_a2[0], b_gla_a[0], gla_norm_w[0], w_conv[0], gdn_A_log[0], gdn_dt_bias[0],
                      gdn_norm_w[0], w_out[0], norm1_w[0], norm2_w[0], w_group_router[0], b_group_router[0],
                      w_expert_router[0], b_expert_router[0], w_exp_gate[0], w_exp_up[0], w_exp_down[0],
                      final_norm_w)
    mod = _ada_mod(jnp.concatenate([c_prompt, c_sample], axis=0), w_ada[0], b_ada[0]).reshape(bp + bs, N_MOD, d)

    y_p, gla_p, gdn_p, conv_p = _trunk(x_prompt, mod[:bp], None, None, None, w,
                                       chunk=64, sub=GLA_SUBCHUNK, nseq=4, t_valid=tp, tm=min(1024, tp),
                                       tm_moe=min(512, tp), act_dtype=BF16)
    ts_pad = SUBLANES
    xs = jnp.pad(x_sample, ((0, 0), (0, ts_pad - ts), (0, 0)))
    y_s, gla_s, gdn_s, conv_s = _trunk(xs, mod[bp:], state_gla[0], state_gdn[0], state_conv[0], w,
                                       chunk=ts_pad, sub=ts_pad, nseq=4, t_valid=ts, tm=min(512, bs * ts_pad),
                                       tm_moe=128, act_dtype=F32)
    return (y_p, y_s, gla_p[None], gdn_p[None], conv_p[None], gla_s[None], gdn_s[None], conv_s[None])
```

```python
import functools
import math

import jax
import numpy as np
import jax.numpy as jnp
from jax import lax
from jax.experimental import pallas as pl
from jax.experimental.pallas import tpu as pltpu

F32 = jnp.float32
BF16 = jnp.bfloat16
HIGHEST = lax.Precision.HIGHEST

D_MODEL = 1024
GLA_HEADS = 4
GLA_DK = 128
GLA_DV = 256
GLA_QK = GLA_HEADS * GLA_DK
GLA_VW = GLA_HEADS * GLA_DV
GLA_GATE_RANK = 16
GLA_TAU = 16.0
GLA_SUBCHUNK = 16
GDN_HEADS = 8
GDN_DK = 128
GDN_DV = 128
GDN_QK = GDN_HEADS * GDN_DK
GDN_VW = GDN_HEADS * GDN_DV
GDN_CONV = 4
GDN_CONV_DIM = 2 * GDN_QK + GDN_VW
GDN_INV_BLOCK = 16
N_GROUPS = 4
EXPERTS_PER_GROUP = 4
N_EXPERTS = N_GROUPS * EXPERTS_PER_GROUP
D_EXPERT = D_MODEL // 2
N_MOD = 6
NORM_EPS = 1e-6
L2_EPS = 1e-6

LANES = 128
SUBLANES = 8
VMEM_LIMIT = 56 * 1024 * 1024

PROJ_MAIN = 2 * GLA_QK + 2 * GLA_VW + GDN_CONV_DIM + GDN_VW + 2 * D_MODEL
PROJ_PART = 1024
INPROJ_TN = 3 * PROJ_PART
COL_QK, COL_V, COL_GA = 0, 3, 6
COL_CQ, COL_CK, COL_CV = 1, 4, 7
COL_ZB, COL_GATE_A, COL_GATE_B = 2, 5, 8
SMALL_RA, SMALL_BETA, SMALL_A = 0, GLA_GATE_RANK, GLA_GATE_RANK + GDN_HEADS
SMALL_ROWS = 32
ROUTER_GROUP_STRIDE = 8
ROUTER_ROWS = ROUTER_GROUP_STRIDE * (1 + N_GROUPS)

GROUP_CLASSES = np.array([[g] + list(range(EXPERTS_PER_GROUP)) for g in range(N_GROUPS)], np.int32)

HX_WIDTH = D_MODEL + LANES
SCALAR_UNROLL = 16
FINAL_TILE = 512

NT_DIMS = (((1,), (1,)), ((), ()))
TN_DIMS = (((0,), (0,)), ((), ()))


def _sigmoid(x):
    return 0.5 * jnp.tanh(0.5 * x) + 0.5


def _silu(x):
    return x * _sigmoid(x)


def _softplus(x):
    return jnp.maximum(x, 0.0) + jnp.log1p(jnp.exp(-jnp.abs(x)))


def _rms(x, w):
    return x * lax.rsqrt(jnp.mean(x * x, axis=-1, keepdims=True) + NORM_EPS) * w


def _mod(ref, rows):
    v = ref[...]
    if v.ndim == 3:
        v = jnp.broadcast_to(v, (v.shape[0], rows // v.shape[0], v.shape[2])).reshape(rows, v.shape[2])
    return v


def _params(*sem):
    return pltpu.CompilerParams(dimension_semantics=sem, vmem_limit_bytes=VMEM_LIMIT)


def _ada_kernel(c_ref, w_ref, b_ref, o_ref):
    cs = _silu(c_ref[...])
    o_ref[...] = jnp.dot(cs.astype(BF16), w_ref[...].astype(BF16), preferred_element_type=F32) + b_ref[...]


def _ada_mod(c_all, w_ada, b_ada):
    rows, d = c_all.shape
    n = w_ada.shape[1]
    tn = 1024
    return pl.pallas_call(
        _ada_kernel,
        out_shape=jax.ShapeDtypeStruct((rows, n), F32),
        grid=(n // tn,),
        in_specs=[pl.BlockSpec((rows, d), lambda j: (0, 0)),
                  pl.BlockSpec((d, tn), lambda j: (0, j)),
                  pl.BlockSpec((1, tn), lambda j: (0, j))],
        out_specs=pl.BlockSpec((rows, tn), lambda j: (0, j)),
        compiler_params=_params("arbitrary"),
        name="ada_mod",
    )(c_all, w_ada, b_ada.reshape(1, n))


def _inproj_kernel(x_ref, shift_ref, scale_ref, nw_ref, w_ref, ws_ref, wst_ref,
                   o_ref, os_ref, ost_ref, h_scr):
    @pl.when(pl.program_id(1) == 0)
    def _():
        rows = x_ref.shape[0]
        h = _rms(x_ref[...], nw_ref[...]) * (1.0 + _mod(scale_ref, rows)) + _mod(shift_ref, rows)
        hb = h.astype(BF16)
        h_scr[...] = hb
        os_ref[...] = jnp.dot(hb, ws_ref[...], preferred_element_type=F32)
        ost_ref[...] = lax.dot_general(wst_ref[...], hb, NT_DIMS, preferred_element_type=F32)

    o_ref[...] = jnp.dot(h_scr[...], w_ref[...], preferred_element_type=F32).astype(o_ref.dtype)


def _inproj(x2d, shift, scale, mod_specs, nw, w_main, w_small, w_small_t, tm, out_dtype):
    m, d = x2d.shape
    n = PROJ_MAIN
    tn = INPROJ_TN
    return pl.pallas_call(
        _inproj_kernel,
        out_shape=(jax.ShapeDtypeStruct((m, n), out_dtype),
                   jax.ShapeDtypeStruct((m, LANES), F32),
                   jax.ShapeDtypeStruct((SMALL_ROWS, m), F32)),
        grid=(m // tm, n // tn),
        in_specs=[pl.BlockSpec((tm, d), lambda i, j: (i, 0)),
                  mod_specs[0], mod_specs[1],
                  pl.BlockSpec((1, d), lambda i, j: (0, 0)),
                  pl.BlockSpec((d, tn), lambda i, j: (0, j)),
                  pl.BlockSpec((d, LANES), lambda i, j: (0, 0)),
                  pl.BlockSpec((SMALL_ROWS, d), lambda i, j: (0, 0))],
        out_specs=(pl.BlockSpec((tm, tn), lambda i, j: (i, j)),
                   pl.BlockSpec((tm, LANES), lambda i, j: (i, 0)),
                   pl.BlockSpec((SMALL_ROWS, tm), lambda i, j: (0, i))),
        scratch_shapes=[pltpu.VMEM((tm, d), BF16)],
        compiler_params=_params("parallel", "arbitrary"),
        name="inproj",
    )(x2d, shift, scale, nw, w_main, w_small, w_small_t)


def _gla_stages(ins, s0_ref, sout_ref, st_scr, out, *, chunk, sub, nchunks, nseq, t_valid):
    qk_ref, v_ref, ga_ref, gt_ref, sm_ref, wa2_ref, ba_ref, nw_ref = ins
    has_state = s0_ref is not None
    n = pl.program_id(1)
    masked = t_valid < chunk * nchunks
    units = [(s, h) for s in range(nseq) for h in range(GLA_HEADS)]

    @pl.when(n == 0)
    def _():
        for u, (s, h) in enumerate(units):
            st_scr[u] = s0_ref[s, h].T if has_state else jnp.zeros((GLA_DV, GLA_DK), F32)

    row = lax.broadcasted_iota(jnp.int32, (chunk, chunk), 0)
    col = lax.broadcasted_iota(jnp.int32, (chunk, chunk), 1)
    rowi = lax.broadcasted_iota(jnp.int32, (chunk, 1), 0)
    valid = (rowi + n * chunk) < t_valid
    tri = (row >= col).astype(F32)
    wa2 = wa2_ref[...].astype(BF16)
    b_all = []
    for s in range(nseq):
        ra = sm_ref[s, :, SMALL_RA:SMALL_RA + GLA_GATE_RANK]
        x = jnp.dot(ra.astype(BF16), wa2, preferred_element_type=F32) + ba_ref[...]
        g = (jnp.minimum(x, 0.0) - jnp.log1p(jnp.exp(-jnp.abs(x)))) * (1.0 / GLA_TAU)
        if masked:
            g = jnp.where(valid, g, 0.0)
        b_all.append(jnp.dot(tri, g, precision=HIGHEST, preferred_element_type=F32))
    yield

    q, k, v, b = [], [], [], []
    for s, h in units:
        q.append(qk_ref[s, :, h * GLA_DK:(h + 1) * GLA_DK].astype(F32) * (GLA_DK ** -0.5))
        kh = qk_ref[s, :, GLA_QK + h * GLA_DK:GLA_QK + (h + 1) * GLA_DK].astype(F32)
        k.append(jnp.where(valid, kh, 0.0) if masked else kh)
        v.append(v_ref[s, :, h * GLA_DV:(h + 1) * GLA_DV].astype(BF16))
        b.append(b_all[s][:, h * GLA_DK:(h + 1) * GLA_DK])
    nu = range(len(units))
    st = [st_scr[u] for u in nu]
    o = [lax.dot_general((q[u] * jnp.exp(b[u])).astype(BF16), st[u].astype(BF16), NT_DIMS,
                         preferred_element_type=F32) for u in nu]
    yield
    blocks = [[] for _ in nu]
    for i in range(chunk // sub):
        r0, r1 = i * sub, (i + 1) * sub
        for u in nu:
            bref = b[u][r0 - 1:r0] if i > 0 else jnp.zeros((1, GLA_DK), F32)
            qt = (q[u][r0:r1] * jnp.exp(b[u][r0:r1] - bref)).astype(BF16)
            expo = bref - b[u]
            if r1 < chunk:
                expo = jnp.where(rowi < r1, expo, 0.0)
            kt = (k[u] * jnp.exp(expo)).astype(BF16)
            blocks[u].append(lax.dot_general(qt, kt, NT_DIMS, preferred_element_type=F32))
        yield
    for u in nu:
        a = blocks[u][0] if len(blocks[u]) == 1 else jnp.concatenate(blocks[u], axis=0)
        a = jnp.where(col <= row, a, 0.0)
        o[u] = o[u] + jnp.dot(a.astype(BF16), v[u], preferred_element_type=F32)
    yield
    for u in nu:
        bl = b[u][chunk - 1:chunk]
        kt = (k[u] * jnp.exp(bl - b[u])).astype(BF16)
        st_scr[u] = st[u] * jnp.exp(bl) + lax.dot_general(v[u], kt, TN_DIMS, preferred_element_type=F32)
    yield
    for u, (s, h) in enumerate(units):
        sl = slice(h * GLA_DV, (h + 1) * GLA_DV)
        out[s, h] = (_rms(o[u], nw_ref[...]) * _silu(ga_ref[s, :, sl].astype(F32))
                     * _sigmoid(gt_ref[s, :, sl].astype(F32)))

    @pl.when(n == nchunks - 1)
    def _():
        for u, (s, h) in enumerate(units):
            sout_ref[s, h] = st_scr[u].T


def _gdn_stages(ins, s0_ref, c0_ref, sout_ref, cout_ref, s_scr, xp_scr, out, *, chunk, nchunks, nseq, t_valid):
    (cq_ref, ck_ref, cv_ref, zb_ref, gt_ref, sm_ref, smt_ref, wc_ref, alog_ref, dtb_ref, alogc_ref, dtbc_ref,
     nw_ref) = ins
    has_state = s0_ref is not None
    n = pl.program_id(1)
    masked = t_valid < chunk * nchunks
    pad = SUBLANES
    units = [(s, h) for s in range(nseq) for h in range(GDN_HEADS)]
    nu = range(len(units))

    @pl.when(n == 0)
    def _():
        for u, (s, h) in enumerate(units):
            s_scr[u] = s0_ref[s, h] if has_state else jnp.zeros((GDN_DK, GDN_DV), F32)
        for s in range(nseq):
            xp_scr[s, 0:pad, :] = jnp.zeros((pad, GDN_CONV_DIM), F32)
            if has_state:
                xp_scr[s, pad - (GDN_CONV - 1):pad, :] = c0_ref[s]

    row = lax.broadcasted_iota(jnp.int32, (chunk, chunk), 0)
    col = lax.broadcasted_iota(jnp.int32, (chunk, chunk), 1)
    valid_c = (lax.broadcasted_iota(jnp.int32, (chunk, 1), 0) + n * chunk) < t_valid
    valid_r = (lax.broadcasted_iota(jnp.int32, (1, chunk), 1) + n * chunk) < t_valid
    tri_c = (row >= col).astype(F32)
    tri_r = (row <= col).astype(F32)
    b_col, b_row, beta_col = [], [], []
    for s in range(nseq):
        for c, ref in enumerate((cq_ref, ck_ref, cv_ref)):
            xp_scr[s, pad:pad + chunk, c * PROJ_PART:(c + 1) * PROJ_PART] = ref[s].astype(F32)
        g_col = -jnp.exp(alog_ref[...]) * _softplus(sm_ref[s, :, SMALL_A:SMALL_A + GDN_HEADS] + dtb_ref[...])
        bt = _sigmoid(sm_ref[s, :, SMALL_BETA:SMALL_BETA + GDN_HEADS])
        g_row = -jnp.exp(alogc_ref[...]) * _softplus(smt_ref[s, SMALL_A:SMALL_A + GDN_HEADS, :] + dtbc_ref[...])
        if masked:
            g_col = jnp.where(valid_c, g_col, 0.0)
            bt = jnp.where(valid_c, bt, 0.0)
            g_row = jnp.where(valid_r, g_row, 0.0)
        beta_col.append(bt)
        b_col.append(jnp.dot(tri_c, g_col, precision=HIGHEST, preferred_element_type=F32))
        b_row.append(jnp.dot(g_row, tri_r, precision=HIGHEST, preferred_element_type=F32))
    yield

    def conv_silu(s, c0):
        cs = slice(c0, c0 + LANES)
        y = xp_scr[s, pad:pad + chunk, cs] * wc_ref[GDN_CONV - 1:GDN_CONV, cs]
        for j in range(1, GDN_CONV):
            y = y + xp_scr[s, pad - j:pad - j + chunk, cs] * wc_ref[GDN_CONV - 1 - j:GDN_CONV - j, cs]
        return _silu(y)

    bdot = lambda a, b: jnp.dot(a.astype(BF16), b.astype(BF16), preferred_element_type=F32)
    q, k, v, kb, dec, bc, beta = [], [], [], [], [], [], []
    for u, (s, h) in enumerate(units):
        qh = conv_silu(s, h * GDN_DK)
        kh = conv_silu(s, GDN_QK + h * GDN_DK)
        v.append(conv_silu(s, 2 * GDN_QK + h * GDN_DV))
        q.append(qh * lax.rsqrt(jnp.sum(qh * qh, axis=-1, keepdims=True) + L2_EPS) * (GDN_DK ** -0.5))
        k.append(kh * lax.rsqrt(jnp.sum(kh * kh, axis=-1, keepdims=True) + L2_EPS))
        bc.append(b_col[s][:, h:h + 1])
        beta.append(beta_col[s][:, h:h + 1])
        dec.append(jnp.where(row >= col, jnp.exp(jnp.minimum(bc[u] - b_row[s][h:h + 1, :], 0.0)), 0.0))
        kb.append(k[u] * beta[u])
        if u % GDN_HEADS == GDN_HEADS - 1:
            yield
    kq = [lax.dot_general(jnp.concatenate([kb[u], q[u]], axis=0).astype(BF16), k[u].astype(BF16), NT_DIMS,
                          preferred_element_type=F32) for u in nu]
    yield
    blk = min(GDN_INV_BLOCK, chunk)
    nblk = chunk // blk
    same_blk = (row // blk) == (col // blk)
    lmat = [jnp.where(row > col, kq[u][:chunk] * dec[u], 0.0) for u in nu]
    p = [jnp.where(same_blk, -lmat[u], 0.0) for u in nu]
    r = p
    for _ in range(int(math.log2(blk)) - 1):
        p = [bdot(p[u], p[u]) for u in nu]
        r = [r[u] + p[u] + bdot(r[u], p[u]) for u in nu]
        yield
    if nblk > 1:
        lo = [jnp.where(same_blk, 0.0, lmat[u]) for u in nu]
        p = [-(lo[u] + bdot(r[u], lo[u])) for u in nu]
        qm = p
        yield
        for _ in range(int(math.log2(nblk)) - 1):
            p = [bdot(p[u], p[u]) for u in nu]
            qm = [qm[u] + p[u] + bdot(qm[u], p[u]) for u in nu]
            yield
        r = [r[u] + qm[u] + bdot(qm[u], r[u]) for u in nu]
    rhs = [jnp.concatenate([v[u] * beta[u], kb[u] * jnp.exp(bc[u])], axis=-1) for u in nu]
    uw = [rhs[u] + bdot(r[u], rhs[u]) for u in nu]
    yield
    st = [s_scr[u] for u in nu]
    ws = [bdot(jnp.concatenate([uw[u][:, GDN_DV:], q[u] * jnp.exp(bc[u])], axis=0), st[u]) for u in nu]
    v_new = [uw[u][:, :GDN_DV] - ws[u][:chunk] for u in nu]
    yield
    o = [ws[u][chunk:] + bdot(kq[u][chunk:] * dec[u], v_new[u]) for u in nu]
    for u in nu:
        bl = bc[u][chunk - 1:chunk, :]
        s_scr[u] = jnp.exp(bl) * st[u] + lax.dot_general((k[u] * jnp.exp(bl - bc[u])).astype(BF16),
                                                         v_new[u].astype(BF16), TN_DIMS, preferred_element_type=F32)
    yield
    for u, (s, h) in enumerate(units):
        sl = slice(h * GDN_DV, (h + 1) * GDN_DV)
        out[s, h] = (_rms(o[u], nw_ref[...]) * _silu(zb_ref[s, :, sl].astype(F32))
                     * _sigmoid(gt_ref[s, :, sl].astype(F32)))

    @pl.when(n == nchunks - 1)
    def _():
        last = t_valid - (nchunks - 1) * chunk
        for s in range(nseq):
            cout_ref[s] = xp_scr[s, pad + last - (GDN_CONV - 1):pad + last, :]
        for u, (s, h) in enumerate(units):
            sout_ref[s, h] = s_scr[u]

    for s in range(nseq):
        xp_scr[s, 0:pad, :] = xp_scr[s, chunk:chunk + pad, :]


N_GLA_IN, N_GDN_IN = 8, 13
STATE_RING = 3


def _mixer_kernel(*refs, chunk, sub, nchunks, nseq, has_state, t_valid):
    gla_in, refs = refs[:N_GLA_IN], refs[N_GLA_IN:]
    gdn_in, refs = refs[:N_GDN_IN], refs[N_GDN_IN:]
    if has_state:
        (sa_hbm, sb_hbm, c0_ref), refs = refs[:3], refs[3:]
        o_ref, sa_out, sb_out, c_out, sa_scr, sb_scr, xp_scr, ring_a, ring_b, sem = refs
        b, nsteps = pl.program_id(0), pl.num_programs(0)

        def copies(step, slot):
            rows = pl.ds(step * nseq, nseq)
            return (pltpu.make_async_copy(sa_hbm.at[rows], ring_a.at[slot], sem.at[0, slot]),
                    pltpu.make_async_copy(sb_hbm.at[rows], ring_b.at[slot], sem.at[1, slot]))

        @pl.when(b == 0)
        def _():
            for c in copies(0, 0):
                c.start()

        @pl.when(jnp.logical_and(b == 0, nsteps > 1))
        def _():
            for c in copies(1, 1):
                c.start()

        @pl.when(b + STATE_RING - 1 < nsteps)
        def _():
            for c in copies(b + STATE_RING - 1, (b + STATE_RING - 1) % STATE_RING):
                c.start()

        for c in copies(b, b % STATE_RING):
            c.wait()
        sa0_ref, sb0_ref = ring_a.at[b % STATE_RING], ring_b.at[b % STATE_RING]
    else:
        sa0_ref = sb0_ref = c0_ref = None
        o_ref, sa_out, sb_out, c_out, sa_scr, sb_scr, xp_scr = refs
    out_a, out_b = {}, {}
    gla = _gla_stages(gla_in, sa0_ref, sa_out, sa_scr, out_a,
                      chunk=chunk, sub=sub, nchunks=nchunks, nseq=nseq, t_valid=t_valid)
    gdn = _gdn_stages(gdn_in, sb0_ref, c0_ref, sb_out, c_out, sb_scr, xp_scr, out_b,
                      chunk=chunk, nchunks=nchunks, nseq=nseq, t_valid=t_valid)
    live = [gdn, gla]
    while live:
        for g in list(live):
            if next(g, StopIteration) is StopIteration:
                live.remove(g)
    per = GLA_DV // GDN_DV
    for (s, h), ob in out_b.items():
        oa = out_a[s, h // per][:, (h % per) * GDN_DV:(h % per + 1) * GDN_DV]
        o_ref[s, :, h * GDN_DV:(h + 1) * GDN_DV] = (oa + ob).astype(o_ref.dtype)


def _mixer(proj3, small3, small_t4, w, s_gla, s_gdn, s_conv, *, chunk, sub, nseq, t_valid, out_dtype):
    nb, t_pad, _ = proj3.shape
    nchunks = t_pad // chunk
    has_state = s_gla is not None
    blk = lambda width, c: pl.BlockSpec((nseq, chunk, width), lambda b, n, c=c: (b, n, c))
    col = lambda c: blk(PROJ_PART, c)
    full = lambda a: pl.BlockSpec(a.shape, lambda b, n: (0,) * a.ndim)
    alog_r, dtb_r = w["a_log"].reshape(1, GDN_HEADS), w["dt_bias"].reshape(1, GDN_HEADS)
    alog_c, dtb_c = w["a_log"].reshape(GDN_HEADS, 1), w["dt_bias"].reshape(GDN_HEADS, 1)
    small_spec = blk(LANES, 0)
    gla_consts = [w["wa2"], w["ba"], w["gla_norm"]]
    gdn_consts = [w["w_conv"], alog_r, dtb_r, alog_c, dtb_c, w["gdn_norm"]]
    in_specs = ([col(COL_QK), col(COL_V), col(COL_GA), col(COL_GATE_A), small_spec] + [full(a) for a in gla_consts]
                + [col(COL_CQ), col(COL_CK), col(COL_CV), col(COL_ZB), col(COL_GATE_B), small_spec,
                   pl.BlockSpec((nseq, None, SMALL_ROWS, chunk), lambda b, n: (b, n, 0, 0))]
                + [full(a) for a in gdn_consts])
    args = [proj3] * 4 + [small3] + gla_consts + [proj3] * 5 + [small3, small_t4] + gdn_consts
    assert len(in_specs) == N_GLA_IN + N_GDN_IN
    sa_spec = pl.BlockSpec((nseq, GLA_HEADS, GLA_DK, GLA_DV), lambda b, n: (b, 0, 0, 0))
    sb_spec = pl.BlockSpec((nseq, GDN_HEADS, GDN_DK, GDN_DV), lambda b, n: (b, 0, 0, 0))
    conv_spec = pl.BlockSpec((nseq, GDN_CONV - 1, GDN_CONV_DIM), lambda b, n: (b, 0, 0))
    scratch = [pltpu.VMEM((nseq * GLA_HEADS, GLA_DV, GLA_DK), F32),
               pltpu.VMEM((nseq * GDN_HEADS, GDN_DK, GDN_DV), F32),
               pltpu.VMEM((nseq, chunk + 2 * SUBLANES, GDN_CONV_DIM), F32)]
    if has_state:
        assert nchunks == 1, "the state ring is advanced once per grid step"
        in_specs += [pl.BlockSpec(memory_space=pl.ANY), pl.BlockSpec(memory_space=pl.ANY), conv_spec]
        args += [s_gla, s_gdn, s_conv]
        scratch += [pltpu.VMEM((STATE_RING, nseq, GLA_HEADS, GLA_DK, GLA_DV), F32),
                    pltpu.VMEM((STATE_RING, nseq, GDN_HEADS, GDN_DK, GDN_DV), F32),
                    pltpu.SemaphoreType.DMA((2, STATE_RING))]
    return pl.pallas_call(
        functools.partial(_mixer_kernel, chunk=chunk, sub=sub, nchunks=nchunks, nseq=nseq, has_state=has_state,
                          t_valid=t_valid),
        out_shape=(jax.ShapeDtypeStruct((nb, t_pad, D_MODEL), out_dtype),
                   jax.ShapeDtypeStruct((nb, GLA_HEADS, GLA_DK, GLA_DV), F32),
                   jax.ShapeDtypeStruct((nb, GDN_HEADS, GDN_DK, GDN_DV), F32),
                   jax.ShapeDtypeStruct((nb, GDN_CONV - 1, GDN_CONV_DIM), F32)),
        grid=(nb // nseq, nchunks),
        in_specs=in_specs,
        out_specs=(blk(D_MODEL, 0), sa_spec, sb_spec, conv_spec),
        scratch_shapes=scratch,
        compiler_params=_params("arbitrary" if has_state else "parallel", "arbitrary"),
        name="mixer",
    )(*args)


def _post_kernel(mg_ref, x_ref, gate_ref, shift_ref, scale_ref, wo_ref, nw_ref, wrt_ref, brt_ref,
                 x1_ref, hx_ref, gid_ref):
    y = jnp.dot(mg_ref[...].astype(BF16), wo_ref[...], preferred_element_type=F32)
    rows = x_ref.shape[0]
    x1 = x_ref[...] + _mod(gate_ref, rows) * y
    x1_ref[...] = x1
    h2 = _rms(x1, nw_ref[...]) * (1.0 + _mod(scale_ref, rows)) + _mod(shift_ref, rows)
    hx_ref[:, 0:D_MODEL] = h2
    lt = lax.dot_general(wrt_ref[...], h2.astype(BF16), NT_DIMS, preferred_element_type=F32) + brt_ref[...]
    tm = lt.shape[1]
    gl = lt[0:N_GROUPS]
    gidx = lax.broadcasted_iota(jnp.int32, (N_GROUPS, tm), 0)
    gmax = jnp.max(gl, axis=0, keepdims=True)
    g_w = 1.0 / jnp.sum(jnp.exp(gl - gmax), axis=0, keepdims=True)
    g_sel = jnp.min(jnp.where(gl == gmax, gidx, N_GROUPS), axis=0, keepdims=True)
    el = jnp.zeros((EXPERTS_PER_GROUP, tm), F32)
    for g in range(N_GROUPS):
        r = ROUTER_GROUP_STRIDE * (1 + g)
        el = el + jnp.where(g_sel == g, lt[r:r + EXPERTS_PER_GROUP], 0.0)
    eidx = lax.broadcasted_iota(jnp.int32, (EXPERTS_PER_GROUP, tm), 0)
    m1 = jnp.max(el, axis=0, keepdims=True)
    i1 = jnp.min(jnp.where(el == m1, eidx, EXPERTS_PER_GROUP), axis=0, keepdims=True)
    el2 = jnp.where(eidx == i1, -jnp.inf, el)
    m2 = jnp.max(el2, axis=0, keepdims=True)
    i2 = jnp.min(jnp.where(el2 == m2, eidx, EXPERTS_PER_GROUP), axis=0, keepdims=True)
    r21 = jnp.exp(m2 - m1)
    w1 = 1.0 / (1.0 + r21)
    w2 = r21 / (1.0 + r21)
    comb_t = g_w * (jnp.where(eidx == i1, w1, 0.0) + jnp.where(eidx == i2, w2, 0.0))
    ident = (lax.broadcasted_iota(jnp.int32, (EXPERTS_PER_GROUP, LANES), 0)
             == lax.broadcasted_iota(jnp.int32, (EXPERTS_PER_GROUP, LANES), 1)).astype(F32)
    hx_ref[:, D_MODEL:] = lax.dot_general(comb_t, ident, TN_DIMS, precision=HIGHEST, preferred_element_type=F32)
    gid_ref[...] = jnp.broadcast_to(g_sel, gid_ref.shape)


def _post(merged, x2d, gate, shift, scale, mod_specs, wo, nw, wrt, brt, tm):
    m, d = x2d.shape
    row = lambda: pl.BlockSpec((tm, d), lambda i: (i, 0))
    full = lambda a: pl.BlockSpec(a.shape, lambda i: (0,) * a.ndim)
    return pl.pallas_call(
        _post_kernel,
        out_shape=(jax.ShapeDtypeStruct((m, d), F32),
                   jax.ShapeDtypeStruct((m, HX_WIDTH), F32),
                   jax.ShapeDtypeStruct((SUBLANES, m), jnp.int32)),
        grid=(m // tm,),
        in_specs=[row(), row(), mod_specs[0], mod_specs[1], mod_specs[2],
                  full(wo), full(nw), full(wrt), full(brt)],
        out_specs=(row(), pl.BlockSpec((tm, HX_WIDTH), lambda i: (i, 0)),
                   pl.BlockSpec((SUBLANES, tm), lambda i: (0, i))),
        compiler_params=_params("parallel"),
        name="post_mixer",
    )(merged, x2d, gate, shift, scale, wo, nw, wrt, brt)


def _invert_kernel(pos_ref, src_ref):
    def clear(i, carry):
        src_ref[i] = 0
        return carry

    def place(t, carry):
        src_ref[pos_ref[t]] = t
        return carry

    lax.fori_loop(0, src_ref.shape[0], clear, 0, unroll=SCALAR_UNROLL)
    lax.fori_loop(0, pos_ref.shape[0], place, 0, unroll=SCALAR_UNROLL)


def _invert_rows(pos, rows):
    smem = pl.BlockSpec(memory_space=pltpu.SMEM)
    return pl.pallas_call(
        _invert_kernel,
        out_shape=jax.ShapeDtypeStruct((rows,), jnp.int32),
        in_specs=[smem],
        out_specs=smem,
        name="invert_rows",
    )(pos)


def _route(cls, ncls, tm):
    m = cls.shape[0]
    ntiles_max = m // tm + ncls - 1
    onehot = (cls[:, None] == jnp.arange(ncls, dtype=jnp.int32)[None, :]).astype(jnp.int32)
    incl = jnp.cumsum(onehot, axis=0)
    tiles = (incl[-1] + tm - 1) // tm
    tile_end = jnp.cumsum(tiles)
    pos = jnp.sum(onehot * ((tile_end - tiles) * tm + incl - onehot), axis=1)
    src = _invert_rows(pos, ntiles_max * tm)
    t = jnp.arange(ntiles_max, dtype=jnp.int32)
    tile_class = jnp.minimum(jnp.sum((t[:, None] >= tile_end[None, :]).astype(jnp.int32), axis=1), ncls - 1)
    return pos, src, tile_class, tile_end[-1:]


def _row_gather(idx_ref, src_hbm, buf, sem, rows, first=0):
    for r in range(first, first + rows):
        pltpu.make_async_copy(src_hbm.at[pl.ds(idx_ref[0, r], 1)], buf.at[pl.ds(r, 1)],
                              sem).start(priority=r % 2)


def _row_gather_wait(src_hbm, buf, sem):
    pltpu.make_async_copy(src_hbm.at[pl.ds(0, buf.shape[0])], buf, sem).wait()


def _experts_kernel(meta_ref, nt_ref, src_ref, srcn_ref, hx_hbm, *rest, tm, nslot):
    w_refs, (y_ref, xbuf, sem) = rest[:3 * nslot], rest[3 * nslot:]
    t = pl.program_id(0)
    nt = nt_ref[0]
    slot = t % 2

    @pl.when(t == 0)
    def _():
        _row_gather(src_ref, hx_hbm, xbuf.at[0], sem.at[0], tm)

    @pl.when(t < nt)
    def _():
        cur, nxt, sem_cur, sem_nxt = xbuf.at[slot], xbuf.at[1 - slot], sem.at[slot], sem.at[1 - slot]
        _row_gather_wait(hx_hbm, cur, sem_cur)
        x = cur[:, 0:D_MODEL].astype(BF16)
        acc = jnp.zeros((tm, D_MODEL), F32)
        for j in range(nslot):
            wg_ref, wu_ref, wd_ref = w_refs[3 * j:3 * j + 3]
            e = meta_ref[t * (1 + nslot) + 1 + j]
            share = tm // (3 * nslot)
            a = jnp.dot(x, wg_ref[0, 0], preferred_element_type=F32)
            _row_gather(srcn_ref, hx_hbm, nxt, sem_nxt, share, first=(3 * j) * share)
            u = jnp.dot(x, wu_ref[0, 0], preferred_element_type=F32)
            _row_gather(srcn_ref, hx_hbm, nxt, sem_nxt, share, first=(3 * j + 1) * share)
            cw = jnp.zeros((tm, 1), F32)
            for c in range(EXPERTS_PER_GROUP):
                cw = cw + jnp.where(e == c, cur[:, D_MODEL + c:D_MODEL + c + 1], 0.0)
            acc = acc + jnp.dot((_silu(a) * u * cw).astype(BF16), wd_ref[0, 0], preferred_element_type=F32)
            last = tm - (3 * nslot - 1) * share if j == nslot - 1 else share
            _row_gather(srcn_ref, hx_hbm, nxt, sem_nxt, last, first=(3 * j + 2) * share)
        y_ref[:, 0, :] = acc

        @pl.when(t == nt - 1)
        def _():
            _row_gather_wait(hx_hbm, nxt, sem_nxt)

    @pl.when(t >= nt)
    def _():
        y_ref[...] = jnp.zeros_like(y_ref)


def _experts(hx, src, tile_class, ntiles, class_table, wg, wu, wd, tm):
    ntiles_max = tile_class.shape[0]
    nslot = class_table.shape[1] - 1
    d, f = D_MODEL, D_EXPERT
    src3 = src.reshape(ntiles_max, 1, tm)
    meta = jnp.asarray(class_table, jnp.int32)[tile_class].reshape(-1)
    stride = 1 + nslot
    wspec = lambda j, shape: pl.BlockSpec(
        (1, 1) + shape, lambda t, meta, nt, j=j: (meta[t * stride], meta[t * stride + 1 + j], 0, 0))
    w_specs, w_args = [], []
    for j in range(nslot):
        w_specs += [wspec(j, (d, f)), wspec(j, (d, f)), wspec(j, (f, d))]
        w_args += [wg, wu, wd]
    return pl.pallas_call(
        functools.partial(_experts_kernel, tm=tm, nslot=nslot),
        out_shape=jax.ShapeDtypeStruct((ntiles_max * tm, 1, d), F32),
        grid_spec=pltpu.PrefetchScalarGridSpec(
            num_scalar_prefetch=2,
            grid=(ntiles_max,),
            in_specs=[pl.BlockSpec((None, 1, tm), lambda t, meta, nt: (t, 0, 0), memory_space=pltpu.SMEM),
                      pl.BlockSpec((None, 1, tm),
                                   lambda t, meta, nt: (jnp.maximum(jnp.minimum(t + 1, nt[0] - 1), 0), 0, 0),
                                   memory_space=pltpu.SMEM),
                      pl.BlockSpec(memory_space=pl.ANY)] + w_specs,
            out_specs=pl.BlockSpec((tm, 1, d), lambda t, meta, nt: (t, 0, 0)),
            scratch_shapes=[pltpu.VMEM((2, tm, HX_WIDTH), F32), pltpu.SemaphoreType.DMA((2,))]),
        compiler_params=_params("arbitrary"),
        name="experts",
    )(meta, ntiles, src3, src3, hx, *w_args)


def _final_kernel(pos_ref, posn_ref, ys_hbm, x1_ref, gate_ref, nw_ref, o_ref, ybuf, sem, *, tm, nsteps):
    i = pl.program_id(0)
    slot = i % 2

    @pl.when(i == 0)
    def _():
        _row_gather(pos_ref, ys_hbm, ybuf.at[0], sem.at[0], tm)

    _row_gather(posn_ref, ys_hbm, ybuf.at[1 - slot], sem.at[1 - slot], tm)
    _row_gather_wait(ys_hbm, ybuf.at[slot], sem.at[slot])
    o_ref[...] = _rms(x1_ref[...] + _mod(gate_ref, tm) * ybuf[slot, :, 0, :], nw_ref[...])

    @pl.when(i == nsteps - 1)
    def _():
        _row_gather_wait(ys_hbm, ybuf.at[1 - slot], sem.at[1 - slot])


def _final(ys, pos, x1, gate, gate_spec, nw, tm):
    m, d = x1.shape
    nsteps = m // tm
    pos3 = pos.reshape(nsteps, 1, tm)
    return pl.pallas_call(
        functools.partial(_final_kernel, tm=tm, nsteps=nsteps),
        out_shape=jax.ShapeDtypeStruct((m, d), F32),
        grid=(nsteps,),
        in_specs=[pl.BlockSpec((None, 1, tm), lambda i: (i, 0, 0), memory_space=pltpu.SMEM),
                  pl.BlockSpec((None, 1, tm), lambda i: (jnp.minimum(i + 1, nsteps - 1), 0, 0),
                               memory_space=pltpu.SMEM),
                  pl.BlockSpec(memory_space=pl.ANY),
                  pl.BlockSpec((tm, d), lambda i: (i, 0)),
                  gate_spec,
                  pl.BlockSpec((1, d), lambda i: (0, 0))],
        out_specs=pl.BlockSpec((tm, d), lambda i: (i, 0)),
        scratch_shapes=[pltpu.VMEM((2, tm, 1, d), F32), pltpu.SemaphoreType.DMA((2,))],
        compiler_params=_params("arbitrary"),
        name="final",
    )(pos3, pos3, ys, x1, gate, nw)


def _mod_spec(idx, tm, t, ngrid):
    if t % tm == 0:
        per = t // tm
        shape, index = (None, None, 1, D_MODEL), lambda i: (i // per, idx, 0, 0)
    else:
        shape, index = (tm // t, None, 1, D_MODEL), lambda i: (i, idx, 0, 0)
    if ngrid == 2:
        return pl.BlockSpec(shape, lambda i, j: index(i))
    return pl.BlockSpec(shape, index)


def _trunk(x, mod, s_gla, s_gdn, s_conv, w, *, chunk, sub, nseq, t_valid, tm, tm_moe, act_dtype):
    nb, t_pad, d = x.shape
    m = nb * t_pad
    x2d = x.reshape(m, d)
    mods = [mod.reshape(nb, N_MOD, 1, d)] * N_MOD
    spec = lambda idx, ngrid: _mod_spec(idx, tm, t_pad, ngrid)

    proj, small, small_t = _inproj(x2d, mods[0], mods[1], (spec(0, 2), spec(1, 2)), w["norm1"],
                                   w["w_main"], w["w_small"], w["w_small_t"], tm, act_dtype)
    proj3 = proj.reshape(nb, t_pad, PROJ_MAIN)
    small3 = small.reshape(nb, t_pad, LANES)
    small_t4 = small_t.reshape(SMALL_ROWS, nb, t_pad // chunk, chunk).transpose(1, 2, 0, 3)
    merged, new_gla, new_gdn, new_conv = _mixer(proj3, small3, small_t4, w, s_gla, s_gdn, s_conv, chunk=chunk,
                                                sub=sub, nseq=nseq, t_valid=t_valid, out_dtype=act_dtype)
    if t_valid < t_pad:
        merged, x = merged[:, :t_valid], x[:, :t_valid]
    m = nb * t_valid
    tm = min(tm, m)
    tm_moe = min(tm_moe, m)
    spec = lambda idx, ngrid: _mod_spec(idx, tm, t_valid, ngrid)
    x1, hx, gid = _post(merged.reshape(m, d), x.reshape(m, d), mods[2], mods[3], mods[4],
                        (spec(2, 1), spec(3, 1), spec(4, 1)),
                        w["w_out"], w["norm2"], w["w_router_t"], w["b_router_t"], tm)
    class_table = GROUP_CLASSES
    pos, src, tile_class, ntiles = _route(gid[0], class_table.shape[0], tm_moe)
    ys = _experts(hx, src, tile_class, ntiles, class_table, w["w_gate"], w["w_up"], w["w_down"], tm_moe)
    tm_fin = min(tm, FINAL_TILE)
    y = _final(ys, pos, x1, mods[5], _mod_spec(5, tm_fin, t_valid, 1), w["final_norm"], tm_fin)
    return y.reshape(nb, t_valid, d), new_gla, new_gdn, new_conv


def _regroup_kernel(q_ref, r_ref, a_ref, b_ref, o_ref, *, shifts):
    window = jnp.concatenate([a_ref[...], b_ref[...]], axis=0)
    r = r_ref[pl.program_id(0)]
    for s in shifts:
        @pl.when(r == s)
        def _():
            o_ref[...] = window[s:s + PROJ_PART, :].T.astype(o_ref.dtype)


def _regroup_columns(w, starts):
    d = w.shape[1]
    assert d == PROJ_PART and all(s % SUBLANES == 0 for s in starts)
    w = jnp.swapaxes(w, 1, 2)
    q = np.array([s // PROJ_PART for s in starts], np.int32)
    r = np.array([s % PROJ_PART for s in starts], np.int32)
    assert r.max() <= LANES
    per = PROJ_PART // LANES
    return pl.pallas_call(
        functools.partial(_regroup_kernel, shifts=tuple(sorted(set(r.tolist())))),
        out_shape=jax.ShapeDtypeStruct((d, len(starts) * PROJ_PART), BF16),
        grid_spec=pltpu.PrefetchScalarGridSpec(
            num_scalar_prefetch=2,
            grid=(len(starts),),
            in_specs=[pl.BlockSpec((None, PROJ_PART, d), lambda i, q, r: (0, q[i], 0)),
                      pl.BlockSpec((None, LANES, d), lambda i, q, r: (0, per * (q[i] + 1), 0))],
            out_specs=pl.BlockSpec((d, PROJ_PART), lambda i, q, r: (0, i))),
        compiler_params=_params("arbitrary"),
        name="regroup_w_in",
    )(jnp.asarray(q), jnp.asarray(r), w, w)


def _prep_weights(w_in, w_gla_a2, b_gla_a, gla_norm_w, w_conv, gdn_A_log, gdn_dt_bias, gdn_norm_w, w_out,
                  norm1_w, norm2_w, w_group_router, b_group_router, w_expert_router, b_expert_router,
                  w_exp_gate, w_exp_up, w_exp_down, final_norm_w):
    d = D_MODEL
    o = 0
    start = {}
    for name, width in (("gla", 2 * GLA_QK + 2 * GLA_VW), ("ra", GLA_GATE_RANK), ("qkv", GDN_CONV_DIM),
                        ("zb", GDN_VW), ("beta", GDN_HEADS), ("a", GDN_HEADS), ("gates", 2 * D_MODEL)):
        start[name] = o
        o += width
    gate_starts = [start["zb"], start["gates"], start["gates"] + PROJ_PART]
    main_starts = [s for j in range(3) for s in (start["gla"] + j * PROJ_PART, start["qkv"] + j * PROJ_PART,
                                                 gate_starts[j])]
    assert start["a"] == start["beta"] + GDN_HEADS
    w_main = _regroup_columns(w_in, main_starts + [start["ra"], start["beta"]])
    small = jnp.concatenate([w_main[:, PROJ_MAIN:PROJ_MAIN + GLA_GATE_RANK],
                             w_main[:, PROJ_MAIN + PROJ_PART:PROJ_MAIN + PROJ_PART + 2 * GDN_HEADS]], axis=1)
    w_small = jnp.pad(small, ((0, 0), (0, LANES - small.shape[1])))
    w_small_t = small.T
    assert N_GROUPS == EXPERTS_PER_GROUP
    stride_pad = ((0, 0), (0, 0), (0, ROUTER_GROUP_STRIDE - EXPERTS_PER_GROUP))

    def router_rows(group_part, expert_part):
        both = jnp.concatenate([group_part, expert_part], axis=1).reshape(-1, 1 + N_GROUPS, EXPERTS_PER_GROUP)
        return jnp.pad(both, stride_pad).reshape(-1, ROUTER_ROWS)

    wr_t = router_rows(w_group_router, w_expert_router).T
    br_t = router_rows(b_group_router[None], b_expert_router[None]).T
    return dict(
        w_main=w_main, w_small=w_small, w_small_t=w_small_t,
        norm1=norm1_w.reshape(1, d), norm2=norm2_w.reshape(1, d), final_norm=final_norm_w.reshape(1, d),
        wa2=w_gla_a2, ba=b_gla_a.reshape(1, GLA_QK), gla_norm=gla_norm_w.reshape(1, GLA_DV),
        w_conv=w_conv, a_log=gdn_A_log, dt_bias=gdn_dt_bias, gdn_norm=gdn_norm_w.reshape(1, GDN_DV),
        w_out=w_out.astype(BF16), w_router_t=wr_t.astype(BF16), b_router_t=br_t,
        w_gate=w_exp_gate.astype(BF16), w_up=w_exp_up.astype(BF16), w_down=w_exp_down.astype(BF16),
    )


def kernel(x_prompt, x_sample, c_prompt, c_sample, state_gla, state_gdn, state_conv, w_ada, b_ada, norm1_w, w_in, w_gla_a2, b_gla_a, gla_norm_w, w_conv, gdn_A_log, gdn_dt_bias, gdn_norm_w, w_out, norm2_w, w_group_router, b_group_router, w_expert_router, b_expert_router, w_exp_gate, w_exp_up, w_exp_down, final_norm_w):
    assert w_ada.shape[0] == 1, "single layer"
    bp, tp, d = x_prompt.shape
    bs, ts, _ = x_sample.shape
    w = _prep_weights(w_in, w_gla_a2[0], b_gla_a[0], gla_norm_w[0], w_conv[0], gdn_A_log[0], gdn_dt_bias[0],
                      gdn_norm_w[0], w_out[0], norm1_w[0], norm2_w[0], w_group_router[0], b_group_router[0],
                      w_expert_router[0], b_expert_router[0], w_exp_gate[0], w_exp_up[0], w_exp_down[0],
                      final_norm_w)
    mod = _ada_mod(jnp.concatenate([c_prompt, c_sample], axis=0), w_ada[0], b_ada[0]).reshape(bp + bs, N_MOD, d)

    y_p, gla_p, gdn_p, conv_p = _trunk(x_prompt, mod[:bp], None, None, None, w,
                                       chunk=64, sub=GLA_SUBCHUNK, nseq=4, t_valid=tp, tm=min(1024, tp),
                                       tm_moe=min(512, tp), act_dtype=BF16)
    ts_pad = SUBLANES
    xs = jnp.pad(x_sample, ((0, 0), (0, ts_pad - ts), (0, 0)))
    y_s, gla_s, gdn_s, conv_s = _trunk(xs, mod[bp:], state_gla[0], state_gdn[0], state_conv[0], w,
                                       chunk=ts_pad, sub=ts_pad, nseq=4, t_valid=ts, tm=min(512, bs * ts_pad),
                                       tm_moe=128, act_dtype=F32)
    return (y_p, y_s, gla_p[None], gdn_p[None], conv_p[None], gla_s[None], gdn_s[None], conv_s[None])
```
